```python
import math
import jax, jax.numpy as jnp
from jax import lax
import numpy as np

D_MODEL = 1024
BATCH = 4
SEQ = 4096
DEPTH = 1

ATTN_HEADS = 4
QK_DIM = 64
V_DIM = 2 * QK_DIM
ATTN_WIDTH = ATTN_HEADS * V_DIM
ROPE_DIM = QK_DIM // 4
ROPE_THETA = 500000.0
Q_BLOCK = 128
HYENA_WIDTH = D_MODEL // 2
HYENA_ORDER = 2
FILTER_BANDS = 16
FILTER_EMB = 2 * FILTER_BANDS + 1
FILTER_HIDDEN = 64
DECAY_TARGET = 1e-2
FAST_DECAY_PCT = 0.3
SLOW_DECAY_PCT = 1.5
Q_COLS = ATTN_HEADS * 2 * QK_DIM
K_COLS = ATTN_HEADS * 2 * QK_DIM
V_COLS = ATTN_WIDTH
HY_COLS = (HYENA_ORDER + 1) * HYENA_WIDTH
GATE_COLS = 2 * D_MODEL
IN_COLS = Q_COLS + K_COLS + V_COLS + HY_COLS + GATE_COLS
N_EXPERTS = 16
CAPACITY_FACTOR = 2
EXPERT_HIDDEN = 1024
EPS = 1e-6

kernel_name = "hybrid_diffattn_hyena_ecmoe_block"


def rms_norm(x, g):
    xf = x.astype(jnp.float32)
    y = xf * lax.rsqrt(jnp.mean(xf * xf, axis=-1, keepdims=True) + EPS)
    return (y * g.astype(jnp.float32)).astype(x.dtype)


def rope_partial(t, cos, sin):
    half = ROPE_DIM // 2
    r1 = t[..., :half]
    r2 = t[..., half:ROPE_DIM]
    rot = jnp.concatenate([r1 * cos - r2 * sin, r2 * cos + r1 * sin], axis=-1)
    return jnp.concatenate([rot.astype(t.dtype), t[..., ROPE_DIM:]], axis=-1)


def diff_attention(q, k, v, lam, subln_g, lambda_init):
    b, h, _, l, dk = q.shape
    nb = l // Q_BLOCK
    scale = 1.0 / math.sqrt(dk)
    qb = q.reshape(b, h, 2, nb, Q_BLOCK, dk).transpose(3, 0, 1, 2, 4, 5)

    def block(qi):
        s = jnp.einsum('bhmqd,bhmkd->bhmqk', qi, k).astype(jnp.float32) * scale
        pr = jax.nn.softmax(s, axis=-1)
        a = pr[:, :, 0] - lam * pr[:, :, 1]
        return jnp.einsum('bhqk,bhkd->bhqd', a.astype(v.dtype), v)

    o = lax.map(block, qb)
    o = o.transpose(1, 2, 0, 3, 4).reshape(b, h, l, V_DIM)
    o = rms_norm(o, subln_g) * (1.0 - lambda_init)
    return o.transpose(0, 2, 1, 3).reshape(b, l, h * V_DIM)


def hyena_filter_spectra(l, w1, b1, w2, b2, w3, b3, w_out, freq):
    f32 = jnp.float32
    pos = jnp.arange(l, dtype=f32)
    t = pos / max(l - 1, 1)
    bands = jnp.linspace(1e-4, FILTER_BANDS - 1, FILTER_BANDS, dtype=f32)
    ang = (2.0 * math.pi / l) * pos[:, None] * bands[None, :]
    emb = jnp.concatenate([t[:, None], jnp.cos(ang), -jnp.sin(ang)], axis=-1)
    fr = freq.astype(f32)
    hid = jnp.sin(fr * (emb @ w1.astype(f32) + b1.astype(f32)))
    hid = jnp.sin(fr * (hid @ w2.astype(f32) + b2.astype(f32)))
    hid = jnp.sin(fr * (hid @ w3.astype(f32) + b3.astype(f32)))
    filt = (hid @ w_out.astype(f32)).reshape(l, HYENA_ORDER, 2, HYENA_WIDTH)
    min_decay = math.log(DECAY_TARGET) / SLOW_DECAY_PCT
    max_decay = math.log(DECAY_TARGET) / FAST_DECAY_PCT
    deltas = jnp.abs(jnp.linspace(min_decay, max_decay, HYENA_WIDTH, dtype=f32))
    filt = filt * jnp.exp(-t[:, None] * deltas[None, :])[:, None, None, :]
    kern = jnp.concatenate([filt[:, :, 0],
                            jnp.zeros((1, HYENA_ORDER, HYENA_WIDTH), f32),
                            filt[:0:-1, :, 1]], axis=0)
    return jnp.fft.rfft(kern, axis=0)


def long_conv(z, spec):
    l = z.shape[1]
    zf = jnp.fft.rfft(z.astype(jnp.float32), n=2 * l, axis=1)
    y = jnp.fft.irfft(zf * spec[None], n=2 * l, axis=1)[:, :l]
    return y.astype(z.dtype)


def short_conv(z, w, b):
    zp = jnp.pad(z, ((0, 0), (1, 1), (0, 0)))
    return zp[:, :-2] * w[0] + zp[:, 1:-1] * w[1] + zp[:, 2:] * w[2] + b


def hyena_mixer(hy, conv_w, conv_b, spec, skip):
    hy = short_conv(hy, conv_w, conv_b)
    z, g1, g2 = jnp.split(hy, HYENA_ORDER + 1, axis=-1)
    gates = (g1, g2)
    for n in range(HYENA_ORDER):
        z = gates[n] * (long_conv(z, spec[:, n]) + skip[n] * z)
    return z


def expert_choice_ffn(u, w_router, w_gate, w_up, w_down):
    b, l, d = u.shape
    cap = CAPACITY_FACTOR * l // N_EXPERTS
    logits = jnp.einsum('bld,de->ble', u, w_router).astype(jnp.float32)
    aff = jax.nn.softmax(logits, axis=-1)
    g, idx = lax.top_k(aff.transpose(0, 2, 1), cap)
    xin = jax.vmap(lambda ub, ib: ub[ib])(u, idx)
    hg = jnp.einsum('becd,edf->becf', xin, w_gate)
    hu = jnp.einsum('becd,edf->becf', xin, w_up)
    eo = jnp.einsum('becf,efd->becd', jax.nn.silu(hg) * hu, w_down)
    weighted = eo * g[..., None].astype(eo.dtype)
    return jax.vmap(lambda ib, wb: jnp.zeros((l, d), wb.dtype).at[ib.reshape(-1)].add(wb.reshape(-1, d)))(idx, weighted)


def setup_inputs(seed: int = 0) -> dict:
    key = jax.random.key(seed)
    ks = iter(jax.random.split(key, 32))

    def nrm(shape, std):
        return jax.random.normal(next(ks), shape, jnp.float32) * std

    def gain(shape):
        return 1.0 + nrm(shape, 0.02)

    L_ = DEPTH
    return {
        "x": nrm((BATCH, SEQ, D_MODEL), 1.0),
        "norm1_g": gain((L_, D_MODEL)),
        "w_in": nrm((L_, D_MODEL, IN_COLS), D_MODEL ** -0.5),
        "short_conv_w": nrm((L_, 3, HY_COLS), 3 ** -0.5),
        "short_conv_b": nrm((L_, HY_COLS), 0.01),
        "q_norm_g": gain((L_, QK_DIM)),
        "k_norm_g": gain((L_, QK_DIM)),
        "lambda_q1": nrm((L_, QK_DIM), 0.1),
        "lambda_k1": nrm((L_, QK_DIM), 0.1),
        "lambda_q2": nrm((L_, QK_DIM), 0.1),
        "lambda_k2": nrm((L_, QK_DIM), 0.1),
        "subln_g": gain((L_, V_DIM)),
        "filt_w1": nrm((L_, FILTER_EMB, FILTER_HIDDEN), FILTER_EMB ** -0.5),
        "filt_b1": nrm((L_, FILTER_HIDDEN), 0.01),
        "filt_w2": nrm((L_, FILTER_HIDDEN, FILTER_HIDDEN), FILTER_HIDDEN ** -0.5),
        "filt_b2": nrm((L_, FILTER_HIDDEN), 0.01),
        "filt_w3": nrm((L_, FILTER_HIDDEN, FILTER_HIDDEN), FILTER_HIDDEN ** -0.5),
        "filt_b3": nrm((L_, FILTER_HIDDEN), 0.01),
        "filt_w_out": nrm((L_, FILTER_HIDDEN, HYENA_ORDER * 2 * HYENA_WIDTH), 0.003),
        "filt_freq": 1.0 + nrm((L_, FILTER_HIDDEN), 0.1),
        "hyena_skip": nrm((L_, HYENA_ORDER, HYENA_WIDTH), 0.5),
        "w_branch_attn": nrm((L_, ATTN_WIDTH, D_MODEL), ATTN_WIDTH ** -0.5),
        "w_branch_hyena": nrm((L_, HYENA_WIDTH, D_MODEL), HYENA_WIDTH ** -0.5),
        "w_out": nrm((L_, D_MODEL, D_MODEL), D_MODEL ** -0.5),
        "norm2_g": gain((L_, D_MODEL)),
        "w_router": nrm((L_, D_MODEL, N_EXPERTS), D_MODEL ** -0.5),
        "w_gate": nrm((L_, N_EXPERTS, D_MODEL, EXPERT_HIDDEN), D_MODEL ** -0.5),
        "w_up": nrm((L_, N_EXPERTS, D_MODEL, EXPERT_HIDDEN), D_MODEL ** -0.5),
        "w_down": nrm((L_, N_EXPERTS, EXPERT_HIDDEN, D_MODEL), EXPERT_HIDDEN ** -0.5),
    }


def reference(x, norm1_g, w_in, short_conv_w, short_conv_b, q_norm_g, k_norm_g,
              lambda_q1, lambda_k1, lambda_q2, lambda_k2, subln_g,
              filt_w1, filt_b1, filt_w2, filt_b2, filt_w3, filt_b3, filt_w_out, filt_freq,
              hyena_skip, w_branch_attn, w_branch_hyena, w_out, norm2_g,
              w_router, w_gate, w_up, w_down):
    b, l, _ = x.shape
    pos = jnp.arange(l, dtype=jnp.float32)
    inv_freq = ROPE_THETA ** (-jnp.arange(0, ROPE_DIM, 2, dtype=jnp.float32) / ROPE_DIM)
    ang = pos[:, None] * inv_freq[None, :]
    cos, sin = jnp.cos(ang), jnp.sin(ang)
    splits = [Q_COLS, Q_COLS + K_COLS, Q_COLS + K_COLS + V_COLS,
              Q_COLS + K_COLS + V_COLS + HY_COLS]

    for li in range(DEPTH):
        lambda_init = 0.8 - 0.6 * math.exp(-0.3 * li)
        u = rms_norm(x, norm1_g[li])
        proj = jnp.einsum('bld,dc->blc', u, w_in[li])
        q, k, v, hy, gts = jnp.split(proj, splits, axis=-1)
        q = q.reshape(b, l, ATTN_HEADS, 2, QK_DIM).transpose(0, 2, 3, 1, 4)
        k = k.reshape(b, l, ATTN_HEADS, 2, QK_DIM).transpose(0, 2, 3, 1, 4)
        v = v.reshape(b, l, ATTN_HEADS, V_DIM).transpose(0, 2, 1, 3)
        q = rope_partial(rms_norm(q, q_norm_g[li]), cos, sin)
        k = rope_partial(rms_norm(k, k_norm_g[li]), cos, sin)
        lam = (jnp.exp(jnp.sum(lambda_q1[li] * lambda_k1[li]).astype(jnp.float32))
               - jnp.exp(jnp.sum(lambda_q2[li] * lambda_k2[li]).astype(jnp.float32))
               + lambda_init)
        attn = diff_attention(q, k, v, lam, subln_g[li], lambda_init)

        spec = hyena_filter_spectra(l, filt_w1[li], filt_b1[li], filt_w2[li], filt_b2[li],
                                    filt_w3[li], filt_b3[li], filt_w_out[li], filt_freq[li])
        hyena = hyena_mixer(hy, short_conv_w[li], short_conv_b[li], spec, hyena_skip[li])

        gate = jax.nn.sigmoid(gts.astype(jnp.float32)).astype(x.dtype)
        g_attn, g_hyena = jnp.split(gate, 2, axis=-1)
        merged = (g_attn * jnp.einsum('blc,cd->bld', attn, w_branch_attn[li])
                  + g_hyena * jnp.einsum('blc,cd->bld', hyena, w_branch_hyena[li]))
        x = x + jnp.einsum('bld,de->ble', merged, w_out[li])
        x = x + expert_choice_ffn(rms_norm(x, norm2_g[li]), w_router[li], w_gate[li], w_up[li], w_down[li])
    return x
```

```python
import functools
import math

import jax
import jax.numpy as jnp
import numpy as np
from jax import lax
from jax.experimental import pallas as pl
from jax.experimental.pallas import tpu as pltpu

F32 = jnp.float32
BF16 = jnp.bfloat16

ATTN_HEADS = 4
QK_DIM = 64
V_DIM = 128
ROPE_DIM = 16
ROPE_THETA = 500000.0
HYENA_WIDTH = 512
HYENA_ORDER = 2
FILTER_BANDS = 16
DECAY_TARGET = 1e-2
FAST_DECAY_PCT = 0.3
SLOW_DECAY_PCT = 1.5
N_EXPERTS = 16
CAPACITY_FACTOR = 2
EPS = 1e-6
LANES = 128
VMEM_LIMIT = 56 * 1024 * 1024


def _cparams(sem):
    return pltpu.CompilerParams(dimension_semantics=sem, vmem_limit_bytes=VMEM_LIMIT)


def _inproj_kernel(x_ref, g_ref, w_ref, o_ref, u_ref):
    @pl.when(pl.program_id(1) == 0)
    def _():
        x = x_ref[...]
        ms = jnp.mean(x * x, axis=-1, keepdims=True)
        u_ref[...] = (x * lax.rsqrt(ms + EPS) * g_ref[...]).astype(BF16)

    o_ref[...] = jnp.dot(u_ref[...], w_ref[...], preferred_element_type=F32).astype(o_ref.dtype)


def _inproj(x2, g, w_bf16, tm=1024, tn=1280):
    n, d = x2.shape
    cols = w_bf16.shape[1]
    return pl.pallas_call(
        _inproj_kernel,
        out_shape=jax.ShapeDtypeStruct((n, cols), BF16),
        grid=(n // tm, cols // tn),
        in_specs=[
            pl.BlockSpec((tm, d), lambda i, j: (i, 0)),
            pl.BlockSpec((1, d), lambda i, j: (0, 0)),
            pl.BlockSpec((d, tn), lambda i, j: (0, j)),
        ],
        out_specs=pl.BlockSpec((tm, tn), lambda i, j: (i, j)),
        scratch_shapes=[pltpu.VMEM((tm, d), BF16)],
        compiler_params=_cparams(("arbitrary", "arbitrary")),
        name="inproj",
    )(x2, g, w_bf16)


def _qkprep_kernel(p_ref, c_ref, s1_ref, s2_ref, g_ref, o_ref):
    lane = lax.broadcasted_iota(jnp.int32, (1, LANES), 1)
    lo = lane < QK_DIM
    c = c_ref[...]
    s1 = s1_ref[...]
    s2 = s2_ref[...]
    for j in range(p_ref.shape[1] // LANES):
        t = p_ref[:, j * LANES:(j + 1) * LANES].astype(F32)
        sq = t * t
        ss_lo = jnp.sum(jnp.where(lo, sq, 0.0), axis=-1, keepdims=True)
        ss_hi = jnp.sum(jnp.where(lo, 0.0, sq), axis=-1, keepdims=True)
        r = lax.rsqrt(jnp.where(lo, ss_lo, ss_hi) * (1.0 / QK_DIM) + EPS)
        y = t * r * g_ref[:, j * LANES:(j + 1) * LANES]
        y = y * c + pltpu.roll(y, LANES - ROPE_DIM // 2, 1) * s1 + pltpu.roll(y, ROPE_DIM // 2, 1) * s2
        o_ref[:, j * LANES:(j + 1) * LANES] = y.astype(o_ref.dtype)


def _qkprep(proj, ctab, s1tab, s2tab, gqk, seq, tm=512):
    n = proj.shape[0]
    w = gqk.shape[1]
    nb = seq // tm
    return pl.pallas_call(
        _qkprep_kernel,
        out_shape=jax.ShapeDtypeStruct((n, w), BF16),
        grid=(n // tm,),
        in_specs=[
            pl.BlockSpec((tm, w), lambda i: (i, 0)),
            pl.BlockSpec((tm, LANES), lambda i: (i % nb, 0)),
            pl.BlockSpec((tm, LANES), lambda i: (i % nb, 0)),
            pl.BlockSpec((tm, LANES), lambda i: (i % nb, 0)),
            pl.BlockSpec((1, w), lambda i: (0, 0)),
        ],
        out_specs=pl.BlockSpec((tm, w), lambda i: (i, 0)),
        compiler_params=_cparams(("arbitrary",)),
        name="qkprep",
    )(proj, ctab, s1tab, s2tab, gqk)


def _attn_kernel(q_ref, k_ref, v_ref, lam_ref, g_ref, o_ref, *, tk, lambda_init):
    tq = q_ref.shape[0]
    seq = k_ref.shape[0]
    lane = lax.broadcasted_iota(jnp.int32, (1, LANES), 1)
    q = q_ref[...]
    zero = jnp.zeros_like(q)
    q1 = jnp.where(lane < QK_DIM, q, zero)
    q2 = jnp.where(lane < QK_DIM, zero, q)
    lv = lam_ref[...]
    lam = (jnp.exp(jnp.sum(lv[0:1] * lv[1:2], axis=-1, keepdims=True))
           - jnp.exp(jnp.sum(lv[2:3] * lv[3:4], axis=-1, keepdims=True)) + lambda_init)

    def step(c, carry):
        m1, l1, a1, m2, l2, a2 = carry
        off = pl.multiple_of(c * tk, tk)
        kc = k_ref[pl.ds(off, tk), :]
        vc = v_ref[pl.ds(off, tk), :]

        def one(qm, m, l, a):
            s = lax.dot_general(qm, kc, (((1,), (1,)), ((), ())), preferred_element_type=F32)
            mn = jnp.maximum(m, jnp.max(s, axis=-1, keepdims=True))
            p = jnp.exp(s - mn)
            al = jnp.exp(m - mn)
            l = al * l + jnp.sum(p, axis=-1, keepdims=True)
            a = al * a + jnp.dot(p.astype(BF16), vc, preferred_element_type=F32)
            return mn, l, a

        m1, l1, a1 = one(q1, m1, l1, a1)
        m2, l2, a2 = one(q2, m2, l2, a2)
        return m1, l1, a1, m2, l2, a2

    neg = jnp.full((tq, 1), -jnp.inf, F32)
    z1 = jnp.zeros((tq, 1), F32)
    za = jnp.zeros((tq, V_DIM), F32)
    m1, l1, a1, m2, l2, a2 = lax.fori_loop(0, seq // tk, step, (neg, z1, za, neg, z1, za))
    o = a1 / l1 - lam * (a2 / l2)
    ms = jnp.mean(o * o, axis=-1, keepdims=True)
    o = o * lax.rsqrt(ms + EPS) * g_ref[...] * (1.0 - lambda_init)
    o_ref[...] = o.astype(o_ref.dtype)


def _attention(qk, proj, lamv, subln_g, batch, seq, lambda_init, tq=256, tk=512):
    nq = seq // tq
    vcol0 = (2 * ATTN_HEADS * 2 * QK_DIM) // V_DIM
    return pl.pallas_call(
        functools.partial(_attn_kernel, tk=tk, lambda_init=lambda_init),
        out_shape=jax.ShapeDtypeStruct((batch * seq, ATTN_HEADS * V_DIM), BF16),
        grid=(batch, ATTN_HEADS, nq),
        in_specs=[
            pl.BlockSpec((tq, LANES), lambda b, h, i: (b * nq + i, h)),
            pl.BlockSpec((seq, LANES), lambda b, h, i: (b, ATTN_HEADS + h)),
            pl.BlockSpec((seq, V_DIM), lambda b, h, i: (b, vcol0 + h)),
            pl.BlockSpec((4, QK_DIM), lambda b, h, i: (0, 0)),
            pl.BlockSpec((1, V_DIM), lambda b, h, i: (0, 0)),
        ],
        out_specs=pl.BlockSpec((tq, V_DIM), lambda b, h, i: (b * nq + i, h)),
        compiler_params=_cparams(("arbitrary", "arbitrary", "arbitrary")),
        name="diffattn",
    )(qk, qk, proj, lamv, subln_g)


def _merge_kernel(at_ref, hy_ref, ga_ref, gh_ref, x_ref, wpa_ref, wph_ref, wo_ref, g2_ref, wr_ref,
                  x1_ref, u2_ref, aff_ref):
    ga = jax.nn.sigmoid(ga_ref[...].astype(F32))
    gh = jax.nn.sigmoid(gh_ref[...].astype(F32))
    merged = (ga * jnp.dot(at_ref[...], wpa_ref[...], preferred_element_type=F32)
              + gh * jnp.dot(hy_ref[...], wph_ref[...], preferred_element_type=F32))
    x1 = x_ref[...] + jnp.dot(merged.astype(BF16), wo_ref[...], preferred_element_type=F32)
    x1_ref[...] = x1
    ms = jnp.mean(x1 * x1, axis=-1, keepdims=True)
    u2 = (x1 * lax.rsqrt(ms + EPS) * g2_ref[...]).astype(BF16)
    u2_ref[...] = u2
    logits = jnp.dot(u2, wr_ref[...], preferred_element_type=F32)
    lane = lax.broadcasted_iota(jnp.int32, logits.shape, 1)
    logits = jnp.where(lane < N_EXPERTS, logits, -jnp.inf)
    e = jnp.exp(logits - jnp.max(logits, axis=-1, keepdims=True))
    aff_ref[...] = e / jnp.sum(e, axis=-1, keepdims=True)


def _merge(attn, hyena, proj, x2, wpa, wph, wo, g2, wr_pad, tm=512):
    n, d = x2.shape
    gcol0 = (proj.shape[1] - 2 * d) // d
    full = lambda shape: pl.BlockSpec(shape, lambda i: (0, 0))
    return pl.pallas_call(
        _merge_kernel,
        out_shape=(jax.ShapeDtypeStruct((n, d), F32), jax.ShapeDtypeStruct((n, d), BF16),
                   jax.ShapeDtypeStruct((n, LANES), F32)),
        grid=(n // tm,),
        in_specs=[
            pl.BlockSpec((tm, attn.shape[1]), lambda i: (i, 0)),
            pl.BlockSpec((tm, hyena.shape[1]), lambda i: (i, 0)),
            pl.BlockSpec((tm, d), lambda i: (i, gcol0)),
            pl.BlockSpec((tm, d), lambda i: (i, gcol0 + 1)),
            pl.BlockSpec((tm, d), lambda i: (i, 0)),
            full(wpa.shape), full(wph.shape), full(wo.shape), full(g2.shape), full(wr_pad.shape),
        ],
        out_specs=(pl.BlockSpec((tm, d), lambda i: (i, 0)), pl.BlockSpec((tm, d), lambda i: (i, 0)),
                   pl.BlockSpec((tm, LANES), lambda i: (i, 0))),
        compiler_params=_cparams(("arbitrary",)),
        name="merge",
    )(attn, hyena, proj, proj, x2, wpa, wph, wo, g2, wr_pad)


def _excl_cumsum_rows(mask_f32, tri, blk):
    rows = mask_f32.shape[0]
    carry = jnp.zeros((1, LANES), F32)
    outs = []
    for r in range(rows // blk):
        mb = mask_f32[r * blk:(r + 1) * blk]
        outs.append(jnp.dot(tri, mb.astype(BF16), preferred_element_type=F32) + carry)
        carry = carry + jnp.sum(mb, axis=0, keepdims=True)
    return jnp.concatenate(outs, axis=0)


def _topk_kernel(aff_ref, pos_ref, post_ref, afft_ref, *, cap, blk):
    aff = aff_ref[...]
    bits = pltpu.bitcast(aff, jnp.int32)

    def search(i, prefix):
        cand = prefix | jnp.left_shift(jnp.int32(1), 30 - i)
        cnt = jnp.sum((bits >= cand).astype(F32), axis=0, keepdims=True)
        return jnp.where(cnt >= cap, cand, prefix)

    thr = lax.fori_loop(0, 31, search, jnp.zeros((1, LANES), jnp.int32))
    gt = (bits > thr).astype(F32)
    eq = (bits == thr).astype(F32)
    need = cap - jnp.sum(gt, axis=0, keepdims=True)
    ri = lax.broadcasted_iota(jnp.int32, (blk, blk), 0)
    ci = lax.broadcasted_iota(jnp.int32, (blk, blk), 1)
    tri = (ci < ri).astype(BF16)
    sel = gt + eq * (_excl_cumsum_rows(eq, tri, blk) < need).astype(F32)
    pos = jnp.where(sel > 0.0, _excl_cumsum_rows(sel, tri, blk), -1.0)
    pos_ref[...] = pos
    post_ref[0] = pos.T[:N_EXPERTS]
    afft_ref[0] = aff.T[:N_EXPERTS]


def _topk(aff, batch, seq, cap, blk=256):
    return pl.pallas_call(
        functools.partial(_topk_kernel, cap=cap, blk=blk),
        out_shape=(jax.ShapeDtypeStruct((batch * seq, LANES), F32),
                   jax.ShapeDtypeStruct((batch, N_EXPERTS, seq), F32),
                   jax.ShapeDtypeStruct((batch, N_EXPERTS, seq), F32)),
        grid=(batch,),
        in_specs=[pl.BlockSpec((seq, LANES), lambda b: (b, 0))],
        out_specs=(pl.BlockSpec((seq, LANES), lambda b: (b, 0)),
                   pl.BlockSpec((1, N_EXPERTS, seq), lambda b: (b, 0, 0)),
                   pl.BlockSpec((1, N_EXPERTS, seq), lambda b: (b, 0, 0))),
        compiler_params=_cparams(("arbitrary",)),
        name="topk",
    )(aff)


def _expert_kernel(post_ref, afft_ref, u_ref, wg_ref, wu_ref, wd_ref, o_ref, *, cap, tc):
    e = pl.program_id(0)
    seq = u_ref.shape[0]
    slot = lax.broadcasted_iota(jnp.int32, (cap, tc), 0).astype(F32)

    def gather(c, carry):
        xin, gs = carry
        off = pl.multiple_of(c * tc, tc)
        prow = post_ref[0, pl.ds(e, 1), pl.ds(off, tc)]
        arow = afft_ref[0, pl.ds(e, 1), pl.ds(off, tc)]
        hit = slot == prow
        gs = gs + jnp.sum(jnp.where(hit, arow, 0.0), axis=-1, keepdims=True)
        xin = xin + jnp.dot(jnp.where(hit, 1.0, 0.0).astype(BF16), u_ref[pl.ds(off, tc), :],
                            preferred_element_type=F32)
        return xin, gs

    xin, gs = lax.fori_loop(0, seq // tc, gather,
                            (jnp.zeros((cap, u_ref.shape[1]), F32), jnp.zeros((cap, 1), F32)))
    xb = xin.astype(BF16)
    hg = jnp.dot(xb, wg_ref[0], preferred_element_type=F32)
    hu = jnp.dot(xb, wu_ref[0], preferred_element_type=F32)
    act = (hg * jax.nn.sigmoid(hg) * hu).astype(BF16)
    eo = jnp.dot(act, wd_ref[0], preferred_element_type=F32)
    o_ref[0, 0] = (eo * gs).astype(o_ref.dtype)


def _experts(post, afft, u2, wg, wu, wd, batch, seq, cap, tc=512):
    d = u2.shape[1]
    hid = wg.shape[2]
    return pl.pallas_call(
        functools.partial(_expert_kernel, cap=cap, tc=tc),
        out_shape=jax.ShapeDtypeStruct((batch, N_EXPERTS, cap, d), BF16),
        grid=(N_EXPERTS, batch),
        in_specs=[
            pl.BlockSpec((1, N_EXPERTS, seq), lambda e, b: (b, 0, 0)),
            pl.BlockSpec((1, N_EXPERTS, seq), lambda e, b: (b, 0, 0)),
            pl.BlockSpec((seq, d), lambda e, b: (b, 0)),
            pl.BlockSpec((1, d, hid), lambda e, b: (e, 0, 0)),
            pl.BlockSpec((1, d, hid), lambda e, b: (e, 0, 0)),
            pl.BlockSpec((1, hid, d), lambda e, b: (e, 0, 0)),
        ],
        out_specs=pl.BlockSpec((1, 1, cap, d), lambda e, b: (b, e, 0, 0)),
        compiler_params=_cparams(("arbitrary", "arbitrary")),
        name="experts",
    )(post, afft, u2, wg, wu, wd)


def _combine_kernel(pos_ref, x1_ref, weo_ref, o_ref, *, cap):
    tt = pos_ref.shape[0]
    pos = pos_ref[...]
    slot = lax.broadcasted_iota(jnp.int32, (tt, cap), 1).astype(F32)
    acc = x1_ref[...]
    for e in range(N_EXPERTS):
        hit = slot == pos[:, e:e + 1]
        acc = acc + jnp.dot(jnp.where(hit, 1.0, 0.0).astype(BF16), weo_ref[0, e], preferred_element_type=F32)
    o_ref[...] = acc


def _combine(pos, x1, weo, batch, seq, cap, tt=512):
    d = x1.shape[1]
    nt = seq // tt
    return pl.pallas_call(
        functools.partial(_combine_kernel, cap=cap),
        out_shape=jax.ShapeDtypeStruct(x1.shape, F32),
        grid=(batch, nt),
        in_specs=[
            pl.BlockSpec((tt, LANES), lambda b, t: (b * nt + t, 0)),
            pl.BlockSpec((tt, d), lambda b, t: (b * nt + t, 0)),
            pl.BlockSpec((1, N_EXPERTS, cap, d), lambda b, t: (b, 0, 0, 0)),
        ],
        out_specs=pl.BlockSpec((tt, d), lambda b, t: (b * nt + t, 0)),
        compiler_params=_cparams(("arbitrary", "arbitrary")),
        name="combine",
    )(pos, x1, weo)


def _hyena_jax(hy, conv_w, conv_b, w1, b1, w2, b2, w3, b3, w_out, freq, skip, l):
    f32 = jnp.float32
    pos = jnp.arange(l, dtype=f32)
    t = pos / max(l - 1, 1)
    bands = jnp.linspace(1e-4, FILTER_BANDS - 1, FILTER_BANDS, dtype=f32)
    ang = (2.0 * math.pi / l) * pos[:, None] * bands[None, :]
    emb = jnp.concatenate([t[:, None], jnp.cos(ang), -jnp.sin(ang)], axis=-1)
    hid = jnp.sin(freq * (emb @ w1 + b1))
    hid = jnp.sin(freq * (hid @ w2 + b2))
    hid = jnp.sin(freq * (hid @ w3 + b3))
    filt = (hid @ w_out).reshape(l, HYENA_ORDER, 2, HYENA_WIDTH)
    min_decay = math.log(DECAY_TARGET) / SLOW_DECAY_PCT
    max_decay = math.log(DECAY_TARGET) / FAST_DECAY_PCT
    deltas = jnp.abs(jnp.linspace(min_decay, max_decay, HYENA_WIDTH, dtype=f32))
    filt = filt * jnp.exp(-t[:, None] * deltas[None, :])[:, None, None, :]
    kern = jnp.concatenate([filt[:, :, 0], jnp.zeros((1, HYENA_ORDER, HYENA_WIDTH), f32), filt[:0:-1, :, 1]], axis=0)
    spec = jnp.fft.rfft(kern, axis=0)
    hy = hy.astype(f32)
    zp = jnp.pad(hy, ((0, 0), (1, 1), (0, 0)))
    hy = zp[:, :-2] * conv_w[0] + zp[:, 1:-1] * conv_w[1] + zp[:, 2:] * conv_w[2] + conv_b
    z, g1, g2 = jnp.split(hy, 3, axis=-1)
    for n, g in enumerate((g1, g2)):
        zf = jnp.fft.rfft(z, n=2 * l, axis=1)
        y = jnp.fft.irfft(zf * spec[None, :, n], n=2 * l, axis=1)[:, :l]
        z = g * (y + skip[n] * z)
    return z


def _rope_tables(seq):
    pos = np.arange(seq, dtype=np.float32)
    inv_freq = (ROPE_THETA ** (-np.arange(0, ROPE_DIM, 2, dtype=np.float32) / ROPE_DIM)).astype(np.float32)
    ang = pos[:, None] * inv_freq[None, :]
    cos, sin = np.cos(ang).astype(np.float32), np.sin(ang).astype(np.float32)
    half = ROPE_DIM // 2
    c = np.ones((seq, QK_DIM), np.float32)
    s1 = np.zeros((seq, QK_DIM), np.float32)
    s2 = np.zeros((seq, QK_DIM), np.float32)
    c[:, :half] = cos
    c[:, half:ROPE_DIM] = cos
    s1[:, :half] = -sin
    s2[:, half:ROPE_DIM] = sin
    rep = LANES // QK_DIM
    return tuple(jnp.asarray(np.tile(a, (1, rep))) for a in (c, s1, s2))


def kernel(x, norm1_g, w_in, short_conv_w, short_conv_b, q_norm_g, k_norm_g, lambda_q1, lambda_k1, lambda_q2, lambda_k2, subln_g, filt_w1, filt_b1, filt_w2, filt_b2, filt_w3, filt_b3, filt_w_out, filt_freq, hyena_skip, w_branch_attn, w_branch_hyena, w_out, norm2_g, w_router, w_gate, w_up, w_down):
    b, l, d = x.shape
    depth = w_in.shape[0]
    n = b * l
    cap = CAPACITY_FACTOR * l // N_EXPERTS
    q_cols = ATTN_HEADS * 2 * QK_DIM
    ctab, s1tab, s2tab = _rope_tables(l)
    xc = x.reshape(n, d)
    for li in range(depth):
        lambda_init = 0.8 - 0.6 * math.exp(-0.3 * li)
        proj = _inproj(xc, norm1_g[li][None], w_in[li].astype(BF16))
        scale = 1.0 / math.sqrt(QK_DIM)
        gqk = jnp.concatenate([jnp.tile(q_norm_g[li], q_cols // QK_DIM) * scale,
                               jnp.tile(k_norm_g[li], q_cols // QK_DIM)])[None]
        qk = _qkprep(proj, ctab, s1tab, s2tab, gqk, l)
        lamv = jnp.stack([lambda_q1[li], lambda_k1[li], lambda_q2[li], lambda_k2[li]])
        attn = _attention(qk, proj, lamv, subln_g[li][None], b, l, lambda_init)

        hy = proj[:, 3 * q_cols:3 * q_cols + 3 * HYENA_WIDTH].reshape(b, l, 3 * HYENA_WIDTH)
        hyena = _hyena_jax(hy, short_conv_w[li], short_conv_b[li], filt_w1[li], filt_b1[li], filt_w2[li],
                           filt_b2[li], filt_w3[li], filt_b3[li], filt_w_out[li], filt_freq[li],
                           hyena_skip[li], l).reshape(n, HYENA_WIDTH).astype(BF16)

        wr_pad = jnp.pad(w_router[li], ((0, 0), (0, LANES - N_EXPERTS))).astype(BF16)
        x1, u2, aff = _merge(attn, hyena, proj, xc, w_branch_attn[li].astype(BF16),
                             w_branch_hyena[li].astype(BF16), w_out[li].astype(BF16),
                             norm2_g[li][None], wr_pad)
        pos, post, afft = _topk(aff, b, l, cap)
        weo = _experts(post, afft, u2, w_gate[li].astype(BF16), w_up[li].astype(BF16),
                       w_down[li].astype(BF16), b, l, cap)
        xc = _combine(pos, x1, weo, b, l, cap)
    return xc.reshape(b, l, d)
```

```python
import functools
import math

import jax
import jax.numpy as jnp
import numpy as np
from jax import lax
from jax.experimental import pallas as pl
from jax.experimental.pallas import tpu as pltpu

F32 = jnp.float32
BF16 = jnp.bfloat16

ATTN_HEADS = 4
QK_DIM = 64
V_DIM = 128
ROPE_DIM = 16
ROPE_THETA = 500000.0
HYENA_WIDTH = 512
HYENA_ORDER = 2
FILTER_BANDS = 16
DECAY_TARGET = 1e-2
FAST_DECAY_PCT = 0.3
SLOW_DECAY_PCT = 1.5
N_EXPERTS = 16
CAPACITY_FACTOR = 2
EPS = 1e-6
LANES = 128
VMEM_LIMIT = 56 * 1024 * 1024


def _cparams(sem):
    return pltpu.CompilerParams(dimension_semantics=sem, vmem_limit_bytes=VMEM_LIMIT)


def _inproj_kernel(x_ref, g_ref, w_ref, o_ref, u_ref):
    @pl.when(pl.program_id(1) == 0)
    def _():
        x = x_ref[...]
        ms = jnp.mean(x * x, axis=-1, keepdims=True)
        u_ref[...] = (x * lax.rsqrt(ms + EPS) * g_ref[...]).astype(BF16)

    o_ref[...] = jnp.dot(u_ref[...], w_ref[...], preferred_element_type=F32).astype(o_ref.dtype)


def _inproj(x2, g, w_bf16, tm=1024, tn=1280):
    n, d = x2.shape
    cols = w_bf16.shape[1]
    return pl.pallas_call(
        _inproj_kernel,
        out_shape=jax.ShapeDtypeStruct((n, cols), BF16),
        grid=(n // tm, cols // tn),
        in_specs=[
            pl.BlockSpec((tm, d), lambda i, j: (i, 0)),
            pl.BlockSpec((1, d), lambda i, j: (0, 0)),
            pl.BlockSpec((d, tn), lambda i, j: (0, j)),
        ],
        out_specs=pl.BlockSpec((tm, tn), lambda i, j: (i, j)),
        scratch_shapes=[pltpu.VMEM((tm, d), BF16)],
        compiler_params=_cparams(("arbitrary", "arbitrary")),
        name="inproj",
    )(x2, g, w_bf16)


def _qkprep_kernel(p_ref, c_ref, s1_ref, s2_ref, g_ref, o_ref):
    lane = lax.broadcasted_iota(jnp.int32, (1, LANES), 1)
    lo = lane < QK_DIM
    c = c_ref[...]
    s1 = s1_ref[...]
    s2 = s2_ref[...]
    for j in range(p_ref.shape[1] // LANES):
        t = p_ref[:, j * LANES:(j + 1) * LANES].astype(F32)
        sq = t * t
        ss_lo = jnp.sum(jnp.where(lo, sq, 0.0), axis=-1, keepdims=True)
        ss_hi = jnp.sum(jnp.where(lo, 0.0, sq), axis=-1, keepdims=True)
        r = lax.rsqrt(jnp.where(lo, ss_lo, ss_hi) * (1.0 / QK_DIM) + EPS)
        y = t * r * g_ref[:, j * LANES:(j + 1) * LANES]
        y = y * c + pltpu.roll(y, LANES - ROPE_DIM // 2, 1) * s1 + pltpu.roll(y, ROPE_DIM // 2, 1) * s2
        o_ref[:, j * LANES:(j + 1) * LANES] = y.astype(o_ref.dtype)


def _qkprep(proj, ctab, s1tab, s2tab, gqk, seq, tm=512):
    n = proj.shape[0]
    w = gqk.shape[1]
    nb = seq // tm
    return pl.pallas_call(
        _qkprep_kernel,
        out_shape=jax.ShapeDtypeStruct((n, w), BF16),
        grid=(n // tm,),
        in_specs=[
            pl.BlockSpec((tm, w), lambda i: (i, 0)),
            pl.BlockSpec((tm, LANES), lambda i: (i % nb, 0)),
            pl.BlockSpec((tm, LANES), lambda i: (i % nb, 0)),
            pl.BlockSpec((tm, LANES), lambda i: (i % nb, 0)),
            pl.BlockSpec((1, w), lambda i: (0, 0)),
        ],
        out_specs=pl.BlockSpec((tm, w), lambda i: (i, 0)),
        compiler_params=_cparams(("arbitrary",)),
        name="qkprep",
    )(proj, ctab, s1tab, s2tab, gqk)


def _attn_kernel(q_ref, k_ref, v_ref, lam_ref, g_ref, o_ref, *, tk, lambda_init):
    tq = q_ref.shape[0]
    seq = k_ref.shape[0]
    lane = lax.broadcasted_iota(jnp.int32, (1, LANES), 1)
    q = q_ref[...]
    zero = jnp.zeros_like(q)
    q1 = jnp.where(lane < QK_DIM, q, zero)
    q2 = jnp.where(lane < QK_DIM, zero, q)
    lv = lam_ref[...]
    lam = (jnp.exp(jnp.sum(lv[0:1] * lv[1:2], axis=-1, keepdims=True))
           - jnp.exp(jnp.sum(lv[2:3] * lv[3:4], axis=-1, keepdims=True)) + lambda_init)

    def step(c, carry):
        m1, l1, a1, m2, l2, a2 = carry
        off = pl.multiple_of(c * tk, tk)
        kc = k_ref[pl.ds(off, tk), :]
        vc = v_ref[pl.ds(off, tk), :]

        def one(qm, m, l, a):
            s = lax.dot_general(qm, kc, (((1,), (1,)), ((), ())), preferred_element_type=F32)
            mn = jnp.maximum(m, jnp.max(s, axis=-1, keepdims=True))
            p = jnp.exp(s - mn)
            al = jnp.exp(m - mn)
            l = al * l + jnp.sum(p, axis=-1, keepdims=True)
            a = al * a + jnp.dot(p.astype(BF16), vc, preferred_element_type=F32)
            return mn, l, a

        m1, l1, a1 = one(q1, m1, l1, a1)
        m2, l2, a2 = one(q2, m2, l2, a2)
        return m1, l1, a1, m2, l2, a2

    neg = jnp.full((tq, 1), -jnp.inf, F32)
    z1 = jnp.zeros((tq, 1), F32)
    za = jnp.zeros((tq, V_DIM), F32)
    m1, l1, a1, m2, l2, a2 = lax.fori_loop(0, seq // tk, step, (neg, z1, za, neg, z1, za))
    o = a1 / l1 - lam * (a2 / l2)
    ms = jnp.mean(o * o, axis=-1, keepdims=True)
    o = o * lax.rsqrt(ms + EPS) * g_ref[...] * (1.0 - lambda_init)
    o_ref[...] = o.astype(o_ref.dtype)


def _attention(qk, proj, lamv, subln_g, batch, seq, lambda_init, tq=256, tk=512):
    nq = seq // tq
    vcol0 = (2 * ATTN_HEADS * 2 * QK_DIM) // V_DIM
    return pl.pallas_call(
        functools.partial(_attn_kernel, tk=tk, lambda_init=lambda_init),
        out_shape=jax.ShapeDtypeStruct((batch * seq, ATTN_HEADS * V_DIM), BF16),
        grid=(batch, ATTN_HEADS, nq),
        in_specs=[
            pl.BlockSpec((tq, LANES), lambda b, h, i: (b * nq + i, h)),
            pl.BlockSpec((seq, LANES), lambda b, h, i: (b, ATTN_HEADS + h)),
            pl.BlockSpec((seq, V_DIM), lambda b, h, i: (b, vcol0 + h)),
            pl.BlockSpec((4, QK_DIM), lambda b, h, i: (0, 0)),
            pl.BlockSpec((1, V_DIM), lambda b, h, i: (0, 0)),
        ],
        out_specs=pl.BlockSpec((tq, V_DIM), lambda b, h, i: (b * nq + i, h)),
        compiler_params=_cparams(("arbitrary", "arbitrary", "arbitrary")),
        name="diffattn",
    )(qk, qk, proj, lamv, subln_g)


def _merge_kernel(at_ref, hy_ref, ga_ref, gh_ref, x_ref, wpa_ref, wph_ref, wo_ref, g2_ref, wr_ref,
                  x1_ref, u2_ref, aff_ref):
    ga = jax.nn.sigmoid(ga_ref[...].astype(F32))
    gh = jax.nn.sigmoid(gh_ref[...].astype(F32))
    merged = (ga * jnp.dot(at_ref[...], wpa_ref[...], preferred_element_type=F32)
              + gh * jnp.dot(hy_ref[...], wph_ref[...], preferred_element_type=F32))
    x1 = x_ref[...] + jnp.dot(merged.astype(BF16), wo_ref[...], preferred_element_type=F32)
    x1_ref[...] = x1
    ms = jnp.mean(x1 * x1, axis=-1, keepdims=True)
    u2 = (x1 * lax.rsqrt(ms + EPS) * g2_ref[...]).astype(BF16)
    u2_ref[...] = u2
    logits = jnp.dot(u2, wr_ref[...], preferred_element_type=F32)
    lane = lax.broadcasted_iota(jnp.int32, logits.shape, 1)
    logits = jnp.where(lane < N_EXPERTS, logits, -jnp.inf)
    e = jnp.exp(logits - jnp.max(logits, axis=-1, keepdims=True))
    aff_ref[...] = e / jnp.sum(e, axis=-1, keepdims=True)


def _merge(attn, hyena, proj, x2, wpa, wph, wo, g2, wr_pad, tm=512):
    n, d = x2.shape
    gcol0 = (proj.shape[1] - 2 * d) // d
    full = lambda shape: pl.BlockSpec(shape, lambda i: (0, 0))
    return pl.pallas_call(
        _merge_kernel,
        out_shape=(jax.ShapeDtypeStruct((n, d), F32), jax.ShapeDtypeStruct((n, d), BF16),
                   jax.ShapeDtypeStruct((n, LANES), F32)),
        grid=(n // tm,),
        in_specs=[
            pl.BlockSpec((tm, attn.shape[1]), lambda i: (i, 0)),
            pl.BlockSpec((tm, hyena.shape[1]), lambda i: (i, 0)),
            pl.BlockSpec((tm, d), lambda i: (i, gcol0)),
            pl.BlockSpec((tm, d), lambda i: (i, gcol0 + 1)),
            pl.BlockSpec((tm, d), lambda i: (i, 0)),
            full(wpa.shape), full(wph.shape), full(wo.shape), full(g2.shape), full(wr_pad.shape),
        ],
        out_specs=(pl.BlockSpec((tm, d), lambda i: (i, 0)), pl.BlockSpec((tm, d), lambda i: (i, 0)),
                   pl.BlockSpec((tm, LANES), lambda i: (i, 0))),
        compiler_params=_cparams(("arbitrary",)),
        name="merge",
    )(attn, hyena, proj, proj, x2, wpa, wph, wo, g2, wr_pad)


def _excl_cumsum_rows(mask_f32, tri, blk):
    rows = mask_f32.shape[0]
    carry = jnp.zeros((1, LANES), F32)
    outs = []
    for r in range(rows // blk):
        mb = mask_f32[r * blk:(r + 1) * blk]
        outs.append(jnp.dot(tri, mb.astype(BF16), preferred_element_type=F32) + carry)
        carry = carry + jnp.sum(mb, axis=0, keepdims=True)
    return jnp.concatenate(outs, axis=0)


def _topk_kernel(aff_ref, pos_ref, post_ref, afft_ref, *, cap, blk):
    aff = aff_ref[...]

    def search(i, prefix):
        cand = prefix | jnp.left_shift(jnp.int32(1), 29 - i)
        cnt = jnp.sum((aff >= pltpu.bitcast(cand, F32)[0:1]).astype(F32), axis=0, keepdims=True)
        return jnp.where(cnt >= cap, cand, prefix)

    lo = pltpu.bitcast(lax.fori_loop(0, 30, search, jnp.zeros((8, LANES), jnp.int32)), F32)[0:1]
    thr = jnp.min(jnp.where(aff >= lo, aff, jnp.inf), axis=0, keepdims=True)
    gt = (aff > thr).astype(F32)
    eq = (aff == thr).astype(F32)
    need = cap - jnp.sum(gt, axis=0, keepdims=True)
    ri = lax.broadcasted_iota(jnp.int32, (blk, blk), 0)
    ci = lax.broadcasted_iota(jnp.int32, (blk, blk), 1)
    tri = (ci < ri).astype(BF16)
    sel = gt + eq * (_excl_cumsum_rows(eq, tri, blk) < need).astype(F32)
    pos = jnp.where(sel > 0.0, _excl_cumsum_rows(sel, tri, blk), -1.0)
    pos_ref[...] = pos
    post_ref[0] = pos.T[:N_EXPERTS]
    afft_ref[0] = aff.T[:N_EXPERTS]


def _topk(aff, batch, seq, cap, blk=256):
    return pl.pallas_call(
        functools.partial(_topk_kernel, cap=cap, blk=blk),
        out_shape=(jax.ShapeDtypeStruct((batch * seq, LANES), F32),
                   jax.ShapeDtypeStruct((batch, N_EXPERTS, seq), F32),
                   jax.ShapeDtypeStruct((batch, N_EXPERTS, seq), F32)),
        grid=(batch,),
        in_specs=[pl.BlockSpec((seq, LANES), lambda b: (b, 0))],
        out_specs=(pl.BlockSpec((seq, LANES), lambda b: (b, 0)),
                   pl.BlockSpec((1, N_EXPERTS, seq), lambda b: (b, 0, 0)),
                   pl.BlockSpec((1, N_EXPERTS, seq), lambda b: (b, 0, 0))),
        compiler_params=_cparams(("arbitrary",)),
        name="topk",
    )(aff)


def _expert_kernel(post_ref, afft_ref, u_ref, wg_ref, wu_ref, wd_ref, o_ref, *, cap, tc):
    e = pl.program_id(0)
    seq = u_ref.shape[0]
    slot = lax.broadcasted_iota(jnp.int32, (cap, tc), 0).astype(F32)

    def gather(c, carry):
        xin, gs = carry
        off = pl.multiple_of(c * tc, tc)
        prow = post_ref[0, pl.ds(e, 1), pl.ds(off, tc)]
        arow = afft_ref[0, pl.ds(e, 1), pl.ds(off, tc)]
        hit = slot == prow
        gs = gs + jnp.sum(jnp.where(hit, arow, 0.0), axis=-1, keepdims=True)
        xin = xin + jnp.dot(jnp.where(hit, 1.0, 0.0).astype(BF16), u_ref[pl.ds(off, tc), :],
                            preferred_element_type=F32)
        return xin, gs

    xin, gs = lax.fori_loop(0, seq // tc, gather,
                            (jnp.zeros((cap, u_ref.shape[1]), F32), jnp.zeros((cap, 1), F32)))
    xb = xin.astype(BF16)
    hg = jnp.dot(xb, wg_ref[0], preferred_element_type=F32)
    hu = jnp.dot(xb, wu_ref[0], preferred_element_type=F32)
    act = (hg * jax.nn.sigmoid(hg) * hu).astype(BF16)
    eo = jnp.dot(act, wd_ref[0], preferred_element_type=F32)
    o_ref[0, 0] = (eo * gs).astype(o_ref.dtype)


def _experts(post, afft, u2, wg, wu, wd, batch, seq, cap, tc=512):
    d = u2.shape[1]
    hid = wg.shape[2]
    return pl.pallas_call(
        functools.partial(_expert_kernel, cap=cap, tc=tc),
        out_shape=jax.ShapeDtypeStruct((batch, N_EXPERTS, cap, d), BF16),
        grid=(N_EXPERTS, batch),
        in_specs=[
            pl.BlockSpec((1, N_EXPERTS, seq), lambda e, b: (b, 0, 0)),
            pl.BlockSpec((1, N_EXPERTS, seq), lambda e, b: (b, 0, 0)),
            pl.BlockSpec((seq, d), lambda e, b: (b, 0)),
            pl.BlockSpec((1, d, hid), lambda e, b: (e, 0, 0)),
            pl.BlockSpec((1, d, hid), lambda e, b: (e, 0, 0)),
            pl.BlockSpec((1, hid, d), lambda e, b: (e, 0, 0)),
        ],
        out_specs=pl.BlockSpec((1, 1, cap, d), lambda e, b: (b, e, 0, 0)),
        compiler_params=_cparams(("arbitrary", "arbitrary")),
        name="experts",
    )(post, afft, u2, wg, wu, wd)


def _combine_kernel(pos_ref, x1_ref, weo_ref, o_ref, *, cap):
    tt = pos_ref.shape[0]
    pos = pos_ref[...]
    slot = lax.broadcasted_iota(jnp.int32, (tt, cap), 1).astype(F32)
    acc = x1_ref[...]
    for e in range(N_EXPERTS):
        hit = slot == pos[:, e:e + 1]
        acc = acc + jnp.dot(jnp.where(hit, 1.0, 0.0).astype(BF16), weo_ref[0, e], preferred_element_type=F32)
    o_ref[...] = acc


def _combine(pos, x1, weo, batch, seq, cap, tt=512):
    d = x1.shape[1]
    nt = seq // tt
    return pl.pallas_call(
        functools.partial(_combine_kernel, cap=cap),
        out_shape=jax.ShapeDtypeStruct(x1.shape, F32),
        grid=(batch, nt),
        in_specs=[
            pl.BlockSpec((tt, LANES), lambda b, t: (b * nt + t, 0)),
            pl.BlockSpec((tt, d), lambda b, t: (b * nt + t, 0)),
            pl.BlockSpec((1, N_EXPERTS, cap, d), lambda b, t: (b, 0, 0, 0)),
        ],
        out_specs=pl.BlockSpec((tt, d), lambda b, t: (b * nt + t, 0)),
        compiler_params=_cparams(("arbitrary", "arbitrary")),
        name="combine",
    )(pos, x1, weo)


FFT_R = 64
FFT_PITCH = FFT_R + 8


def _dft_tables(seq):
    r = FFT_R
    assert seq == r * r
    n = np.arange(r)
    f = np.exp(-2j * np.pi * np.outer(n, n) / r)
    w = lambda e: np.exp(-2j * np.pi * e / (2 * seq))

    def real_rep(a):
        return np.block([[a.real, -a.imag], [a.imag, a.real]]).astype(np.float32)

    m1 = [f, f * w(r * n)[None, :]]
    m2 = [f * w(n * m)[None, :] for m in range(2 * r)]
    m4 = [f.conj() / (2 * seq), w(-r * n)[:, None] * f.conj() / (2 * seq)]
    t1 = np.stack([real_rep(a) for a in m1])
    t2 = np.stack([real_rep(a) for a in m2])
    t3 = np.stack([real_rep(a.conj().T) for a in m2])
    t4 = np.stack([real_rep(a) for a in m4])
    t1r = np.stack([np.concatenate([a.real, a.imag], axis=0).astype(np.float32) for a in m1])
    return tuple(jnp.asarray(t).astype(BF16) for t in (t1, t2, t3, t4, t1r))


def _slab(r):
    return pl.ds(pl.multiple_of(r * FFT_PITCH, 8), FFT_R)


def _across(r):
    return pl.ds(r, FFT_R, stride=FFT_PITCH)


def _cat_bf16(a, b):
    return jnp.concatenate([a, b], axis=0).astype(BF16)


def _loop(body, unroll=4):
    def wrapped(r, c):
        body(r)
        return c
    lax.fori_loop(0, FFT_R, wrapped, 0, unroll=unroll)


def _filter_kernel(bands_ref, w1t_ref, w1c_ref, w1s_ref, b1_ref, w2_ref, b2_ref, w3_ref, b3_ref, fr_ref,
                   wf_ref, wb_ref, dl_ref, hs_ref, hd_ref, *, seq):
    tm = hs_ref.shape[0]
    hi = lax.Precision.HIGHEST
    pos = (pl.program_id(0) * tm + lax.broadcasted_iota(jnp.int32, (tm, 1), 0)).astype(F32)
    fr = fr_ref[...]

    def mlp(p):
        t = p / max(seq - 1, 1)
        ang = ((2.0 * math.pi / seq) * p) * bands_ref[...]
        pre = (t * w1t_ref[...] + jnp.dot(jnp.cos(ang), w1c_ref[...], precision=hi, preferred_element_type=F32)
               + jnp.dot(-jnp.sin(ang), w1s_ref[...], precision=hi, preferred_element_type=F32) + b1_ref[...])
        h = jnp.sin(fr * pre)
        h = jnp.sin(fr * (jnp.dot(h, w2_ref[...], precision=hi, preferred_element_type=F32) + b2_ref[...]))
        h = jnp.sin(fr * (jnp.dot(h, w3_ref[...], precision=hi, preferred_element_type=F32) + b3_ref[...]))
        return h, t

    hf_h, tf = mlp(pos)
    hb_h, tb = mlp(seq - pos)
    dl = dl_ref[...]
    hf = jnp.dot(hf_h, wf_ref[...], precision=hi, preferred_element_type=F32) * jnp.exp(-tf * dl)
    hb = jnp.dot(hb_h, wb_ref[...], precision=hi, preferred_element_type=F32) * jnp.exp(-tb * dl)
    hb = jnp.where(pos > 0.0, hb, 0.0)
    hs_ref[...] = hf + hb
    hd_ref[...] = hf - hb


def _filters(bands, w1t, w1c, w1s, b1, w2, b2, w3, b3, fr, wf, wb, dl, seq, tm=512):
    cols = wf.shape[1]
    args = (bands, w1t, w1c, w1s, b1, w2, b2, w3, b3, fr, wf, wb, dl)
    return pl.pallas_call(
        functools.partial(_filter_kernel, seq=seq),
        out_shape=(jax.ShapeDtypeStruct((seq, cols), F32), jax.ShapeDtypeStruct((seq, cols), F32)),
        grid=(seq // tm,),
        in_specs=[pl.BlockSpec(a.shape, lambda i: (0, 0)) for a in args],
        out_specs=(pl.BlockSpec((tm, cols), lambda i: (i, 0)), pl.BlockSpec((tm, cols), lambda i: (i, 0))),
        compiler_params=_cparams(("arbitrary",)),
        name="hyfilter",
    )(*args)


def _spectra_kernel(hs_ref, hd_ref, t1r_ref, t2_ref, h_ref, x_ref, pr_ref, pi_ref):
    for par, src in ((0, hs_ref), (1, hd_ref)):
        def fill(r, src=src):
            x_ref[_slab(r), :] = src[pl.ds(pl.multiple_of(r * FFT_R, FFT_R), FFT_R), :]
        _loop(fill)

        def first(r, par=par):
            o = jnp.dot(t1r_ref[par], x_ref[_across(r), :].astype(BF16), preferred_element_type=F32)
            pr_ref[_slab(r), :] = o[:FFT_R]
            pi_ref[_slab(r), :] = o[FFT_R:]
        _loop(first)

        def second(r, par=par):
            o = jnp.dot(t2_ref[2 * r + par], _cat_bf16(pr_ref[_across(r), :], pi_ref[_across(r), :]),
                        preferred_element_type=F32)
            rows = pl.ds(pl.multiple_of(r * FFT_R, FFT_R), FFT_R)
            h_ref[0, 2 * par, rows, :] = o[:FFT_R].astype(h_ref.dtype)
            h_ref[0, 2 * par + 1, rows, :] = o[FFT_R:].astype(h_ref.dtype)
        _loop(second)


def _spectra(hs, hd, t1r, t2, seq):
    ncol = hs.shape[1] // LANES
    nct = ncol // HYENA_ORDER
    prow = FFT_R * FFT_PITCH
    const = lambda a: pl.BlockSpec(a.shape, lambda o, c: (0,) * a.ndim, pipeline_mode=pl.Buffered(1))
    return pl.pallas_call(
        _spectra_kernel,
        out_shape=jax.ShapeDtypeStruct((HYENA_ORDER, 4, seq, nct * LANES), BF16),
        grid=(HYENA_ORDER, nct),
        in_specs=[pl.BlockSpec((seq, LANES), lambda o, c: (0, o * nct + c)),
                  pl.BlockSpec((seq, LANES), lambda o, c: (0, o * nct + c)),
                  const(t1r), const(t2)],
        out_specs=pl.BlockSpec((1, 4, seq, LANES), lambda o, c: (o, 0, 0, c)),
        scratch_shapes=[pltpu.VMEM((prow, LANES), F32)] * 3,
        compiler_params=_cparams(("arbitrary", "arbitrary")),
        name="hyspectra",
    )(hs, hd, t1r, t2)


def _short_conv(x, w_ref, b_ref):
    rows = x.shape[0]
    ri = lax.broadcasted_iota(jnp.int32, x.shape, 0)
    prev = jnp.where(ri == 0, 0.0, pltpu.roll(x, 1, 0))
    nxt = jnp.where(ri == rows - 1, 0.0, pltpu.roll(x, rows - 1, 0))
    return prev * w_ref[0:1, :] + x * w_ref[1:2, :] + nxt * w_ref[2:3, :] + b_ref[...]


def _hyconv_kernel(z_ref, g_ref, cwz_ref, cbz_ref, cwg_ref, cbg_ref, skip_ref, h_ref,
                   t1_ref, t2_ref, t3_ref, t4_ref, o_ref,
                   xr_ref, xi_ref, pr_ref, pi_ref, qr_ref, qi_ref, yr_ref, yi_ref, *, conv_z):
    seq = z_ref.shape[0] // 2
    for half, dst in ((0, xr_ref), (1, xi_ref)):
        z = z_ref[half * seq:(half + 1) * seq, :].astype(F32)
        if conv_z:
            z = _short_conv(z, cwz_ref, cbz_ref)
        for j in range(FFT_R):
            dst[j * FFT_PITCH:j * FFT_PITCH + FFT_R, :] = z[j * FFT_R:(j + 1) * FFT_R]

    for par in (0, 1):
        def first(r, par=par):
            o = jnp.dot(t1_ref[par], _cat_bf16(xr_ref[_across(r), :], xi_ref[_across(r), :]),
                        preferred_element_type=F32)
            pr_ref[_slab(r), :] = o[:FFT_R]
            pi_ref[_slab(r), :] = o[FFT_R:]
        _loop(first)

        def second(r, par=par):
            o = jnp.dot(t2_ref[2 * r + par], _cat_bf16(pr_ref[_across(r), :], pi_ref[_across(r), :]),
                        preferred_element_type=F32)
            rows = pl.ds(pl.multiple_of(r * FFT_R, FFT_R), FFT_R)
            hr = h_ref[0, 2 * par, rows, :].astype(F32)
            hi = h_ref[0, 2 * par + 1, rows, :].astype(F32)
            ar, ai = o[:FFT_R], o[FFT_R:]
            qr_ref[_slab(r), :] = ar * hr - ai * hi
            qi_ref[_slab(r), :] = ar * hi + ai * hr
        _loop(second)

        def third(r, par=par):
            o = jnp.dot(t3_ref[2 * r + par], _cat_bf16(qr_ref[_slab(r), :], qi_ref[_slab(r), :]),
                        preferred_element_type=F32)
            pr_ref[_across(r), :] = o[:FFT_R]
            pi_ref[_across(r), :] = o[FFT_R:]
        _loop(third)

        def fourth(r, par=par):
            o = jnp.dot(t4_ref[par], _cat_bf16(pr_ref[_slab(r), :], pi_ref[_slab(r), :]),
                        preferred_element_type=F32)
            if par == 0:
                yr_ref[_across(r), :] = o[:FFT_R]
                yi_ref[_across(r), :] = o[FFT_R:]
            else:
                yr_ref[_across(r), :] = yr_ref[_across(r), :] + o[:FFT_R]
                yi_ref[_across(r), :] = yi_ref[_across(r), :] + o[FFT_R:]
        _loop(fourth)

    skip = skip_ref[...]
    for half, (y_ref, x_ref) in enumerate(((yr_ref, xr_ref), (yi_ref, xi_ref))):
        g = _short_conv(g_ref[half * seq:(half + 1) * seq, :].astype(F32), cwg_ref, cbg_ref)
        for j in range(FFT_R):
            src = slice(j * FFT_PITCH, j * FFT_PITCH + FFT_R)
            dst = slice(half * seq + j * FFT_R, half * seq + (j + 1) * FFT_R)
            o_ref[dst, :] = (g[j * FFT_R:(j + 1) * FFT_R] * (y_ref[src, :] + skip * x_ref[src, :])).astype(o_ref.dtype)


def _hyconv(zsrc, zcol0, gcol0, proj, conv_w, conv_b, zpart, gpart, skip, spec, order, tabs, batch, seq, conv_z):
    t1, t2, t3, t4 = tabs
    nct = HYENA_WIDTH // LANES
    prow = FFT_R * FFT_PITCH
    const = lambda a: pl.BlockSpec(a.shape, lambda c, p: (0,) * a.ndim, pipeline_mode=pl.Buffered(1))
    return pl.pallas_call(
        functools.partial(_hyconv_kernel, conv_z=conv_z),
        out_shape=jax.ShapeDtypeStruct((batch * seq, HYENA_WIDTH), BF16),
        grid=(nct, batch // 2),
        in_specs=[
            pl.BlockSpec((2 * seq, LANES), lambda c, p: (p, zcol0 + c)),
            pl.BlockSpec((2 * seq, LANES), lambda c, p: (p, gcol0 + c)),
            pl.BlockSpec((3, LANES), lambda c, p: (0, zpart * nct + c)),
            pl.BlockSpec((1, LANES), lambda c, p: (0, zpart * nct + c)),
            pl.BlockSpec((3, LANES), lambda c, p: (0, gpart * nct + c)),
            pl.BlockSpec((1, LANES), lambda c, p: (0, gpart * nct + c)),
            pl.BlockSpec((1, LANES), lambda c, p: (0, c)),
            pl.BlockSpec((1, 4, seq, LANES), lambda c, p: (order, 0, 0, c), pipeline_mode=pl.Buffered(1)),
            const(t1), const(t2), const(t3), const(t4),
        ],
        out_specs=pl.BlockSpec((2 * seq, LANES), lambda c, p: (p, c)),
        scratch_shapes=[pltpu.VMEM((prow, LANES), F32)] * 8,
        compiler_params=_cparams(("arbitrary", "arbitrary")),
        name=f"hyconv{order}",
    )(zsrc, proj, conv_w, conv_b, conv_w, conv_b, skip[order:order + 1], spec, t1, t2, t3, t4)


def _hyena(proj, hycol0, conv_w, conv_b, w1, b1, w2, b2, w3, b3, w_out, freq, skip, batch, seq):
    hid = w2.shape[0]
    padc = lambda a: jnp.pad(a, ((0, 0), (0, LANES - a.shape[1])))
    padr = lambda a: jnp.pad(a, ((0, LANES - a.shape[0]), (0, 0)))
    bands = padc(jnp.linspace(1e-4, FILTER_BANDS - 1, FILTER_BANDS, dtype=F32)[None])
    min_decay = math.log(DECAY_TARGET) / SLOW_DECAY_PCT
    max_decay = math.log(DECAY_TARGET) / FAST_DECAY_PCT
    deltas = jnp.abs(jnp.linspace(min_decay, max_decay, HYENA_WIDTH, dtype=F32))
    wo4 = w_out.reshape(hid, HYENA_ORDER, 2, HYENA_WIDTH)
    hs, hd = _filters(
        bands, padc(w1[0:1]), padr(padc(w1[1:1 + FILTER_BANDS])), padr(padc(w1[1 + FILTER_BANDS:])),
        padc(b1[None]), padr(padc(w2)), padc(b2[None]), padr(padc(w3)), padc(b3[None]), padc(freq[None]),
        padr(wo4[:, :, 0].reshape(hid, -1)), padr(wo4[:, :, 1].reshape(hid, -1)),
        jnp.tile(deltas, HYENA_ORDER)[None], seq)
    t1, t2, t3, t4, t1r = _dft_tables(seq)
    spec = _spectra(hs, hd, t1r, t2, seq)
    c0 = hycol0 // LANES
    nct = HYENA_WIDTH // LANES
    cb = conv_b[None]
    z1 = _hyconv(proj, c0, c0 + nct, proj, conv_w, cb, 0, 1, skip, spec, 0, (t1, t2, t3, t4), batch, seq, True)
    return _hyconv(z1, 0, c0 + 2 * nct, proj, conv_w, cb, 0, 2, skip, spec, 1, (t1, t2, t3, t4), batch, seq, False)


def _rope_tables(seq):
    pos = np.arange(seq, dtype=np.float32)
    inv_freq = (ROPE_THETA ** (-np.arange(0, ROPE_DIM, 2, dtype=np.float32) / ROPE_DIM)).astype(np.float32)
    ang = pos[:, None] * inv_freq[None, :]
    cos, sin = np.cos(ang).astype(np.float32), np.sin(ang).astype(np.float32)
    half = ROPE_DIM // 2
    c = np.ones((seq, QK_DIM), np.float32)
    s1 = np.zeros((seq, QK_DIM), np.float32)
    s2 = np.zeros((seq, QK_DIM), np.float32)
    c[:, :half] = cos
    c[:, half:ROPE_DIM] = cos
    s1[:, :half] = -sin
    s2[:, half:ROPE_DIM] = sin
    rep = LANES // QK_DIM
    return tuple(jnp.asarray(np.tile(a, (1, rep))) for a in (c, s1, s2))


def kernel(x, norm1_g, w_in, short_conv_w, short_conv_b, q_norm_g, k_norm_g, lambda_q1, lambda_k1, lambda_q2, lambda_k2, subln_g, filt_w1, filt_b1, filt_w2, filt_b2, filt_w3, filt_b3, filt_w_out, filt_freq, hyena_skip, w_branch_attn, w_branch_hyena, w_out, norm2_g, w_router, w_gate, w_up, w_down):
    b, l, d = x.shape
    depth = w_in.shape[0]
    n = b * l
    cap = CAPACITY_FACTOR * l // N_EXPERTS
    q_cols = ATTN_HEADS * 2 * QK_DIM
    ctab, s1tab, s2tab = _rope_tables(l)
    xc = x.reshape(n, d)
    for li in range(depth):
        lambda_init = 0.8 - 0.6 * math.exp(-0.3 * li)
        proj = _inproj(xc, norm1_g[li][None], w_in[li].astype(BF16))
        scale = 1.0 / math.sqrt(QK_DIM)
        gqk = jnp.concatenate([jnp.tile(q_norm_g[li], q_cols // QK_DIM) * scale,
                               jnp.tile(k_norm_g[li], q_cols // QK_DIM)])[None]
        qk = _qkprep(proj, ctab, s1tab, s2tab, gqk, l)
        lamv = jnp.stack([lambda_q1[li], lambda_k1[li], lambda_q2[li], lambda_k2[li]])
        attn = _attention(qk, proj, lamv, subln_g[li][None], b, l, lambda_init)

        hyena = _hyena(proj, 3 * q_cols, short_conv_w[li], short_conv_b[li], filt_w1[li], filt_b1[li],
                       filt_w2[li], filt_b2[li], filt_w3[li], filt_b3[li], filt_w_out[li], filt_freq[li],
                       hyena_skip[li], b, l)

        wr_pad = jnp.pad(w_router[li], ((0, 0), (0, LANES - N_EXPERTS))).astype(BF16)
        x1, u2, aff = _merge(attn, hyena, proj, xc, w_branch_attn[li].astype(BF16),
                             w_branch_hyena[li].astype(BF16), w_out[li].astype(BF16),
                             norm2_g[li][None], wr_pad)
        pos, post, afft = _topk(aff, b, l, cap)
        weo = _experts(post, afft, u2, w_gate[li].astype(BF16), w_up[li].astype(BF16),
                       w_down[li].astype(BF16), b, l, cap)
        xc = _combine(pos, x1, weo, b, l, cap)
    return xc.reshape(b, l, d)
```

```python
import functools
import math

import jax
import jax.numpy as jnp
import numpy as np
from jax import lax
from jax.experimental import pallas as pl
from jax.experimental.pallas import tpu as pltpu

F32 = jnp.float32
BF16 = jnp.bfloat16

ATTN_HEADS = 4
QK_DIM = 64
V_DIM = 128
ROPE_DIM = 16
ROPE_THETA = 500000.0
HYENA_WIDTH = 512
HYENA_ORDER = 2
FILTER_BANDS = 16
DECAY_TARGET = 1e-2
FAST_DECAY_PCT = 0.3
SLOW_DECAY_PCT = 1.5
N_EXPERTS = 16
CAPACITY_FACTOR = 2
EPS = 1e-6
LANES = 128
VMEM_LIMIT = 56 * 1024 * 1024


def _cparams(sem):
    return pltpu.CompilerParams(dimension_semantics=sem, vmem_limit_bytes=VMEM_LIMIT)


def _inproj_kernel(x_ref, g_ref, w_ref, o_ref, u_ref):
    @pl.when(pl.program_id(1) == 0)
    def _():
        x = x_ref[...]
        ms = jnp.mean(x * x, axis=-1, keepdims=True)
        u_ref[...] = (x * lax.rsqrt(ms + EPS) * g_ref[...]).astype(BF16)

    o_ref[...] = jnp.dot(u_ref[...], w_ref[...], preferred_element_type=F32).astype(o_ref.dtype)


def _inproj(x2, g, w_bf16, tm=1024, tn=1280):
    n, d = x2.shape
    cols = w_bf16.shape[1]
    return pl.pallas_call(
        _inproj_kernel,
        out_shape=jax.ShapeDtypeStruct((n, cols), BF16),
        grid=(n // tm, cols // tn),
        in_specs=[
            pl.BlockSpec((tm, d), lambda i, j: (i, 0)),
            pl.BlockSpec((1, d), lambda i, j: (0, 0)),
            pl.BlockSpec((d, tn), lambda i, j: (0, j)),
        ],
        out_specs=pl.BlockSpec((tm, tn), lambda i, j: (i, j)),
        scratch_shapes=[pltpu.VMEM((tm, d), BF16)],
        compiler_params=_cparams(("arbitrary", "arbitrary")),
        name="inproj",
    )(x2, g, w_bf16)


def _qkprep_kernel(p_ref, c_ref, s1_ref, s2_ref, g_ref, o_ref):
    lane = lax.broadcasted_iota(jnp.int32, (1, LANES), 1)
    lo = lane < QK_DIM
    c = c_ref[...]
    s1 = s1_ref[...]
    s2 = s2_ref[...]
    for j in range(p_ref.shape[1] // LANES):
        t = p_ref[:, j * LANES:(j + 1) * LANES].astype(F32)
        sq = t * t
        ss_lo = jnp.sum(jnp.where(lo, sq, 0.0), axis=-1, keepdims=True)
        ss_hi = jnp.sum(jnp.where(lo, 0.0, sq), axis=-1, keepdims=True)
        r = lax.rsqrt(jnp.where(lo, ss_lo, ss_hi) * (1.0 / QK_DIM) + EPS)
        y = t * r * g_ref[:, j * LANES:(j + 1) * LANES]
        y = y * c + pltpu.roll(y, LANES - ROPE_DIM // 2, 1) * s1 + pltpu.roll(y, ROPE_DIM // 2, 1) * s2
        o_ref[:, j * LANES:(j + 1) * LANES] = y.astype(o_ref.dtype)


def _qkprep(proj, ctab, s1tab, s2tab, gqk, seq, tm=512):
    n = proj.shape[0]
    w = gqk.shape[1]
    nb = seq // tm
    return pl.pallas_call(
        _qkprep_kernel,
        out_shape=jax.ShapeDtypeStruct((n, w), BF16),
        grid=(n // tm,),
        in_specs=[
            pl.BlockSpec((tm, w), lambda i: (i, 0)),
            pl.BlockSpec((tm, LANES), lambda i: (i % nb, 0)),
            pl.BlockSpec((tm, LANES), lambda i: (i % nb, 0)),
            pl.BlockSpec((tm, LANES), lambda i: (i % nb, 0)),
            pl.BlockSpec((1, w), lambda i: (0, 0)),
        ],
        out_specs=pl.BlockSpec((tm, w), lambda i: (i, 0)),
        compiler_params=_cparams(("arbitrary",)),
        name="qkprep",
    )(proj, ctab, s1tab, s2tab, gqk)


def _attn_kernel(q_ref, k_ref, v_ref, lam_ref, g_ref, o_ref, *, tk, unroll, lambda_init):
    tq = q_ref.shape[0]
    seq = k_ref.shape[0]
    lane = lax.broadcasted_iota(jnp.int32, (1, LANES), 1)
    q = q_ref[...]
    zero = jnp.zeros_like(q)
    qs = jnp.concatenate([jnp.where(lane < QK_DIM, q, zero), jnp.where(lane < QK_DIM, zero, q)], axis=0)
    lv = lam_ref[...]
    lam = (jnp.exp(jnp.sum(lv[0:1] * lv[1:2], axis=-1, keepdims=True))
           - jnp.exp(jnp.sum(lv[2:3] * lv[3:4], axis=-1, keepdims=True)) + lambda_init)

    def step(c, carry):
        m, l, a = carry
        off = pl.multiple_of(c * tk, tk)
        kc = k_ref[pl.ds(off, tk), :]
        vc = v_ref[pl.ds(off, tk), :]
        s = lax.dot_general(qs, kc, (((1,), (1,)), ((), ())), preferred_element_type=F32)
        mn = jnp.maximum(m, jnp.max(s, axis=-1, keepdims=True))
        p = jnp.exp2(s - mn)
        al = jnp.exp2(m - mn)
        l = al * l + jnp.sum(p, axis=-1, keepdims=True)
        a = al * a + jnp.dot(p.astype(BF16), vc, preferred_element_type=F32)
        return mn, l, a

    init = (jnp.full((2 * tq, 1), -jnp.inf, F32), jnp.zeros((2 * tq, 1), F32), jnp.zeros((2 * tq, V_DIM), F32))
    _, l, a = lax.fori_loop(0, seq // tk, step, init, unroll=unroll)
    a = a / l
    o = a[:tq] - lam * a[tq:]
    ms = jnp.mean(o * o, axis=-1, keepdims=True)
    o = o * lax.rsqrt(ms + EPS) * g_ref[...] * (1.0 - lambda_init)
    o_ref[...] = o.astype(o_ref.dtype)


def _attention(qk, proj, lamv, subln_g, batch, seq, lambda_init, tq=512, tk=512, unroll=4):
    nq = seq // tq
    vcol0 = (2 * ATTN_HEADS * 2 * QK_DIM) // V_DIM
    return pl.pallas_call(
        functools.partial(_attn_kernel, tk=tk, unroll=unroll, lambda_init=lambda_init),
        out_shape=jax.ShapeDtypeStruct((batch * seq, ATTN_HEADS * V_DIM), BF16),
        grid=(batch, ATTN_HEADS, nq),
        in_specs=[
            pl.BlockSpec((tq, LANES), lambda b, h, i: (b * nq + i, h)),
            pl.BlockSpec((seq, LANES), lambda b, h, i: (b, ATTN_HEADS + h)),
            pl.BlockSpec((seq, V_DIM), lambda b, h, i: (b, vcol0 + h)),
            pl.BlockSpec((4, QK_DIM), lambda b, h, i: (0, 0)),
            pl.BlockSpec((1, V_DIM), lambda b, h, i: (0, 0)),
        ],
        out_specs=pl.BlockSpec((tq, V_DIM), lambda b, h, i: (b * nq + i, h)),
        compiler_params=_cparams(("arbitrary", "arbitrary", "arbitrary")),
        name="diffattn",
    )(qk, qk, proj, lamv, subln_g)


def _merge_kernel(at_ref, hy_ref, ga_ref, gh_ref, x_ref, wpa_ref, wph_ref, wo_ref, g2_ref, wr_ref,
                  x1_ref, u2_ref, aff_ref):
    ga = jax.nn.sigmoid(ga_ref[...].astype(F32))
    gh = jax.nn.sigmoid(gh_ref[...].astype(F32))
    merged = (ga * jnp.dot(at_ref[...], wpa_ref[...], preferred_element_type=F32)
              + gh * jnp.dot(hy_ref[...], wph_ref[...], preferred_element_type=F32))
    x1 = x_ref[...] + jnp.dot(merged.astype(BF16), wo_ref[...], preferred_element_type=F32)
    x1_ref[...] = x1
    ms = jnp.mean(x1 * x1, axis=-1, keepdims=True)
    u2 = (x1 * lax.rsqrt(ms + EPS) * g2_ref[...]).astype(BF16)
    u2_ref[...] = u2
    logits = jnp.dot(u2, wr_ref[...], preferred_element_type=F32)
    lane = lax.broadcasted_iota(jnp.int32, logits.shape, 1)
    logits = jnp.where(lane < N_EXPERTS, logits, -jnp.inf)
    e = jnp.exp(logits - jnp.max(logits, axis=-1, keepdims=True))
    aff_ref[...] = e / jnp.sum(e, axis=-1, keepdims=True)


def _merge(attn, hyena, proj, x2, wpa, wph, wo, g2, wr_pad, tm=512):
    n, d = x2.shape
    gcol0 = (proj.shape[1] - 2 * d) // d
    full = lambda shape: pl.BlockSpec(shape, lambda i: (0, 0))
    return pl.pallas_call(
        _merge_kernel,
        out_shape=(jax.ShapeDtypeStruct((n, d), F32), jax.ShapeDtypeStruct((n, d), BF16),
                   jax.ShapeDtypeStruct((n, LANES), F32)),
        grid=(n // tm,),
        in_specs=[
            pl.BlockSpec((tm, attn.shape[1]), lambda i: (i, 0)),
            pl.BlockSpec((tm, hyena.shape[1]), lambda i: (i, 0)),
            pl.BlockSpec((tm, d), lambda i: (i, gcol0)),
            pl.BlockSpec((tm, d), lambda i: (i, gcol0 + 1)),
            pl.BlockSpec((tm, d), lambda i: (i, 0)),
            full(wpa.shape), full(wph.shape), full(wo.shape), full(g2.shape), full(wr_pad.shape),
        ],
        out_specs=(pl.BlockSpec((tm, d), lambda i: (i, 0)), pl.BlockSpec((tm, d), lambda i: (i, 0)),
                   pl.BlockSpec((tm, LANES), lambda i: (i, 0))),
        compiler_params=_cparams(("arbitrary",)),
        name="merge",
    )(attn, hyena, proj, proj, x2, wpa, wph, wo, g2, wr_pad)


def _excl_cumsum_rows(mask_f32, tri, blk):
    rows = mask_f32.shape[0]
    carry = jnp.zeros((1, LANES), F32)
    outs = []
    for r in range(rows // blk):
        mb = mask_f32[r * blk:(r + 1) * blk]
        outs.append(jnp.dot(tri, mb.astype(BF16), preferred_element_type=F32) + carry)
        carry = carry + jnp.sum(mb, axis=0, keepdims=True)
    return jnp.concatenate(outs, axis=0)


def _topk_kernel(aff_ref, pos_ref, post_ref, afft_ref, *, cap, blk):
    aff = aff_ref[...]

    def search(i, prefix):
        cand = prefix | jnp.left_shift(jnp.int32(1), 29 - i)
        cnt = jnp.sum((aff >= pltpu.bitcast(cand, F32)[0:1]).astype(F32), axis=0, keepdims=True)
        return jnp.where(cnt >= cap, cand, prefix)

    lo = pltpu.bitcast(lax.fori_loop(0, 30, search, jnp.zeros((8, LANES), jnp.int32)), F32)[0:1]
    thr = jnp.min(jnp.where(aff >= lo, aff, jnp.inf), axis=0, keepdims=True)
    gt = (aff > thr).astype(F32)
    eq = (aff == thr).astype(F32)
    need = cap - jnp.sum(gt, axis=0, keepdims=True)
    ri = lax.broadcasted_iota(jnp.int32, (blk, blk), 0)
    ci = lax.broadcasted_iota(jnp.int32, (blk, blk), 1)
    tri = (ci < ri).astype(BF16)
    sel = gt + eq * (_excl_cumsum_rows(eq, tri, blk) < need).astype(F32)
    pos = jnp.where(sel > 0.0, _excl_cumsum_rows(sel, tri, blk), -1.0)
    pos_ref[...] = pos
    post_ref[0] = pos.T[:N_EXPERTS]
    afft_ref[0] = aff.T[:N_EXPERTS]


def _topk(aff, batch, seq, cap, blk=256):
    return pl.pallas_call(
        functools.partial(_topk_kernel, cap=cap, blk=blk),
        out_shape=(jax.ShapeDtypeStruct((batch * seq, LANES), F32),
                   jax.ShapeDtypeStruct((batch, N_EXPERTS, seq), F32),
                   jax.ShapeDtypeStruct((batch, N_EXPERTS, seq), F32)),
        grid=(batch,),
        in_specs=[pl.BlockSpec((seq, LANES), lambda b: (b, 0))],
        out_specs=(pl.BlockSpec((seq, LANES), lambda b: (b, 0)),
                   pl.BlockSpec((1, N_EXPERTS, seq), lambda b: (b, 0, 0)),
                   pl.BlockSpec((1, N_EXPERTS, seq), lambda b: (b, 0, 0))),
        compiler_params=_cparams(("arbitrary",)),
        name="topk",
    )(aff)


def _expert_kernel(post_ref, afft_ref, u_ref, wg_ref, wu_ref, wd_ref, o_ref, hot_ref, *, cap, tc):
    e = pl.program_id(0)
    seq = u_ref.shape[0]
    slot = lax.broadcasted_iota(jnp.int32, (cap, tc), 0).astype(F32)
    gs = jnp.zeros((cap, 1), F32)
    for c in range(seq // tc):
        prow = post_ref[0, pl.ds(e, 1), c * tc:(c + 1) * tc]
        arow = afft_ref[0, pl.ds(e, 1), c * tc:(c + 1) * tc]
        hit = slot == prow
        gs = gs + jnp.sum(jnp.where(hit, arow, 0.0), axis=-1, keepdims=True)
        hot_ref[:, c * tc:(c + 1) * tc] = jnp.where(hit, 1.0, 0.0).astype(BF16)
    xb = jnp.dot(hot_ref[...], u_ref[...], preferred_element_type=F32).astype(BF16)
    hg = jnp.dot(xb, wg_ref[0], preferred_element_type=F32)
    hu = jnp.dot(xb, wu_ref[0], preferred_element_type=F32)
    act = (hg * jax.nn.sigmoid(hg) * hu).astype(BF16)
    eo = jnp.dot(act, wd_ref[0], preferred_element_type=F32)
    o_ref[0, 0] = (eo * gs).astype(o_ref.dtype)


def _experts(post, afft, u2, wg, wu, wd, batch, seq, cap, tc=512):
    d = u2.shape[1]
    hid = wg.shape[2]
    return pl.pallas_call(
        functools.partial(_expert_kernel, cap=cap, tc=tc),
        out_shape=jax.ShapeDtypeStruct((batch, N_EXPERTS, cap, d), BF16),
        grid=(N_EXPERTS, batch),
        in_specs=[
            pl.BlockSpec((1, N_EXPERTS, seq), lambda e, b: (b, 0, 0)),
            pl.BlockSpec((1, N_EXPERTS, seq), lambda e, b: (b, 0, 0)),
            pl.BlockSpec((seq, d), lambda e, b: (b, 0)),
            pl.BlockSpec((1, d, hid), lambda e, b: (e, 0, 0)),
            pl.BlockSpec((1, d, hid), lambda e, b: (e, 0, 0)),
            pl.BlockSpec((1, hid, d), lambda e, b: (e, 0, 0)),
        ],
        out_specs=pl.BlockSpec((1, 1, cap, d), lambda e, b: (b, e, 0, 0)),
        scratch_shapes=[pltpu.VMEM((cap, seq), BF16)],
        compiler_params=_cparams(("arbitrary", "arbitrary")),
        name="experts",
    )(post, afft, u2, wg, wu, wd)


def _combine_kernel(pos_ref, x1_ref, weo_ref, o_ref, hot_ref, *, cap):
    tt = pos_ref.shape[0]
    d = x1_ref.shape[1]
    pos = pos_ref[...]
    slot = lax.broadcasted_iota(jnp.int32, (tt, cap), 1).astype(F32)
    for e in range(N_EXPERTS):
        hot_ref[:, e * cap:(e + 1) * cap] = jnp.where(slot == pos[:, e:e + 1], 1.0, 0.0).astype(BF16)
    weo = weo_ref[0].reshape(N_EXPERTS * cap, d)
    o_ref[...] = x1_ref[...] + jnp.dot(hot_ref[...], weo, preferred_element_type=F32)


def _combine(pos, x1, weo, batch, seq, cap, tt=512):
    d = x1.shape[1]
    nt = seq // tt
    return pl.pallas_call(
        functools.partial(_combine_kernel, cap=cap),
        out_shape=jax.ShapeDtypeStruct(x1.shape, F32),
        grid=(batch, nt),
        in_specs=[
            pl.BlockSpec((tt, LANES), lambda b, t: (b * nt + t, 0)),
            pl.BlockSpec((tt, d), lambda b, t: (b * nt + t, 0)),
            pl.BlockSpec((1, N_EXPERTS, cap, d), lambda b, t: (b, 0, 0, 0)),
        ],
        out_specs=pl.BlockSpec((tt, d), lambda b, t: (b * nt + t, 0)),
        scratch_shapes=[pltpu.VMEM((tt, N_EXPERTS * cap), BF16)],
        compiler_params=_cparams(("arbitrary", "arbitrary")),
        name="combine",
    )(pos, x1, weo)


FFT_R = 64
FFT_PITCH = FFT_R + 8


def _dft_tables(seq):
    r = FFT_R
    assert seq == r * r
    n = np.arange(r)
    f = np.exp(-2j * np.pi * np.outer(n, n) / r)
    w = lambda e: np.exp(-2j * np.pi * e / (2 * seq))

    def real_rep(a):
        return np.block([[a.real, -a.imag], [a.imag, a.real]]).astype(np.float32)

    m1 = [f, f * w(r * n)[None, :]]
    m2 = [f * w(n * m)[None, :] for m in range(2 * r)]
    m4 = [f.conj() / (2 * seq), w(-r * n)[:, None] * f.conj() / (2 * seq)]
    t1 = np.stack([real_rep(a) for a in m1])
    t2 = np.stack([real_rep(a) for a in m2])
    t3 = np.stack([real_rep(a.conj().T) for a in m2])
    t4 = np.stack([real_rep(a) for a in m4])
    t1r = np.stack([np.concatenate([a.real, a.imag], axis=0).astype(np.float32) for a in m1])
    return tuple(jnp.asarray(t).astype(BF16) for t in (t1, t2, t3, t4, t1r))


def _slab(r):
    return pl.ds(pl.multiple_of(r * FFT_PITCH, 8), FFT_R)


def _across(r):
    return pl.ds(r, FFT_R, stride=FFT_PITCH)


def _cat_bf16(a, b):
    return jnp.concatenate([a, b], axis=0).astype(BF16)


def _loop(body, unroll=16):
    def wrapped(r, c):
        body(r)
        return c
    lax.fori_loop(0, FFT_R, wrapped, 0, unroll=unroll)


def _filter_kernel(bands_ref, w1t_ref, w1c_ref, w1s_ref, b1_ref, w2_ref, b2_ref, w3_ref, b3_ref, fr_ref,
                   wf_ref, wb_ref, dl_ref, hs_ref, hd_ref, *, seq):
    tm = hs_ref.shape[0]
    hi = lax.Precision.HIGHEST
    pos = (pl.program_id(0) * tm + lax.broadcasted_iota(jnp.int32, (tm, 1), 0)).astype(F32)
    fr = fr_ref[...]

    def mlp(p):
        t = p / max(seq - 1, 1)
        ang = ((2.0 * math.pi / seq) * p) * bands_ref[...]
        pre = (t * w1t_ref[...] + jnp.dot(jnp.cos(ang), w1c_ref[...], precision=hi, preferred_element_type=F32)
               + jnp.dot(-jnp.sin(ang), w1s_ref[...], precision=hi, preferred_element_type=F32) + b1_ref[...])
        h = jnp.sin(fr * pre)
        h = jnp.sin(fr * (jnp.dot(h, w2_ref[...], precision=hi, preferred_element_type=F32) + b2_ref[...]))
        h = jnp.sin(fr * (jnp.dot(h, w3_ref[...], precision=hi, preferred_element_type=F32) + b3_ref[...]))
        return h, t

    hf_h, tf = mlp(pos)
    hb_h, tb = mlp(seq - pos)
    dl = dl_ref[...]
    hf = jnp.dot(hf_h, wf_ref[...], precision=hi, preferred_element_type=F32) * jnp.exp(-tf * dl)
    hb = jnp.dot(hb_h, wb_ref[...], precision=hi, preferred_element_type=F32) * jnp.exp(-tb * dl)
    hb = jnp.where(pos > 0.0, hb, 0.0)
    hs_ref[...] = hf + hb
    hd_ref[...] = hf - hb


def _filters(bands, w1t, w1c, w1s, b1, w2, b2, w3, b3, fr, wf, wb, dl, seq, tm=512):
    cols = wf.shape[1]
    args = (bands, w1t, w1c, w1s, b1, w2, b2, w3, b3, fr, wf, wb, dl)
    return pl.pallas_call(
        functools.partial(_filter_kernel, seq=seq),
        out_shape=(jax.ShapeDtypeStruct((seq, cols), F32), jax.ShapeDtypeStruct((seq, cols), F32)),
        grid=(seq // tm,),
        in_specs=[pl.BlockSpec(a.shape, lambda i: (0, 0)) for a in args],
        out_specs=(pl.BlockSpec((tm, cols), lambda i: (i, 0)), pl.BlockSpec((tm, cols), lambda i: (i, 0))),
        compiler_params=_cparams(("arbitrary",)),
        name="hyfilter",
    )(*args)


def _spectra_kernel(hs_ref, hd_ref, t1r_ref, t2_ref, h_ref, x_ref, pr_ref, pi_ref):
    for par, src in ((0, hs_ref), (1, hd_ref)):
        def fill(r, src=src):
            x_ref[_slab(r), :] = src[pl.ds(pl.multiple_of(r * FFT_R, FFT_R), FFT_R), :]
        _loop(fill)

        def first(r, par=par):
            o = jnp.dot(t1r_ref[par], x_ref[_across(r), :].astype(BF16), preferred_element_type=F32)
            pr_ref[_slab(r), :] = o[:FFT_R]
            pi_ref[_slab(r), :] = o[FFT_R:]
        _loop(first)

        def second(r, par=par):
            o = jnp.dot(t2_ref[2 * r + par], _cat_bf16(pr_ref[_across(r), :], pi_ref[_across(r), :]),
                        preferred_element_type=F32)
            rows = pl.ds(pl.multiple_of(r * FFT_R, FFT_R), FFT_R)
            h_ref[0, 2 * par, rows, :] = o[:FFT_R].astype(h_ref.dtype)
            h_ref[0, 2 * par + 1, rows, :] = o[FFT_R:].astype(h_ref.dtype)
        _loop(second)


def _spectra(hs, hd, t1r, t2, seq):
    ncol = hs.shape[1] // LANES
    nct = ncol // HYENA_ORDER
    prow = FFT_R * FFT_PITCH
    const = lambda a: pl.BlockSpec(a.shape, lambda o, c: (0,) * a.ndim, pipeline_mode=pl.Buffered(1))
    return pl.pallas_call(
        _spectra_kernel,
        out_shape=jax.ShapeDtypeStruct((HYENA_ORDER, 4, seq, nct * LANES), BF16),
        grid=(HYENA_ORDER, nct),
        in_specs=[pl.BlockSpec((seq, LANES), lambda o, c: (0, o * nct + c)),
                  pl.BlockSpec((seq, LANES), lambda o, c: (0, o * nct + c)),
                  const(t1r), const(t2)],
        out_specs=pl.BlockSpec((1, 4, seq, LANES), lambda o, c: (o, 0, 0, c)),
        scratch_shapes=[pltpu.VMEM((prow, LANES), F32)] * 3,
        compiler_params=_cparams(("arbitrary", "arbitrary")),
        name="hyspectra",
    )(hs, hd, t1r, t2)


def _short_conv(x, w_ref, b_ref):
    rows = x.shape[0]
    ri = lax.broadcasted_iota(jnp.int32, x.shape, 0)
    prev = jnp.where(ri == 0, 0.0, pltpu.roll(x, 1, 0))
    nxt = jnp.where(ri == rows - 1, 0.0, pltpu.roll(x, rows - 1, 0))
    return prev * w_ref[0:1, :] + x * w_ref[1:2, :] + nxt * w_ref[2:3, :] + b_ref[...]


def _hyconv_kernel(z_ref, g_ref, cwz_ref, cbz_ref, cwg_ref, cbg_ref, skip_ref, h_ref,
                   t1_ref, t2_ref, t3_ref, t4_ref, o_ref,
                   xr_ref, xi_ref, pr_ref, pi_ref, qr_ref, qi_ref, yr_ref, yi_ref, *, conv_z):
    seq = z_ref.shape[0] // 2
    for half, dst in ((0, xr_ref), (1, xi_ref)):
        z = z_ref[half * seq:(half + 1) * seq, :].astype(F32)
        if conv_z:
            z = _short_conv(z, cwz_ref, cbz_ref)
        for j in range(FFT_R):
            dst[j * FFT_PITCH:j * FFT_PITCH + FFT_R, :] = z[j * FFT_R:(j + 1) * FFT_R]

    for par in (0, 1):
        def first(r, par=par):
            o = jnp.dot(t1_ref[par], _cat_bf16(xr_ref[_across(r), :], xi_ref[_across(r), :]),
                        preferred_element_type=F32)
            pr_ref[_slab(r), :] = o[:FFT_R]
            pi_ref[_slab(r), :] = o[FFT_R:]
        _loop(first)

        def second(r, par=par):
            o = jnp.dot(t2_ref[2 * r + par], _cat_bf16(pr_ref[_across(r), :], pi_ref[_across(r), :]),
                        preferred_element_type=F32)
            rows = pl.ds(pl.multiple_of(r * FFT_R, FFT_R), FFT_R)
            hr = h_ref[0, 2 * par, rows, :].astype(F32)
            hi = h_ref[0, 2 * par + 1, rows, :].astype(F32)
            ar, ai = o[:FFT_R], o[FFT_R:]
            qr_ref[_slab(r), :] = ar * hr - ai * hi
            qi_ref[_slab(r), :] = ar * hi + ai * hr
        _loop(second)

        def third(r, par=par):
            o = jnp.dot(t3_ref[2 * r + par], _cat_bf16(qr_ref[_slab(r), :], qi_ref[_slab(r), :]),
                        preferred_element_type=F32)
            pr_ref[_across(r), :] = o[:FFT_R]
            pi_ref[_across(r), :] = o[FFT_R:]
        _loop(third)

        def fourth(r, par=par):
            o = jnp.dot(t4_ref[par], _cat_bf16(pr_ref[_slab(r), :], pi_ref[_slab(r), :]),
                        preferred_element_type=F32)
            if par == 0:
                yr_ref[_across(r), :] = o[:FFT_R]
                yi_ref[_across(r), :] = o[FFT_R:]
            else:
                yr_ref[_across(r), :] = yr_ref[_across(r), :] + o[:FFT_R]
                yi_ref[_across(r), :] = yi_ref[_across(r), :] + o[FFT_R:]
        _loop(fourth)

    skip = skip_ref[...]
    for half, (y_ref, x_ref) in enumerate(((yr_ref, xr_ref), (yi_ref, xi_ref))):
        g = _short_conv(g_ref[half * seq:(half + 1) * seq, :].astype(F32), cwg_ref, cbg_ref)
        for j in range(FFT_R):
            src = slice(j * FFT_PITCH, j * FFT_PITCH + FFT_R)
            dst = slice(half * seq + j * FFT_R, half * seq + (j + 1) * FFT_R)
            o_ref[dst, :] = (g[j * FFT_R:(j + 1) * FFT_R] * (y_ref[src, :] + skip * x_ref[src, :])).astype(o_ref.dtype)


def _hyconv(zsrc, zcol0, gcol0, proj, conv_w, conv_b, zpart, gpart, skip, spec, order, tabs, batch, seq, conv_z):
    t1, t2, t3, t4 = tabs
    nct = HYENA_WIDTH // LANES
    prow = FFT_R * FFT_PITCH
    const = lambda a: pl.BlockSpec(a.shape, lambda c, p: (0,) * a.ndim, pipeline_mode=pl.Buffered(1))
    return pl.pallas_call(
        functools.partial(_hyconv_kernel, conv_z=conv_z),
        out_shape=jax.ShapeDtypeStruct((batch * seq, HYENA_WIDTH), BF16),
        grid=(nct, batch // 2),
        in_specs=[
            pl.BlockSpec((2 * seq, LANES), lambda c, p: (p, zcol0 + c)),
            pl.BlockSpec((2 * seq, LANES), lambda c, p: (p, gcol0 + c)),
            pl.BlockSpec((3, LANES), lambda c, p: (0, zpart * nct + c)),
            pl.BlockSpec((1, LANES), lambda c, p: (0, zpart * nct + c)),
            pl.BlockSpec((3, LANES), lambda c, p: (0, gpart * nct + c)),
            pl.BlockSpec((1, LANES), lambda c, p: (0, gpart * nct + c)),
            pl.BlockSpec((1, LANES), lambda c, p: (0, c)),
            pl.BlockSpec((1, 4, seq, LANES), lambda c, p: (order, 0, 0, c), pipeline_mode=pl.Buffered(1)),
            const(t1), const(t2), const(t3), const(t4),
        ],
        out_specs=pl.BlockSpec((2 * seq, LANES), lambda c, p: (p, c)),
        scratch_shapes=[pltpu.VMEM((prow, LANES), F32)] * 8,
        compiler_params=_cparams(("arbitrary", "arbitrary")),
        name=f"hyconv{order}",
    )(zsrc, proj, conv_w, conv_b, conv_w, conv_b, skip[order:order + 1], spec, t1, t2, t3, t4)


def _hyena(proj, hycol0, conv_w, conv_b, w1, b1, w2, b2, w3, b3, w_out, freq, skip, batch, seq):
    hid = w2.shape[0]
    padc = lambda a: jnp.pad(a, ((0, 0), (0, LANES - a.shape[1])))
    padr = lambda a: jnp.pad(a, ((0, LANES - a.shape[0]), (0, 0)))
    bands = padc(jnp.linspace(1e-4, FILTER_BANDS - 1, FILTER_BANDS, dtype=F32)[None])
    min_decay = math.log(DECAY_TARGET) / SLOW_DECAY_PCT
    max_decay = math.log(DECAY_TARGET) / FAST_DECAY_PCT
    deltas = jnp.abs(jnp.linspace(min_decay, max_decay, HYENA_WIDTH, dtype=F32))
    wo4 = w_out.reshape(hid, HYENA_ORDER, 2, HYENA_WIDTH)
    hs, hd = _filters(
        bands, padc(w1[0:1]), padr(padc(w1[1:1 + FILTER_BANDS])), padr(padc(w1[1 + FILTER_BANDS:])),
        padc(b1[None]), padr(padc(w2)), padc(b2[None]), padr(padc(w3)), padc(b3[None]), padc(freq[None]),
        padr(wo4[:, :, 0].reshape(hid, -1)), padr(wo4[:, :, 1].reshape(hid, -1)),
        jnp.tile(deltas, HYENA_ORDER)[None], seq)
    t1, t2, t3, t4, t1r = _dft_tables(seq)
    spec = _spectra(hs, hd, t1r, t2, seq)
    c0 = hycol0 // LANES
    nct = HYENA_WIDTH // LANES
    cb = conv_b[None]
    z1 = _hyconv(proj, c0, c0 + nct, proj, conv_w, cb, 0, 1, skip, spec, 0, (t1, t2, t3, t4), batch, seq, True)
    return _hyconv(z1, 0, c0 + 2 * nct, proj, conv_w, cb, 0, 2, skip, spec, 1, (t1, t2, t3, t4), batch, seq, False)


def _rope_tables(seq):
    pos = np.arange(seq, dtype=np.float32)
    inv_freq = (ROPE_THETA ** (-np.arange(0, ROPE_DIM, 2, dtype=np.float32) / ROPE_DIM)).astype(np.float32)
    ang = pos[:, None] * inv_freq[None, :]
    cos, sin = np.cos(ang).astype(np.float32), np.sin(ang).astype(np.float32)
    half = ROPE_DIM // 2
    c = np.ones((seq, QK_DIM), np.float32)
    s1 = np.zeros((seq, QK_DIM), np.float32)
    s2 = np.zeros((seq, QK_DIM), np.float32)
    c[:, :half] = cos
    c[:, half:ROPE_DIM] = cos
    s1[:, :half] = -sin
    s2[:, half:ROPE_DIM] = sin
    rep = LANES // QK_DIM
    return tuple(jnp.asarray(np.tile(a, (1, rep))) for a in (c, s1, s2))


def kernel(x, norm1_g, w_in, short_conv_w, short_conv_b, q_norm_g, k_norm_g, lambda_q1, lambda_k1, lambda_q2, lambda_k2, subln_g, filt_w1, filt_b1, filt_w2, filt_b2, filt_w3, filt_b3, filt_w_out, filt_freq, hyena_skip, w_branch_attn, w_branch_hyena, w_out, norm2_g, w_router, w_gate, w_up, w_down):
    b, l, d = x.shape
    depth = w_in.shape[0]
    n = b * l
    cap = CAPACITY_FACTOR * l // N_EXPERTS
    q_cols = ATTN_HEADS * 2 * QK_DIM
    ctab, s1tab, s2tab = _rope_tables(l)
    xc = x.reshape(n, d)
    for li in range(depth):
        lambda_init = 0.8 - 0.6 * math.exp(-0.3 * li)
        proj = _inproj(xc, norm1_g[li][None], w_in[li].astype(BF16))
        scale = math.log2(math.e) / math.sqrt(QK_DIM)
        gqk = jnp.concatenate([jnp.tile(q_norm_g[li], q_cols // QK_DIM) * scale,
                               jnp.tile(k_norm_g[li], q_cols // QK_DIM)])[None]
        qk = _qkprep(proj, ctab, s1tab, s2tab, gqk, l)
        lamv = jnp.stack([lambda_q1[li], lambda_k1[li], lambda_q2[li], lambda_k2[li]])
        attn = _attention(qk, proj, lamv, subln_g[li][None], b, l, lambda_init)

        hyena = _hyena(proj, 3 * q_cols, short_conv_w[li], short_conv_b[li], filt_w1[li], filt_b1[li],
                       filt_w2[li], filt_b2[li], filt_w3[li], filt_b3[li], filt_w_out[li], filt_freq[li],
                       hyena_skip[li], b, l)

        wr_pad = jnp.pad(w_router[li], ((0, 0), (0, LANES - N_EXPERTS))).astype(BF16)
        x1, u2, aff = _merge(attn, hyena, proj, xc, w_branch_attn[li].astype(BF16),
                             w_branch_hyena[li].astype(BF16), w_out[li].astype(BF16),
                             norm2_g[li][None], wr_pad)
        pos, post, afft = _topk(aff, b, l, cap)
        weo = _experts(post, afft, u2, w_gate[li].astype(BF16), w_up[li].astype(BF16),
                       w_down[li].astype(BF16), b, l, cap)
        xc = _combine(pos, x1, weo, b, l, cap)
    return xc.reshape(b, l, d)
```

```python
import functools
import math

import jax
import jax.numpy as jnp
import numpy as np
from jax import lax
from jax.experimental import pallas as pl
from jax.experimental.pallas import tpu as pltpu

F32 = jnp.float32
BF16 = jnp.bfloat16

ATTN_HEADS = 4
QK_DIM = 64
V_DIM = 128
ROPE_DIM = 16
ROPE_THETA = 500000.0
HYENA_WIDTH = 512
HYENA_ORDER = 2
FILTER_BANDS = 16
DECAY_TARGET = 1e-2
FAST_DECAY_PCT = 0.3
SLOW_DECAY_PCT = 1.5
N_EXPERTS = 16
CAPACITY_FACTOR = 2
MOE_TILE = 512
EPS = 1e-6
LANES = 128
VMEM_LIMIT = 56 * 1024 * 1024


def _cparams(sem):
    return pltpu.CompilerParams(dimension_semantics=sem, vmem_limit_bytes=VMEM_LIMIT)


def _inproj_kernel(x_ref, g_ref, w_ref, o_ref, u_ref):
    @pl.when(pl.program_id(1) == 0)
    def _():
        x = x_ref[...]
        ms = jnp.mean(x * x, axis=-1, keepdims=True)
        u_ref[...] = (x * lax.rsqrt(ms + EPS) * g_ref[...]).astype(BF16)

    o_ref[...] = jnp.dot(u_ref[...], w_ref[...], preferred_element_type=F32).astype(o_ref.dtype)


def _inproj(x2, g, w_bf16, tm=1024, tn=1280):
    n, d = x2.shape
    cols = w_bf16.shape[1]
    return pl.pallas_call(
        _inproj_kernel,
        out_shape=jax.ShapeDtypeStruct((n, cols), BF16),
        grid=(n // tm, cols // tn),
        in_specs=[
            pl.BlockSpec((tm, d), lambda i, j: (i, 0)),
            pl.BlockSpec((1, d), lambda i, j: (0, 0)),
            pl.BlockSpec((d, tn), lambda i, j: (0, j)),
        ],
        out_specs=pl.BlockSpec((tm, tn), lambda i, j: (i, j)),
        scratch_shapes=[pltpu.VMEM((tm, d), BF16)],
        compiler_params=_cparams(("arbitrary", "arbitrary")),
        name="inproj",
    )(x2, g, w_bf16)


def _qkprep_kernel(p_ref, c_ref, s1_ref, s2_ref, g_ref, o_ref):
    lane = lax.broadcasted_iota(jnp.int32, (1, LANES), 1)
    lo = lane < QK_DIM
    c = c_ref[...]
    s1 = s1_ref[...]
    s2 = s2_ref[...]
    for j in range(p_ref.shape[1] // LANES):
        t = p_ref[:, j * LANES:(j + 1) * LANES].astype(F32)
        sq = t * t
        ss_lo = jnp.sum(jnp.where(lo, sq, 0.0), axis=-1, keepdims=True)
        ss_hi = jnp.sum(jnp.where(lo, 0.0, sq), axis=-1, keepdims=True)
        r = lax.rsqrt(jnp.where(lo, ss_lo, ss_hi) * (1.0 / QK_DIM) + EPS)
        y = t * r * g_ref[:, j * LANES:(j + 1) * LANES]
        y = y * c + pltpu.roll(y, LANES - ROPE_DIM // 2, 1) * s1 + pltpu.roll(y, ROPE_DIM // 2, 1) * s2
        o_ref[:, j * LANES:(j + 1) * LANES] = y.astype(o_ref.dtype)


def _qkprep(proj, ctab, s1tab, s2tab, gqk, seq, tm=512):
    n = proj.shape[0]
    w = gqk.shape[1]
    nb = seq // tm
    return pl.pallas_call(
        _qkprep_kernel,
        out_shape=jax.ShapeDtypeStruct((n, w), BF16),
        grid=(n // tm,),
        in_specs=[
            pl.BlockSpec((tm, w), lambda i: (i, 0)),
            pl.BlockSpec((tm, LANES), lambda i: (i % nb, 0)),
            pl.BlockSpec((tm, LANES), lambda i: (i % nb, 0)),
            pl.BlockSpec((tm, LANES), lambda i: (i % nb, 0)),
            pl.BlockSpec((1, w), lambda i: (0, 0)),
        ],
        out_specs=pl.BlockSpec((tm, w), lambda i: (i, 0)),
        compiler_params=_cparams(("arbitrary",)),
        name="qkprep",
    )(proj, ctab, s1tab, s2tab, gqk)


def _attn_kernel(q_ref, k_ref, v_ref, lam_ref, g_ref, o_ref, *, tk, unroll, lambda_init):
    tq = q_ref.shape[0]
    seq = k_ref.shape[0]
    lane = lax.broadcasted_iota(jnp.int32, (1, LANES), 1)
    q = q_ref[...]
    zero = jnp.zeros_like(q)
    qs = jnp.concatenate([jnp.where(lane < QK_DIM, q, zero), jnp.where(lane < QK_DIM, zero, q)], axis=0)
    lv = lam_ref[...]
    lam = (jnp.exp(jnp.sum(lv[0:1] * lv[1:2], axis=-1, keepdims=True))
           - jnp.exp(jnp.sum(lv[2:3] * lv[3:4], axis=-1, keepdims=True)) + lambda_init)

    def step(c, carry):
        m, l, a = carry
        off = pl.multiple_of(c * tk, tk)
        kc = k_ref[pl.ds(off, tk), :]
        vc = v_ref[pl.ds(off, tk), :]
        s = lax.dot_general(qs, kc, (((1,), (1,)), ((), ())), preferred_element_type=F32)
        mn = jnp.maximum(m, jnp.max(s, axis=-1, keepdims=True))
        p = jnp.exp2(s - mn)
        al = jnp.exp2(m - mn)
        l = al * l + jnp.sum(p, axis=-1, keepdims=True)
        a = al * a + jnp.dot(p.astype(BF16), vc, preferred_element_type=F32)
        return mn, l, a

    init = (jnp.full((2 * tq, 1), -jnp.inf, F32), jnp.zeros((2 * tq, 1), F32), jnp.zeros((2 * tq, V_DIM), F32))
    _, l, a = lax.fori_loop(0, seq // tk, step, init, unroll=unroll)
    a = a / l
    o = a[:tq] - lam * a[tq:]
    ms = jnp.mean(o * o, axis=-1, keepdims=True)
    o = o * lax.rsqrt(ms + EPS) * g_ref[...] * (1.0 - lambda_init)
    o_ref[...] = o.astype(o_ref.dtype)


def _attention(qk, proj, lamv, subln_g, batch, seq, lambda_init, tq=512, tk=512, unroll=4):
    nq = seq // tq
    vcol0 = (2 * ATTN_HEADS * 2 * QK_DIM) // V_DIM
    return pl.pallas_call(
        functools.partial(_attn_kernel, tk=tk, unroll=unroll, lambda_init=lambda_init),
        out_shape=jax.ShapeDtypeStruct((batch * seq, ATTN_HEADS * V_DIM), BF16),
        grid=(batch, ATTN_HEADS, nq),
        in_specs=[
            pl.BlockSpec((tq, LANES), lambda b, h, i: (b * nq + i, h)),
            pl.BlockSpec((seq, LANES), lambda b, h, i: (b, ATTN_HEADS + h)),
            pl.BlockSpec((seq, V_DIM), lambda b, h, i: (b, vcol0 + h)),
            pl.BlockSpec((4, QK_DIM), lambda b, h, i: (0, 0)),
            pl.BlockSpec((1, V_DIM), lambda b, h, i: (0, 0)),
        ],
        out_specs=pl.BlockSpec((tq, V_DIM), lambda b, h, i: (b * nq + i, h)),
        compiler_params=_cparams(("arbitrary", "arbitrary", "arbitrary")),
        name="diffattn",
    )(qk, qk, proj, lamv, subln_g)


def _merge_kernel(at_ref, hy_ref, ga_ref, gh_ref, x_ref, wpa_ref, wph_ref, wo_ref, g2_ref, wr_ref,
                  x1_ref, u2_ref, aff_ref):
    ga = jax.nn.sigmoid(ga_ref[...].astype(F32))
    gh = jax.nn.sigmoid(gh_ref[...].astype(F32))
    merged = (ga * jnp.dot(at_ref[...], wpa_ref[...], preferred_element_type=F32)
              + gh * jnp.dot(hy_ref[...], wph_ref[...], preferred_element_type=F32))
    x1 = x_ref[...] + jnp.dot(merged.astype(BF16), wo_ref[...], preferred_element_type=F32)
    x1_ref[...] = x1
    ms = jnp.mean(x1 * x1, axis=-1, keepdims=True)
    u2 = (x1 * lax.rsqrt(ms + EPS) * g2_ref[...]).astype(BF16)
    u2_ref[...] = u2
    logits = jnp.dot(u2, wr_ref[...], preferred_element_type=F32)
    lane = lax.broadcasted_iota(jnp.int32, logits.shape, 1)
    logits = jnp.where(lane < N_EXPERTS, logits, -jnp.inf)
    e = jnp.exp(logits - jnp.max(logits, axis=-1, keepdims=True))
    aff_ref[...] = e / jnp.sum(e, axis=-1, keepdims=True)


def _merge(attn, hyena, proj, x2, wpa, wph, wo, g2, wr_pad, tm=512):
    n, d = x2.shape
    gcol0 = (proj.shape[1] - 2 * d) // d
    full = lambda shape: pl.BlockSpec(shape, lambda i: (0, 0))
    return pl.pallas_call(
        _merge_kernel,
        out_shape=(jax.ShapeDtypeStruct((n, d), F32), jax.ShapeDtypeStruct((n, d), BF16),
                   jax.ShapeDtypeStruct((n, LANES), F32)),
        grid=(n // tm,),
        in_specs=[
            pl.BlockSpec((tm, attn.shape[1]), lambda i: (i, 0)),
            pl.BlockSpec((tm, hyena.shape[1]), lambda i: (i, 0)),
            pl.BlockSpec((tm, d), lambda i: (i, gcol0)),
            pl.BlockSpec((tm, d), lambda i: (i, gcol0 + 1)),
            pl.BlockSpec((tm, d), lambda i: (i, 0)),
            full(wpa.shape), full(wph.shape), full(wo.shape), full(g2.shape), full(wr_pad.shape),
        ],
        out_specs=(pl.BlockSpec((tm, d), lambda i: (i, 0)), pl.BlockSpec((tm, d), lambda i: (i, 0)),
                   pl.BlockSpec((tm, LANES), lambda i: (i, 0))),
        compiler_params=_cparams(("arbitrary",)),
        name="merge",
    )(attn, hyena, proj, proj, x2, wpa, wph, wo, g2, wr_pad)


def _excl_cumsum_rows(mask_f32, tri, blk):
    rows = mask_f32.shape[0]
    carry = jnp.zeros((1, LANES), F32)
    outs = []
    for r in range(rows // blk):
        mb = mask_f32[r * blk:(r + 1) * blk]
        outs.append(jnp.dot(tri, mb.astype(BF16), preferred_element_type=F32) + carry)
        carry = carry + jnp.sum(mb, axis=0, keepdims=True)
    return jnp.concatenate(outs, axis=0)


def _topk_kernel(aff_ref, pos_ref, post_ref, afft_ref, offs_ref, *, cap, blk):
    aff = aff_ref[...]

    def search(i, prefix):
        cand = prefix | jnp.left_shift(jnp.int32(1), 29 - i)
        cnt = jnp.sum((aff >= pltpu.bitcast(cand, F32)[0:1]).astype(F32), axis=0, keepdims=True)
        return jnp.where(cnt >= cap, cand, prefix)

    lo = pltpu.bitcast(lax.fori_loop(0, 30, search, jnp.zeros((8, LANES), jnp.int32)), F32)[0:1]
    thr = jnp.min(jnp.where(aff >= lo, aff, jnp.inf), axis=0, keepdims=True)
    gt = (aff > thr).astype(F32)
    eq = (aff == thr).astype(F32)
    need = cap - jnp.sum(gt, axis=0, keepdims=True)
    ri = lax.broadcasted_iota(jnp.int32, (blk, blk), 0)
    ci = lax.broadcasted_iota(jnp.int32, (blk, blk), 1)
    tri = (ci < ri).astype(BF16)
    sel = gt + eq * (_excl_cumsum_rows(eq, tri, blk) < need).astype(F32)
    before = _excl_cumsum_rows(sel, tri, blk)
    pos = jnp.where(sel > 0.0, before, -1.0)
    pos_ref[...] = pos
    post_ref[0] = pos.T[:N_EXPERTS]
    afft_ref[0] = aff.T[:N_EXPERTS]
    offs_ref[0] = jnp.concatenate([before[j * MOE_TILE:j * MOE_TILE + 1] for j in range(offs_ref.shape[1])], axis=0)


def _topk(aff, batch, seq, cap, blk=256):
    nt = seq // MOE_TILE
    return pl.pallas_call(
        functools.partial(_topk_kernel, cap=cap, blk=blk),
        out_shape=(jax.ShapeDtypeStruct((batch * seq, LANES), F32),
                   jax.ShapeDtypeStruct((batch, N_EXPERTS, seq), F32),
                   jax.ShapeDtypeStruct((batch, N_EXPERTS, seq), F32),
                   jax.ShapeDtypeStruct((batch, nt, LANES), F32)),
        grid=(batch,),
        in_specs=[pl.BlockSpec((seq, LANES), lambda b: (b, 0))],
        out_specs=(pl.BlockSpec((seq, LANES), lambda b: (b, 0)),
                   pl.BlockSpec((1, N_EXPERTS, seq), lambda b: (b, 0, 0)),
                   pl.BlockSpec((1, N_EXPERTS, seq), lambda b: (b, 0, 0)),
                   pl.BlockSpec((1, nt, LANES), lambda b: (b, 0, 0))),
        compiler_params=_cparams(("arbitrary",)),
        name="topk",
    )(aff)


def _align_down(x, m):
    sh = m.bit_length() - 1
    return lax.shift_left(lax.shift_right_logical(x, sh), sh)


def _ceil_div(x, m):
    return lax.shift_right_logical(x + (m - 1), m.bit_length() - 1)


def _expert_kernel(offs_ref, post_ref, afft_ref, u_ref, wg_ref, wu_ref, wd_ref, o_ref, xin_ref, gs_ref, *, cap, win):
    e = pl.program_id(0)
    b = pl.program_id(1)
    seq = u_ref.shape[0]
    nt = seq // MOE_TILE
    xin_ref[...] = jnp.zeros_like(xin_ref)
    gs_ref[...] = jnp.zeros_like(gs_ref)
    slot0 = lax.broadcasted_iota(jnp.int32, (win, MOE_TILE), 0).astype(F32)
    for c in range(nt):
        base = (b * nt + c) * N_EXPERTS + e
        lo0 = _align_down(offs_ref[base], 8)
        end = offs_ref[base + N_EXPERTS] if c + 1 < nt else cap
        prow = post_ref[0, pl.ds(e, 1), c * MOE_TILE:(c + 1) * MOE_TILE]
        arow = afft_ref[0, pl.ds(e, 1), c * MOE_TILE:(c + 1) * MOE_TILE]
        uc = u_ref[c * MOE_TILE:(c + 1) * MOE_TILE, :]

        def window(k, carry, lo0=lo0, prow=prow, arow=arow, uc=uc):
            lo = lo0 + k * win
            s = pl.multiple_of(jnp.minimum(lo, cap - win), 8)
            srow = slot0 + s.astype(F32)
            hit = jnp.logical_and(srow == prow, srow >= lo.astype(F32))
            rows = pl.ds(s, win)
            gs_ref[rows, :] += jnp.sum(jnp.where(hit, arow, 0.0), axis=-1, keepdims=True)
            xin_ref[rows, :] += jnp.dot(jnp.where(hit, 1.0, 0.0).astype(BF16), uc, preferred_element_type=F32)
            return carry

        window(0, 0)
        lax.fori_loop(1, _ceil_div(end - lo0, win), window, 0)
    xb = xin_ref[...].astype(BF16)
    hg = jnp.dot(xb, wg_ref[0], preferred_element_type=F32)
    hu = jnp.dot(xb, wu_ref[0], preferred_element_type=F32)
    act = (hg * jax.nn.sigmoid(hg) * hu).astype(BF16)
    eo = jnp.dot(act, wd_ref[0], preferred_element_type=F32)
    o_ref[0, 0] = (eo * gs_ref[...]).astype(o_ref.dtype)


def _experts(offs, post, afft, u2, wg, wu, wd, batch, seq, cap, win=128):
    d = u2.shape[1]
    hid = wg.shape[2]
    grid_spec = pltpu.PrefetchScalarGridSpec(
        num_scalar_prefetch=1,
        grid=(N_EXPERTS, batch),
        in_specs=[
            pl.BlockSpec((1, N_EXPERTS, seq), lambda e, b, o: (b, 0, 0)),
            pl.BlockSpec((1, N_EXPERTS, seq), lambda e, b, o: (b, 0, 0)),
            pl.BlockSpec((seq, d), lambda e, b, o: (b, 0)),
            pl.BlockSpec((1, d, hid), lambda e, b, o: (e, 0, 0)),
            pl.BlockSpec((1, d, hid), lambda e, b, o: (e, 0, 0)),
            pl.BlockSpec((1, hid, d), lambda e, b, o: (e, 0, 0)),
        ],
        out_specs=pl.BlockSpec((1, 1, cap, d), lambda e, b, o: (b, e, 0, 0)),
        scratch_shapes=[pltpu.VMEM((cap, d), F32), pltpu.VMEM((cap, 1), F32)],
    )
    return pl.pallas_call(
        functools.partial(_expert_kernel, cap=cap, win=win),
        out_shape=jax.ShapeDtypeStruct((batch, N_EXPERTS, cap, d), BF16),
        grid_spec=grid_spec,
        compiler_params=_cparams(("arbitrary", "arbitrary")),
        name="experts",
    )(offs, post, afft, u2, wg, wu, wd)


def _combine_kernel(offs_ref, pos_ref, x1_ref, weo_ref, o_ref, *, cap, win):
    b = pl.program_id(0)
    t = pl.program_id(1)
    nt = pl.num_programs(1)
    tt = pos_ref.shape[0]
    pos = pos_ref[...]
    slot0 = lax.broadcasted_iota(jnp.int32, (tt, win), 1).astype(F32)
    base = (b * nt + t) * N_EXPERTS
    last = pl.num_programs(0) * nt * N_EXPERTS - 1
    acc = x1_ref[...]
    for e in range(N_EXPERTS):
        s = pl.multiple_of(jnp.minimum(_align_down(offs_ref[base + e], 16), cap - win), 16)
        hit = (slot0 + s.astype(F32)) == pos[:, e:e + 1]
        acc = acc + jnp.dot(jnp.where(hit, 1.0, 0.0).astype(BF16), weo_ref[0, e, pl.ds(s, win), :],
                            preferred_element_type=F32)
    o_ref[...] = acc
    for e in range(N_EXPERTS):
        lo0 = _align_down(offs_ref[base + e], 16)
        end = jnp.where(t + 1 < nt, offs_ref[jnp.minimum(base + N_EXPERTS + e, last)], cap)

        def window(k, carry, lo0=lo0, e=e):
            lo = lo0 + k * win
            s = pl.multiple_of(jnp.minimum(lo, cap - win), 16)
            srow = slot0 + s.astype(F32)
            hit = jnp.logical_and(srow == pos[:, e:e + 1], srow >= lo.astype(F32))
            o_ref[...] += jnp.dot(jnp.where(hit, 1.0, 0.0).astype(BF16), weo_ref[0, e, pl.ds(s, win), :],
                                  preferred_element_type=F32)
            return carry

        lax.fori_loop(1, _ceil_div(end - lo0, win), window, 0)


def _combine(offs, pos, x1, weo, batch, seq, cap, win=256):
    d = x1.shape[1]
    tt = MOE_TILE
    nt = seq // tt
    grid_spec = pltpu.PrefetchScalarGridSpec(
        num_scalar_prefetch=1,
        grid=(batch, nt),
        in_specs=[
            pl.BlockSpec((tt, LANES), lambda b, t, o: (b * nt + t, 0)),
            pl.BlockSpec((tt, d), lambda b, t, o: (b * nt + t, 0)),
            pl.BlockSpec((1, N_EXPERTS, cap, d), lambda b, t, o: (b, 0, 0, 0)),
        ],
        out_specs=pl.BlockSpec((tt, d), lambda b, t, o: (b * nt + t, 0)),
    )
    return pl.pallas_call(
        functools.partial(_combine_kernel, cap=cap, win=win),
        out_shape=jax.ShapeDtypeStruct(x1.shape, F32),
        grid_spec=grid_spec,
        compiler_params=_cparams(("arbitrary", "arbitrary")),
        name="combine",
    )(offs, pos, x1, weo)


FFT_R = 64
FFT_PITCH = FFT_R + 8


def _dft_tables(seq):
    r = FFT_R
    assert seq == r * r
    n = np.arange(r)
    f = np.exp(-2j * np.pi * np.outer(n, n) / r)
    w = lambda e: np.exp(-2j * np.pi * e / (2 * seq))

    def real_rep(a):
        return np.block([[a.real, -a.imag], [a.imag, a.real]]).astype(np.float32)

    m1 = [f, f * w(r * n)[None, :]]
    m2 = [f * w(n * m)[None, :] for m in range(2 * r)]
    m4 = [f.conj() / (2 * seq), w(-r * n)[:, None] * f.conj() / (2 * seq)]
    t1 = np.stack([real_rep(a) for a in m1])
    t2 = np.stack([real_rep(a) for a in m2])
    t3 = np.stack([real_rep(a.conj().T) for a in m2])
    t4 = np.stack([real_rep(a) for a in m4])
    t1r = np.stack([np.concatenate([a.real, a.imag], axis=0).astype(np.float32) for a in m1])
    return tuple(jnp.asarray(t).astype(BF16) for t in (t1, t2, t3, t4, t1r))


def _slab(r):
    return pl.ds(pl.multiple_of(r * FFT_PITCH, 8), FFT_R)


def _across(r):
    return pl.ds(r, FFT_R, stride=FFT_PITCH)


def _cat_bf16(a, b):
    return jnp.concatenate([a, b], axis=0).astype(BF16)


def _loop(body, unroll=16):
    def wrapped(r, c):
        body(r)
        return c
    lax.fori_loop(0, FFT_R, wrapped, 0, unroll=unroll)


def _filter_kernel(bands_ref, w1t_ref, w1c_ref, w1s_ref, b1_ref, w2_ref, b2_ref, w3_ref, b3_ref, fr_ref,
                   wf_ref, wb_ref, dl_ref, hs_ref, hd_ref, *, seq):
    tm = hs_ref.shape[0]
    hi = lax.Precision.HIGHEST
    pos = (pl.program_id(0) * tm + lax.broadcasted_iota(jnp.int32, (tm, 1), 0)).astype(F32)
    fr = fr_ref[...]

    def mlp(p):
        t = p / max(seq - 1, 1)
        ang = ((2.0 * math.pi / seq) * p) * bands_ref[...]
        pre = (t * w1t_ref[...] + jnp.dot(jnp.cos(ang), w1c_ref[...], precision=hi, preferred_element_type=F32)
               + jnp.dot(-jnp.sin(ang), w1s_ref[...], precision=hi, preferred_element_type=F32) + b1_ref[...])
        h = jnp.sin(fr * pre)
        h = jnp.sin(fr * (jnp.dot(h, w2_ref[...], precision=hi, preferred_element_type=F32) + b2_ref[...]))
        h = jnp.sin(fr * (jnp.dot(h, w3_ref[...], precision=hi, preferred_element_type=F32) + b3_ref[...]))
        return h, t

    hf_h, tf = mlp(pos)
    hb_h, tb = mlp(seq - pos)
    dl = dl_ref[...]
    hf = jnp.dot(hf_h, wf_ref[...], precision=hi, preferred_element_type=F32) * jnp.exp(-tf * dl)
    hb = jnp.dot(hb_h, wb_ref[...], precision=hi, preferred_element_type=F32) * jnp.exp(-tb * dl)
    hb = jnp.where(pos > 0.0, hb, 0.0)
    hs_ref[...] = hf + hb
    hd_ref[...] = hf - hb


def _filters(bands, w1t, w1c, w1s, b1, w2, b2, w3, b3, fr, wf, wb, dl, seq, tm=512):
    cols = wf.shape[1]
    args = (bands, w1t, w1c, w1s, b1, w2, b2, w3, b3, fr, wf, wb, dl)
    return pl.pallas_call(
        functools.partial(_filter_kernel, seq=seq),
        out_shape=(jax.ShapeDtypeStruct((seq, cols), F32), jax.ShapeDtypeStruct((seq, cols), F32)),
        grid=(seq // tm,),
        in_specs=[pl.BlockSpec(a.shape, lambda i: (0, 0)) for a in args],
        out_specs=(pl.BlockSpec((tm, cols), lambda i: (i, 0)), pl.BlockSpec((tm, cols), lambda i: (i, 0))),
        compiler_params=_cparams(("arbitrary",)),
        name="hyfilter",
    )(*args)


def _spectra_kernel(hs_ref, hd_ref, t1r_ref, t2_ref, h_ref, x_ref, pr_ref, pi_ref):
    for par, src in ((0, hs_ref), (1, hd_ref)):
        def fill(r, src=src):
            x_ref[_slab(r), :] = src[pl.ds(pl.multiple_of(r * FFT_R, FFT_R), FFT_R), :]
        _loop(fill)

        def first(r, par=par):
            o = jnp.dot(t1r_ref[par], x_ref[_across(r), :].astype(BF16), preferred_element_type=F32)
            pr_ref[_slab(r), :] = o[:FFT_R]
            pi_ref[_slab(r), :] = o[FFT_R:]
        _loop(first)

        def second(r, par=par):
            o = jnp.dot(t2_ref[2 * r + par], _cat_bf16(pr_ref[_across(r), :], pi_ref[_across(r), :]),
                        preferred_element_type=F32)
            rows = pl.ds(pl.multiple_of(r * FFT_R, FFT_R), FFT_R)
            h_ref[0, 2 * par, rows, :] = o[:FFT_R].astype(h_ref.dtype)
            h_ref[0, 2 * par + 1, rows, :] = o[FFT_R:].astype(h_ref.dtype)
        _loop(second)


def _spectra(hs, hd, t1r, t2, seq):
    ncol = hs.shape[1] // LANES
    nct = ncol // HYENA_ORDER
    prow = FFT_R * FFT_PITCH
    const = lambda a: pl.BlockSpec(a.shape, lambda o, c: (0,) * a.ndim, pipeline_mode=pl.Buffered(1))
    return pl.pallas_call(
        _spectra_kernel,
        out_shape=jax.ShapeDtypeStruct((HYENA_ORDER, 4, seq, nct * LANES), BF16),
        grid=(HYENA_ORDER, nct),
        in_specs=[pl.BlockSpec((seq, LANES), lambda o, c: (0, o * nct + c)),
                  pl.BlockSpec((seq, LANES), lambda o, c: (0, o * nct + c)),
                  const(t1r), const(t2)],
        out_specs=pl.BlockSpec((1, 4, seq, LANES), lambda o, c: (o, 0, 0, c)),
        scratch_shapes=[pltpu.VMEM((prow, LANES), F32)] * 3,
        compiler_params=_cparams(("arbitrary", "arbitrary")),
        name="hyspectra",
    )(hs, hd, t1r, t2)


def _short_conv(x, w_ref, b_ref):
    rows = x.shape[0]
    ri = lax.broadcasted_iota(jnp.int32, x.shape, 0)
    prev = jnp.where(ri == 0, 0.0, pltpu.roll(x, 1, 0))
    nxt = jnp.where(ri == rows - 1, 0.0, pltpu.roll(x, rows - 1, 0))
    return prev * w_ref[0:1, :] + x * w_ref[1:2, :] + nxt * w_ref[2:3, :] + b_ref[...]


def _hyconv_kernel(z_ref, g_ref, cwz_ref, cbz_ref, cwg_ref, cbg_ref, skip_ref, h_ref,
                   t1_ref, t2_ref, t3_ref, t4_ref, o_ref,
                   xr_ref, xi_ref, pr_ref, pi_ref, qr_ref, qi_ref, yr_ref, yi_ref, *, conv_z):
    seq = z_ref.shape[0] // 2
    for half, dst in ((0, xr_ref), (1, xi_ref)):
        z = z_ref[half * seq:(half + 1) * seq, :].astype(F32)
        if conv_z:
            z = _short_conv(z, cwz_ref, cbz_ref)
        for j in range(FFT_R):
            dst[j * FFT_PITCH:j * FFT_PITCH + FFT_R, :] = z[j * FFT_R:(j + 1) * FFT_R]

    for par in (0, 1):
        def first(r, par=par):
            o = jnp.dot(t1_ref[par], _cat_bf16(xr_ref[_across(r), :], xi_ref[_across(r), :]),
                        preferred_element_type=F32)
            pr_ref[_slab(r), :] = o[:FFT_R]
            pi_ref[_slab(r), :] = o[FFT_R:]
        _loop(first)

        def second(r, par=par):
            o = jnp.dot(t2_ref[2 * r + par], _cat_bf16(pr_ref[_across(r), :], pi_ref[_across(r), :]),
                        preferred_element_type=F32)
            rows = pl.ds(pl.multiple_of(r * FFT_R, FFT_R), FFT_R)
            hr = h_ref[0, 2 * par, rows, :].astype(F32)
            hi = h_ref[0, 2 * par + 1, rows, :].astype(F32)
            ar, ai = o[:FFT_R], o[FFT_R:]
            qr_ref[_slab(r), :] = ar * hr - ai * hi
            qi_ref[_slab(r), :] = ar * hi + ai * hr
        _loop(second)

        def third(r, par=par):
            o = jnp.dot(t3_ref[2 * r + par], _cat_bf16(qr_ref[_slab(r), :], qi_ref[_slab(r), :]),
                        preferred_element_type=F32)
            pr_ref[_across(r), :] = o[:FFT_R]
            pi_ref[_across(r), :] = o[FFT_R:]
        _loop(third)

        def fourth(r, par=par):
            o = jnp.dot(t4_ref[par], _cat_bf16(pr_ref[_slab(r), :], pi_ref[_slab(r), :]),
                        preferred_element_type=F32)
            if par == 0:
                yr_ref[_across(r), :] = o[:FFT_R]
                yi_ref[_across(r), :] = o[FFT_R:]
            else:
                yr_ref[_across(r), :] = yr_ref[_across(r), :] + o[:FFT_R]
                yi_ref[_across(r), :] = yi_ref[_across(r), :] + o[FFT_R:]
        _loop(fourth)

    skip = skip_ref[...]
    for half, (y_ref, x_ref) in enumerate(((yr_ref, xr_ref), (yi_ref, xi_ref))):
        g = _short_conv(g_ref[half * seq:(half + 1) * seq, :].astype(F32), cwg_ref, cbg_ref)
        for j in range(FFT_R):
            src = slice(j * FFT_PITCH, j * FFT_PITCH + FFT_R)
            dst = slice(half * seq + j * FFT_R, half * seq + (j + 1) * FFT_R)
            o_ref[dst, :] = (g[j * FFT_R:(j + 1) * FFT_R] * (y_ref[src, :] + skip * x_ref[src, :])).astype(o_ref.dtype)


def _hyconv(zsrc, zcol0, gcol0, proj, conv_w, conv_b, zpart, gpart, skip, spec, order, tabs, batch, seq, conv_z):
    t1, t2, t3, t4 = tabs
    nct = HYENA_WIDTH // LANES
    prow = FFT_R * FFT_PITCH
    const = lambda a: pl.BlockSpec(a.shape, lambda c, p: (0,) * a.ndim, pipeline_mode=pl.Buffered(1))
    return pl.pallas_call(
        functools.partial(_hyconv_kernel, conv_z=conv_z),
        out_shape=jax.ShapeDtypeStruct((batch * seq, HYENA_WIDTH), BF16),
        grid=(nct, batch // 2),
        in_specs=[
            pl.BlockSpec((2 * seq, LANES), lambda c, p: (p, zcol0 + c)),
            pl.BlockSpec((2 * seq, LANES), lambda c, p: (p, gcol0 + c)),
            pl.BlockSpec((3, LANES), lambda c, p: (0, zpart * nct + c)),
            pl.BlockSpec((1, LANES), lambda c, p: (0, zpart * nct + c)),
            pl.BlockSpec((3, LANES), lambda c, p: (0, gpart * nct + c)),
            pl.BlockSpec((1, LANES), lambda c, p: (0, gpart * nct + c)),
            pl.BlockSpec((1, LANES), lambda c, p: (0, c)),
            pl.BlockSpec((1, 4, seq, LANES), lambda c, p: (order, 0, 0, c), pipeline_mode=pl.Buffered(1)),
            const(t1), const(t2), const(t3), const(t4),
        ],
        out_specs=pl.BlockSpec((2 * seq, LANES), lambda c, p: (p, c)),
        scratch_shapes=[pltpu.VMEM((prow, LANES), F32)] * 8,
        compiler_params=_cparams(("arbitrary", "arbitrary")),
        name=f"hyconv{order}",
    )(zsrc, proj, conv_w, conv_b, conv_w, conv_b, skip[order:order + 1], spec, t1, t2, t3, t4)


def _hyena(proj, hycol0, conv_w, conv_b, w1, b1, w2, b2, w3, b3, w_out, freq, skip, batch, seq):
    hid = w2.shape[0]
    padc = lambda a: jnp.pad(a, ((0, 0), (0, LANES - a.shape[1])))
    padr = lambda a: jnp.pad(a, ((0, LANES - a.shape[0]), (0, 0)))
    bands = padc(jnp.linspace(1e-4, FILTER_BANDS - 1, FILTER_BANDS, dtype=F32)[None])
    min_decay = math.log(DECAY_TARGET) / SLOW_DECAY_PCT
    max_decay = math.log(DECAY_TARGET) / FAST_DECAY_PCT
    deltas = jnp.abs(jnp.linspace(min_decay, max_decay, HYENA_WIDTH, dtype=F32))
    wo4 = w_out.reshape(hid, HYENA_ORDER, 2, HYENA_WIDTH)
    hs, hd = _filters(
        bands, padc(w1[0:1]), padr(padc(w1[1:1 + FILTER_BANDS])), padr(padc(w1[1 + FILTER_BANDS:])),
        padc(b1[None]), padr(padc(w2)), padc(b2[None]), padr(padc(w3)), padc(b3[None]), padc(freq[None]),
        padr(wo4[:, :, 0].reshape(hid, -1)), padr(wo4[:, :, 1].reshape(hid, -1)),
        jnp.tile(deltas, HYENA_ORDER)[None], seq)
    t1, t2, t3, t4, t1r = _dft_tables(seq)
    spec = _spectra(hs, hd, t1r, t2, seq)
    c0 = hycol0 // LANES
    nct = HYENA_WIDTH // LANES
    cb = conv_b[None]
    z1 = _hyconv(proj, c0, c0 + nct, proj, conv_w, cb, 0, 1, skip, spec, 0, (t1, t2, t3, t4), batch, seq, True)
    return _hyconv(z1, 0, c0 + 2 * nct, proj, conv_w, cb, 0, 2, skip, spec, 1, (t1, t2, t3, t4), batch, seq, False)


def _rope_tables(seq):
    pos = np.arange(seq, dtype=np.float32)
    inv_freq = (ROPE_THETA ** (-np.arange(0, ROPE_DIM, 2, dtype=np.float32) / ROPE_DIM)).astype(np.float32)
    ang = pos[:, None] * inv_freq[None, :]
    cos, sin = np.cos(ang).astype(np.float32), np.sin(ang).astype(np.float32)
    half = ROPE_DIM // 2
    c = np.ones((seq, QK_DIM), np.float32)
    s1 = np.zeros((seq, QK_DIM), np.float32)
    s2 = np.zeros((seq, QK_DIM), np.float32)
    c[:, :half] = cos
    c[:, half:ROPE_DIM] = cos
    s1[:, :half] = -sin
    s2[:, half:ROPE_DIM] = sin
    rep = LANES // QK_DIM
    return tuple(jnp.asarray(np.tile(a, (1, rep))) for a in (c, s1, s2))


def kernel(x, norm1_g, w_in, short_conv_w, short_conv_b, q_norm_g, k_norm_g, lambda_q1, lambda_k1, lambda_q2, lambda_k2, subln_g, filt_w1, filt_b1, filt_w2, filt_b2, filt_w3, filt_b3, filt_w_out, filt_freq, hyena_skip, w_branch_attn, w_branch_hyena, w_out, norm2_g, w_router, w_gate, w_up, w_down):
    b, l, d = x.shape
    depth = w_in.shape[0]
    n = b * l
    cap = CAPACITY_FACTOR * l // N_EXPERTS
    q_cols = ATTN_HEADS * 2 * QK_DIM
    ctab, s1tab, s2tab = _rope_tables(l)
    xc = x.reshape(n, d)
    for li in range(depth):
        lambda_init = 0.8 - 0.6 * math.exp(-0.3 * li)
        proj = _inproj(xc, norm1_g[li][None], w_in[li].astype(BF16))
        scale = math.log2(math.e) / math.sqrt(QK_DIM)
        gqk = jnp.concatenate([jnp.tile(q_norm_g[li], q_cols // QK_DIM) * scale,
                               jnp.tile(k_norm_g[li], q_cols // QK_DIM)])[None]
        qk = _qkprep(proj, ctab, s1tab, s2tab, gqk, l)
        lamv = jnp.stack([lambda_q1[li], lambda_k1[li], lambda_q2[li], lambda_k2[li]])
        attn = _attention(qk, proj, lamv, subln_g[li][None], b, l, lambda_init)

        hyena = _hyena(proj, 3 * q_cols, short_conv_w[li], short_conv_b[li], filt_w1[li], filt_b1[li],
                       filt_w2[li], filt_b2[li], filt_w3[li], filt_b3[li], filt_w_out[li], filt_freq[li],
                       hyena_skip[li], b, l)

        wr_pad = jnp.pad(w_router[li], ((0, 0), (0, LANES - N_EXPERTS))).astype(BF16)
        x1, u2, aff = _merge(attn, hyena, proj, xc, w_branch_attn[li].astype(BF16),
                             w_branch_hyena[li].astype(BF16), w_out[li].astype(BF16),
                             norm2_g[li][None], wr_pad)
        pos, post, afft, offs = _topk(aff, b, l, cap)
        offs = offs[:, :, :N_EXPERTS].astype(jnp.int32).reshape(-1)
        weo = _experts(offs, post, afft, u2, w_gate[li].astype(BF16), w_up[li].astype(BF16),
                       w_down[li].astype(BF16), b, l, cap)
        xc = _combine(offs, pos, x1, weo, b, l, cap)
    return xc.reshape(b, l, d)
```

```python
import functools
import math

import jax
import jax.numpy as jnp
import numpy as np
from jax import lax
from jax.experimental import pallas as pl
from jax.experimental.pallas import tpu as pltpu

F32 = jnp.float32
BF16 = jnp.bfloat16

ATTN_HEADS = 4
QK_DIM = 64
V_DIM = 128
ROPE_DIM = 16
ROPE_THETA = 500000.0
HYENA_WIDTH = 512
HYENA_ORDER = 2
FILTER_BANDS = 16
DECAY_TARGET = 1e-2
FAST_DECAY_PCT = 0.3
SLOW_DECAY_PCT = 1.5
N_EXPERTS = 16
CAPACITY_FACTOR = 2
MOE_TILE = 512
EPS = 1e-6
LANES = 128
FILT_GROUPS = LANES // FILTER_BANDS
VMEM_LIMIT = 56 * 1024 * 1024


def _cparams(sem):
    return pltpu.CompilerParams(dimension_semantics=sem, vmem_limit_bytes=VMEM_LIMIT)


def _inproj_kernel(x_ref, g_ref, w_ref, o_ref, u_ref):
    @pl.when(pl.program_id(1) == 0)
    def _():
        x = x_ref[...]
        ms = jnp.mean(x * x, axis=-1, keepdims=True)
        u_ref[...] = (x * lax.rsqrt(ms + EPS) * g_ref[...]).astype(BF16)

    o_ref[...] = jnp.dot(u_ref[...], w_ref[...], preferred_element_type=F32).astype(o_ref.dtype)


def _inproj(x2, g, w_bf16, tm=1024, tn=1280):
    n, d = x2.shape
    cols = w_bf16.shape[1]
    return pl.pallas_call(
        _inproj_kernel,
        out_shape=jax.ShapeDtypeStruct((n, cols), BF16),
        grid=(n // tm, cols // tn),
        in_specs=[
            pl.BlockSpec((tm, d), lambda i, j: (i, 0)),
            pl.BlockSpec((1, d), lambda i, j: (0, 0)),
            pl.BlockSpec((d, tn), lambda i, j: (0, j)),
        ],
        out_specs=pl.BlockSpec((tm, tn), lambda i, j: (i, j)),
        scratch_shapes=[pltpu.VMEM((tm, d), BF16)],
        compiler_params=_cparams(("arbitrary", "arbitrary")),
        name="inproj",
    )(x2, g, w_bf16)


def _qkprep_kernel(p_ref, c_ref, s_ref, perm_ref, g_ref, o_ref):
    lane = lax.broadcasted_iota(jnp.int32, (1, LANES), 1)
    lo = lane < QK_DIM
    c = c_ref[...]
    s = s_ref[...]
    perm = perm_ref[...]
    for j in range(p_ref.shape[1] // LANES):
        t = p_ref[:, j * LANES:(j + 1) * LANES].astype(F32)
        sq = t * t
        ss_lo = jnp.sum(jnp.where(lo, sq, 0.0), axis=-1, keepdims=True)
        ss_hi = jnp.sum(jnp.where(lo, 0.0, sq), axis=-1, keepdims=True)
        r = lax.rsqrt(jnp.where(lo, ss_lo, ss_hi) * (1.0 / QK_DIM) + EPS)
        y = t * r * g_ref[:, j * LANES:(j + 1) * LANES]
        y_hi = y.astype(BF16)
        y_lo = (y - y_hi.astype(F32)).astype(BF16)
        partner = (jnp.dot(y_hi, perm, preferred_element_type=F32)
                   + jnp.dot(y_lo, perm, preferred_element_type=F32))
        o_ref[:, j * LANES:(j + 1) * LANES] = (y * c + partner * s).astype(o_ref.dtype)


def _qkprep(proj, ctab, stab, perm, gqk, seq, tm=512):
    n = proj.shape[0]
    w = gqk.shape[1]
    nb = seq // tm
    return pl.pallas_call(
        _qkprep_kernel,
        out_shape=jax.ShapeDtypeStruct((n, w), BF16),
        grid=(n // tm,),
        in_specs=[
            pl.BlockSpec((tm, w), lambda i: (i, 0)),
            pl.BlockSpec((tm, LANES), lambda i: (i % nb, 0)),
            pl.BlockSpec((tm, LANES), lambda i: (i % nb, 0)),
            pl.BlockSpec((LANES, LANES), lambda i: (0, 0)),
            pl.BlockSpec((1, w), lambda i: (0, 0)),
        ],
        out_specs=pl.BlockSpec((tm, w), lambda i: (i, 0)),
        compiler_params=_cparams(("arbitrary",)),
        name="qkprep",
    )(proj, ctab, stab, perm, gqk)


def _attn_kernel(q_ref, k_ref, v_ref, lam_ref, g_ref, o_ref, *, tk, unroll, lambda_init):
    tq = q_ref.shape[0]
    seq = k_ref.shape[0]
    lane = lax.broadcasted_iota(jnp.int32, (1, LANES), 1)
    q = q_ref[...]
    zero = jnp.zeros_like(q)
    qs = jnp.concatenate([jnp.where(lane < QK_DIM, q, zero), jnp.where(lane < QK_DIM, zero, q)], axis=0)
    lv = lam_ref[...]
    lam = (jnp.exp(jnp.sum(lv[0:1] * lv[1:2], axis=-1, keepdims=True))
           - jnp.exp(jnp.sum(lv[2:3] * lv[3:4], axis=-1, keepdims=True)) + lambda_init)

    def step(c, carry):
        m, l, a = carry
        off = pl.multiple_of(c * tk, tk)
        kc = k_ref[pl.ds(off, tk), :]
        vc = v_ref[pl.ds(off, tk), :]
        s = lax.dot_general(qs, kc, (((1,), (1,)), ((), ())), preferred_element_type=F32)
        mn = jnp.maximum(m, jnp.max(s, axis=-1, keepdims=True))
        p = jnp.exp2(s - mn)
        al = jnp.exp2(m - mn)
        l = al * l + jnp.sum(p, axis=-1, keepdims=True)
        a = al * a + jnp.dot(p.astype(BF16), vc, preferred_element_type=F32)
        return mn, l, a

    init = (jnp.full((2 * tq, 1), -jnp.inf, F32), jnp.zeros((2 * tq, 1), F32), jnp.zeros((2 * tq, V_DIM), F32))
    _, l, a = lax.fori_loop(0, seq // tk, step, init, unroll=unroll)
    a = a / l
    o = a[:tq] - lam * a[tq:]
    ms = jnp.mean(o * o, axis=-1, keepdims=True)
    o = o * lax.rsqrt(ms + EPS) * g_ref[...] * (1.0 - lambda_init)
    o_ref[...] = o.astype(o_ref.dtype)


def _attention(qk, proj, lamv, subln_g, batch, seq, lambda_init, tq=512, tk=1024, unroll=4):
    nq = seq // tq
    vcol0 = (2 * ATTN_HEADS * 2 * QK_DIM) // V_DIM
    return pl.pallas_call(
        functools.partial(_attn_kernel, tk=tk, unroll=unroll, lambda_init=lambda_init),
        out_shape=jax.ShapeDtypeStruct((batch * seq, ATTN_HEADS * V_DIM), BF16),
        grid=(batch, ATTN_HEADS, nq),
        in_specs=[
            pl.BlockSpec((tq, LANES), lambda b, h, i: (b * nq + i, h)),
            pl.BlockSpec((seq, LANES), lambda b, h, i: (b, ATTN_HEADS + h)),
            pl.BlockSpec((seq, V_DIM), lambda b, h, i: (b, vcol0 + h)),
            pl.BlockSpec((4, QK_DIM), lambda b, h, i: (0, 0)),
            pl.BlockSpec((1, V_DIM), lambda b, h, i: (0, 0)),
        ],
        out_specs=pl.BlockSpec((tq, V_DIM), lambda b, h, i: (b * nq + i, h)),
        compiler_params=_cparams(("arbitrary", "arbitrary", "arbitrary")),
        name="diffattn",
    )(qk, qk, proj, lamv, subln_g)


def _merge_kernel(at_ref, hy_ref, ga_ref, gh_ref, x_ref, wpa_ref, wph_ref, wo_ref, g2_ref, wr_ref,
                  x1_ref, u2_ref, aff_ref):
    ga = jax.nn.sigmoid(ga_ref[...].astype(F32))
    gh = jax.nn.sigmoid(gh_ref[...].astype(F32))
    merged = (ga * jnp.dot(at_ref[...], wpa_ref[...], preferred_element_type=F32)
              + gh * jnp.dot(hy_ref[...], wph_ref[...], preferred_element_type=F32))
    x1 = x_ref[...] + jnp.dot(merged.astype(BF16), wo_ref[...], preferred_element_type=F32)
    x1_ref[...] = x1
    ms = jnp.mean(x1 * x1, axis=-1, keepdims=True)
    u2 = (x1 * lax.rsqrt(ms + EPS) * g2_ref[...]).astype(BF16)
    u2_ref[...] = u2
    logits = jnp.dot(u2, wr_ref[...], preferred_element_type=F32)
    lane = lax.broadcasted_iota(jnp.int32, logits.shape, 1)
    logits = jnp.where(lane < N_EXPERTS, logits, -jnp.inf)
    e = jnp.exp(logits - jnp.max(logits, axis=-1, keepdims=True))
    aff_ref[...] = e / jnp.sum(e, axis=-1, keepdims=True)


def _merge(attn, hyena, proj, x2, wpa, wph, wo, g2, wr_pad, tm=512):
    n, d = x2.shape
    gcol0 = (proj.shape[1] - 2 * d) // d
    full = lambda shape: pl.BlockSpec(shape, lambda i: (0, 0))
    return pl.pallas_call(
        _merge_kernel,
        out_shape=(jax.ShapeDtypeStruct((n, d), F32), jax.ShapeDtypeStruct((n, d), BF16),
                   jax.ShapeDtypeStruct((n, LANES), F32)),
        grid=(n // tm,),
        in_specs=[
            pl.BlockSpec((tm, attn.shape[1]), lambda i: (i, 0)),
            pl.BlockSpec((tm, hyena.shape[1]), lambda i: (i, 0)),
            pl.BlockSpec((tm, d), lambda i: (i, gcol0)),
            pl.BlockSpec((tm, d), lambda i: (i, gcol0 + 1)),
            pl.BlockSpec((tm, d), lambda i: (i, 0)),
            full(wpa.shape), full(wph.shape), full(wo.shape), full(g2.shape), full(wr_pad.shape),
        ],
        out_specs=(pl.BlockSpec((tm, d), lambda i: (i, 0)), pl.BlockSpec((tm, d), lambda i: (i, 0)),
                   pl.BlockSpec((tm, LANES), lambda i: (i, 0))),
        compiler_params=_cparams(("arbitrary",)),
        name="merge",
    )(attn, hyena, proj, proj, x2, wpa, wph, wo, g2, wr_pad)


def _excl_cumsum_rows(mask_f32, tri, blk):
    rows = mask_f32.shape[0]
    carry = jnp.zeros((1, LANES), F32)
    outs = []
    for r in range(rows // blk):
        mb = mask_f32[r * blk:(r + 1) * blk]
        outs.append(jnp.dot(tri, mb.astype(BF16), preferred_element_type=F32) + carry)
        carry = carry + jnp.sum(mb, axis=0, keepdims=True)
    return jnp.concatenate(outs, axis=0)


def _topk_kernel(aff_ref, pos_ref, post_ref, afft_ref, offs_ref, *, cap, blk, batch):
    seq = aff_ref.shape[0] // batch
    aff = aff_ref[0:seq]
    for b in range(1, batch):
        aff = aff + pltpu.roll(aff_ref[b * seq:(b + 1) * seq], b * N_EXPERTS, 1)

    def search(i, prefix):
        cand = prefix | jnp.left_shift(jnp.int32(1), 29 - i)
        cnt = jnp.sum((aff >= pltpu.bitcast(cand, F32)[0:1]).astype(F32), axis=0, keepdims=True)
        return jnp.where(cnt >= cap, cand, prefix)

    lo = pltpu.bitcast(lax.fori_loop(0, 30, search, jnp.zeros((8, LANES), jnp.int32)), F32)[0:1]
    thr = jnp.min(jnp.where(aff >= lo, aff, jnp.inf), axis=0, keepdims=True)
    gt = (aff > thr).astype(F32)
    eq = (aff == thr).astype(F32)
    need = cap - jnp.sum(gt, axis=0, keepdims=True)
    ri = lax.broadcasted_iota(jnp.int32, (blk, blk), 0)
    ci = lax.broadcasted_iota(jnp.int32, (blk, blk), 1)
    tri = (ci < ri).astype(BF16)
    sel = gt + eq * (_excl_cumsum_rows(eq, tri, blk) < need).astype(F32)
    before = _excl_cumsum_rows(sel, tri, blk)
    pos = jnp.where(sel > 0.0, before, -1.0)
    pos_t = pos.T
    aff_t = aff.T
    lane = lax.broadcasted_iota(jnp.int32, (1, LANES), 1)
    for b in range(batch):
        mine = pos if b == 0 else pltpu.roll(pos, LANES - b * N_EXPERTS, 1)
        pos_ref[b * seq:(b + 1) * seq] = jnp.where(lane < N_EXPERTS, mine, -1.0)
        post_ref[b] = pos_t[b * N_EXPERTS:(b + 1) * N_EXPERTS]
        afft_ref[b] = aff_t[b * N_EXPERTS:(b + 1) * N_EXPERTS]
    offs_ref[...] = jnp.concatenate([before[j * MOE_TILE:j * MOE_TILE + 1] for j in range(offs_ref.shape[0])], axis=0)


def _topk(aff, batch, seq, cap, blk=256):
    assert batch * N_EXPERTS <= LANES
    nt = seq // MOE_TILE
    pos, post, afft, offs = pl.pallas_call(
        functools.partial(_topk_kernel, cap=cap, blk=blk, batch=batch),
        out_shape=(jax.ShapeDtypeStruct((batch * seq, LANES), F32),
                   jax.ShapeDtypeStruct((batch, N_EXPERTS, seq), F32),
                   jax.ShapeDtypeStruct((batch, N_EXPERTS, seq), F32),
                   jax.ShapeDtypeStruct((nt, LANES), F32)),
        compiler_params=pltpu.CompilerParams(vmem_limit_bytes=VMEM_LIMIT),
        name="topk",
    )(aff)
    offs = offs[:, :batch * N_EXPERTS].reshape(nt, batch, N_EXPERTS).transpose(1, 0, 2)
    return pos, post, afft, offs.astype(jnp.int32).reshape(-1)


def _align_down(x, m):
    sh = m.bit_length() - 1
    return lax.shift_left(lax.shift_right_logical(x, sh), sh)


def _ceil_div(x, m):
    return lax.shift_right_logical(x + (m - 1), m.bit_length() - 1)


def _expert_kernel(offs_ref, post_ref, afft_ref, u_ref, wg_ref, wu_ref, wd_ref, o_ref, xin_ref, gs_ref, *, cap, win):
    e = pl.program_id(0)
    b = pl.program_id(1)
    seq = u_ref.shape[0]
    nt = seq // MOE_TILE
    xin_ref[...] = jnp.zeros_like(xin_ref)
    gs_ref[...] = jnp.zeros_like(gs_ref)
    slot0 = lax.broadcasted_iota(jnp.int32, (win, MOE_TILE), 0).astype(F32)
    for c in range(nt):
        base = (b * nt + c) * N_EXPERTS + e
        lo0 = _align_down(offs_ref[base], 8)
        end = offs_ref[base + N_EXPERTS] if c + 1 < nt else cap
        prow = post_ref[0, pl.ds(e, 1), c * MOE_TILE:(c + 1) * MOE_TILE]
        arow = afft_ref[0, pl.ds(e, 1), c * MOE_TILE:(c + 1) * MOE_TILE]
        uc = u_ref[c * MOE_TILE:(c + 1) * MOE_TILE, :]

        def window(k, carry, lo0=lo0, prow=prow, arow=arow, uc=uc):
            lo = lo0 + k * win
            s = pl.multiple_of(jnp.minimum(lo, cap - win), 8)
            srow = slot0 + s.astype(F32)
            hit = jnp.logical_and(srow == prow, srow >= lo.astype(F32))
            rows = pl.ds(s, win)
            gs_ref[rows, :] += jnp.sum(jnp.where(hit, arow, 0.0), axis=-1, keepdims=True)
            xin_ref[rows, :] += jnp.dot(jnp.where(hit, 1.0, 0.0).astype(BF16), uc, preferred_element_type=F32)
            return carry

        window(0, 0)
        lax.fori_loop(1, _ceil_div(end - lo0, win), window, 0)
    xb = xin_ref[...].astype(BF16)
    hg = jnp.dot(xb, wg_ref[0], preferred_element_type=F32)
    hu = jnp.dot(xb, wu_ref[0], preferred_element_type=F32)
    act = (hg * jax.nn.sigmoid(hg) * hu).astype(BF16)
    eo = jnp.dot(act, wd_ref[0], preferred_element_type=F32)
    o_ref[0, 0] = (eo * gs_ref[...]).astype(o_ref.dtype)


def _experts(offs, post, afft, u2, wg, wu, wd, batch, seq, cap, win=128):
    d = u2.shape[1]
    hid = wg.shape[2]
    grid_spec = pltpu.PrefetchScalarGridSpec(
        num_scalar_prefetch=1,
        grid=(N_EXPERTS, batch),
        in_specs=[
            pl.BlockSpec((1, N_EXPERTS, seq), lambda e, b, o: (b, 0, 0)),
            pl.BlockSpec((1, N_EXPERTS, seq), lambda e, b, o: (b, 0, 0)),
            pl.BlockSpec((seq, d), lambda e, b, o: (b, 0)),
            pl.BlockSpec((1, d, hid), lambda e, b, o: (e, 0, 0)),
            pl.BlockSpec((1, d, hid), lambda e, b, o: (e, 0, 0)),
            pl.BlockSpec((1, hid, d), lambda e, b, o: (e, 0, 0)),
        ],
        out_specs=pl.BlockSpec((1, 1, cap, d), lambda e, b, o: (b, e, 0, 0)),
        scratch_shapes=[pltpu.VMEM((cap, d), F32), pltpu.VMEM((cap, 1), F32)],
    )
    return pl.pallas_call(
        functools.partial(_expert_kernel, cap=cap, win=win),
        out_shape=jax.ShapeDtypeStruct((batch, N_EXPERTS, cap, d), BF16),
        grid_spec=grid_spec,
        compiler_params=_cparams(("arbitrary", "arbitrary")),
        name="experts",
    )(offs, post, afft, u2, wg, wu, wd)


def _combine_kernel(offs_ref, pos_ref, x1_ref, weo_ref, o_ref, *, cap, win):
    b = pl.program_id(0)
    t = pl.program_id(1)
    nt = pl.num_programs(1)
    tt = pos_ref.shape[0]
    pos = pos_ref[...]
    slot0 = lax.broadcasted_iota(jnp.int32, (tt, win), 1).astype(F32)
    base = (b * nt + t) * N_EXPERTS
    last = pl.num_programs(0) * nt * N_EXPERTS - 1
    acc = x1_ref[...]
    for e in range(N_EXPERTS):
        s = pl.multiple_of(jnp.minimum(_align_down(offs_ref[base + e], 16), cap - win), 16)
        hit = (slot0 + s.astype(F32)) == pos[:, e:e + 1]
        acc = acc + jnp.dot(jnp.where(hit, 1.0, 0.0).astype(BF16), weo_ref[0, e, pl.ds(s, win), :],
                            preferred_element_type=F32)
    o_ref[...] = acc
    for e in range(N_EXPERTS):
        lo0 = _align_down(offs_ref[base + e], 16)
        end = jnp.where(t + 1 < nt, offs_ref[jnp.minimum(base + N_EXPERTS + e, last)], cap)

        def window(k, carry, lo0=lo0, e=e):
            lo = lo0 + k * win
            s = pl.multiple_of(jnp.minimum(lo, cap - win), 16)
            srow = slot0 + s.astype(F32)
            hit = jnp.logical_and(srow == pos[:, e:e + 1], srow >= lo.astype(F32))
            o_ref[...] += jnp.dot(jnp.where(hit, 1.0, 0.0).astype(BF16), weo_ref[0, e, pl.ds(s, win), :],
                                  preferred_element_type=F32)
            return carry

        lax.fori_loop(1, _ceil_div(end - lo0, win), window, 0)


def _combine(offs, pos, x1, weo, batch, seq, cap, win=256):
    d = x1.shape[1]
    tt = MOE_TILE
    nt = seq // tt
    grid_spec = pltpu.PrefetchScalarGridSpec(
        num_scalar_prefetch=1,
        grid=(batch, nt),
        in_specs=[
            pl.BlockSpec((tt, LANES), lambda b, t, o: (b * nt + t, 0)),
            pl.BlockSpec((tt, d), lambda b, t, o: (b * nt + t, 0)),
            pl.BlockSpec((1, N_EXPERTS, cap, d), lambda b, t, o: (b, 0, 0, 0)),
        ],
        out_specs=pl.BlockSpec((tt, d), lambda b, t, o: (b * nt + t, 0)),
    )
    return pl.pallas_call(
        functools.partial(_combine_kernel, cap=cap, win=win),
        out_shape=jax.ShapeDtypeStruct(x1.shape, F32),
        grid_spec=grid_spec,
        compiler_params=_cparams(("arbitrary", "arbitrary")),
        name="combine",
    )(offs, pos, x1, weo)


FFT_R = 64
FFT_PITCH = FFT_R + 8


def _dft_tables(seq):
    r = FFT_R
    assert seq == r * r
    n = np.arange(r)
    f = np.exp(-2j * np.pi * np.outer(n, n) / r)
    w = lambda e: np.exp(-2j * np.pi * e / (2 * seq))

    def real_rep(a):
        return np.block([[a.real, -a.imag], [a.imag, a.real]]).astype(np.float32)

    m1 = [f, f * w(r * n)[None, :]]
    m2 = [f * w(n * m)[None, :] for m in range(2 * r)]
    m4 = [f.conj() / (2 * seq), w(-r * n)[:, None] * f.conj() / (2 * seq)]
    t1 = np.stack([real_rep(a) for a in m1])
    t2 = np.stack([real_rep(a) for a in m2])
    t3 = np.stack([real_rep(a.conj().T) for a in m2])
    t4 = np.stack([real_rep(a) for a in m4])
    t1r = np.stack([np.concatenate([a.real, a.imag], axis=0).astype(np.float32) for a in m1])
    return tuple(jnp.asarray(t).astype(BF16) for t in (t1, t2, t3, t4, t1r))


def _slab(r):
    return pl.ds(pl.multiple_of(r * FFT_PITCH, 8), FFT_R)


def _across(r):
    return pl.ds(r, FFT_R, stride=FFT_PITCH)


def _cat_bf16(a, b):
    return jnp.concatenate([a, b], axis=0).astype(BF16)


def _loop(body, unroll=16):
    def wrapped(r, c):
        body(r)
        return c
    lax.fori_loop(0, FFT_R, wrapped, 0, unroll=unroll)


def _bf16_pieces(w):
    hi = w.astype(BF16)
    return hi, (w - hi.astype(F32)).astype(BF16)


def _dot3(a, w_ref):
    a_hi, a_lo = _bf16_pieces(a)
    return (jnp.dot(a_hi, w_ref[0], preferred_element_type=F32)
            + (jnp.dot(a_hi, w_ref[1], preferred_element_type=F32)
               + jnp.dot(a_lo, w_ref[0], preferred_element_type=F32)))


def _filter_kernel(bands_ref, w1t_ref, w1c_ref, w1s_ref, b1_ref, w2_ref, b2_ref, w3_ref, b3_ref, fr_ref,
                   wf_ref, wb_ref, dl_ref, hs_ref, hd_ref, *, seq):
    tm = hs_ref.shape[0]
    rows = tm // FILT_GROUPS
    hid = wf_ref.shape[1]
    base = pl.program_id(0) * tm
    lane = lax.broadcasted_iota(jnp.int32, (rows, LANES), 1)
    grp = lax.shift_right_logical(lane, FILTER_BANDS.bit_length() - 1)
    pos = (base + grp * rows + lax.broadcasted_iota(jnp.int32, (rows, LANES), 0)).astype(F32)
    first = jnp.bitwise_and(lane, FILTER_BANDS - 1) == 0
    fr = fr_ref[...]
    tscale = 1.0 / max(seq - 1, 1)

    def mlp(p):
        ang = ((2.0 * math.pi / seq) * p) * bands_ref[...]
        pre = (_dot3(jnp.cos(ang), w1c_ref) + _dot3(-jnp.sin(ang), w1s_ref)
               + _dot3(jnp.where(first, p * tscale, 0.0), w1t_ref) + b1_ref[...])
        h = jnp.sin(fr * pre)
        h = jnp.sin(fr * (_dot3(h, w2_ref) + b2_ref[...]))
        return jnp.sin(fr * (_dot3(h, w3_ref) + b3_ref[...]))

    hf_h = mlp(pos)
    hb_h = mlp(seq - pos)
    dl = dl_ref[...]
    for g in range(FILT_GROUPS):
        pcol = (base + g * rows + lax.broadcasted_iota(jnp.int32, (rows, 1), 0)).astype(F32)
        hf = _dot3(hf_h[:, g * hid:(g + 1) * hid], wf_ref) * jnp.exp(-(pcol * tscale) * dl)
        hb = _dot3(hb_h[:, g * hid:(g + 1) * hid], wb_ref) * jnp.exp(-((seq - pcol) * tscale) * dl)
        hb = jnp.where(pcol > 0.0, hb, 0.0)
        hs_ref[g * rows:(g + 1) * rows, :] = hf + hb
        hd_ref[g * rows:(g + 1) * rows, :] = hf - hb


def _filters(bands, w1t, w1c, w1s, b1, w2, b2, w3, b3, fr, wf, wb, dl, seq, tm=512):
    cols = wf.shape[1]
    pieces = lambda w: jnp.stack(_bf16_pieces(w))
    args = (bands, pieces(w1t), pieces(w1c), pieces(w1s), b1, pieces(w2), b2, pieces(w3), b3, fr,
            pieces(wf), pieces(wb), dl)
    return pl.pallas_call(
        functools.partial(_filter_kernel, seq=seq),
        out_shape=(jax.ShapeDtypeStruct((seq, cols), F32), jax.ShapeDtypeStruct((seq, cols), F32)),
        grid=(seq // tm,),
        in_specs=[pl.BlockSpec(a.shape, lambda i, nd=a.ndim: (0,) * nd) for a in args],
        out_specs=(pl.BlockSpec((tm, cols), lambda i: (i, 0)), pl.BlockSpec((tm, cols), lambda i: (i, 0))),
        compiler_params=_cparams(("arbitrary",)),
        name="hyfilter",
    )(*args)


def _spectra_kernel(hs_ref, hd_ref, t1r_ref, t2_ref, h_ref, x_ref, pr_ref, pi_ref):
    for par, src in ((0, hs_ref), (1, hd_ref)):
        def fill(r, src=src):
            x_ref[_slab(r), :] = src[pl.ds(pl.multiple_of(r * FFT_R, FFT_R), FFT_R), :]
        _loop(fill)

        def first(r, par=par):
            o = jnp.dot(t1r_ref[par], x_ref[_across(r), :].astype(BF16), preferred_element_type=F32)
            pr_ref[_slab(r), :] = o[:FFT_R]
            pi_ref[_slab(r), :] = o[FFT_R:]
        _loop(first)

        def second(r, par=par):
            o = jnp.dot(t2_ref[2 * r + par], _cat_bf16(pr_ref[_across(r), :], pi_ref[_across(r), :]),
                        preferred_element_type=F32)
            rows = pl.ds(pl.multiple_of(r * FFT_R, FFT_R), FFT_R)
            h_ref[0, 2 * par, rows, :] = o[:FFT_R].astype(h_ref.dtype)
            h_ref[0, 2 * par + 1, rows, :] = o[FFT_R:].astype(h_ref.dtype)
        _loop(second)


def _spectra(hs, hd, t1r, t2, seq):
    ncol = hs.shape[1] // LANES
    nct = ncol // HYENA_ORDER
    prow = FFT_R * FFT_PITCH
    const = lambda a: pl.BlockSpec(a.shape, lambda o, c: (0,) * a.ndim, pipeline_mode=pl.Buffered(1))
    return pl.pallas_call(
        _spectra_kernel,
        out_shape=jax.ShapeDtypeStruct((HYENA_ORDER, 4, seq, nct * LANES), BF16),
        grid=(HYENA_ORDER, nct),
        in_specs=[pl.BlockSpec((seq, LANES), lambda o, c: (0, o * nct + c)),
                  pl.BlockSpec((seq, LANES), lambda o, c: (0, o * nct + c)),
                  const(t1r), const(t2)],
        out_specs=pl.BlockSpec((1, 4, seq, LANES), lambda o, c: (o, 0, 0, c)),
        scratch_shapes=[pltpu.VMEM((prow, LANES), F32)] * 3,
        compiler_params=_cparams(("arbitrary", "arbitrary")),
        name="hyspectra",
    )(hs, hd, t1r, t2)


def _short_conv(x, w_ref, b_ref):
    rows = x.shape[0]
    ri = lax.broadcasted_iota(jnp.int32, x.shape, 0)
    prev = jnp.where(ri == 0, 0.0, pltpu.roll(x, 1, 0))
    nxt = jnp.where(ri == rows - 1, 0.0, pltpu.roll(x, rows - 1, 0))
    return prev * w_ref[0:1, :] + x * w_ref[1:2, :] + nxt * w_ref[2:3, :] + b_ref[...]


def _hyconv_kernel(z_ref, g_ref, cwz_ref, cbz_ref, cwg_ref, cbg_ref, skip_ref, h_ref,
                   t1_ref, t2_ref, t3_ref, t4_ref, o_ref,
                   xr_ref, xi_ref, pr_ref, pi_ref, qr_ref, qi_ref, yr_ref, yi_ref, *, conv_z):
    seq = z_ref.shape[0] // 2
    for half, dst in ((0, xr_ref), (1, xi_ref)):
        z = z_ref[half * seq:(half + 1) * seq, :].astype(F32)
        if conv_z:
            z = _short_conv(z, cwz_ref, cbz_ref)
        for j in range(FFT_R):
            dst[j * FFT_PITCH:j * FFT_PITCH + FFT_R, :] = z[j * FFT_R:(j + 1) * FFT_R]

    for par in (0, 1):
        def first(r, par=par):
            o = jnp.dot(t1_ref[par], _cat_bf16(xr_ref[_across(r), :], xi_ref[_across(r), :]),
                        preferred_element_type=F32)
            pr_ref[_slab(r), :] = o[:FFT_R]
            pi_ref[_slab(r), :] = o[FFT_R:]
        _loop(first)

        def second(r, par=par):
            o = jnp.dot(t2_ref[2 * r + par], _cat_bf16(pr_ref[_across(r), :], pi_ref[_across(r), :]),
                        preferred_element_type=F32)
            rows = pl.ds(pl.multiple_of(r * FFT_R, FFT_R), FFT_R)
            hr = h_ref[0, 2 * par, rows, :].astype(F32)
            hi = h_ref[0, 2 * par + 1, rows, :].astype(F32)
            ar, ai = o[:FFT_R], o[FFT_R:]
            qr_ref[_slab(r), :] = ar * hr - ai * hi
            qi_ref[_slab(r), :] = ar * hi + ai * hr
        _loop(second)

        def third(r, par=par):
            o = jnp.dot(t3_ref[2 * r + par], _cat_bf16(qr_ref[_slab(r), :], qi_ref[_slab(r), :]),
                        preferred_element_type=F32)
            pr_ref[_across(r), :] = o[:FFT_R]
            pi_ref[_across(r), :] = o[FFT_R:]
        _loop(third)

        def fourth(r, par=par):
            o = jnp.dot(t4_ref[par], _cat_bf16(pr_ref[_slab(r), :], pi_ref[_slab(r), :]),
                        preferred_element_type=F32)
            if par == 0:
                yr_ref[_across(r), :] = o[:FFT_R]
                yi_ref[_across(r), :] = o[FFT_R:]
            else:
                yr_ref[_across(r), :] = yr_ref[_across(r), :] + o[:FFT_R]
                yi_ref[_across(r), :] = yi_ref[_across(r), :] + o[FFT_R:]
        _loop(fourth)

    skip = skip_ref[...]
    for half, (y_ref, x_ref) in enumerate(((yr_ref, xr_ref), (yi_ref, xi_ref))):
        g = _short_conv(g_ref[half * seq:(half + 1) * seq, :].astype(F32), cwg_ref, cbg_ref)
        for j in range(FFT_R):
            src = slice(j * FFT_PITCH, j * FFT_PITCH + FFT_R)
            dst = slice(half * seq + j * FFT_R, half * seq + (j + 1) * FFT_R)
            o_ref[dst, :] = (g[j * FFT_R:(j + 1) * FFT_R] * (y_ref[src, :] + skip * x_ref[src, :])).astype(o_ref.dtype)


def _hyconv(zsrc, zcol0, gcol0, proj, conv_w, conv_b, zpart, gpart, skip, spec, order, tabs, batch, seq, conv_z):
    t1, t2, t3, t4 = tabs
    nct = HYENA_WIDTH // LANES
    prow = FFT_R * FFT_PITCH
    const = lambda a: pl.BlockSpec(a.shape, lambda c, p: (0,) * a.ndim, pipeline_mode=pl.Buffered(1))
    return pl.pallas_call(
        functools.partial(_hyconv_kernel, conv_z=conv_z),
        out_shape=jax.ShapeDtypeStruct((batch * seq, HYENA_WIDTH), BF16),
        grid=(nct, batch // 2),
        in_specs=[
            pl.BlockSpec((2 * seq, LANES), lambda c, p: (p, zcol0 + c)),
            pl.BlockSpec((2 * seq, LANES), lambda c, p: (p, gcol0 + c)),
            pl.BlockSpec((3, LANES), lambda c, p: (0, zpart * nct + c)),
            pl.BlockSpec((1, LANES), lambda c, p: (0, zpart * nct + c)),
            pl.BlockSpec((3, LANES), lambda c, p: (0, gpart * nct + c)),
            pl.BlockSpec((1, LANES), lambda c, p: (0, gpart * nct + c)),
            pl.BlockSpec((1, LANES), lambda c, p: (0, c)),
            pl.BlockSpec((1, 4, seq, LANES), lambda c, p: (order, 0, 0, c), pipeline_mode=pl.Buffered(1)),
            const(t1), const(t2), const(t3), const(t4),
        ],
        out_specs=pl.BlockSpec((2 * seq, LANES), lambda c, p: (p, c)),
        scratch_shapes=[pltpu.VMEM((prow, LANES), F32)] * 8,
        compiler_params=_cparams(("arbitrary", "arbitrary")),
        name=f"hyconv{order}",
    )(zsrc, proj, conv_w, conv_b, conv_w, conv_b, skip[order:order + 1], spec, t1, t2, t3, t4)


def _hyena(proj, hycol0, conv_w, conv_b, w1, b1, w2, b2, w3, b3, w_out, freq, skip, batch, seq):
    hid = w2.shape[0]
    eye = jnp.eye(FILT_GROUPS, dtype=F32)
    bdiag = lambda a: jnp.kron(eye, a)
    tile = lambda a: jnp.tile(a[None], (1, FILT_GROUPS))
    bands = tile(jnp.linspace(1e-4, FILTER_BANDS - 1, FILTER_BANDS, dtype=F32))
    min_decay = math.log(DECAY_TARGET) / SLOW_DECAY_PCT
    max_decay = math.log(DECAY_TARGET) / FAST_DECAY_PCT
    deltas = jnp.abs(jnp.linspace(min_decay, max_decay, HYENA_WIDTH, dtype=F32))
    wo4 = w_out.reshape(hid, HYENA_ORDER, 2, HYENA_WIDTH)
    w1t = jnp.pad(w1[0:1], ((0, FILTER_BANDS - 1), (0, 0)))
    hs, hd = _filters(
        bands, bdiag(w1t), bdiag(w1[1:1 + FILTER_BANDS]), bdiag(w1[1 + FILTER_BANDS:]),
        tile(b1), bdiag(w2), tile(b2), bdiag(w3), tile(b3), tile(freq),
        wo4[:, :, 0].reshape(hid, -1), wo4[:, :, 1].reshape(hid, -1),
        jnp.tile(deltas, HYENA_ORDER)[None], seq)
    t1, t2, t3, t4, t1r = _dft_tables(seq)
    spec = _spectra(hs, hd, t1r, t2, seq)
    c0 = hycol0 // LANES
    nct = HYENA_WIDTH // LANES
    cb = conv_b[None]
    z1 = _hyconv(proj, c0, c0 + nct, proj, conv_w, cb, 0, 1, skip, spec, 0, (t1, t2, t3, t4), batch, seq, True)
    return _hyconv(z1, 0, c0 + 2 * nct, proj, conv_w, cb, 0, 2, skip, spec, 1, (t1, t2, t3, t4), batch, seq, False)


def _rope_tables(seq):
    pos = np.arange(seq, dtype=np.float32)
    inv_freq = (ROPE_THETA ** (-np.arange(0, ROPE_DIM, 2, dtype=np.float32) / ROPE_DIM)).astype(np.float32)
    ang = pos[:, None] * inv_freq[None, :]
    cos, sin = np.cos(ang).astype(np.float32), np.sin(ang).astype(np.float32)
    half = ROPE_DIM // 2
    c = np.ones((seq, QK_DIM), np.float32)
    s = np.zeros((seq, QK_DIM), np.float32)
    c[:, :half] = cos
    c[:, half:ROPE_DIM] = cos
    s[:, :half] = -sin
    s[:, half:ROPE_DIM] = sin
    perm = np.zeros((LANES, LANES), np.float32)
    for j in range(LANES):
        if j % QK_DIM < half:
            perm[j + half, j] = 1.0
        elif j % QK_DIM < ROPE_DIM:
            perm[j - half, j] = 1.0
    rep = LANES // QK_DIM
    return (jnp.asarray(np.tile(c, (1, rep))), jnp.asarray(np.tile(s, (1, rep))), jnp.asarray(perm).astype(BF16))


def kernel(x, norm1_g, w_in, short_conv_w, short_conv_b, q_norm_g, k_norm_g, lambda_q1, lambda_k1, lambda_q2, lambda_k2, subln_g, filt_w1, filt_b1, filt_w2, filt_b2, filt_w3, filt_b3, filt_w_out, filt_freq, hyena_skip, w_branch_attn, w_branch_hyena, w_out, norm2_g, w_router, w_gate, w_up, w_down):
    b, l, d = x.shape
    depth = w_in.shape[0]
    n = b * l
    cap = CAPACITY_FACTOR * l // N_EXPERTS
    q_cols = ATTN_HEADS * 2 * QK_DIM
    ctab, stab, perm = _rope_tables(l)
    xc = x.reshape(n, d)
    for li in range(depth):
        lambda_init = 0.8 - 0.6 * math.exp(-0.3 * li)
        proj = _inproj(xc, norm1_g[li][None], w_in[li].astype(BF16))
        scale = math.log2(math.e) / math.sqrt(QK_DIM)
        gqk = jnp.concatenate([jnp.tile(q_norm_g[li], q_cols // QK_DIM) * scale,
                               jnp.tile(k_norm_g[li], q_cols // QK_DIM)])[None]
        qk = _qkprep(proj, ctab, stab, perm, gqk, l)
        lamv = jnp.stack([lambda_q1[li], lambda_k1[li], lambda_q2[li], lambda_k2[li]])
        attn = _attention(qk, proj, lamv, subln_g[li][None], b, l, lambda_init)

        hyena = _hyena(proj, 3 * q_cols, short_conv_w[li], short_conv_b[li], filt_w1[li], filt_b1[li],
                       filt_w2[li], filt_b2[li], filt_w3[li], filt_b3[li], filt_w_out[li], filt_freq[li],
                       hyena_skip[li], b, l)

        wr_pad = jnp.pad(w_router[li], ((0, 0), (0, LANES - N_EXPERTS))).astype(BF16)
        x1, u2, aff = _merge(attn, hyena, proj, xc, w_branch_attn[li].astype(BF16),
                             w_branch_hyena[li].astype(BF16), w_out[li].astype(BF16),
                             norm2_g[li][None], wr_pad)
        pos, post, afft, offs = _topk(aff, b, l, cap)
        weo = _experts(offs, post, afft, u2, w_gate[li].astype(BF16), w_up[li].astype(BF16),
                       w_down[li].astype(BF16), b, l, cap)
        xc = _combine(offs, pos, x1, weo, b, l, cap)
    return xc.reshape(b, l, d)
```

```python
import functools
import math

import jax
import jax.numpy as jnp
import numpy as np
from jax import lax
from jax.experimental import pallas as pl
from jax.experimental.pallas import tpu as pltpu

F32 = jnp.float32
BF16 = jnp.bfloat16

ATTN_HEADS = 4
QK_DIM = 64
V_DIM = 128
ROPE_DIM = 16
ROPE_THETA = 500000.0
HYENA_WIDTH = 512
HYENA_ORDER = 2
FILTER_BANDS = 16
DECAY_TARGET = 1e-2
FAST_DECAY_PCT = 0.3
SLOW_DECAY_PCT = 1.5
N_EXPERTS = 16
CAPACITY_FACTOR = 2
MOE_TILE = 512
EPS = 1e-6
LANES = 128
FILT_GROUPS = LANES // FILTER_BANDS
VMEM_LIMIT = 56 * 1024 * 1024


def _cparams(sem):
    return pltpu.CompilerParams(dimension_semantics=sem, vmem_limit_bytes=VMEM_LIMIT)


def _inproj_kernel(x_ref, g_ref, w_ref, o_ref, u_ref):
    @pl.when(pl.program_id(1) == 0)
    def _():
        x = x_ref[...]
        ms = jnp.mean(x * x, axis=-1, keepdims=True)
        u_ref[...] = (x * lax.rsqrt(ms + EPS) * g_ref[...]).astype(BF16)

    o_ref[...] = jnp.dot(u_ref[...], w_ref[...], preferred_element_type=F32).astype(o_ref.dtype)


def _inproj(x2, g, w_bf16, tm=1024, tn=1280):
    n, d = x2.shape
    cols = w_bf16.shape[1]
    return pl.pallas_call(
        _inproj_kernel,
        out_shape=jax.ShapeDtypeStruct((n, cols), BF16),
        grid=(n // tm, cols // tn),
        in_specs=[
            pl.BlockSpec((tm, d), lambda i, j: (i, 0)),
            pl.BlockSpec((1, d), lambda i, j: (0, 0)),
            pl.BlockSpec((d, tn), lambda i, j: (0, j)),
        ],
        out_specs=pl.BlockSpec((tm, tn), lambda i, j: (i, j)),
        scratch_shapes=[pltpu.VMEM((tm, d), BF16)],
        compiler_params=_cparams(("arbitrary", "arbitrary")),
        name="inproj",
    )(x2, g, w_bf16)


def _qkprep_kernel(p_ref, c_ref, s_ref, perm_ref, g_ref, o_ref):
    lane = lax.broadcasted_iota(jnp.int32, (1, LANES), 1)
    lo = lane < QK_DIM
    c = c_ref[...]
    s = s_ref[...]
    perm = perm_ref[...]
    for j in range(p_ref.shape[1] // LANES):
        t = p_ref[:, j * LANES:(j + 1) * LANES].astype(F32)
        sq = t * t
        ss_lo = jnp.sum(jnp.where(lo, sq, 0.0), axis=-1, keepdims=True)
        ss_hi = jnp.sum(jnp.where(lo, 0.0, sq), axis=-1, keepdims=True)
        r = lax.rsqrt(jnp.where(lo, ss_lo, ss_hi) * (1.0 / QK_DIM) + EPS)
        y = t * r * g_ref[:, j * LANES:(j + 1) * LANES]
        y_hi = y.astype(BF16)
        y_lo = (y - y_hi.astype(F32)).astype(BF16)
        partner = (jnp.dot(y_hi, perm, preferred_element_type=F32)
                   + jnp.dot(y_lo, perm, preferred_element_type=F32))
        o_ref[:, j * LANES:(j + 1) * LANES] = (y * c + partner * s).astype(o_ref.dtype)


def _qkprep(proj, ctab, stab, perm, gqk, seq, tm=512):
    n = proj.shape[0]
    w = gqk.shape[1]
    nb = seq // tm
    return pl.pallas_call(
        _qkprep_kernel,
        out_shape=jax.ShapeDtypeStruct((n, w), BF16),
        grid=(n // tm,),
        in_specs=[
            pl.BlockSpec((tm, w), lambda i: (i, 0)),
            pl.BlockSpec((tm, LANES), lambda i: (i % nb, 0)),
            pl.BlockSpec((tm, LANES), lambda i: (i % nb, 0)),
            pl.BlockSpec((LANES, LANES), lambda i: (0, 0)),
            pl.BlockSpec((1, w), lambda i: (0, 0)),
        ],
        out_specs=pl.BlockSpec((tm, w), lambda i: (i, 0)),
        compiler_params=_cparams(("arbitrary",)),
        name="qkprep",
    )(proj, ctab, stab, perm, gqk)


def _attn_kernel(q_ref, k_ref, v_ref, lam_ref, g_ref, o_ref, *, tk, unroll, lambda_init):
    tq = q_ref.shape[0]
    seq = k_ref.shape[0]
    lane = lax.broadcasted_iota(jnp.int32, (1, LANES), 1)
    q = q_ref[...]
    zero = jnp.zeros_like(q)
    qs = jnp.concatenate([jnp.where(lane < QK_DIM, q, zero), jnp.where(lane < QK_DIM, zero, q)], axis=0)
    lv = lam_ref[...]
    lam = (jnp.exp(jnp.sum(lv[0:1] * lv[1:2], axis=-1, keepdims=True))
           - jnp.exp(jnp.sum(lv[2:3] * lv[3:4], axis=-1, keepdims=True)) + lambda_init)

    def step(c, carry):
        m, l, a = carry
        off = pl.multiple_of(c * tk, tk)
        kc = k_ref[pl.ds(off, tk), :]
        vc = v_ref[pl.ds(off, tk), :]
        s = lax.dot_general(qs, kc, (((1,), (1,)), ((), ())), preferred_element_type=F32)
        mn = jnp.maximum(m, jnp.max(s, axis=-1, keepdims=True))
        p = jnp.exp2(s - mn)
        al = jnp.exp2(m - mn)
        l = al * l + jnp.sum(p, axis=-1, keepdims=True)
        a = al * a + jnp.dot(p.astype(BF16), vc, preferred_element_type=F32)
        return mn, l, a

    init = (jnp.full((2 * tq, 1), -jnp.inf, F32), jnp.zeros((2 * tq, 1), F32), jnp.zeros((2 * tq, V_DIM), F32))
    _, l, a = lax.fori_loop(0, seq // tk, step, init, unroll=unroll)
    a = a / l
    o = a[:tq] - lam * a[tq:]
    ms = jnp.mean(o * o, axis=-1, keepdims=True)
    o = o * lax.rsqrt(ms + EPS) * g_ref[...] * (1.0 - lambda_init)
    o_ref[...] = o.astype(o_ref.dtype)


def _attention(qk, proj, lamv, subln_g, batch, seq, lambda_init, tq=512, tk=1024, unroll=4):
    nq = seq // tq
    vcol0 = (2 * ATTN_HEADS * 2 * QK_DIM) // V_DIM
    return pl.pallas_call(
        functools.partial(_attn_kernel, tk=tk, unroll=unroll, lambda_init=lambda_init),
        out_shape=jax.ShapeDtypeStruct((batch * seq, ATTN_HEADS * V_DIM), BF16),
        grid=(batch, ATTN_HEADS, nq),
        in_specs=[
            pl.BlockSpec((tq, LANES), lambda b, h, i: (b * nq + i, h)),
            pl.BlockSpec((seq, LANES), lambda b, h, i: (b, ATTN_HEADS + h)),
            pl.BlockSpec((seq, V_DIM), lambda b, h, i: (b, vcol0 + h)),
            pl.BlockSpec((4, QK_DIM), lambda b, h, i: (0, 0)),
            pl.BlockSpec((1, V_DIM), lambda b, h, i: (0, 0)),
        ],
        out_specs=pl.BlockSpec((tq, V_DIM), lambda b, h, i: (b * nq + i, h)),
        compiler_params=_cparams(("arbitrary", "arbitrary", "arbitrary")),
        name="diffattn",
    )(qk, qk, proj, lamv, subln_g)


def _merge_kernel(at_ref, hy_ref, ga_ref, gh_ref, x_ref, wpa_ref, wph_ref, wo_ref, g2_ref, wr_ref,
                  x1_ref, u2_ref, aff_ref):
    ga = jax.nn.sigmoid(ga_ref[...].astype(F32))
    gh = jax.nn.sigmoid(gh_ref[...].astype(F32))
    merged = (ga * jnp.dot(at_ref[...], wpa_ref[...], preferred_element_type=F32)
              + gh * jnp.dot(hy_ref[...], wph_ref[...], preferred_element_type=F32))
    x1 = x_ref[...] + jnp.dot(merged.astype(BF16), wo_ref[...], preferred_element_type=F32)
    x1_ref[...] = x1
    ms = jnp.mean(x1 * x1, axis=-1, keepdims=True)
    u2 = (x1 * lax.rsqrt(ms + EPS) * g2_ref[...]).astype(BF16)
    u2_ref[...] = u2
    logits = jnp.dot(u2, wr_ref[...], preferred_element_type=F32)
    lane = lax.broadcasted_iota(jnp.int32, logits.shape, 1)
    logits = jnp.where(lane < N_EXPERTS, logits, -jnp.inf)
    e = jnp.exp(logits - jnp.max(logits, axis=-1, keepdims=True))
    aff_ref[...] = e / jnp.sum(e, axis=-1, keepdims=True)


def _merge(attn, hyena, proj, x2, wpa, wph, wo, g2, wr_pad, tm=512):
    n, d = x2.shape
    gcol0 = (proj.shape[1] - 2 * d) // d
    full = lambda shape: pl.BlockSpec(shape, lambda i: (0, 0))
    return pl.pallas_call(
        _merge_kernel,
        out_shape=(jax.ShapeDtypeStruct((n, d), F32), jax.ShapeDtypeStruct((n, d), BF16),
                   jax.ShapeDtypeStruct((n, LANES), F32)),
        grid=(n // tm,),
        in_specs=[
            pl.BlockSpec((tm, attn.shape[1]), lambda i: (i, 0)),
            pl.BlockSpec((tm, hyena.shape[1]), lambda i: (i, 0)),
            pl.BlockSpec((tm, d), lambda i: (i, gcol0)),
            pl.BlockSpec((tm, d), lambda i: (i, gcol0 + 1)),
            pl.BlockSpec((tm, d), lambda i: (i, 0)),
            full(wpa.shape), full(wph.shape), full(wo.shape), full(g2.shape), full(wr_pad.shape),
        ],
        out_specs=(pl.BlockSpec((tm, d), lambda i: (i, 0)), pl.BlockSpec((tm, d), lambda i: (i, 0)),
                   pl.BlockSpec((tm, LANES), lambda i: (i, 0))),
        compiler_params=_cparams(("arbitrary",)),
        name="merge",
    )(attn, hyena, proj, proj, x2, wpa, wph, wo, g2, wr_pad)


def _excl_cumsum_rows(mask_f32, tri, blk):
    rows = mask_f32.shape[0]
    carry = jnp.zeros((1, LANES), F32)
    outs = []
    for r in range(rows // blk):
        mb = mask_f32[r * blk:(r + 1) * blk]
        outs.append(jnp.dot(tri, mb.astype(BF16), preferred_element_type=F32) + carry)
        carry = carry + jnp.sum(mb, axis=0, keepdims=True)
    return jnp.concatenate(outs, axis=0)


def _topk_kernel(aff_ref, pos_ref, post_ref, afft_ref, offs_ref, *, cap, blk, batch):
    seq = aff_ref.shape[0] // batch
    aff = aff_ref[0:seq]
    for b in range(1, batch):
        aff = aff + pltpu.roll(aff_ref[b * seq:(b + 1) * seq], b * N_EXPERTS, 1)

    def search(i, prefix):
        cand = prefix | jnp.left_shift(jnp.int32(1), 29 - i)
        cnt = jnp.sum((aff >= pltpu.bitcast(cand, F32)[0:1]).astype(F32), axis=0, keepdims=True)
        return jnp.where(cnt >= cap, cand, prefix)

    lo = pltpu.bitcast(lax.fori_loop(0, 30, search, jnp.zeros((8, LANES), jnp.int32)), F32)[0:1]
    thr = jnp.min(jnp.where(aff >= lo, aff, jnp.inf), axis=0, keepdims=True)
    gt = (aff > thr).astype(F32)
    eq = (aff == thr).astype(F32)
    need = cap - jnp.sum(gt, axis=0, keepdims=True)
    ri = lax.broadcasted_iota(jnp.int32, (blk, blk), 0)
    ci = lax.broadcasted_iota(jnp.int32, (blk, blk), 1)
    tri = (ci < ri).astype(BF16)
    sel = gt + eq * (_excl_cumsum_rows(eq, tri, blk) < need).astype(F32)
    before = _excl_cumsum_rows(sel, tri, blk)
    pos = jnp.where(sel > 0.0, before, -1.0)
    pos_t = pos.T
    aff_t = aff.T
    lane = lax.broadcasted_iota(jnp.int32, (1, LANES), 1)
    for b in range(batch):
        mine = pos if b == 0 else pltpu.roll(pos, LANES - b * N_EXPERTS, 1)
        pos_ref[b * seq:(b + 1) * seq] = jnp.where(lane < N_EXPERTS, mine, -1.0)
        post_ref[b] = pos_t[b * N_EXPERTS:(b + 1) * N_EXPERTS]
        afft_ref[b] = aff_t[b * N_EXPERTS:(b + 1) * N_EXPERTS]
    offs_ref[...] = jnp.concatenate([before[j * MOE_TILE:j * MOE_TILE + 1] for j in range(offs_ref.shape[0])], axis=0)


def _topk(aff, batch, seq, cap, blk=256):
    assert batch * N_EXPERTS <= LANES
    nt = seq // MOE_TILE
    pos, post, afft, offs = pl.pallas_call(
        functools.partial(_topk_kernel, cap=cap, blk=blk, batch=batch),
        out_shape=(jax.ShapeDtypeStruct((batch * seq, LANES), F32),
                   jax.ShapeDtypeStruct((batch, N_EXPERTS, seq), F32),
                   jax.ShapeDtypeStruct((batch, N_EXPERTS, seq), F32),
                   jax.ShapeDtypeStruct((nt, LANES), F32)),
        compiler_params=pltpu.CompilerParams(vmem_limit_bytes=VMEM_LIMIT),
        name="topk",
    )(aff)
    offs = offs[:, :batch * N_EXPERTS].reshape(nt, batch, N_EXPERTS).transpose(1, 0, 2)
    return pos, post, afft, offs.astype(jnp.int32).reshape(-1)


def _align_down(x, m):
    sh = m.bit_length() - 1
    return lax.shift_left(lax.shift_right_logical(x, sh), sh)


def _ceil_div(x, m):
    return lax.shift_right_logical(x + (m - 1), m.bit_length() - 1)


def _expert_kernel(offs_ref, post_ref, afft_ref, u_ref, wg_ref, wu_ref, wd_ref, o_ref, xin_ref, gs_ref, w_ref,
                   *, cap, win):
    e = pl.program_id(0)
    b = pl.program_id(1)

    @pl.when(e < N_EXPERTS)
    def _():
        for m, src in enumerate((wg_ref, wu_ref, wd_ref)):
            part = src.shape[2]
            w_ref[jnp.bitwise_and(e, 1), m, pl.ds(pl.multiple_of(b * part, part), part), :] = src[0, 0].astype(BF16)

    @pl.when(e == 0)
    def _():
        o_ref[...] = jnp.zeros_like(o_ref)

    @pl.when(e > 0)
    def _():
        _expert_body(offs_ref, post_ref, afft_ref, u_ref, w_ref.at[1 - jnp.bitwise_and(e, 1)], o_ref, xin_ref, gs_ref,
                     e - 1, b, cap, win)


def _expert_body(offs_ref, post_ref, afft_ref, u_ref, w_ref, o_ref, xin_ref, gs_ref, e, b, cap, win):
    seq = u_ref.shape[0]
    nt = seq // MOE_TILE
    xin_ref[...] = jnp.zeros_like(xin_ref)
    gs_ref[...] = jnp.zeros_like(gs_ref)
    slot0 = lax.broadcasted_iota(jnp.int32, (win, MOE_TILE), 0).astype(F32)
    for c in range(nt):
        base = (b * nt + c) * N_EXPERTS + e
        lo0 = _align_down(offs_ref[base], 8)
        end = offs_ref[base + N_EXPERTS] if c + 1 < nt else cap
        prow = post_ref[0, pl.ds(e, 1), c * MOE_TILE:(c + 1) * MOE_TILE]
        arow = afft_ref[0, pl.ds(e, 1), c * MOE_TILE:(c + 1) * MOE_TILE]
        uc = u_ref[c * MOE_TILE:(c + 1) * MOE_TILE, :]

        def window(k, carry, lo0=lo0, prow=prow, arow=arow, uc=uc):
            lo = lo0 + k * win
            s = pl.multiple_of(jnp.minimum(lo, cap - win), 8)
            srow = slot0 + s.astype(F32)
            hit = jnp.logical_and(srow == prow, srow >= lo.astype(F32))
            rows = pl.ds(s, win)
            gs_ref[rows, :] += jnp.sum(jnp.where(hit, arow, 0.0), axis=-1, keepdims=True)
            xin_ref[rows, :] += jnp.dot(jnp.where(hit, 1.0, 0.0).astype(BF16), uc, preferred_element_type=F32)
            return carry

        window(0, 0)
        lax.fori_loop(1, _ceil_div(end - lo0, win), window, 0)
    xb = xin_ref[...].astype(BF16)
    hg = jnp.dot(xb, w_ref[0], preferred_element_type=F32)
    hu = jnp.dot(xb, w_ref[1], preferred_element_type=F32)
    act = (hg * jax.nn.sigmoid(hg) * hu).astype(BF16)
    eo = jnp.dot(act, w_ref[2], preferred_element_type=F32)
    o_ref[0, 0] = (eo * gs_ref[...]).astype(o_ref.dtype)


def _experts(offs, post, afft, u2, wg, wu, wd, layer, batch, seq, cap, win=128):
    d = u2.shape[1]
    assert wg.shape[2:] == (d, d) and wd.shape[2:] == (d, d) and d % (16 * batch) == 0
    part = d // batch
    wspec = pl.BlockSpec((1, 1, part, d), lambda e, b, o: (layer, jnp.minimum(e, N_EXPERTS - 1), b, 0))
    grid_spec = pltpu.PrefetchScalarGridSpec(
        num_scalar_prefetch=1,
        grid=(N_EXPERTS + 1, batch),
        in_specs=[
            pl.BlockSpec((1, N_EXPERTS, seq), lambda e, b, o: (b, 0, 0)),
            pl.BlockSpec((1, N_EXPERTS, seq), lambda e, b, o: (b, 0, 0)),
            pl.BlockSpec((seq, d), lambda e, b, o: (b, 0)),
            wspec, wspec, wspec,
        ],
        out_specs=pl.BlockSpec((1, 1, cap, d), lambda e, b, o: (b, jnp.where(e == 0, N_EXPERTS, e - 1), 0, 0)),
        scratch_shapes=[pltpu.VMEM((cap, d), F32), pltpu.VMEM((cap, 1), F32), pltpu.VMEM((2, 3, d, d), BF16)],
    )
    return pl.pallas_call(
        functools.partial(_expert_kernel, cap=cap, win=win),
        out_shape=jax.ShapeDtypeStruct((batch, N_EXPERTS + 1, cap, d), BF16),
        grid_spec=grid_spec,
        compiler_params=_cparams(("arbitrary", "arbitrary")),
        name="experts",
    )(offs, post, afft, u2, wg, wu, wd)


def _combine_kernel(offs_ref, pos_ref, x1_ref, weo_ref, o_ref, *, cap, win):
    b = pl.program_id(0)
    t = pl.program_id(1)
    nt = pl.num_programs(1)
    tt = pos_ref.shape[0]
    pos = pos_ref[...]
    slot0 = lax.broadcasted_iota(jnp.int32, (tt, win), 1).astype(F32)
    base = (b * nt + t) * N_EXPERTS
    last = pl.num_programs(0) * nt * N_EXPERTS - 1
    acc = x1_ref[...]
    for e0 in range(0, N_EXPERTS, 2):
        hots, rows = [], []
        for e in (e0, e0 + 1):
            s = pl.multiple_of(jnp.minimum(_align_down(offs_ref[base + e], 16), cap - win), 16)
            hots.append(jnp.where((slot0 + s.astype(F32)) == pos[:, e:e + 1], 1.0, 0.0).astype(BF16))
            rows.append(weo_ref[0, e, pl.ds(s, win), :])
        acc = acc + jnp.dot(jnp.concatenate(hots, axis=1), jnp.concatenate(rows, axis=0),
                            preferred_element_type=F32)
    o_ref[...] = acc
    for e in range(N_EXPERTS):
        lo0 = _align_down(offs_ref[base + e], 16)
        end = jnp.where(t + 1 < nt, offs_ref[jnp.minimum(base + N_EXPERTS + e, last)], cap)

        def window(k, carry, lo0=lo0, e=e):
            lo = lo0 + k * win
            s = pl.multiple_of(jnp.minimum(lo, cap - win), 16)
            srow = slot0 + s.astype(F32)
            hit = jnp.logical_and(srow == pos[:, e:e + 1], srow >= lo.astype(F32))
            o_ref[...] += jnp.dot(jnp.where(hit, 1.0, 0.0).astype(BF16), weo_ref[0, e, pl.ds(s, win), :],
                                  preferred_element_type=F32)
            return carry

        lax.fori_loop(1, _ceil_div(end - lo0, win), window, 0)


def _combine(offs, pos, x1, weo, batch, seq, cap, win=128):
    d = x1.shape[1]
    tt = MOE_TILE
    nt = seq // tt
    grid_spec = pltpu.PrefetchScalarGridSpec(
        num_scalar_prefetch=1,
        grid=(batch, nt),
        in_specs=[
            pl.BlockSpec((tt, LANES), lambda b, t, o: (b * nt + t, 0)),
            pl.BlockSpec((tt, d), lambda b, t, o: (b * nt + t, 0)),
            pl.BlockSpec((1, N_EXPERTS, cap, d), lambda b, t, o: (b, 0, 0, 0)),
        ],
        out_specs=pl.BlockSpec((tt, d), lambda b, t, o: (b * nt + t, 0)),
    )
    return pl.pallas_call(
        functools.partial(_combine_kernel, cap=cap, win=win),
        out_shape=jax.ShapeDtypeStruct(x1.shape, F32),
        grid_spec=grid_spec,
        compiler_params=_cparams(("arbitrary", "arbitrary")),
        name="combine",
    )(offs, pos, x1, weo)


FFT_R = 64
FFT_PITCH = FFT_R + 8


def _dft_tables(seq):
    r = FFT_R
    assert seq == r * r
    n = np.arange(r)
    f = np.exp(-2j * np.pi * np.outer(n, n) / r)
    w = lambda e: np.exp(-2j * np.pi * e / (2 * seq))

    def real_rep(a):
        return np.block([[a.real, -a.imag], [a.imag, a.real]]).astype(np.float32)

    m1 = [f, f * w(r * n)[None, :]]
    m2 = [f * w(n * m)[None, :] for m in range(2 * r)]
    m4 = [f.conj() / (2 * seq), w(-r * n)[:, None] * f.conj() / (2 * seq)]
    t1 = np.stack([real_rep(a) for a in m1])
    t2 = np.stack([real_rep(a) for a in m2])
    t3 = np.stack([real_rep(a.conj().T) for a in m2])
    t4 = np.stack([real_rep(a) for a in m4])
    t1r = np.stack([np.concatenate([a.real, a.imag], axis=0).astype(np.float32) for a in m1])
    return tuple(jnp.asarray(t).astype(BF16) for t in (t1, t2, t3, t4, t1r))


def _slab(r):
    return pl.ds(pl.multiple_of(r * FFT_PITCH, 8), FFT_R)


def _across(r):
    return pl.ds(r, FFT_R, stride=FFT_PITCH)


def _cat_bf16(a, b):
    return jnp.concatenate([a, b], axis=0).astype(BF16)


FFT_UNROLL = 32


def _loop(body, step=1):
    def wrapped(i, c):
        body(i * step)
        return c
    lax.fori_loop(0, FFT_R // step, wrapped, 0, unroll=FFT_UNROLL // step)


def _pair_bf16(re_ref, im_ref, idx0, idx1):
    return jnp.concatenate([_cat_bf16(re_ref[idx0, :], im_ref[idx0, :]),
                            _cat_bf16(re_ref[idx1, :], im_ref[idx1, :])], axis=1)


def _bf16_pieces(w):
    hi = w.astype(BF16)
    return hi, (w - hi.astype(F32)).astype(BF16)


def _dot3(a, w_ref):
    a_hi, a_lo = _bf16_pieces(a)
    return (jnp.dot(a_hi, w_ref[0], preferred_element_type=F32)
            + (jnp.dot(a_hi, w_ref[1], preferred_element_type=F32)
               + jnp.dot(a_lo, w_ref[0], preferred_element_type=F32)))


def _filter_kernel(bands_ref, w1t_ref, w1c_ref, w1s_ref, b1_ref, w2_ref, b2_ref, w3_ref, b3_ref, fr_ref,
                   wf_ref, wb_ref, dl_ref, hs_ref, hd_ref, *, seq):
    tm = hs_ref.shape[0]
    rows = tm // FILT_GROUPS
    hid = wf_ref.shape[1]
    base = pl.program_id(0) * tm
    lane = lax.broadcasted_iota(jnp.int32, (rows, LANES), 1)
    grp = lax.shift_right_logical(lane, FILTER_BANDS.bit_length() - 1)
    pos = (base + grp * rows + lax.broadcasted_iota(jnp.int32, (rows, LANES), 0)).astype(F32)
    first = jnp.bitwise_and(lane, FILTER_BANDS - 1) == 0
    fr = fr_ref[...]
    tscale = 1.0 / max(seq - 1, 1)

    def mlp(p):
        ang = ((2.0 * math.pi / seq) * p) * bands_ref[...]
        pre = (_dot3(jnp.cos(ang), w1c_ref) + _dot3(-jnp.sin(ang), w1s_ref)
               + _dot3(jnp.where(first, p * tscale, 0.0), w1t_ref) + b1_ref[...])
        h = jnp.sin(fr * pre)
        h = jnp.sin(fr * (_dot3(h, w2_ref) + b2_ref[...]))
        return jnp.sin(fr * (_dot3(h, w3_ref) + b3_ref[...]))

    hf_h = mlp(pos)
    hb_h = mlp(seq - pos)
    dl = dl_ref[...]
    for g in range(FILT_GROUPS):
        pcol = (base + g * rows + lax.broadcasted_iota(jnp.int32, (rows, 1), 0)).astype(F32)
        hf = _dot3(hf_h[:, g * hid:(g + 1) * hid], wf_ref) * jnp.exp(-(pcol * tscale) * dl)
        hb = _dot3(hb_h[:, g * hid:(g + 1) * hid], wb_ref) * jnp.exp(-((seq - pcol) * tscale) * dl)
        hb = jnp.where(pcol > 0.0, hb, 0.0)
        hs_ref[g * rows:(g + 1) * rows, :] = hf + hb
        hd_ref[g * rows:(g + 1) * rows, :] = hf - hb


def _filters(bands, w1t, w1c, w1s, b1, w2, b2, w3, b3, fr, wf, wb, dl, seq, tm=512):
    cols = wf.shape[1]
    pieces = lambda w: jnp.stack(_bf16_pieces(w))
    args = (bands, pieces(w1t), pieces(w1c), pieces(w1s), b1, pieces(w2), b2, pieces(w3), b3, fr,
            pieces(wf), pieces(wb), dl)
    return pl.pallas_call(
        functools.partial(_filter_kernel, seq=seq),
        out_shape=(jax.ShapeDtypeStruct((seq, cols), F32), jax.ShapeDtypeStruct((seq, cols), F32)),
        grid=(seq // tm,),
        in_specs=[pl.BlockSpec(a.shape, lambda i, nd=a.ndim: (0,) * nd) for a in args],
        out_specs=(pl.BlockSpec((tm, cols), lambda i: (i, 0)), pl.BlockSpec((tm, cols), lambda i: (i, 0))),
        compiler_params=_cparams(("arbitrary",)),
        name="hyfilter",
    )(*args)


def _spectra_kernel(hs_ref, hd_ref, t1r_ref, t2_ref, h_ref, x_ref, pr_ref, pi_ref):
    for par, src in ((0, hs_ref), (1, hd_ref)):
        def fill(r, src=src):
            x_ref[_slab(r), :] = src[pl.ds(pl.multiple_of(r * FFT_R, FFT_R), FFT_R), :]
        _loop(fill)

        def first(r, par=par):
            o = jnp.dot(t1r_ref[par], x_ref[_across(r), :].astype(BF16), preferred_element_type=F32)
            pr_ref[_slab(r), :] = o[:FFT_R]
            pi_ref[_slab(r), :] = o[FFT_R:]
        _loop(first)

        def second(r, par=par):
            o = jnp.dot(t2_ref[2 * r + par], _cat_bf16(pr_ref[_across(r), :], pi_ref[_across(r), :]),
                        preferred_element_type=F32)
            rows = pl.ds(pl.multiple_of(r * FFT_R, FFT_R), FFT_R)
            h_ref[0, 2 * par, rows, :] = o[:FFT_R].astype(h_ref.dtype)
            h_ref[0, 2 * par + 1, rows, :] = o[FFT_R:].astype(h_ref.dtype)
        _loop(second)


def _spectra(hs, hd, t1r, t2, seq):
    ncol = hs.shape[1] // LANES
    nct = ncol // HYENA_ORDER
    prow = FFT_R * FFT_PITCH
    const = lambda a: pl.BlockSpec(a.shape, lambda o, c: (0,) * a.ndim, pipeline_mode=pl.Buffered(1))
    return pl.pallas_call(
        _spectra_kernel,
        out_shape=jax.ShapeDtypeStruct((HYENA_ORDER, 4, seq, nct * LANES), BF16),
        grid=(HYENA_ORDER, nct),
        in_specs=[pl.BlockSpec((seq, LANES), lambda o, c: (0, o * nct + c)),
                  pl.BlockSpec((seq, LANES), lambda o, c: (0, o * nct + c)),
                  const(t1r), const(t2)],
        out_specs=pl.BlockSpec((1, 4, seq, LANES), lambda o, c: (o, 0, 0, c)),
        scratch_shapes=[pltpu.VMEM((prow, LANES), F32)] * 3,
        compiler_params=_cparams(("arbitrary", "arbitrary")),
        name="hyspectra",
    )(hs, hd, t1r, t2)


def _short_conv(x, w_ref, b_ref):
    rows = x.shape[0]
    ri = lax.broadcasted_iota(jnp.int32, x.shape, 0)
    prev = jnp.where(ri == 0, 0.0, pltpu.roll(x, 1, 0))
    nxt = jnp.where(ri == rows - 1, 0.0, pltpu.roll(x, rows - 1, 0))
    return prev * w_ref[0:1, :] + x * w_ref[1:2, :] + nxt * w_ref[2:3, :] + b_ref[...]


def _hyconv_kernel(z_ref, g_ref, cwz_ref, cbz_ref, cwg_ref, cbg_ref, skip_ref, h_ref,
                   t1_ref, t2_ref, t3_ref, t4_ref, o_ref,
                   xr_ref, xi_ref, pr_ref, pi_ref, qr_ref, qi_ref, yr_ref, yi_ref, *, conv_z):
    seq = z_ref.shape[0] // 2
    for half, dst in ((0, xr_ref), (1, xi_ref)):
        z = z_ref[half * seq:(half + 1) * seq, :].astype(F32)
        if conv_z:
            z = _short_conv(z, cwz_ref, cbz_ref)
        for j in range(FFT_R):
            dst[j * FFT_PITCH:j * FFT_PITCH + FFT_R, :] = z[j * FFT_R:(j + 1) * FFT_R]

    for par in (0, 1):
        def first(r, par=par):
            o = jnp.dot(t1_ref[par], _pair_bf16(xr_ref, xi_ref, _across(r), _across(r + 1)),
                        preferred_element_type=F32)
            pr_ref[_slab(r), :] = o[:FFT_R, :LANES]
            pi_ref[_slab(r), :] = o[FFT_R:, :LANES]
            pr_ref[_slab(r + 1), :] = o[:FFT_R, LANES:]
            pi_ref[_slab(r + 1), :] = o[FFT_R:, LANES:]
        _loop(first, step=2)

        def second(r, par=par):
            o = jnp.dot(t2_ref[2 * r + par], _cat_bf16(pr_ref[_across(r), :], pi_ref[_across(r), :]),
                        preferred_element_type=F32)
            rows = pl.ds(pl.multiple_of(r * FFT_R, FFT_R), FFT_R)
            hr = h_ref[0, 2 * par, rows, :].astype(F32)
            hi = h_ref[0, 2 * par + 1, rows, :].astype(F32)
            ar, ai = o[:FFT_R], o[FFT_R:]
            qr_ref[_slab(r), :] = ar * hr - ai * hi
            qi_ref[_slab(r), :] = ar * hi + ai * hr
        _loop(second)

        def third(r, par=par):
            o = jnp.dot(t3_ref[2 * r + par], _cat_bf16(qr_ref[_slab(r), :], qi_ref[_slab(r), :]),
                        preferred_element_type=F32)
            pr_ref[_across(r), :] = o[:FFT_R]
            pi_ref[_across(r), :] = o[FFT_R:]
        _loop(third)

        def fourth(r, par=par):
            o = jnp.dot(t4_ref[par], _pair_bf16(pr_ref, pi_ref, _slab(r), _slab(r + 1)),
                        preferred_element_type=F32)
            for k, cols in ((r, slice(0, LANES)), (r + 1, slice(LANES, 2 * LANES))):
                if par == 0:
                    yr_ref[_across(k), :] = o[:FFT_R, cols]
                    yi_ref[_across(k), :] = o[FFT_R:, cols]
                else:
                    yr_ref[_across(k), :] = yr_ref[_across(k), :] + o[:FFT_R, cols]
                    yi_ref[_across(k), :] = yi_ref[_across(k), :] + o[FFT_R:, cols]
        _loop(fourth, step=2)

    skip = skip_ref[...]
    for half, (y_ref, x_ref) in enumerate(((yr_ref, xr_ref), (yi_ref, xi_ref))):
        g = _short_conv(g_ref[half * seq:(half + 1) * seq, :].astype(F32), cwg_ref, cbg_ref)
        for j in range(FFT_R):
            src = slice(j * FFT_PITCH, j * FFT_PITCH + FFT_R)
            dst = slice(half * seq + j * FFT_R, half * seq + (j + 1) * FFT_R)
            o_ref[dst, :] = (g[j * FFT_R:(j + 1) * FFT_R] * (y_ref[src, :] + skip * x_ref[src, :])).astype(o_ref.dtype)


def _hyconv(zsrc, zcol0, gcol0, proj, conv_w, conv_b, zpart, gpart, skip, spec, order, tabs, batch, seq, conv_z):
    t1, t2, t3, t4 = tabs
    nct = HYENA_WIDTH // LANES
    prow = FFT_R * FFT_PITCH
    const = lambda a: pl.BlockSpec(a.shape, lambda c, p: (0,) * a.ndim, pipeline_mode=pl.Buffered(1))
    return pl.pallas_call(
        functools.partial(_hyconv_kernel, conv_z=conv_z),
        out_shape=jax.ShapeDtypeStruct((batch * seq, HYENA_WIDTH), BF16),
        grid=(nct, batch // 2),
        in_specs=[
            pl.BlockSpec((2 * seq, LANES), lambda c, p: (p, zcol0 + c)),
            pl.BlockSpec((2 * seq, LANES), lambda c, p: (p, gcol0 + c)),
            pl.BlockSpec((3, LANES), lambda c, p: (0, zpart * nct + c)),
            pl.BlockSpec((1, LANES), lambda c, p: (0, zpart * nct + c)),
            pl.BlockSpec((3, LANES), lambda c, p: (0, gpart * nct + c)),
            pl.BlockSpec((1, LANES), lambda c, p: (0, gpart * nct + c)),
            pl.BlockSpec((1, LANES), lambda c, p: (0, c)),
            pl.BlockSpec((1, 4, seq, LANES), lambda c, p: (order, 0, 0, c), pipeline_mode=pl.Buffered(1)),
            const(t1), const(t2), const(t3), const(t4),
        ],
        out_specs=pl.BlockSpec((2 * seq, LANES), lambda c, p: (p, c)),
        scratch_shapes=[pltpu.VMEM((prow, LANES), F32)] * 8,
        compiler_params=_cparams(("arbitrary", "arbitrary")),
        name=f"hyconv{order}",
    )(zsrc, proj, conv_w, conv_b, conv_w, conv_b, skip[order:order + 1], spec, t1, t2, t3, t4)


def _hyena(proj, hycol0, conv_w, conv_b, w1, b1, w2, b2, w3, b3, w_out, freq, skip, batch, seq):
    hid = w2.shape[0]
    eye = jnp.eye(FILT_GROUPS, dtype=F32)
    bdiag = lambda a: jnp.kron(eye, a)
    tile = lambda a: jnp.tile(a[None], (1, FILT_GROUPS))
    bands = tile(jnp.linspace(1e-4, FILTER_BANDS - 1, FILTER_BANDS, dtype=F32))
    min_decay = math.log(DECAY_TARGET) / SLOW_DECAY_PCT
    max_decay = math.log(DECAY_TARGET) / FAST_DECAY_PCT
    deltas = jnp.abs(jnp.linspace(min_decay, max_decay, HYENA_WIDTH, dtype=F32))
    wo4 = w_out.reshape(hid, HYENA_ORDER, 2, HYENA_WIDTH)
    w1t = jnp.pad(w1[0:1], ((0, FILTER_BANDS - 1), (0, 0)))
    hs, hd = _filters(
        bands, bdiag(w1t), bdiag(w1[1:1 + FILTER_BANDS]), bdiag(w1[1 + FILTER_BANDS:]),
        tile(b1), bdiag(w2), tile(b2), bdiag(w3), tile(b3), tile(freq),
        wo4[:, :, 0].reshape(hid, -1), wo4[:, :, 1].reshape(hid, -1),
        jnp.tile(deltas, HYENA_ORDER)[None], seq)
    t1, t2, t3, t4, t1r = _dft_tables(seq)
    spec = _spectra(hs, hd, t1r, t2, seq)
    c0 = hycol0 // LANES
    nct = HYENA_WIDTH // LANES
    cb = conv_b[None]
    z1 = _hyconv(proj, c0, c0 + nct, proj, conv_w, cb, 0, 1, skip, spec, 0, (t1, t2, t3, t4), batch, seq, True)
    return _hyconv(z1, 0, c0 + 2 * nct, proj, conv_w, cb, 0, 2, skip, spec, 1, (t1, t2, t3, t4), batch, seq, False)


def _rope_tables(seq):
    pos = np.arange(seq, dtype=np.float32)
    inv_freq = (ROPE_THETA ** (-np.arange(0, ROPE_DIM, 2, dtype=np.float32) / ROPE_DIM)).astype(np.float32)
    ang = pos[:, None] * inv_freq[None, :]
    cos, sin = np.cos(ang).astype(np.float32), np.sin(ang).astype(np.float32)
    half = ROPE_DIM // 2
    c = np.ones((seq, QK_DIM), np.float32)
    s = np.zeros((seq, QK_DIM), np.float32)
    c[:, :half] = cos
    c[:, half:ROPE_DIM] = cos
    s[:, :half] = -sin
    s[:, half:ROPE_DIM] = sin
    perm = np.zeros((LANES, LANES), np.float32)
    for j in range(LANES):
        if j % QK_DIM < half:
            perm[j + half, j] = 1.0
        elif j % QK_DIM < ROPE_DIM:
            perm[j - half, j] = 1.0
    rep = LANES // QK_DIM
    return (jnp.asarray(np.tile(c, (1, rep))), jnp.asarray(np.tile(s, (1, rep))), jnp.asarray(perm).astype(BF16))


def kernel(x, norm1_g, w_in, short_conv_w, short_conv_b, q_norm_g, k_norm_g, lambda_q1, lambda_k1, lambda_q2, lambda_k2, subln_g, filt_w1, filt_b1, filt_w2, filt_b2, filt_w3, filt_b3, filt_w_out, filt_freq, hyena_skip, w_branch_attn, w_branch_hyena, w_out, norm2_g, w_router, w_gate, w_up, w_down):
    b, l, d = x.shape
    depth = w_in.shape[0]
    n = b * l
    cap = CAPACITY_FACTOR * l // N_EXPERTS
    q_cols = ATTN_HEADS * 2 * QK_DIM
    ctab, stab, perm = _rope_tables(l)
    xc = x.reshape(n, d)
    for li in range(depth):
        lambda_init = 0.8 - 0.6 * math.exp(-0.3 * li)
        proj = _inproj(xc, norm1_g[li][None], w_in[li].astype(BF16))
        scale = math.log2(math.e) / math.sqrt(QK_DIM)
        gqk = jnp.concatenate([jnp.tile(q_norm_g[li], q_cols // QK_DIM) * scale,
                               jnp.tile(k_norm_g[li], q_cols // QK_DIM)])[None]
        qk = _qkprep(proj, ctab, stab, perm, gqk, l)
        lamv = jnp.stack([lambda_q1[li], lambda_k1[li], lambda_q2[li], lambda_k2[li]])
        attn = _attention(qk, proj, lamv, subln_g[li][None], b, l, lambda_init)

        hyena = _hyena(proj, 3 * q_cols, short_conv_w[li], short_conv_b[li], filt_w1[li], filt_b1[li],
                       filt_w2[li], filt_b2[li], filt_w3[li], filt_b3[li], filt_w_out[li], filt_freq[li],
                       hyena_skip[li], b, l)

        wr_pad = jnp.pad(w_router[li], ((0, 0), (0, LANES - N_EXPERTS))).astype(BF16)
        x1, u2, aff = _merge(attn, hyena, proj, xc, w_branch_attn[li].astype(BF16),
                             w_branch_hyena[li].astype(BF16), w_out[li].astype(BF16),
                             norm2_g[li][None], wr_pad)
        pos, post, afft, offs = _topk(aff, b, l, cap)
        weo = _experts(offs, post, afft, u2, w_gate, w_up, w_down, li, b, l, cap)
        xc = _combine(offs, pos, x1, weo, b, l, cap)
    return xc.reshape(b, l, d)
```

```python
import functools
import math

import jax
import jax.numpy as jnp
import numpy as np
from jax import lax
from jax.experimental import pallas as pl
from jax.experimental.pallas import tpu as pltpu

F32 = jnp.float32
BF16 = jnp.bfloat16

ATTN_HEADS = 4
QK_DIM = 64
V_DIM = 128
ROPE_DIM = 16
ROPE_THETA = 500000.0
HYENA_WIDTH = 512
HYENA_ORDER = 2
FILTER_BANDS = 16
DECAY_TARGET = 1e-2
FAST_DECAY_PCT = 0.3
SLOW_DECAY_PCT = 1.5
N_EXPERTS = 16
CAPACITY_FACTOR = 2
MOE_TILE = 512
EPS = 1e-6
LANES = 128
FILT_GROUPS = LANES // FILTER_BANDS
VMEM_LIMIT = 56 * 1024 * 1024


def _cparams(sem):
    return pltpu.CompilerParams(dimension_semantics=sem, vmem_limit_bytes=VMEM_LIMIT)


def _inproj_kernel(x_ref, g_ref, w_ref, o_ref, u_ref):
    @pl.when(pl.program_id(1) == 0)
    def _():
        x = x_ref[...]
        ms = jnp.mean(x * x, axis=-1, keepdims=True)
        u_ref[...] = (x * lax.rsqrt(ms + EPS) * g_ref[...]).astype(BF16)

    o_ref[...] = jnp.dot(u_ref[...], w_ref[...], preferred_element_type=F32).astype(o_ref.dtype)


def _inproj(x2, g, w_bf16, tm=1024, tn=1280):
    n, d = x2.shape
    cols = w_bf16.shape[1]
    return pl.pallas_call(
        _inproj_kernel,
        out_shape=jax.ShapeDtypeStruct((n, cols), BF16),
        grid=(n // tm, cols // tn),
        in_specs=[
            pl.BlockSpec((tm, d), lambda i, j: (i, 0)),
            pl.BlockSpec((1, d), lambda i, j: (0, 0)),
            pl.BlockSpec((d, tn), lambda i, j: (0, j)),
        ],
        out_specs=pl.BlockSpec((tm, tn), lambda i, j: (i, j)),
        scratch_shapes=[pltpu.VMEM((tm, d), BF16)],
        compiler_params=_cparams(("arbitrary", "arbitrary")),
        name="inproj",
    )(x2, g, w_bf16)


def _qkprep_kernel(p_ref, c_ref, s_ref, perm_ref, g_ref, o_ref):
    lane = lax.broadcasted_iota(jnp.int32, (1, LANES), 1)
    lo = lane < QK_DIM
    c = c_ref[...]
    s = s_ref[...]
    perm = perm_ref[...]
    for j in range(p_ref.shape[1] // LANES):
        t = p_ref[:, j * LANES:(j + 1) * LANES].astype(F32)
        sq = t * t
        ss_lo = jnp.sum(jnp.where(lo, sq, 0.0), axis=-1, keepdims=True)
        ss_hi = jnp.sum(jnp.where(lo, 0.0, sq), axis=-1, keepdims=True)
        r = lax.rsqrt(jnp.where(lo, ss_lo, ss_hi) * (1.0 / QK_DIM) + EPS)
        y = t * r * g_ref[:, j * LANES:(j + 1) * LANES]
        y_hi = y.astype(BF16)
        y_lo = (y - y_hi.astype(F32)).astype(BF16)
        partner = (jnp.dot(y_hi, perm, preferred_element_type=F32)
                   + jnp.dot(y_lo, perm, preferred_element_type=F32))
        o_ref[:, j * LANES:(j + 1) * LANES] = (y * c + partner * s).astype(o_ref.dtype)


def _qkprep(proj, ctab, stab, perm, gqk, seq, tm=512):
    n = proj.shape[0]
    w = gqk.shape[1]
    nb = seq // tm
    return pl.pallas_call(
        _qkprep_kernel,
        out_shape=jax.ShapeDtypeStruct((n, w), BF16),
        grid=(n // tm,),
        in_specs=[
            pl.BlockSpec((tm, w), lambda i: (i, 0)),
            pl.BlockSpec((tm, LANES), lambda i: (i % nb, 0)),
            pl.BlockSpec((tm, LANES), lambda i: (i % nb, 0)),
            pl.BlockSpec((LANES, LANES), lambda i: (0, 0)),
            pl.BlockSpec((1, w), lambda i: (0, 0)),
        ],
        out_specs=pl.BlockSpec((tm, w), lambda i: (i, 0)),
        compiler_params=_cparams(("arbitrary",)),
        name="qkprep",
    )(proj, ctab, stab, perm, gqk)


def _attn_kernel(q_ref, k_ref, v_ref, lam_ref, g_ref, o_ref, *, tk, unroll, lambda_init):
    tq = q_ref.shape[0]
    seq = k_ref.shape[0]
    lane = lax.broadcasted_iota(jnp.int32, (1, LANES), 1)
    q = q_ref[...]
    zero = jnp.zeros_like(q)
    qs = jnp.concatenate([jnp.where(lane < QK_DIM, q, zero), jnp.where(lane < QK_DIM, zero, q)], axis=0)
    lv = lam_ref[...]
    lam = (jnp.exp(jnp.sum(lv[0:1] * lv[1:2], axis=-1, keepdims=True))
           - jnp.exp(jnp.sum(lv[2:3] * lv[3:4], axis=-1, keepdims=True)) + lambda_init)

    def step(c, carry):
        m, l, a = carry
        off = pl.multiple_of(c * tk, tk)
        kc = k_ref[pl.ds(off, tk), :]
        vc = v_ref[pl.ds(off, tk), :]
        s = lax.dot_general(qs, kc, (((1,), (1,)), ((), ())), preferred_element_type=F32)
        mn = jnp.maximum(m, jnp.max(s, axis=-1, keepdims=True))
        p = jnp.exp2(s - mn)
        al = jnp.exp2(m - mn)
        l = al * l + jnp.sum(p, axis=-1, keepdims=True)
        a = al * a + jnp.dot(p.astype(BF16), vc, preferred_element_type=F32)
        return mn, l, a

    init = (jnp.full((2 * tq, 1), -jnp.inf, F32), jnp.zeros((2 * tq, 1), F32), jnp.zeros((2 * tq, V_DIM), F32))
    _, l, a = lax.fori_loop(0, seq // tk, step, init, unroll=unroll)
    a = a / l
    o = a[:tq] - lam * a[tq:]
    ms = jnp.mean(o * o, axis=-1, keepdims=True)
    o = o * lax.rsqrt(ms + EPS) * g_ref[...] * (1.0 - lambda_init)
    o_ref[...] = o.astype(o_ref.dtype)


def _attention(qk, proj, lamv, subln_g, batch, seq, lambda_init, tq=512, tk=1024, unroll=4):
    nq = seq // tq
    vcol0 = (2 * ATTN_HEADS * 2 * QK_DIM) // V_DIM
    return pl.pallas_call(
        functools.partial(_attn_kernel, tk=tk, unroll=unroll, lambda_init=lambda_init),
        out_shape=jax.ShapeDtypeStruct((batch * seq, ATTN_HEADS * V_DIM), BF16),
        grid=(batch, ATTN_HEADS, nq),
        in_specs=[
            pl.BlockSpec((tq, LANES), lambda b, h, i: (b * nq + i, h)),
            pl.BlockSpec((seq, LANES), lambda b, h, i: (b, ATTN_HEADS + h)),
            pl.BlockSpec((seq, V_DIM), lambda b, h, i: (b, vcol0 + h)),
            pl.BlockSpec((4, QK_DIM), lambda b, h, i: (0, 0)),
            pl.BlockSpec((1, V_DIM), lambda b, h, i: (0, 0)),
        ],
        out_specs=pl.BlockSpec((tq, V_DIM), lambda b, h, i: (b * nq + i, h)),
        compiler_params=_cparams(("arbitrary", "arbitrary", "arbitrary")),
        name="diffattn",
    )(qk, qk, proj, lamv, subln_g)


def _merge_kernel(at_ref, hy_ref, ga_ref, gh_ref, x_ref, wpa_ref, wph_ref, wo_ref, g2_ref, wr_ref,
                  x1_ref, u2_ref, aff_ref):
    ga = jax.nn.sigmoid(ga_ref[...].astype(F32))
    gh = jax.nn.sigmoid(gh_ref[...].astype(F32))
    merged = (ga * jnp.dot(at_ref[...], wpa_ref[...], preferred_element_type=F32)
              + gh * jnp.dot(hy_ref[...], wph_ref[...], preferred_element_type=F32))
    x1 = x_ref[...] + jnp.dot(merged.astype(BF16), wo_ref[...], preferred_element_type=F32)
    x1_ref[...] = x1
    ms = jnp.mean(x1 * x1, axis=-1, keepdims=True)
    u2 = (x1 * lax.rsqrt(ms + EPS) * g2_ref[...]).astype(BF16)
    u2_ref[...] = u2
    logits = jnp.dot(u2, wr_ref[...], preferred_element_type=F32)
    lane = lax.broadcasted_iota(jnp.int32, logits.shape, 1)
    logits = jnp.where(lane < N_EXPERTS, logits, -jnp.inf)
    e = jnp.exp(logits - jnp.max(logits, axis=-1, keepdims=True))
    aff_ref[...] = e / jnp.sum(e, axis=-1, keepdims=True)


def _merge(attn, hyena, proj, x2, wpa, wph, wo, g2, wr_pad, tm=512):
    n, d = x2.shape
    gcol0 = (proj.shape[1] - 2 * d) // d
    full = lambda shape: pl.BlockSpec(shape, lambda i: (0, 0))
    return pl.pallas_call(
        _merge_kernel,
        out_shape=(jax.ShapeDtypeStruct((n, d), F32), jax.ShapeDtypeStruct((n, d), BF16),
                   jax.ShapeDtypeStruct((n, LANES), F32)),
        grid=(n // tm,),
        in_specs=[
            pl.BlockSpec((tm, attn.shape[1]), lambda i: (i, 0)),
            pl.BlockSpec((tm, hyena.shape[1]), lambda i: (i, 0)),
            pl.BlockSpec((tm, d), lambda i: (i, gcol0)),
            pl.BlockSpec((tm, d), lambda i: (i, gcol0 + 1)),
            pl.BlockSpec((tm, d), lambda i: (i, 0)),
            full(wpa.shape), full(wph.shape), full(wo.shape), full(g2.shape), full(wr_pad.shape),
        ],
        out_specs=(pl.BlockSpec((tm, d), lambda i: (i, 0)), pl.BlockSpec((tm, d), lambda i: (i, 0)),
                   pl.BlockSpec((tm, LANES), lambda i: (i, 0))),
        compiler_params=_cparams(("arbitrary",)),
        name="merge",
    )(attn, hyena, proj, proj, x2, wpa, wph, wo, g2, wr_pad)


def _excl_cumsum_rows(mask_f32, tri, blk):
    rows = mask_f32.shape[0]
    carry = jnp.zeros((1, LANES), F32)
    outs = []
    for r in range(rows // blk):
        mb = mask_f32[r * blk:(r + 1) * blk]
        outs.append(jnp.dot(tri, mb.astype(BF16), preferred_element_type=F32) + carry)
        carry = carry + jnp.sum(mb, axis=0, keepdims=True)
    return jnp.concatenate(outs, axis=0)


def _topk_kernel(aff_ref, pos_ref, post_ref, afft_ref, offs_ref, *, cap, blk, batch):
    seq = aff_ref.shape[0] // batch
    aff = aff_ref[0:seq]
    for b in range(1, batch):
        aff = aff + pltpu.roll(aff_ref[b * seq:(b + 1) * seq], b * N_EXPERTS, 1)

    def search(i, prefix):
        cand = prefix | jnp.left_shift(jnp.int32(1), 29 - i)
        cnt = jnp.sum((aff >= pltpu.bitcast(cand, F32)[0:1]).astype(F32), axis=0, keepdims=True)
        return jnp.where(cnt >= cap, cand, prefix)

    lo = pltpu.bitcast(lax.fori_loop(0, 30, search, jnp.zeros((8, LANES), jnp.int32)), F32)[0:1]
    thr = jnp.min(jnp.where(aff >= lo, aff, jnp.inf), axis=0, keepdims=True)
    gt = (aff > thr).astype(F32)
    eq = (aff == thr).astype(F32)
    need = cap - jnp.sum(gt, axis=0, keepdims=True)
    ri = lax.broadcasted_iota(jnp.int32, (blk, blk), 0)
    ci = lax.broadcasted_iota(jnp.int32, (blk, blk), 1)
    tri = (ci < ri).astype(BF16)
    sel = gt + eq * (_excl_cumsum_rows(eq, tri, blk) < need).astype(F32)
    before = _excl_cumsum_rows(sel, tri, blk)
    pos = jnp.where(sel > 0.0, before, -1.0)
    pos_t = pos.T
    aff_t = aff.T
    lane = lax.broadcasted_iota(jnp.int32, (1, LANES), 1)
    for b in range(batch):
        mine = pos if b == 0 else pltpu.roll(pos, LANES - b * N_EXPERTS, 1)
        pos_ref[b * seq:(b + 1) * seq] = jnp.where(lane < N_EXPERTS, mine, -1.0)
        post_ref[b] = pos_t[b * N_EXPERTS:(b + 1) * N_EXPERTS]
        afft_ref[b] = aff_t[b * N_EXPERTS:(b + 1) * N_EXPERTS]
    offs_ref[...] = jnp.concatenate([before[j * MOE_TILE:j * MOE_TILE + 1] for j in range(offs_ref.shape[0])], axis=0)


def _topk(aff, batch, seq, cap, blk=256):
    assert batch * N_EXPERTS <= LANES
    nt = seq // MOE_TILE
    pos, post, afft, offs = pl.pallas_call(
        functools.partial(_topk_kernel, cap=cap, blk=blk, batch=batch),
        out_shape=(jax.ShapeDtypeStruct((batch * seq, LANES), F32),
                   jax.ShapeDtypeStruct((batch, N_EXPERTS, seq), F32),
                   jax.ShapeDtypeStruct((batch, N_EXPERTS, seq), F32),
                   jax.ShapeDtypeStruct((nt, LANES), F32)),
        compiler_params=pltpu.CompilerParams(vmem_limit_bytes=VMEM_LIMIT),
        name="topk",
    )(aff)
    offs = offs[:, :batch * N_EXPERTS].reshape(nt, batch, N_EXPERTS).transpose(1, 0, 2)
    return pos, post, afft, offs.astype(jnp.int32).reshape(-1)


def _align_down(x, m):
    sh = m.bit_length() - 1
    return lax.shift_left(lax.shift_right_logical(x, sh), sh)


def _ceil_div(x, m):
    return lax.shift_right_logical(x + (m - 1), m.bit_length() - 1)


def _windows(lo0, end, body, win):
    body(0, 0)
    lax.fori_loop(1, _ceil_div(end - lo0, win), body, 0)


def _gather_kernel(offs_ref, post_ref, u_ref, x_ref, *, cap, win):
    b = pl.program_id(0)
    c = pl.program_id(1)
    nt = pl.num_programs(1)

    @pl.when(c == 0)
    def _():
        x_ref[...] = jnp.zeros_like(x_ref)

    uc = u_ref[...]
    slot0 = lax.broadcasted_iota(jnp.int32, (win, MOE_TILE), 0).astype(F32)
    base = (b * nt + c) * N_EXPERTS
    last = pl.num_programs(0) * nt * N_EXPERTS - 1
    starts, hots = [], []
    for e in range(N_EXPERTS):
        s = pl.multiple_of(jnp.minimum(_align_down(offs_ref[base + e], 16), cap - win), 16)
        hots.append(jnp.where((slot0 + s.astype(F32)) == post_ref[0, e:e + 1, :], 1.0, 0.0).astype(BF16))
        starts.append(s)
    rows = jnp.dot(jnp.concatenate(hots, axis=0), uc, preferred_element_type=F32)
    for e in range(N_EXPERTS):
        x_ref[0, e, pl.ds(starts[e], win), :] += rows[e * win:(e + 1) * win].astype(BF16)
    for e in range(N_EXPERTS):
        lo0 = _align_down(offs_ref[base + e], 16)
        end = jnp.where(c + 1 < nt, offs_ref[jnp.minimum(base + N_EXPERTS + e, last)], cap)

        def more(k, carry, lo0=lo0, e=e):
            lo = lo0 + k * win
            s = pl.multiple_of(jnp.minimum(lo, cap - win), 16)
            srow = slot0 + s.astype(F32)
            hit = jnp.logical_and(srow == post_ref[0, e:e + 1, :], srow >= lo.astype(F32))
            x_ref[0, e, pl.ds(s, win), :] += jnp.dot(jnp.where(hit, 1.0, 0.0).astype(BF16), uc,
                                                     preferred_element_type=F32).astype(BF16)
            return carry

        lax.fori_loop(1, _ceil_div(end - lo0, win), more, 0)


def _gather(offs, post, u2, batch, seq, cap, win=128):
    d = u2.shape[1]
    nt = seq // MOE_TILE
    grid_spec = pltpu.PrefetchScalarGridSpec(
        num_scalar_prefetch=1,
        grid=(batch, nt),
        in_specs=[
            pl.BlockSpec((1, N_EXPERTS, MOE_TILE), lambda b, c, o: (b, 0, c)),
            pl.BlockSpec((MOE_TILE, d), lambda b, c, o: (b * nt + c, 0)),
        ],
        out_specs=pl.BlockSpec((1, N_EXPERTS, cap, d), lambda b, c, o: (b, 0, 0, 0)),
    )
    return pl.pallas_call(
        functools.partial(_gather_kernel, cap=cap, win=win),
        out_shape=jax.ShapeDtypeStruct((batch, N_EXPERTS, cap, d), BF16),
        grid_spec=grid_spec,
        compiler_params=_cparams(("arbitrary", "arbitrary")),
        name="gather",
    )(offs, post, u2)


def _expert_kernel(offs_ref, post_ref, afft_ref, x_ref, wg_ref, wu_ref, wd_ref, o_ref, gs_ref, w_ref, *, cap, win):
    e = pl.program_id(0)
    b = pl.program_id(1)

    @pl.when(e < N_EXPERTS)
    def _():
        for m, src in enumerate((wg_ref, wu_ref, wd_ref)):
            part = src.shape[2]
            w_ref[jnp.bitwise_and(e, 1), m, pl.ds(pl.multiple_of(b * part, part), part), :] = src[0, 0].astype(BF16)

    @pl.when(e == 0)
    def _():
        o_ref[...] = jnp.zeros_like(o_ref)

    @pl.when(e > 0)
    def _():
        _expert_body(offs_ref, post_ref, afft_ref, x_ref, w_ref.at[1 - jnp.bitwise_and(e, 1)], o_ref, gs_ref,
                     e - 1, b, cap, win)


def _expert_body(offs_ref, post_ref, afft_ref, x_ref, w_ref, o_ref, gs_ref, e, b, cap, win):
    seq = post_ref.shape[2]
    nt = seq // MOE_TILE
    gs_ref[...] = jnp.zeros_like(gs_ref)
    slot0 = lax.broadcasted_iota(jnp.int32, (win, MOE_TILE), 0).astype(F32)
    for c in range(nt):
        base = (b * nt + c) * N_EXPERTS + e
        lo0 = _align_down(offs_ref[base], 8)
        end = offs_ref[base + N_EXPERTS] if c + 1 < nt else cap
        prow = post_ref[0, pl.ds(e, 1), c * MOE_TILE:(c + 1) * MOE_TILE]
        arow = afft_ref[0, pl.ds(e, 1), c * MOE_TILE:(c + 1) * MOE_TILE]

        def window(k, carry, lo0=lo0, prow=prow, arow=arow):
            lo = lo0 + k * win
            s = pl.multiple_of(jnp.minimum(lo, cap - win), 8)
            srow = slot0 + s.astype(F32)
            hit = jnp.logical_and(srow == prow, srow >= lo.astype(F32))
            gs_ref[pl.ds(s, win), :] += jnp.sum(jnp.where(hit, arow, 0.0), axis=-1, keepdims=True)
            return carry

        _windows(lo0, end, window, win)
    xb = x_ref[0, 0]
    hg = jnp.dot(xb, w_ref[0], preferred_element_type=F32)
    hu = jnp.dot(xb, w_ref[1], preferred_element_type=F32)
    act = (hg * jax.nn.sigmoid(hg) * hu).astype(BF16)
    eo = jnp.dot(act, w_ref[2], preferred_element_type=F32)
    o_ref[0, 0] = (eo * gs_ref[...]).astype(o_ref.dtype)


def _experts(offs, post, afft, xin, wg, wu, wd, layer, batch, seq, cap, win=128):
    d = xin.shape[3]
    assert wg.shape[2:] == (d, d) and wd.shape[2:] == (d, d) and d % (16 * batch) == 0
    part = d // batch
    wspec = pl.BlockSpec((1, 1, part, d), lambda e, b, o: (layer, jnp.minimum(e, N_EXPERTS - 1), b, 0))
    grid_spec = pltpu.PrefetchScalarGridSpec(
        num_scalar_prefetch=1,
        grid=(N_EXPERTS + 1, batch),
        in_specs=[
            pl.BlockSpec((1, N_EXPERTS, seq), lambda e, b, o: (b, 0, 0)),
            pl.BlockSpec((1, N_EXPERTS, seq), lambda e, b, o: (b, 0, 0)),
            pl.BlockSpec((1, 1, cap, d), lambda e, b, o: (b, jnp.maximum(e - 1, 0), 0, 0)),
            wspec, wspec, wspec,
        ],
        out_specs=pl.BlockSpec((1, 1, cap, d), lambda e, b, o: (b, jnp.where(e == 0, N_EXPERTS, e - 1), 0, 0)),
        scratch_shapes=[pltpu.VMEM((cap, 1), F32), pltpu.VMEM((2, 3, d, d), BF16)],
    )
    return pl.pallas_call(
        functools.partial(_expert_kernel, cap=cap, win=win),
        out_shape=jax.ShapeDtypeStruct((batch, N_EXPERTS + 1, cap, d), BF16),
        grid_spec=grid_spec,
        compiler_params=_cparams(("arbitrary", "arbitrary")),
        name="experts",
    )(offs, post, afft, xin, wg, wu, wd)


def _combine_kernel(offs_ref, pos_ref, x1_ref, weo_ref, o_ref, *, cap, win):
    b = pl.program_id(0)
    t = pl.program_id(1)
    nt = pl.num_programs(1)
    tt = pos_ref.shape[0]
    pos = pos_ref[...]
    slot0 = lax.broadcasted_iota(jnp.int32, (tt, win), 1).astype(F32)
    base = (b * nt + t) * N_EXPERTS
    last = pl.num_programs(0) * nt * N_EXPERTS - 1
    acc = x1_ref[...]
    for e0 in range(0, N_EXPERTS, 2):
        hots, rows = [], []
        for e in (e0, e0 + 1):
            s = pl.multiple_of(jnp.minimum(_align_down(offs_ref[base + e], 16), cap - win), 16)
            hots.append(jnp.where((slot0 + s.astype(F32)) == pos[:, e:e + 1], 1.0, 0.0).astype(BF16))
            rows.append(weo_ref[0, e, pl.ds(s, win), :])
        acc = acc + jnp.dot(jnp.concatenate(hots, axis=1), jnp.concatenate(rows, axis=0),
                            preferred_element_type=F32)
    o_ref[...] = acc
    for e in range(N_EXPERTS):
        lo0 = _align_down(offs_ref[base + e], 16)
        end = jnp.where(t + 1 < nt, offs_ref[jnp.minimum(base + N_EXPERTS + e, last)], cap)

        def window(k, carry, lo0=lo0, e=e):
            lo = lo0 + k * win
            s = pl.multiple_of(jnp.minimum(lo, cap - win), 16)
            srow = slot0 + s.astype(F32)
            hit = jnp.logical_and(srow == pos[:, e:e + 1], srow >= lo.astype(F32))
            o_ref[...] += jnp.dot(jnp.where(hit, 1.0, 0.0).astype(BF16), weo_ref[0, e, pl.ds(s, win), :],
                                  preferred_element_type=F32)
            return carry

        lax.fori_loop(1, _ceil_div(end - lo0, win), window, 0)


def _combine(offs, pos, x1, weo, batch, seq, cap, win=128):
    d = x1.shape[1]
    tt = MOE_TILE
    nt = seq // tt
    grid_spec = pltpu.PrefetchScalarGridSpec(
        num_scalar_prefetch=1,
        grid=(batch, nt),
        in_specs=[
            pl.BlockSpec((tt, LANES), lambda b, t, o: (b * nt + t, 0)),
            pl.BlockSpec((tt, d), lambda b, t, o: (b * nt + t, 0)),
            pl.BlockSpec((1, N_EXPERTS, cap, d), lambda b, t, o: (b, 0, 0, 0)),
        ],
        out_specs=pl.BlockSpec((tt, d), lambda b, t, o: (b * nt + t, 0)),
    )
    return pl.pallas_call(
        functools.partial(_combine_kernel, cap=cap, win=win),
        out_shape=jax.ShapeDtypeStruct(x1.shape, F32),
        grid_spec=grid_spec,
        compiler_params=_cparams(("arbitrary", "arbitrary")),
        name="combine",
    )(offs, pos, x1, weo)


FFT_R = 64
FFT_PITCH = FFT_R + 8


def _dft_tables(seq):
    r = FFT_R
    assert seq == r * r
    n = np.arange(r)
    f = np.exp(-2j * np.pi * np.outer(n, n) / r)
    w = lambda e: np.exp(-2j * np.pi * e / (2 * seq))

    def real_rep(a):
        return np.block([[a.real, -a.imag], [a.imag, a.real]]).astype(np.float32)

    m1 = [f, f * w(r * n)[None, :]]
    m2 = [f * w(n * m)[None, :] for m in range(2 * r)]
    m4 = [f.conj() / (2 * seq), w(-r * n)[:, None] * f.conj() / (2 * seq)]
    t1 = np.stack([real_rep(a) for a in m1])
    t2 = np.stack([real_rep(a) for a in m2])
    t3 = np.stack([real_rep(a.conj().T) for a in m2])
    t4 = np.stack([real_rep(a) for a in m4])
    t1r = np.stack([np.concatenate([a.real, a.imag], axis=0).astype(np.float32) for a in m1])
    return tuple(jnp.asarray(t).astype(BF16) for t in (t1, t2, t3, t4, t1r))


def _slab(r):
    return pl.ds(pl.multiple_of(r * FFT_PITCH, 8), FFT_R)


def _across(r):
    return pl.ds(r, FFT_R, stride=FFT_PITCH)


def _cat_bf16(a, b):
    return jnp.concatenate([a, b], axis=0).astype(BF16)


FFT_UNROLL = 32


def _loop(body, step=1):
    def wrapped(i, c):
        body(i * step)
        return c
    lax.fori_loop(0, FFT_R // step, wrapped, 0, unroll=FFT_UNROLL // step)


def _pair_bf16(re_ref, im_ref, idx0, idx1):
    return jnp.concatenate([_cat_bf16(re_ref[idx0, :], im_ref[idx0, :]),
                            _cat_bf16(re_ref[idx1, :], im_ref[idx1, :])], axis=1)


def _bf16_pieces(w):
    hi = w.astype(BF16)
    return hi, (w - hi.astype(F32)).astype(BF16)


def _dot3(a, w_ref):
    a_hi, a_lo = _bf16_pieces(a)
    return (jnp.dot(a_hi, w_ref[0], preferred_element_type=F32)
            + (jnp.dot(a_hi, w_ref[1], preferred_element_type=F32)
               + jnp.dot(a_lo, w_ref[0], preferred_element_type=F32)))


def _filter_kernel(bands_ref, w1t_ref, w1c_ref, w1s_ref, b1_ref, w2_ref, b2_ref, w3_ref, b3_ref, fr_ref,
                   wf_ref, wb_ref, dl_ref, hs_ref, hd_ref, *, seq):
    tm = hs_ref.shape[0]
    rows = tm // FILT_GROUPS
    hid = wf_ref.shape[1]
    base = pl.program_id(0) * tm
    lane = lax.broadcasted_iota(jnp.int32, (rows, LANES), 1)
    grp = lax.shift_right_logical(lane, FILTER_BANDS.bit_length() - 1)
    pos = (base + grp * rows + lax.broadcasted_iota(jnp.int32, (rows, LANES), 0)).astype(F32)
    first = jnp.bitwise_and(lane, FILTER_BANDS - 1) == 0
    fr = fr_ref[...]
    tscale = 1.0 / max(seq - 1, 1)

    def mlp(p):
        ang = ((2.0 * math.pi / seq) * p) * bands_ref[...]
        pre = (_dot3(jnp.cos(ang), w1c_ref) + _dot3(-jnp.sin(ang), w1s_ref)
               + _dot3(jnp.where(first, p * tscale, 0.0), w1t_ref) + b1_ref[...])
        h = jnp.sin(fr * pre)
        h = jnp.sin(fr * (_dot3(h, w2_ref) + b2_ref[...]))
        return jnp.sin(fr * (_dot3(h, w3_ref) + b3_ref[...]))

    hf_h = mlp(pos)
    hb_h = mlp(seq - pos)
    dl = dl_ref[...]
    for g in range(FILT_GROUPS):
        pcol = (base + g * rows + lax.broadcasted_iota(jnp.int32, (rows, 1), 0)).astype(F32)
        hf = _dot3(hf_h[:, g * hid:(g + 1) * hid], wf_ref) * jnp.exp(-(pcol * tscale) * dl)
        hb = _dot3(hb_h[:, g * hid:(g + 1) * hid], wb_ref) * jnp.exp(-((seq - pcol) * tscale) * dl)
        hb = jnp.where(pcol > 0.0, hb, 0.0)
        hs_ref[g * rows:(g + 1) * rows, :] = hf + hb
        hd_ref[g * rows:(g + 1) * rows, :] = hf - hb


def _filters(bands, w1t, w1c, w1s, b1, w2, b2, w3, b3, fr, wf, wb, dl, seq, tm=512):
    cols = wf.shape[1]
    pieces = lambda w: jnp.stack(_bf16_pieces(w))
    args = (bands, pieces(w1t), pieces(w1c), pieces(w1s), b1, pieces(w2), b2, pieces(w3), b3, fr,
            pieces(wf), pieces(wb), dl)
    return pl.pallas_call(
        functools.partial(_filter_kernel, seq=seq),
        out_shape=(jax.ShapeDtypeStruct((seq, cols), F32), jax.ShapeDtypeStruct((seq, cols), F32)),
        grid=(seq // tm,),
        in_specs=[pl.BlockSpec(a.shape, lambda i, nd=a.ndim: (0,) * nd) for a in args],
        out_specs=(pl.BlockSpec((tm, cols), lambda i: (i, 0)), pl.BlockSpec((tm, cols), lambda i: (i, 0))),
        compiler_params=_cparams(("arbitrary",)),
        name="hyfilter",
    )(*args)


def _spectra_kernel(hs_ref, hd_ref, t1r_ref, t2_ref, h_ref, x_ref, pr_ref, pi_ref):
    for par, src in ((0, hs_ref), (1, hd_ref)):
        def fill(r, src=src):
            x_ref[_slab(r), :] = src[pl.ds(pl.multiple_of(r * FFT_R, FFT_R), FFT_R), :]
        _loop(fill)

        def first(r, par=par):
            o = jnp.dot(t1r_ref[par], x_ref[_across(r), :].astype(BF16), preferred_element_type=F32)
            pr_ref[_slab(r), :] = o[:FFT_R]
            pi_ref[_slab(r), :] = o[FFT_R:]
        _loop(first)

        def second(r, par=par):
            o = jnp.dot(t2_ref[2 * r + par], _cat_bf16(pr_ref[_across(r), :], pi_ref[_across(r), :]),
                        preferred_element_type=F32)
            rows = pl.ds(pl.multiple_of(r * FFT_R, FFT_R), FFT_R)
            h_ref[0, 2 * par, rows, :] = o[:FFT_R].astype(h_ref.dtype)
            h_ref[0, 2 * par + 1, rows, :] = o[FFT_R:].astype(h_ref.dtype)
        _loop(second)


def _spectra(hs, hd, t1r, t2, seq):
    ncol = hs.shape[1] // LANES
    nct = ncol // HYENA_ORDER
    prow = FFT_R * FFT_PITCH
    const = lambda a: pl.BlockSpec(a.shape, lambda o, c: (0,) * a.ndim, pipeline_mode=pl.Buffered(1))
    return pl.pallas_call(
        _spectra_kernel,
        out_shape=jax.ShapeDtypeStruct((HYENA_ORDER, 4, seq, nct * LANES), BF16),
        grid=(HYENA_ORDER, nct),
        in_specs=[pl.BlockSpec((seq, LANES), lambda o, c: (0, o * nct + c)),
                  pl.BlockSpec((seq, LANES), lambda o, c: (0, o * nct + c)),
                  const(t1r), const(t2)],
        out_specs=pl.BlockSpec((1, 4, seq, LANES), lambda o, c: (o, 0, 0, c)),
        scratch_shapes=[pltpu.VMEM((prow, LANES), F32)] * 3,
        compiler_params=_cparams(("arbitrary", "arbitrary")),
        name="hyspectra",
    )(hs, hd, t1r, t2)


def _short_conv(x, w_ref, b_ref):
    rows = x.shape[0]
    ri = lax.broadcasted_iota(jnp.int32, x.shape, 0)
    prev = jnp.where(ri == 0, 0.0, pltpu.roll(x, 1, 0))
    nxt = jnp.where(ri == rows - 1, 0.0, pltpu.roll(x, rows - 1, 0))
    return prev * w_ref[0:1, :] + x * w_ref[1:2, :] + nxt * w_ref[2:3, :] + b_ref[...]


def _hyconv_kernel(z_ref, g_ref, cwz_ref, cbz_ref, cwg_ref, cbg_ref, skip_ref, h_ref,
                   t1_ref, t2_ref, t3_ref, t4_ref, o_ref,
                   xr_ref, xi_ref, pr_ref, pi_ref, qr_ref, qi_ref, yr_ref, yi_ref, *, conv_z):
    seq = z_ref.shape[0] // 2
    for half, dst in ((0, xr_ref), (1, xi_ref)):
        z = z_ref[half * seq:(half + 1) * seq, :].astype(F32)
        if conv_z:
            z = _short_conv(z, cwz_ref, cbz_ref)
        for j in range(FFT_R):
            dst[j * FFT_PITCH:j * FFT_PITCH + FFT_R, :] = z[j * FFT_R:(j + 1) * FFT_R]

    for par in (0, 1):
        def first(r, par=par):
            o = jnp.dot(t1_ref[par], _pair_bf16(xr_ref, xi_ref, _across(r), _across(r + 1)),
                        preferred_element_type=F32)
            pr_ref[_slab(r), :] = o[:FFT_R, :LANES]
            pi_ref[_slab(r), :] = o[FFT_R:, :LANES]
            pr_ref[_slab(r + 1), :] = o[:FFT_R, LANES:]
            pi_ref[_slab(r + 1), :] = o[FFT_R:, LANES:]
        _loop(first, step=2)

        def second(r, par=par):
            o = jnp.dot(t2_ref[2 * r + par], _cat_bf16(pr_ref[_across(r), :], pi_ref[_across(r), :]),
                        preferred_element_type=F32)
            rows = pl.ds(pl.multiple_of(r * FFT_R, FFT_R), FFT_R)
            hr = h_ref[0, 2 * par, rows, :].astype(F32)
            hi = h_ref[0, 2 * par + 1, rows, :].astype(F32)
            ar, ai = o[:FFT_R], o[FFT_R:]
            qr_ref[_slab(r), :] = ar * hr - ai * hi
            qi_ref[_slab(r), :] = ar * hi + ai * hr
        _loop(second)

        def third(r, par=par):
            o = jnp.dot(t3_ref[2 * r + par], _cat_bf16(qr_ref[_slab(r), :], qi_ref[_slab(r), :]),
                        preferred_element_type=F32)
            pr_ref[_across(r), :] = o[:FFT_R]
            pi_ref[_across(r), :] = o[FFT_R:]
        _loop(third)

        def fourth(r, par=par):
            o = jnp.dot(t4_ref[par], _pair_bf16(pr_ref, pi_ref, _slab(r), _slab(r + 1)),
                        preferred_element_type=F32)
            for k, cols in ((r, slice(0, LANES)), (r + 1, slice(LANES, 2 * LANES))):
                if par == 0:
                    yr_ref[_across(k), :] = o[:FFT_R, cols]
                    yi_ref[_across(k), :] = o[FFT_R:, cols]
                else:
                    yr_ref[_across(k), :] = yr_ref[_across(k), :] + o[:FFT_R, cols]
                    yi_ref[_across(k), :] = yi_ref[_across(k), :] + o[FFT_R:, cols]
        _loop(fourth, step=2)

    skip = skip_ref[...]
    for half, (y_ref, x_ref) in enumerate(((yr_ref, xr_ref), (yi_ref, xi_ref))):
        g = _short_conv(g_ref[half * seq:(half + 1) * seq, :].astype(F32), cwg_ref, cbg_ref)
        for j in range(FFT_R):
            src = slice(j * FFT_PITCH, j * FFT_PITCH + FFT_R)
            dst = slice(half * seq + j * FFT_R, half * seq + (j + 1) * FFT_R)
            o_ref[dst, :] = (g[j * FFT_R:(j + 1) * FFT_R] * (y_ref[src, :] + skip * x_ref[src, :])).astype(o_ref.dtype)


def _hyconv(zsrc, zcol0, gcol0, proj, conv_w, conv_b, zpart, gpart, skip, spec, order, tabs, batch, seq, conv_z):
    t1, t2, t3, t4 = tabs
    nct = HYENA_WIDTH // LANES
    prow = FFT_R * FFT_PITCH
    const = lambda a: pl.BlockSpec(a.shape, lambda c, p: (0,) * a.ndim, pipeline_mode=pl.Buffered(1))
    return pl.pallas_call(
        functools.partial(_hyconv_kernel, conv_z=conv_z),
        out_shape=jax.ShapeDtypeStruct((batch * seq, HYENA_WIDTH), BF16),
        grid=(nct, batch // 2),
        in_specs=[
            pl.BlockSpec((2 * seq, LANES), lambda c, p: (p, zcol0 + c)),
            pl.BlockSpec((2 * seq, LANES), lambda c, p: (p, gcol0 + c)),
            pl.BlockSpec((3, LANES), lambda c, p: (0, zpart * nct + c)),
            pl.BlockSpec((1, LANES), lambda c, p: (0, zpart * nct + c)),
            pl.BlockSpec((3, LANES), lambda c, p: (0, gpart * nct + c)),
            pl.BlockSpec((1, LANES), lambda c, p: (0, gpart * nct + c)),
            pl.BlockSpec((1, LANES), lambda c, p: (0, c)),
            pl.BlockSpec((1, 4, seq, LANES), lambda c, p: (order, 0, 0, c), pipeline_mode=pl.Buffered(1)),
            const(t1), const(t2), const(t3), const(t4),
        ],
        out_specs=pl.BlockSpec((2 * seq, LANES), lambda c, p: (p, c)),
        scratch_shapes=[pltpu.VMEM((prow, LANES), F32)] * 8,
        compiler_params=_cparams(("arbitrary", "arbitrary")),
        name=f"hyconv{order}",
    )(zsrc, proj, conv_w, conv_b, conv_w, conv_b, skip[order:order + 1], spec, t1, t2, t3, t4)


def _hyena(proj, hycol0, conv_w, conv_b, w1, b1, w2, b2, w3, b3, w_out, freq, skip, batch, seq):
    hid = w2.shape[0]
    eye = jnp.eye(FILT_GROUPS, dtype=F32)
    bdiag = lambda a: jnp.kron(eye, a)
    tile = lambda a: jnp.tile(a[None], (1, FILT_GROUPS))
    bands = tile(jnp.linspace(1e-4, FILTER_BANDS - 1, FILTER_BANDS, dtype=F32))
    min_decay = math.log(DECAY_TARGET) / SLOW_DECAY_PCT
    max_decay = math.log(DECAY_TARGET) / FAST_DECAY_PCT
    deltas = jnp.abs(jnp.linspace(min_decay, max_decay, HYENA_WIDTH, dtype=F32))
    wo4 = w_out.reshape(hid, HYENA_ORDER, 2, HYENA_WIDTH)
    w1t = jnp.pad(w1[0:1], ((0, FILTER_BANDS - 1), (0, 0)))
    hs, hd = _filters(
        bands, bdiag(w1t), bdiag(w1[1:1 + FILTER_BANDS]), bdiag(w1[1 + FILTER_BANDS:]),
        tile(b1), bdiag(w2), tile(b2), bdiag(w3), tile(b3), tile(freq),
        wo4[:, :, 0].reshape(hid, -1), wo4[:, :, 1].reshape(hid, -1),
        jnp.tile(deltas, HYENA_ORDER)[None], seq)
    t1, t2, t3, t4, t1r = _dft_tables(seq)
    spec = _spectra(hs, hd, t1r, t2, seq)
    c0 = hycol0 // LANES
    nct = HYENA_WIDTH // LANES
    cb = conv_b[None]
    z1 = _hyconv(proj, c0, c0 + nct, proj, conv_w, cb, 0, 1, skip, spec, 0, (t1, t2, t3, t4), batch, seq, True)
    return _hyconv(z1, 0, c0 + 2 * nct, proj, conv_w, cb, 0, 2, skip, spec, 1, (t1, t2, t3, t4), batch, seq, False)


def _rope_tables(seq):
    pos = np.arange(seq, dtype=np.float32)
    inv_freq = (ROPE_THETA ** (-np.arange(0, ROPE_DIM, 2, dtype=np.float32) / ROPE_DIM)).astype(np.float32)
    ang = pos[:, None] * inv_freq[None, :]
    cos, sin = np.cos(ang).astype(np.float32), np.sin(ang).astype(np.float32)
    half = ROPE_DIM // 2
    c = np.ones((seq, QK_DIM), np.float32)
    s = np.zeros((seq, QK_DIM), np.float32)
    c[:, :half] = cos
    c[:, half:ROPE_DIM] = cos
    s[:, :half] = -sin
    s[:, half:ROPE_DIM] = sin
    perm = np.zeros((LANES, LANES), np.float32)
    for j in range(LANES):
        if j % QK_DIM < half:
            perm[j + half, j] = 1.0
        elif j % QK_DIM < ROPE_DIM:
            perm[j - half, j] = 1.0
    rep = LANES // QK_DIM
    return (jnp.asarray(np.tile(c, (1, rep))), jnp.asarray(np.tile(s, (1, rep))), jnp.asarray(perm).astype(BF16))


def kernel(x, norm1_g, w_in, short_conv_w, short_conv_b, q_norm_g, k_norm_g, lambda_q1, lambda_k1, lambda_q2, lambda_k2, subln_g, filt_w1, filt_b1, filt_w2, filt_b2, filt_w3, filt_b3, filt_w_out, filt_freq, hyena_skip, w_branch_attn, w_branch_hyena, w_out, norm2_g, w_router, w_gate, w_up, w_down):
    b, l, d = x.shape
    depth = w_in.shape[0]
    n = b * l
    cap = CAPACITY_FACTOR * l // N_EXPERTS
    q_cols = ATTN_HEADS * 2 * QK_DIM
    ctab, stab, perm = _rope_tables(l)
    xc = x.reshape(n, d)
    for li in range(depth):
        lambda_init = 0.8 - 0.6 * math.exp(-0.3 * li)
        proj = _inproj(xc, norm1_g[li][None], w_in[li].astype(BF16))
        scale = math.log2(math.e) / math.sqrt(QK_DIM)
        gqk = jnp.concatenate([jnp.tile(q_norm_g[li], q_cols // QK_DIM) * scale,
                               jnp.tile(k_norm_g[li], q_cols // QK_DIM)])[None]
        qk = _qkprep(proj, ctab, stab, perm, gqk, l)
        lamv = jnp.stack([lambda_q1[li], lambda_k1[li], lambda_q2[li], lambda_k2[li]])
        attn = _attention(qk, proj, lamv, subln_g[li][None], b, l, lambda_init)

        hyena = _hyena(proj, 3 * q_cols, short_conv_w[li], short_conv_b[li], filt_w1[li], filt_b1[li],
                       filt_w2[li], filt_b2[li], filt_w3[li], filt_b3[li], filt_w_out[li], filt_freq[li],
                       hyena_skip[li], b, l)

        wr_pad = jnp.pad(w_router[li], ((0, 0), (0, LANES - N_EXPERTS))).astype(BF16)
        x1, u2, aff = _merge(attn, hyena, proj, xc, w_branch_attn[li].astype(BF16),
                             w_branch_hyena[li].astype(BF16), w_out[li].astype(BF16),
                             norm2_g[li][None], wr_pad)
        pos, post, afft, offs = _topk(aff, b, l, cap)
        xin = _gather(offs, post, u2, b, l, cap)
        weo = _experts(offs, post, afft, xin, w_gate, w_up, w_down, li, b, l, cap)
        xc = _combine(offs, pos, x1, weo, b, l, cap)
    return xc.reshape(b, l, d)
```

```python
import functools
import math

import jax
import jax.numpy as jnp
import numpy as np
from jax import lax
from jax.experimental import pallas as pl
from jax.experimental.pallas import tpu as pltpu

F32 = jnp.float32
BF16 = jnp.bfloat16

ATTN_HEADS = 4
QK_DIM = 64
V_DIM = 128
ROPE_DIM = 16
ROPE_THETA = 500000.0
HYENA_WIDTH = 512
HYENA_ORDER = 2
FILTER_BANDS = 16
DECAY_TARGET = 1e-2
FAST_DECAY_PCT = 0.3
SLOW_DECAY_PCT = 1.5
N_EXPERTS = 16
CAPACITY_FACTOR = 2
MOE_TILE = 512
EPS = 1e-6
LANES = 128
FILT_GROUPS = LANES // FILTER_BANDS
VMEM_LIMIT = 56 * 1024 * 1024


def _cparams(sem):
    return pltpu.CompilerParams(dimension_semantics=sem, vmem_limit_bytes=VMEM_LIMIT)


def _inproj_kernel(x_ref, g_ref, w_ref, o_ref, u_ref):
    @pl.when(pl.program_id(1) == 0)
    def _():
        x = x_ref[...]
        ms = jnp.mean(x * x, axis=-1, keepdims=True)
        u_ref[...] = (x * lax.rsqrt(ms + EPS) * g_ref[...]).astype(BF16)

    o_ref[...] = jnp.dot(u_ref[...], w_ref[...], preferred_element_type=F32).astype(o_ref.dtype)


def _inproj(x2, g, w_bf16, tm=2048, tn=1280):
    n, d = x2.shape
    cols = w_bf16.shape[1]
    return pl.pallas_call(
        _inproj_kernel,
        out_shape=jax.ShapeDtypeStruct((n, cols), BF16),
        grid=(n // tm, cols // tn),
        in_specs=[
            pl.BlockSpec((tm, d), lambda i, j: (i, 0)),
            pl.BlockSpec((1, d), lambda i, j: (0, 0)),
            pl.BlockSpec((d, tn), lambda i, j: (0, j)),
        ],
        out_specs=pl.BlockSpec((tm, tn), lambda i, j: (i, j)),
        scratch_shapes=[pltpu.VMEM((tm, d), BF16)],
        compiler_params=_cparams(("arbitrary", "arbitrary")),
        name="inproj",
    )(x2, g, w_bf16)


def _qkprep_kernel(p_ref, c_ref, s_ref, perm_ref, g_ref, o_ref):
    lane = lax.broadcasted_iota(jnp.int32, (1, LANES), 1)
    lo = lane < QK_DIM
    c = c_ref[...]
    s = s_ref[...]
    perm = perm_ref[...]
    for j in range(p_ref.shape[1] // LANES):
        t = p_ref[:, j * LANES:(j + 1) * LANES].astype(F32)
        sq = t * t
        ss_lo = jnp.sum(jnp.where(lo, sq, 0.0), axis=-1, keepdims=True)
        ss_hi = jnp.sum(jnp.where(lo, 0.0, sq), axis=-1, keepdims=True)
        r = lax.rsqrt(jnp.where(lo, ss_lo, ss_hi) * (1.0 / QK_DIM) + EPS)
        y = t * r * g_ref[:, j * LANES:(j + 1) * LANES]
        y_hi = y.astype(BF16)
        y_lo = (y - y_hi.astype(F32)).astype(BF16)
        partner = (jnp.dot(y_hi, perm, preferred_element_type=F32)
                   + jnp.dot(y_lo, perm, preferred_element_type=F32))
        o_ref[:, j * LANES:(j + 1) * LANES] = (y * c + partner * s).astype(o_ref.dtype)


def _qkprep(proj, ctab, stab, perm, gqk, seq, tm=1024):
    n = proj.shape[0]
    w = gqk.shape[1]
    nb = seq // tm
    return pl.pallas_call(
        _qkprep_kernel,
        out_shape=jax.ShapeDtypeStruct((n, w), BF16),
        grid=(n // tm,),
        in_specs=[
            pl.BlockSpec((tm, w), lambda i: (i, 0)),
            pl.BlockSpec((tm, LANES), lambda i: (i % nb, 0)),
            pl.BlockSpec((tm, LANES), lambda i: (i % nb, 0)),
            pl.BlockSpec((LANES, LANES), lambda i: (0, 0)),
            pl.BlockSpec((1, w), lambda i: (0, 0)),
        ],
        out_specs=pl.BlockSpec((tm, w), lambda i: (i, 0)),
        compiler_params=_cparams(("arbitrary",)),
        name="qkprep",
    )(proj, ctab, stab, perm, gqk)


def _attn_kernel(q_ref, k_ref, v_ref, lam_ref, g_ref, o_ref, *, tk, unroll, lambda_init):
    tq = q_ref.shape[0]
    seq = k_ref.shape[0]
    lane = lax.broadcasted_iota(jnp.int32, (1, LANES), 1)
    q = q_ref[...]
    zero = jnp.zeros_like(q)
    qs = jnp.concatenate([jnp.where(lane < QK_DIM, q, zero), jnp.where(lane < QK_DIM, zero, q)], axis=0)
    lv = lam_ref[...]
    lam = (jnp.exp(jnp.sum(lv[0:1] * lv[1:2], axis=-1, keepdims=True))
           - jnp.exp(jnp.sum(lv[2:3] * lv[3:4], axis=-1, keepdims=True)) + lambda_init)

    def step(c, carry):
        m, l, a = carry
        off = pl.multiple_of(c * tk, tk)
        kc = k_ref[pl.ds(off, tk), :]
        vc = v_ref[pl.ds(off, tk), :]
        s = lax.dot_general(qs, kc, (((1,), (1,)), ((), ())), preferred_element_type=F32)
        mn = jnp.maximum(m, jnp.max(s, axis=-1, keepdims=True))
        p = jnp.exp2((s - mn).astype(BF16))
        al = jnp.exp2(m - mn)
        pf = p.astype(F32)
        part = pf[:, 0:LANES]
        for j in range(1, tk // LANES):
            part = part + pf[:, j * LANES:(j + 1) * LANES]
        l = al * l + part
        a = al * a + jnp.dot(p, vc, preferred_element_type=F32)
        return mn, l, a

    init = (jnp.full((2 * tq, 1), -jnp.inf, F32), jnp.zeros((2 * tq, LANES), F32), jnp.zeros((2 * tq, V_DIM), F32))
    _, l, a = lax.fori_loop(0, seq // tk, step, init, unroll=unroll)
    a = a / jnp.sum(l, axis=-1, keepdims=True)
    o = a[:tq] - lam * a[tq:]
    ms = jnp.mean(o * o, axis=-1, keepdims=True)
    o = o * lax.rsqrt(ms + EPS) * g_ref[...] * (1.0 - lambda_init)
    o_ref[...] = o.astype(o_ref.dtype)


def _attention(qk, proj, lamv, subln_g, batch, seq, lambda_init, tq=1024, tk=512, unroll=8):
    nq = seq // tq
    vcol0 = (2 * ATTN_HEADS * 2 * QK_DIM) // V_DIM
    return pl.pallas_call(
        functools.partial(_attn_kernel, tk=tk, unroll=unroll, lambda_init=lambda_init),
        out_shape=jax.ShapeDtypeStruct((batch * seq, ATTN_HEADS * V_DIM), BF16),
        grid=(batch, ATTN_HEADS, nq),
        in_specs=[
            pl.BlockSpec((tq, LANES), lambda b, h, i: (b * nq + i, h)),
            pl.BlockSpec((seq, LANES), lambda b, h, i: (b, ATTN_HEADS + h)),
            pl.BlockSpec((seq, V_DIM), lambda b, h, i: (b, vcol0 + h)),
            pl.BlockSpec((4, QK_DIM), lambda b, h, i: (0, 0)),
            pl.BlockSpec((1, V_DIM), lambda b, h, i: (0, 0)),
        ],
        out_specs=pl.BlockSpec((tq, V_DIM), lambda b, h, i: (b * nq + i, h)),
        compiler_params=_cparams(("arbitrary", "arbitrary", "arbitrary")),
        name="diffattn",
    )(qk, qk, proj, lamv, subln_g)


def _merge_kernel(at_ref, hy_ref, ga_ref, gh_ref, x_ref, wpa_ref, wph_ref, wo_ref, g2_ref, wr_ref,
                  x1_ref, u2_ref, aff_ref):
    ga = jax.nn.sigmoid(ga_ref[...].astype(F32))
    gh = jax.nn.sigmoid(gh_ref[...].astype(F32))
    merged = (ga * jnp.dot(at_ref[...], wpa_ref[...], preferred_element_type=F32)
              + gh * jnp.dot(hy_ref[...], wph_ref[...], preferred_element_type=F32))
    x1 = x_ref[...] + jnp.dot(merged.astype(BF16), wo_ref[...], preferred_element_type=F32)
    x1_ref[...] = x1
    ms = jnp.mean(x1 * x1, axis=-1, keepdims=True)
    u2 = (x1 * lax.rsqrt(ms + EPS) * g2_ref[...]).astype(BF16)
    u2_ref[...] = u2
    logits = jnp.dot(u2, wr_ref[...], preferred_element_type=F32)
    lane = lax.broadcasted_iota(jnp.int32, logits.shape, 1)
    logits = jnp.where(lane < N_EXPERTS, logits, -jnp.inf)
    e = jnp.exp(logits - jnp.max(logits, axis=-1, keepdims=True))
    aff_ref[...] = e / jnp.sum(e, axis=-1, keepdims=True)


def _merge(attn, hyena, proj, x2, wpa, wph, wo, g2, wr_pad, tm=1024):
    n, d = x2.shape
    gcol0 = (proj.shape[1] - 2 * d) // d
    full = lambda shape: pl.BlockSpec(shape, lambda i: (0, 0))
    return pl.pallas_call(
        _merge_kernel,
        out_shape=(jax.ShapeDtypeStruct((n, d), F32), jax.ShapeDtypeStruct((n, d), BF16),
                   jax.ShapeDtypeStruct((n, LANES), F32)),
        grid=(n // tm,),
        in_specs=[
            pl.BlockSpec((tm, attn.shape[1]), lambda i: (i, 0)),
            pl.BlockSpec((tm, hyena.shape[1]), lambda i: (i, 0)),
            pl.BlockSpec((tm, d), lambda i: (i, gcol0)),
            pl.BlockSpec((tm, d), lambda i: (i, gcol0 + 1)),
            pl.BlockSpec((tm, d), lambda i: (i, 0)),
            full(wpa.shape), full(wph.shape), full(wo.shape), full(g2.shape), full(wr_pad.shape),
        ],
        out_specs=(pl.BlockSpec((tm, d), lambda i: (i, 0)), pl.BlockSpec((tm, d), lambda i: (i, 0)),
                   pl.BlockSpec((tm, LANES), lambda i: (i, 0))),
        compiler_params=_cparams(("arbitrary",)),
        name="merge",
    )(attn, hyena, proj, proj, x2, wpa, wph, wo, g2, wr_pad)


def _excl_cumsum_rows(mask_f32, tri, blk):
    rows = mask_f32.shape[0]
    carry = jnp.zeros((1, LANES), F32)
    outs = []
    for r in range(rows // blk):
        mb = mask_f32[r * blk:(r + 1) * blk]
        outs.append(jnp.dot(tri, mb.astype(BF16), preferred_element_type=F32) + carry)
        carry = carry + jnp.sum(mb, axis=0, keepdims=True)
    return jnp.concatenate(outs, axis=0)


def _topk_kernel(aff_ref, pos_ref, post_ref, afft_ref, offs_ref, *, cap, blk, batch):
    seq = aff_ref.shape[0] // batch
    aff = aff_ref[0:seq]
    for b in range(1, batch):
        aff = aff + pltpu.roll(aff_ref[b * seq:(b + 1) * seq], b * N_EXPERTS, 1)

    def search(i, prefix):
        cand = prefix | jnp.left_shift(jnp.int32(1), 29 - i)
        cnt = jnp.sum((aff >= pltpu.bitcast(cand, F32)[0:1]).astype(F32), axis=0, keepdims=True)
        return jnp.where(cnt >= cap, cand, prefix)

    lo = pltpu.bitcast(lax.fori_loop(0, 30, search, jnp.zeros((8, LANES), jnp.int32)), F32)[0:1]
    thr = jnp.min(jnp.where(aff >= lo, aff, jnp.inf), axis=0, keepdims=True)
    gt = (aff > thr).astype(F32)
    eq = (aff == thr).astype(F32)
    need = cap - jnp.sum(gt, axis=0, keepdims=True)
    ri = lax.broadcasted_iota(jnp.int32, (blk, blk), 0)
    ci = lax.broadcasted_iota(jnp.int32, (blk, blk), 1)
    tri = (ci < ri).astype(BF16)
    sel = gt + eq * (_excl_cumsum_rows(eq, tri, blk) < need).astype(F32)
    before = _excl_cumsum_rows(sel, tri, blk)
    pos = jnp.where(sel > 0.0, before, -1.0)
    pos_t = pos.T
    aff_t = aff.T
    lane = lax.broadcasted_iota(jnp.int32, (1, LANES), 1)
    for b in range(batch):
        mine = pos if b == 0 else pltpu.roll(pos, LANES - b * N_EXPERTS, 1)
        pos_ref[b * seq:(b + 1) * seq] = jnp.where(lane < N_EXPERTS, mine, -1.0)
        post_ref[b] = pos_t[b * N_EXPERTS:(b + 1) * N_EXPERTS]
        afft_ref[b] = aff_t[b * N_EXPERTS:(b + 1) * N_EXPERTS]
    offs_ref[...] = jnp.concatenate([before[j * MOE_TILE:j * MOE_TILE + 1] for j in range(offs_ref.shape[0])], axis=0)


def _topk(aff, batch, seq, cap, blk=256):
    assert batch * N_EXPERTS <= LANES
    nt = seq // MOE_TILE
    pos, post, afft, offs = pl.pallas_call(
        functools.partial(_topk_kernel, cap=cap, blk=blk, batch=batch),
        out_shape=(jax.ShapeDtypeStruct((batch * seq, LANES), F32),
                   jax.ShapeDtypeStruct((batch, N_EXPERTS, seq), F32),
                   jax.ShapeDtypeStruct((batch, N_EXPERTS, seq), F32),
                   jax.ShapeDtypeStruct((nt, LANES), F32)),
        compiler_params=pltpu.CompilerParams(vmem_limit_bytes=VMEM_LIMIT),
        name="topk",
    )(aff)
    offs = offs[:, :batch * N_EXPERTS].reshape(nt, batch, N_EXPERTS).transpose(1, 0, 2)
    return pos, post, afft, offs.astype(jnp.int32).reshape(-1)


def _align_down(x, m):
    sh = m.bit_length() - 1
    return lax.shift_left(lax.shift_right_logical(x, sh), sh)


def _ceil_div(x, m):
    return lax.shift_right_logical(x + (m - 1), m.bit_length() - 1)


def _windows(lo0, end, body, win):
    body(0, 0)
    lax.fori_loop(1, _ceil_div(end - lo0, win), body, 0)


def _gather_kernel(offs_ref, post_ref, u_ref, x_ref, *, cap, win):
    b = pl.program_id(0)
    c = pl.program_id(1)
    nt = pl.num_programs(1)

    @pl.when(c == 0)
    def _():
        x_ref[...] = jnp.zeros_like(x_ref)

    uc = u_ref[...]
    slot0 = lax.broadcasted_iota(jnp.int32, (win, MOE_TILE), 0).astype(F32)
    base = (b * nt + c) * N_EXPERTS
    last = pl.num_programs(0) * nt * N_EXPERTS - 1
    starts, hots = [], []
    for e in range(N_EXPERTS):
        s = pl.multiple_of(jnp.minimum(_align_down(offs_ref[base + e], 16), cap - win), 16)
        hots.append(jnp.where((slot0 + s.astype(F32)) == post_ref[0, e:e + 1, :], 1.0, 0.0).astype(BF16))
        starts.append(s)
    rows = jnp.dot(jnp.concatenate(hots, axis=0), uc, preferred_element_type=F32)
    for e in range(N_EXPERTS):
        x_ref[0, e, pl.ds(starts[e], win), :] += rows[e * win:(e + 1) * win].astype(BF16)
    for e in range(N_EXPERTS):
        lo0 = _align_down(offs_ref[base + e], 16)
        end = jnp.where(c + 1 < nt, offs_ref[jnp.minimum(base + N_EXPERTS + e, last)], cap)

        def more(k, carry, lo0=lo0, e=e):
            lo = lo0 + k * win
            s = pl.multiple_of(jnp.minimum(lo, cap - win), 16)
            srow = slot0 + s.astype(F32)
            hit = jnp.logical_and(srow == post_ref[0, e:e + 1, :], srow >= lo.astype(F32))
            x_ref[0, e, pl.ds(s, win), :] += jnp.dot(jnp.where(hit, 1.0, 0.0).astype(BF16), uc,
                                                     preferred_element_type=F32).astype(BF16)
            return carry

        lax.fori_loop(1, _ceil_div(end - lo0, win), more, 0)


def _gather(offs, post, u2, batch, seq, cap, win=128):
    d = u2.shape[1]
    nt = seq // MOE_TILE
    grid_spec = pltpu.PrefetchScalarGridSpec(
        num_scalar_prefetch=1,
        grid=(batch, nt),
        in_specs=[
            pl.BlockSpec((1, N_EXPERTS, MOE_TILE), lambda b, c, o: (b, 0, c)),
            pl.BlockSpec((MOE_TILE, d), lambda b, c, o: (b * nt + c, 0)),
        ],
        out_specs=pl.BlockSpec((1, N_EXPERTS, cap, d), lambda b, c, o: (b, 0, 0, 0)),
    )
    return pl.pallas_call(
        functools.partial(_gather_kernel, cap=cap, win=win),
        out_shape=jax.ShapeDtypeStruct((batch, N_EXPERTS, cap, d), BF16),
        grid_spec=grid_spec,
        compiler_params=_cparams(("arbitrary", "arbitrary")),
        name="gather",
    )(offs, post, u2)


def _expert_kernel(offs_ref, post_ref, afft_ref, x_ref, wg_ref, wu_ref, wd_ref, o_ref, gs_ref, w_ref, *, cap, win):
    e = pl.program_id(0)
    b = pl.program_id(1)

    @pl.when(e < N_EXPERTS)
    def _():
        for m, src in enumerate((wg_ref, wu_ref, wd_ref)):
            part = src.shape[2]
            w_ref[jnp.bitwise_and(e, 1), m, pl.ds(pl.multiple_of(b * part, part), part), :] = src[0, 0].astype(BF16)

    @pl.when(e == 0)
    def _():
        o_ref[...] = jnp.zeros_like(o_ref)

    @pl.when(e > 0)
    def _():
        _expert_body(offs_ref, post_ref, afft_ref, x_ref, w_ref.at[1 - jnp.bitwise_and(e, 1)], o_ref, gs_ref,
                     e - 1, b, cap, win)


def _expert_body(offs_ref, post_ref, afft_ref, x_ref, w_ref, o_ref, gs_ref, e, b, cap, win):
    seq = post_ref.shape[2]
    nt = seq // MOE_TILE
    gs_ref[...] = jnp.zeros_like(gs_ref)
    slot0 = lax.broadcasted_iota(jnp.int32, (win, MOE_TILE), 0).astype(F32)
    for c in range(nt):
        base = (b * nt + c) * N_EXPERTS + e
        lo0 = _align_down(offs_ref[base], 8)
        end = offs_ref[base + N_EXPERTS] if c + 1 < nt else cap
        prow = post_ref[0, pl.ds(e, 1), c * MOE_TILE:(c + 1) * MOE_TILE]
        arow = afft_ref[0, pl.ds(e, 1), c * MOE_TILE:(c + 1) * MOE_TILE]

        def window(k, carry, lo0=lo0, prow=prow, arow=arow):
            lo = lo0 + k * win
            s = pl.multiple_of(jnp.minimum(lo, cap - win), 8)
            srow = slot0 + s.astype(F32)
            hit = jnp.logical_and(srow == prow, srow >= lo.astype(F32))
            gs_ref[pl.ds(s, win), :] += jnp.sum(jnp.where(hit, arow, 0.0), axis=-1, keepdims=True)
            return carry

        _windows(lo0, end, window, win)
    xb = x_ref[0, 0]
    hg = jnp.dot(xb, w_ref[0], preferred_element_type=F32)
    hu = jnp.dot(xb, w_ref[1], preferred_element_type=F32)
    act = (hg * jax.nn.sigmoid(hg) * hu).astype(BF16)
    eo = jnp.dot(act, w_ref[2], preferred_element_type=F32)
    o_ref[0, 0] = (eo * gs_ref[...]).astype(o_ref.dtype)


def _experts(offs, post, afft, xin, wg, wu, wd, layer, batch, seq, cap, win=128):
    d = xin.shape[3]
    assert wg.shape[2:] == (d, d) and wd.shape[2:] == (d, d) and d % (16 * batch) == 0
    part = d // batch
    wspec = pl.BlockSpec((1, 1, part, d), lambda e, b, o: (layer, jnp.minimum(e, N_EXPERTS - 1), b, 0))
    grid_spec = pltpu.PrefetchScalarGridSpec(
        num_scalar_prefetch=1,
        grid=(N_EXPERTS + 1, batch),
        in_specs=[
            pl.BlockSpec((1, N_EXPERTS, seq), lambda e, b, o: (b, 0, 0)),
            pl.BlockSpec((1, N_EXPERTS, seq), lambda e, b, o: (b, 0, 0)),
            pl.BlockSpec((1, 1, cap, d), lambda e, b, o: (b, jnp.maximum(e - 1, 0), 0, 0)),
            wspec, wspec, wspec,
        ],
        out_specs=pl.BlockSpec((1, 1, cap, d), lambda e, b, o: (b, jnp.where(e == 0, N_EXPERTS, e - 1), 0, 0)),
        scratch_shapes=[pltpu.VMEM((cap, 1), F32), pltpu.VMEM((2, 3, d, d), BF16)],
    )
    return pl.pallas_call(
        functools.partial(_expert_kernel, cap=cap, win=win),
        out_shape=jax.ShapeDtypeStruct((batch, N_EXPERTS + 1, cap, d), BF16),
        grid_spec=grid_spec,
        compiler_params=_cparams(("arbitrary", "arbitrary")),
        name="experts",
    )(offs, post, afft, xin, wg, wu, wd)


def _combine_kernel(offs_ref, pos_ref, x1_ref, weo_ref, o_ref, *, cap, win):
    b = pl.program_id(0)
    t = pl.program_id(1)
    nt = pl.num_programs(1)
    tt = pos_ref.shape[0]
    pos = pos_ref[...]
    slot0 = lax.broadcasted_iota(jnp.int32, (tt, win), 1).astype(F32)
    base = (b * nt + t) * N_EXPERTS
    last = pl.num_programs(0) * nt * N_EXPERTS - 1
    acc = x1_ref[...]
    for e0 in range(0, N_EXPERTS, 2):
        hots, rows = [], []
        for e in (e0, e0 + 1):
            s = pl.multiple_of(jnp.minimum(_align_down(offs_ref[base + e], 16), cap - win), 16)
            hots.append(jnp.where((slot0 + s.astype(F32)) == pos[:, e:e + 1], 1.0, 0.0).astype(BF16))
            rows.append(weo_ref[0, e, pl.ds(s, win), :])
        acc = acc + jnp.dot(jnp.concatenate(hots, axis=1), jnp.concatenate(rows, axis=0),
                            preferred_element_type=F32)
    o_ref[...] = acc
    for e in range(N_EXPERTS):
        lo0 = _align_down(offs_ref[base + e], 16)
        end = jnp.where(t + 1 < nt, offs_ref[jnp.minimum(base + N_EXPERTS + e, last)], cap)

        def window(k, carry, lo0=lo0, e=e):
            lo = lo0 + k * win
            s = pl.multiple_of(jnp.minimum(lo, cap - win), 16)
            srow = slot0 + s.astype(F32)
            hit = jnp.logical_and(srow == pos[:, e:e + 1], srow >= lo.astype(F32))
            o_ref[...] += jnp.dot(jnp.where(hit, 1.0, 0.0).astype(BF16), weo_ref[0, e, pl.ds(s, win), :],
                                  preferred_element_type=F32)
            return carry

        lax.fori_loop(1, _ceil_div(end - lo0, win), window, 0)


def _combine(offs, pos, x1, weo, batch, seq, cap, win=128):
    d = x1.shape[1]
    tt = MOE_TILE
    nt = seq // tt
    grid_spec = pltpu.PrefetchScalarGridSpec(
        num_scalar_prefetch=1,
        grid=(batch, nt),
        in_specs=[
            pl.BlockSpec((tt, LANES), lambda b, t, o: (b * nt + t, 0)),
            pl.BlockSpec((tt, d), lambda b, t, o: (b * nt + t, 0)),
            pl.BlockSpec((1, N_EXPERTS, cap, d), lambda b, t, o: (b, 0, 0, 0)),
        ],
        out_specs=pl.BlockSpec((tt, d), lambda b, t, o: (b * nt + t, 0)),
    )
    return pl.pallas_call(
        functools.partial(_combine_kernel, cap=cap, win=win),
        out_shape=jax.ShapeDtypeStruct(x1.shape, F32),
        grid_spec=grid_spec,
        compiler_params=_cparams(("arbitrary", "arbitrary")),
        name="combine",
    )(offs, pos, x1, weo)


FFT_R = 64
FFT_PITCH = FFT_R + 8


def _dft_tables(seq):
    r = FFT_R
    assert seq == r * r
    n = np.arange(r)
    f = np.exp(-2j * np.pi * np.outer(n, n) / r)
    w = lambda e: np.exp(-2j * np.pi * e / (2 * seq))

    def real_rep(a):
        return np.block([[a.real, -a.imag], [a.imag, a.real]]).astype(np.float32)

    m1 = [f, f * w(r * n)[None, :]]
    m2 = [f * w(n * m)[None, :] for m in range(2 * r)]
    m4 = [f.conj() / (2 * seq), w(-r * n)[:, None] * f.conj() / (2 * seq)]
    t1 = np.stack([real_rep(a) for a in m1])
    t2 = np.stack([real_rep(a) for a in m2])
    t3 = np.stack([real_rep(a.conj().T) for a in m2])
    t4 = np.stack([real_rep(a) for a in m4])
    t1r = np.stack([np.concatenate([a.real, a.imag], axis=0).astype(np.float32) for a in m1])
    return tuple(jnp.asarray(t).astype(BF16) for t in (t1, t2, t3, t4, t1r))


def _slab(r):
    return pl.ds(pl.multiple_of(r * FFT_PITCH, 8), FFT_R)


def _across(r):
    return pl.ds(r, FFT_R, stride=FFT_PITCH)


def _cat_bf16(a, b):
    return jnp.concatenate([a, b], axis=0).astype(BF16)


FFT_UNROLL = 32


def _loop(body, step=1):
    def wrapped(i, c):
        body(i * step)
        return c
    lax.fori_loop(0, FFT_R // step, wrapped, 0, unroll=FFT_UNROLL // step)


def _pair_bf16(re_ref, im_ref, idx0, idx1):
    return jnp.concatenate([_cat_bf16(re_ref[idx0, :], im_ref[idx0, :]),
                            _cat_bf16(re_ref[idx1, :], im_ref[idx1, :])], axis=1)


def _bf16_pieces(w):
    hi = w.astype(BF16)
    return hi, (w - hi.astype(F32)).astype(BF16)


def _dot3(a, w_ref):
    a_hi, a_lo = _bf16_pieces(a)
    return (jnp.dot(a_hi, w_ref[0], preferred_element_type=F32)
            + (jnp.dot(a_hi, w_ref[1], preferred_element_type=F32)
               + jnp.dot(a_lo, w_ref[0], preferred_element_type=F32)))


def _filter_kernel(bands_ref, w1t_ref, w1c_ref, w1s_ref, b1_ref, w2_ref, b2_ref, w3_ref, b3_ref, fr_ref,
                   wf_ref, wb_ref, dl_ref, hs_ref, hd_ref, *, seq):
    tm = hs_ref.shape[0]
    rows = tm // FILT_GROUPS
    hid = wf_ref.shape[1]
    base = pl.program_id(0) * tm
    lane = lax.broadcasted_iota(jnp.int32, (rows, LANES), 1)
    grp = lax.shift_right_logical(lane, FILTER_BANDS.bit_length() - 1)
    pos = (base + grp * rows + lax.broadcasted_iota(jnp.int32, (rows, LANES), 0)).astype(F32)
    first = jnp.bitwise_and(lane, FILTER_BANDS - 1) == 0
    fr = fr_ref[...]
    tscale = 1.0 / max(seq - 1, 1)

    def mlp(p):
        ang = ((2.0 * math.pi / seq) * p) * bands_ref[...]
        pre = (_dot3(jnp.cos(ang), w1c_ref) + _dot3(-jnp.sin(ang), w1s_ref)
               + _dot3(jnp.where(first, p * tscale, 0.0), w1t_ref) + b1_ref[...])
        h = jnp.sin(fr * pre)
        h = jnp.sin(fr * (_dot3(h, w2_ref) + b2_ref[...]))
        return jnp.sin(fr * (_dot3(h, w3_ref) + b3_ref[...]))

    hf_h = mlp(pos)
    hb_h = mlp(seq - pos)
    dl = dl_ref[...]
    for g in range(FILT_GROUPS):
        pcol = (base + g * rows + lax.broadcasted_iota(jnp.int32, (rows, 1), 0)).astype(F32)
        hf = _dot3(hf_h[:, g * hid:(g + 1) * hid], wf_ref) * jnp.exp(-(pcol * tscale) * dl)
        hb = _dot3(hb_h[:, g * hid:(g + 1) * hid], wb_ref) * jnp.exp(-((seq - pcol) * tscale) * dl)
        hb = jnp.where(pcol > 0.0, hb, 0.0)
        hs_ref[g * rows:(g + 1) * rows, :] = hf + hb
        hd_ref[g * rows:(g + 1) * rows, :] = hf - hb


def _filters(bands, w1t, w1c, w1s, b1, w2, b2, w3, b3, fr, wf, wb, dl, seq, tm=512):
    cols = wf.shape[1]
    pieces = lambda w: jnp.stack(_bf16_pieces(w))
    args = (bands, pieces(w1t), pieces(w1c), pieces(w1s), b1, pieces(w2), b2, pieces(w3), b3, fr,
            pieces(wf), pieces(wb), dl)
    return pl.pallas_call(
        functools.partial(_filter_kernel, seq=seq),
        out_shape=(jax.ShapeDtypeStruct((seq, cols), F32), jax.ShapeDtypeStruct((seq, cols), F32)),
        grid=(seq // tm,),
        in_specs=[pl.BlockSpec(a.shape, lambda i, nd=a.ndim: (0,) * nd) for a in args],
        out_specs=(pl.BlockSpec((tm, cols), lambda i: (i, 0)), pl.BlockSpec((tm, cols), lambda i: (i, 0))),
        compiler_params=_cparams(("arbitrary",)),
        name="hyfilter",
    )(*args)


def _spectra_kernel(hs_ref, hd_ref, t1r_ref, t2_ref, h_ref, x_ref, pr_ref, pi_ref):
    for par, src in ((0, hs_ref), (1, hd_ref)):
        def fill(r, src=src):
            x_ref[_slab(r), :] = src[pl.ds(pl.multiple_of(r * FFT_R, FFT_R), FFT_R), :]
        _loop(fill)

        def first(r, par=par):
            o = jnp.dot(t1r_ref[par], x_ref[_across(r), :].astype(BF16), preferred_element_type=F32)
            pr_ref[_slab(r), :] = o[:FFT_R]
            pi_ref[_slab(r), :] = o[FFT_R:]
        _loop(first)

        def second(r, par=par):
            o = jnp.dot(t2_ref[2 * r + par], _cat_bf16(pr_ref[_across(r), :], pi_ref[_across(r), :]),
                        preferred_element_type=F32)
            rows = pl.ds(pl.multiple_of(r * FFT_R, FFT_R), FFT_R)
            h_ref[0, 2 * par, rows, :] = o[:FFT_R].astype(h_ref.dtype)
            h_ref[0, 2 * par + 1, rows, :] = o[FFT_R:].astype(h_ref.dtype)
        _loop(second)


def _spectra(hs, hd, t1r, t2, seq):
    ncol = hs.shape[1] // LANES
    nct = ncol // HYENA_ORDER
    prow = FFT_R * FFT_PITCH
    const = lambda a: pl.BlockSpec(a.shape, lambda o, c: (0,) * a.ndim, pipeline_mode=pl.Buffered(1))
    return pl.pallas_call(
        _spectra_kernel,
        out_shape=jax.ShapeDtypeStruct((HYENA_ORDER, 4, seq, nct * LANES), BF16),
        grid=(HYENA_ORDER, nct),
        in_specs=[pl.BlockSpec((seq, LANES), lambda o, c: (0, o * nct + c)),
                  pl.BlockSpec((seq, LANES), lambda o, c: (0, o * nct + c)),
                  const(t1r), const(t2)],
        out_specs=pl.BlockSpec((1, 4, seq, LANES), lambda o, c: (o, 0, 0, c)),
        scratch_shapes=[pltpu.VMEM((prow, LANES), F32)] * 3,
        compiler_params=_cparams(("arbitrary", "arbitrary")),
        name="hyspectra",
    )(hs, hd, t1r, t2)


def _short_conv(x, w_ref, b_ref):
    rows = x.shape[0]
    ri = lax.broadcasted_iota(jnp.int32, x.shape, 0)
    prev = jnp.where(ri == 0, 0.0, pltpu.roll(x, 1, 0))
    nxt = jnp.where(ri == rows - 1, 0.0, pltpu.roll(x, rows - 1, 0))
    return prev * w_ref[0:1, :] + x * w_ref[1:2, :] + nxt * w_ref[2:3, :] + b_ref[...]


def _hyconv_kernel(z_ref, g_ref, cwz_ref, cbz_ref, cwg_ref, cbg_ref, skip_ref, h_ref,
                   t1_ref, t2_ref, t3_ref, t4_ref, o_ref,
                   xr_ref, xi_ref, pr_ref, pi_ref, qr_ref, qi_ref, yr_ref, yi_ref, *, conv_z):
    seq = z_ref.shape[0] // 2
    for half, dst in ((0, xr_ref), (1, xi_ref)):
        z = z_ref[half * seq:(half + 1) * seq, :].astype(F32)
        if conv_z:
            z = _short_conv(z, cwz_ref, cbz_ref)
        for j in range(FFT_R):
            dst[j * FFT_PITCH:j * FFT_PITCH + FFT_R, :] = z[j * FFT_R:(j + 1) * FFT_R]

    for par in (0, 1):
        def first(r, par=par):
            o = jnp.dot(t1_ref[par], _pair_bf16(xr_ref, xi_ref, _across(r), _across(r + 1)),
                        preferred_element_type=F32)
            pr_ref[_slab(r), :] = o[:FFT_R, :LANES]
            pi_ref[_slab(r), :] = o[FFT_R:, :LANES]
            pr_ref[_slab(r + 1), :] = o[:FFT_R, LANES:]
            pi_ref[_slab(r + 1), :] = o[FFT_R:, LANES:]
        _loop(first, step=2)

        def second(r, par=par):
            o = jnp.dot(t2_ref[2 * r + par], _cat_bf16(pr_ref[_across(r), :], pi_ref[_across(r), :]),
                        preferred_element_type=F32)
            rows = pl.ds(pl.multiple_of(r * FFT_R, FFT_R), FFT_R)
            hr = h_ref[0, 2 * par, rows, :].astype(F32)
            hi = h_ref[0, 2 * par + 1, rows, :].astype(F32)
            ar, ai = o[:FFT_R], o[FFT_R:]
            qr_ref[_slab(r), :] = ar * hr - ai * hi
            qi_ref[_slab(r), :] = ar * hi + ai * hr
        _loop(second)

        def third(r, par=par):
            o = jnp.dot(t3_ref[2 * r + par], _cat_bf16(qr_ref[_slab(r), :], qi_ref[_slab(r), :]),
                        preferred_element_type=F32)
            pr_ref[_across(r), :] = o[:FFT_R]
            pi_ref[_across(r), :] = o[FFT_R:]
        _loop(third)

        def fourth(r, par=par):
            o = jnp.dot(t4_ref[par], _pair_bf16(pr_ref, pi_ref, _slab(r), _slab(r + 1)),
                        preferred_element_type=F32)
            for k, cols in ((r, slice(0, LANES)), (r + 1, slice(LANES, 2 * LANES))):
                if par == 0:
                    yr_ref[_across(k), :] = o[:FFT_R, cols]
                    yi_ref[_across(k), :] = o[FFT_R:, cols]
                else:
                    yr_ref[_across(k), :] = yr_ref[_across(k), :] + o[:FFT_R, cols]
                    yi_ref[_across(k), :] = yi_ref[_across(k), :] + o[FFT_R:, cols]
        _loop(fourth, step=2)

    skip = skip_ref[...]
    for half, (y_ref, x_ref) in enumerate(((yr_ref, xr_ref), (yi_ref, xi_ref))):
        g = _short_conv(g_ref[half * seq:(half + 1) * seq, :].astype(F32), cwg_ref, cbg_ref)
        for j in range(FFT_R):
            src = slice(j * FFT_PITCH, j * FFT_PITCH + FFT_R)
            dst = slice(half * seq + j * FFT_R, half * seq + (j + 1) * FFT_R)
            o_ref[dst, :] = (g[j * FFT_R:(j + 1) * FFT_R] * (y_ref[src, :] + skip * x_ref[src, :])).astype(o_ref.dtype)


def _hyconv(zsrc, zcol0, gcol0, proj, conv_w, conv_b, zpart, gpart, skip, spec, order, tabs, batch, seq, conv_z):
    t1, t2, t3, t4 = tabs
    nct = HYENA_WIDTH // LANES
    prow = FFT_R * FFT_PITCH
    const = lambda a: pl.BlockSpec(a.shape, lambda c, p: (0,) * a.ndim, pipeline_mode=pl.Buffered(1))
    return pl.pallas_call(
        functools.partial(_hyconv_kernel, conv_z=conv_z),
        out_shape=jax.ShapeDtypeStruct((batch * seq, HYENA_WIDTH), BF16),
        grid=(nct, batch // 2),
        in_specs=[
            pl.BlockSpec((2 * seq, LANES), lambda c, p: (p, zcol0 + c)),
            pl.BlockSpec((2 * seq, LANES), lambda c, p: (p, gcol0 + c)),
            pl.BlockSpec((3, LANES), lambda c, p: (0, zpart * nct + c)),
            pl.BlockSpec((1, LANES), lambda c, p: (0, zpart * nct + c)),
            pl.BlockSpec((3, LANES), lambda c, p: (0, gpart * nct + c)),
            pl.BlockSpec((1, LANES), lambda c, p: (0, gpart * nct + c)),
            pl.BlockSpec((1, LANES), lambda c, p: (0, c)),
            pl.BlockSpec((1, 4, seq, LANES), lambda c, p: (order, 0, 0, c), pipeline_mode=pl.Buffered(1)),
            const(t1), const(t2), const(t3), const(t4),
        ],
        out_specs=pl.BlockSpec((2 * seq, LANES), lambda c, p: (p, c)),
        scratch_shapes=[pltpu.VMEM((prow, LANES), F32)] * 8,
        compiler_params=_cparams(("arbitrary", "arbitrary")),
        name=f"hyconv{order}",
    )(zsrc, proj, conv_w, conv_b, conv_w, conv_b, skip[order:order + 1], spec, t1, t2, t3, t4)


def _hyena(proj, hycol0, conv_w, conv_b, w1, b1, w2, b2, w3, b3, w_out, freq, skip, batch, seq):
    hid = w2.shape[0]
    eye = jnp.eye(FILT_GROUPS, dtype=F32)
    bdiag = lambda a: jnp.kron(eye, a)
    tile = lambda a: jnp.tile(a[None], (1, FILT_GROUPS))
    bands = tile(jnp.linspace(1e-4, FILTER_BANDS - 1, FILTER_BANDS, dtype=F32))
    min_decay = math.log(DECAY_TARGET) / SLOW_DECAY_PCT
    max_decay = math.log(DECAY_TARGET) / FAST_DECAY_PCT
    deltas = jnp.abs(jnp.linspace(min_decay, max_decay, HYENA_WIDTH, dtype=F32))
    wo4 = w_out.reshape(hid, HYENA_ORDER, 2, HYENA_WIDTH)
    w1t = jnp.pad(w1[0:1], ((0, FILTER_BANDS - 1), (0, 0)))
    hs, hd = _filters(
        bands, bdiag(w1t), bdiag(w1[1:1 + FILTER_BANDS]), bdiag(w1[1 + FILTER_BANDS:]),
        tile(b1), bdiag(w2), tile(b2), bdiag(w3), tile(b3), tile(freq),
        wo4[:, :, 0].reshape(hid, -1), wo4[:, :, 1].reshape(hid, -1),
        jnp.tile(deltas, HYENA_ORDER)[None], seq)
    t1, t2, t3, t4, t1r = _dft_tables(seq)
    spec = _spectra(hs, hd, t1r, t2, seq)
    c0 = hycol0 // LANES
    nct = HYENA_WIDTH // LANES
    cb = conv_b[None]
    z1 = _hyconv(proj, c0, c0 + nct, proj, conv_w, cb, 0, 1, skip, spec, 0, (t1, t2, t3, t4), batch, seq, True)
    return _hyconv(z1, 0, c0 + 2 * nct, proj, conv_w, cb, 0, 2, skip, spec, 1, (t1, t2, t3, t4), batch, seq, False)


def _rope_tables(seq):
    pos = np.arange(seq, dtype=np.float32)
    inv_freq = (ROPE_THETA ** (-np.arange(0, ROPE_DIM, 2, dtype=np.float32) / ROPE_DIM)).astype(np.float32)
    ang = pos[:, None] * inv_freq[None, :]
    cos, sin = np.cos(ang).astype(np.float32), np.sin(ang).astype(np.float32)
    half = ROPE_DIM // 2
    c = np.ones((seq, QK_DIM), np.float32)
    s = np.zeros((seq, QK_DIM), np.float32)
    c[:, :half] = cos
    c[:, half:ROPE_DIM] = cos
    s[:, :half] = -sin
    s[:, half:ROPE_DIM] = sin
    perm = np.zeros((LANES, LANES), np.float32)
    for j in range(LANES):
        if j % QK_DIM < half:
            perm[j + half, j] = 1.0
        elif j % QK_DIM < ROPE_DIM:
            perm[j - half, j] = 1.0
    rep = LANES // QK_DIM
    return (jnp.asarray(np.tile(c, (1, rep))), jnp.asarray(np.tile(s, (1, rep))), jnp.asarray(perm).astype(BF16))


def kernel(x, norm1_g, w_in, short_conv_w, short_conv_b, q_norm_g, k_norm_g, lambda_q1, lambda_k1, lambda_q2, lambda_k2, subln_g, filt_w1, filt_b1, filt_w2, filt_b2, filt_w3, filt_b3, filt_w_out, filt_freq, hyena_skip, w_branch_attn, w_branch_hyena, w_out, norm2_g, w_router, w_gate, w_up, w_down):
    b, l, d = x.shape
    depth = w_in.shape[0]
    n = b * l
    cap = CAPACITY_FACTOR * l // N_EXPERTS
    q_cols = ATTN_HEADS * 2 * QK_DIM
    ctab, stab, perm = _rope_tables(l)
    xc = x.reshape(n, d)
    for li in range(depth):
        lambda_init = 0.8 - 0.6 * math.exp(-0.3 * li)
        proj = _inproj(xc, norm1_g[li][None], w_in[li].astype(BF16))
        scale = math.log2(math.e) / math.sqrt(QK_DIM)
        gqk = jnp.concatenate([jnp.tile(q_norm_g[li], q_cols // QK_DIM) * scale,
                               jnp.tile(k_norm_g[li], q_cols // QK_DIM)])[None]
        qk = _qkprep(proj, ctab, stab, perm, gqk, l)
        lamv = jnp.stack([lambda_q1[li], lambda_k1[li], lambda_q2[li], lambda_k2[li]])
        attn = _attention(qk, proj, lamv, subln_g[li][None], b, l, lambda_init)

        hyena = _hyena(proj, 3 * q_cols, short_conv_w[li], short_conv_b[li], filt_w1[li], filt_b1[li],
                       filt_w2[li], filt_b2[li], filt_w3[li], filt_b3[li], filt_w_out[li], filt_freq[li],
                       hyena_skip[li], b, l)

        wr_pad = jnp.pad(w_router[li], ((0, 0), (0, LANES - N_EXPERTS))).astype(BF16)
        x1, u2, aff = _merge(attn, hyena, proj, xc, w_branch_attn[li].astype(BF16),
                             w_branch_hyena[li].astype(BF16), w_out[li].astype(BF16),
                             norm2_g[li][None], wr_pad)
        pos, post, afft, offs = _topk(aff, b, l, cap)
        xin = _gather(offs, post, u2, b, l, cap)
        weo = _experts(offs, post, afft, xin, w_gate, w_up, w_down, li, b, l, cap)
        xc = _combine(offs, pos, x1, weo, b, l, cap)
    return xc.reshape(b, l, d)
```

```python
import functools
import math

import jax
import jax.numpy as jnp
import numpy as np
from jax import lax
from jax.experimental import pallas as pl
from jax.experimental.pallas import tpu as pltpu

F32 = jnp.float32
BF16 = jnp.bfloat16

ATTN_HEADS = 4
QK_DIM = 64
V_DIM = 128
ROPE_DIM = 16
ROPE_THETA = 500000.0
HYENA_WIDTH = 512
HYENA_ORDER = 2
FILTER_BANDS = 16
DECAY_TARGET = 1e-2
FAST_DECAY_PCT = 0.3
SLOW_DECAY_PCT = 1.5
N_EXPERTS = 16
CAPACITY_FACTOR = 2
MOE_TILE = 512
EPS = 1e-6
LANES = 128
FILT_GROUPS = LANES // FILTER_BANDS
VMEM_LIMIT = 56 * 1024 * 1024


def _cparams(sem):
    return pltpu.CompilerParams(dimension_semantics=sem, vmem_limit_bytes=VMEM_LIMIT)


def _inproj_kernel(x_ref, g_ref, w_ref, o_ref, u_ref):
    @pl.when(pl.program_id(1) == 0)
    def _():
        x = x_ref[...]
        ms = jnp.mean(x * x, axis=-1, keepdims=True)
        u_ref[...] = (x * lax.rsqrt(ms + EPS) * g_ref[...]).astype(BF16)

    o_ref[...] = jnp.dot(u_ref[...], w_ref[...], preferred_element_type=F32).astype(o_ref.dtype)


def _inproj(x2, g, w_bf16, tm=2048, tn=1280):
    n, d = x2.shape
    cols = w_bf16.shape[1]
    return pl.pallas_call(
        _inproj_kernel,
        out_shape=jax.ShapeDtypeStruct((n, cols), BF16),
        grid=(n // tm, cols // tn),
        in_specs=[
            pl.BlockSpec((tm, d), lambda i, j: (i, 0)),
            pl.BlockSpec((1, d), lambda i, j: (0, 0)),
            pl.BlockSpec((d, tn), lambda i, j: (0, j)),
        ],
        out_specs=pl.BlockSpec((tm, tn), lambda i, j: (i, j)),
        scratch_shapes=[pltpu.VMEM((tm, d), BF16)],
        compiler_params=_cparams(("arbitrary", "arbitrary")),
        name="inproj",
    )(x2, g, w_bf16)


def _qkprep_kernel(p_ref, c_ref, s_ref, perm_ref, g_ref, o_ref):
    lane = lax.broadcasted_iota(jnp.int32, (1, LANES), 1)
    lo = lane < QK_DIM
    c = c_ref[...]
    s = s_ref[...]
    perm = perm_ref[...]
    for j in range(p_ref.shape[1] // LANES):
        t = p_ref[:, j * LANES:(j + 1) * LANES].astype(F32)
        sq = t * t
        ss_lo = jnp.sum(jnp.where(lo, sq, 0.0), axis=-1, keepdims=True)
        ss_hi = jnp.sum(jnp.where(lo, 0.0, sq), axis=-1, keepdims=True)
        r = lax.rsqrt(jnp.where(lo, ss_lo, ss_hi) * (1.0 / QK_DIM) + EPS)
        y = t * r * g_ref[:, j * LANES:(j + 1) * LANES]
        y_hi = y.astype(BF16)
        y_lo = (y - y_hi.astype(F32)).astype(BF16)
        partner = (jnp.dot(y_hi, perm, preferred_element_type=F32)
                   + jnp.dot(y_lo, perm, preferred_element_type=F32))
        o_ref[:, j * LANES:(j + 1) * LANES] = (y * c + partner * s).astype(o_ref.dtype)


def _qkprep(proj, ctab, stab, perm, gqk, seq, tm=1024):
    n = proj.shape[0]
    w = gqk.shape[1]
    nb = seq // tm
    return pl.pallas_call(
        _qkprep_kernel,
        out_shape=jax.ShapeDtypeStruct((n, w), BF16),
        grid=(n // tm,),
        in_specs=[
            pl.BlockSpec((tm, w), lambda i: (i, 0)),
            pl.BlockSpec((tm, LANES), lambda i: (i % nb, 0)),
            pl.BlockSpec((tm, LANES), lambda i: (i % nb, 0)),
            pl.BlockSpec((LANES, LANES), lambda i: (0, 0)),
            pl.BlockSpec((1, w), lambda i: (0, 0)),
        ],
        out_specs=pl.BlockSpec((tm, w), lambda i: (i, 0)),
        compiler_params=_cparams(("arbitrary",)),
        name="qkprep",
    )(proj, ctab, stab, perm, gqk)


def _attn_kernel(q_ref, k_ref, v_ref, lam_ref, g_ref, o_ref, *, tk, unroll, lambda_init):
    tq = q_ref.shape[0]
    seq = k_ref.shape[0]
    lane = lax.broadcasted_iota(jnp.int32, (1, LANES), 1)
    q = q_ref[...]
    zero = jnp.zeros_like(q)
    qs = jnp.concatenate([jnp.where(lane < QK_DIM, q, zero), jnp.where(lane < QK_DIM, zero, q)], axis=0)
    lv = lam_ref[...]
    lam = (jnp.exp(jnp.sum(lv[0:1] * lv[1:2], axis=-1, keepdims=True))
           - jnp.exp(jnp.sum(lv[2:3] * lv[3:4], axis=-1, keepdims=True)) + lambda_init)

    def step(c, carry):
        m, l, a = carry
        off = pl.multiple_of(c * tk, tk)
        kc = k_ref[pl.ds(off, tk), :]
        vc = v_ref[pl.ds(off, tk), :]
        s = lax.dot_general(qs, kc, (((1,), (1,)), ((), ())), preferred_element_type=F32)
        mn = jnp.maximum(m, jnp.max(s, axis=-1, keepdims=True))
        p = jnp.exp2((s - mn).astype(BF16))
        al = jnp.exp2(m - mn)
        pf = p.astype(F32)
        part = pf[:, 0:LANES]
        for j in range(1, tk // LANES):
            part = part + pf[:, j * LANES:(j + 1) * LANES]
        l = al * l + part
        a = al * a + jnp.dot(p, vc, preferred_element_type=F32)
        return mn, l, a

    init = (jnp.full((2 * tq, 1), -jnp.inf, F32), jnp.zeros((2 * tq, LANES), F32), jnp.zeros((2 * tq, V_DIM), F32))
    _, l, a = lax.fori_loop(0, seq // tk, step, init, unroll=unroll)
    a = a / jnp.sum(l, axis=-1, keepdims=True)
    o = a[:tq] - lam * a[tq:]
    ms = jnp.mean(o * o, axis=-1, keepdims=True)
    o = o * lax.rsqrt(ms + EPS) * g_ref[...] * (1.0 - lambda_init)
    o_ref[...] = o.astype(o_ref.dtype)


def _attention(qk, proj, lamv, subln_g, batch, seq, lambda_init, tq=1024, tk=512, unroll=8):
    nq = seq // tq
    vcol0 = (2 * ATTN_HEADS * 2 * QK_DIM) // V_DIM
    return pl.pallas_call(
        functools.partial(_attn_kernel, tk=tk, unroll=unroll, lambda_init=lambda_init),
        out_shape=jax.ShapeDtypeStruct((batch * seq, ATTN_HEADS * V_DIM), BF16),
        grid=(batch, ATTN_HEADS, nq),
        in_specs=[
            pl.BlockSpec((tq, LANES), lambda b, h, i: (b * nq + i, h)),
            pl.BlockSpec((seq, LANES), lambda b, h, i: (b, ATTN_HEADS + h)),
            pl.BlockSpec((seq, V_DIM), lambda b, h, i: (b, vcol0 + h)),
            pl.BlockSpec((4, QK_DIM), lambda b, h, i: (0, 0)),
            pl.BlockSpec((1, V_DIM), lambda b, h, i: (0, 0)),
        ],
        out_specs=pl.BlockSpec((tq, V_DIM), lambda b, h, i: (b * nq + i, h)),
        compiler_params=_cparams(("arbitrary", "arbitrary", "arbitrary")),
        name="diffattn",
    )(qk, qk, proj, lamv, subln_g)


def _merge_kernel(at_ref, hy_ref, ga_ref, gh_ref, x_ref, wpa_ref, wph_ref, wo_ref, g2_ref, wr_ref,
                  x1_ref, u2_ref, aff_ref):
    ga = jax.nn.sigmoid(ga_ref[...].astype(F32))
    gh = jax.nn.sigmoid(gh_ref[...].astype(F32))
    merged = (ga * jnp.dot(at_ref[...], wpa_ref[...], preferred_element_type=F32)
              + gh * jnp.dot(hy_ref[...], wph_ref[...], preferred_element_type=F32))
    x1 = x_ref[...] + jnp.dot(merged.astype(BF16), wo_ref[...], preferred_element_type=F32)
    x1_ref[...] = x1
    ms = jnp.mean(x1 * x1, axis=-1, keepdims=True)
    u2 = (x1 * lax.rsqrt(ms + EPS) * g2_ref[...]).astype(BF16)
    u2_ref[...] = u2
    logits = jnp.dot(u2, wr_ref[...], preferred_element_type=F32)
    lane = lax.broadcasted_iota(jnp.int32, logits.shape, 1)
    logits = jnp.where(lane < N_EXPERTS, logits, -jnp.inf)
    e = jnp.exp(logits - jnp.max(logits, axis=-1, keepdims=True))
    aff_ref[...] = e / jnp.sum(e, axis=-1, keepdims=True)


def _merge(attn, hyena, proj, x2, wpa, wph, wo, g2, wr_pad, tm=1024):
    n, d = x2.shape
    gcol0 = (proj.shape[1] - 2 * d) // d
    full = lambda shape: pl.BlockSpec(shape, lambda i: (0, 0))
    return pl.pallas_call(
        _merge_kernel,
        out_shape=(jax.ShapeDtypeStruct((n, d), F32), jax.ShapeDtypeStruct((n, d), BF16),
                   jax.ShapeDtypeStruct((n, LANES), F32)),
        grid=(n // tm,),
        in_specs=[
            pl.BlockSpec((tm, attn.shape[1]), lambda i: (i, 0)),
            pl.BlockSpec((tm, hyena.shape[1]), lambda i: (i, 0)),
            pl.BlockSpec((tm, d), lambda i: (i, gcol0)),
            pl.BlockSpec((tm, d), lambda i: (i, gcol0 + 1)),
            pl.BlockSpec((tm, d), lambda i: (i, 0)),
            full(wpa.shape), full(wph.shape), full(wo.shape), full(g2.shape), full(wr_pad.shape),
        ],
        out_specs=(pl.BlockSpec((tm, d), lambda i: (i, 0)), pl.BlockSpec((tm, d), lambda i: (i, 0)),
                   pl.BlockSpec((tm, LANES), lambda i: (i, 0))),
        compiler_params=_cparams(("arbitrary",)),
        name="merge",
    )(attn, hyena, proj, proj, x2, wpa, wph, wo, g2, wr_pad)


def _excl_cumsum_rows(mask_f32, tri, blk):
    rows = mask_f32.shape[0]
    carry = jnp.zeros((1, LANES), F32)
    outs = []
    for r in range(rows // blk):
        mb = mask_f32[r * blk:(r + 1) * blk]
        outs.append(jnp.dot(tri, mb.astype(BF16), preferred_element_type=F32) + carry)
        carry = carry + jnp.sum(mb, axis=0, keepdims=True)
    return jnp.concatenate(outs, axis=0)


def _topk_kernel(aff_ref, pos_ref, post_ref, afft_ref, offs_ref, *, cap, blk, batch):
    seq = aff_ref.shape[0] // batch
    aff = aff_ref[0:seq]
    for b in range(1, batch):
        aff = aff + pltpu.roll(aff_ref[b * seq:(b + 1) * seq], b * N_EXPERTS, 1)

    def search(i, prefix):
        cand = prefix | jnp.left_shift(jnp.int32(1), 29 - i)
        cnt = jnp.sum((aff >= pltpu.bitcast(cand, F32)[0:1]).astype(F32), axis=0, keepdims=True)
        return jnp.where(cnt >= cap, cand, prefix)

    lo = pltpu.bitcast(lax.fori_loop(0, 30, search, jnp.zeros((8, LANES), jnp.int32)), F32)[0:1]
    thr = jnp.min(jnp.where(aff >= lo, aff, jnp.inf), axis=0, keepdims=True)
    gt = (aff > thr).astype(F32)
    eq = (aff == thr).astype(F32)
    need = cap - jnp.sum(gt, axis=0, keepdims=True)
    ri = lax.broadcasted_iota(jnp.int32, (blk, blk), 0)
    ci = lax.broadcasted_iota(jnp.int32, (blk, blk), 1)
    tri = (ci < ri).astype(BF16)
    sel = gt + eq * (_excl_cumsum_rows(eq, tri, blk) < need).astype(F32)
    before = _excl_cumsum_rows(sel, tri, blk)
    pos = jnp.where(sel > 0.0, before, -1.0)
    pos_t = pos.T
    aff_t = aff.T
    lane = lax.broadcasted_iota(jnp.int32, (1, LANES), 1)
    for b in range(batch):
        mine = pos if b == 0 else pltpu.roll(pos, LANES - b * N_EXPERTS, 1)
        pos_ref[b * seq:(b + 1) * seq] = jnp.where(lane < N_EXPERTS, mine, -1.0)
        post_ref[b] = pos_t[b * N_EXPERTS:(b + 1) * N_EXPERTS]
        afft_ref[b] = aff_t[b * N_EXPERTS:(b + 1) * N_EXPERTS]
    offs_ref[...] = jnp.concatenate([before[j * MOE_TILE:j * MOE_TILE + 1] for j in range(offs_ref.shape[0])], axis=0)


def _topk(aff, batch, seq, cap, blk=256):
    assert batch * N_EXPERTS <= LANES
    nt = seq // MOE_TILE
    pos, post, afft, offs = pl.pallas_call(
        functools.partial(_topk_kernel, cap=cap, blk=blk, batch=batch),
        out_shape=(jax.ShapeDtypeStruct((batch * seq, LANES), F32),
                   jax.ShapeDtypeStruct((batch, N_EXPERTS, seq), F32),
                   jax.ShapeDtypeStruct((batch, N_EXPERTS, seq), F32),
                   jax.ShapeDtypeStruct((nt, LANES), F32)),
        compiler_params=pltpu.CompilerParams(vmem_limit_bytes=VMEM_LIMIT),
        name="topk",
    )(aff)
    offs = offs[:, :batch * N_EXPERTS].reshape(nt, batch, N_EXPERTS).transpose(1, 0, 2)
    return pos, post, afft, offs.astype(jnp.int32).reshape(-1)


def _align_down(x, m):
    sh = m.bit_length() - 1
    return lax.shift_left(lax.shift_right_logical(x, sh), sh)


def _ceil_div(x, m):
    return lax.shift_right_logical(x + (m - 1), m.bit_length() - 1)


def _windows(lo0, end, body, win):
    body(0, 0)
    lax.fori_loop(1, _ceil_div(end - lo0, win), body, 0)


def _gather_kernel(offs_ref, post_ref, u_ref, x_ref, *, cap, win):
    b = pl.program_id(0)
    c = pl.program_id(1)
    nt = pl.num_programs(1)

    @pl.when(c == 0)
    def _():
        x_ref[...] = jnp.zeros_like(x_ref)

    uc = u_ref[...]
    slot0 = lax.broadcasted_iota(jnp.int32, (win, MOE_TILE), 0).astype(F32)
    base = (b * nt + c) * N_EXPERTS
    last = pl.num_programs(0) * nt * N_EXPERTS - 1
    starts, hots = [], []
    for e in range(N_EXPERTS):
        s = pl.multiple_of(jnp.minimum(_align_down(offs_ref[base + e], 16), cap - win), 16)
        hots.append(jnp.where((slot0 + s.astype(F32)) == post_ref[0, e:e + 1, :], 1.0, 0.0).astype(BF16))
        starts.append(s)
    rows = jnp.dot(jnp.concatenate(hots, axis=0), uc, preferred_element_type=F32)
    for e in range(N_EXPERTS):
        x_ref[0, e, pl.ds(starts[e], win), :] += rows[e * win:(e + 1) * win].astype(BF16)
    for e in range(N_EXPERTS):
        lo0 = _align_down(offs_ref[base + e], 16)
        end = jnp.where(c + 1 < nt, offs_ref[jnp.minimum(base + N_EXPERTS + e, last)], cap)

        def more(k, carry, lo0=lo0, e=e):
            lo = lo0 + k * win
            s = pl.multiple_of(jnp.minimum(lo, cap - win), 16)
            srow = slot0 + s.astype(F32)
            hit = jnp.logical_and(srow == post_ref[0, e:e + 1, :], srow >= lo.astype(F32))
            x_ref[0, e, pl.ds(s, win), :] += jnp.dot(jnp.where(hit, 1.0, 0.0).astype(BF16), uc,
                                                     preferred_element_type=F32).astype(BF16)
            return carry

        lax.fori_loop(1, _ceil_div(end - lo0, win), more, 0)


def _gather(offs, post, u2, batch, seq, cap, win=128):
    d = u2.shape[1]
    nt = seq // MOE_TILE
    grid_spec = pltpu.PrefetchScalarGridSpec(
        num_scalar_prefetch=1,
        grid=(batch, nt),
        in_specs=[
            pl.BlockSpec((1, N_EXPERTS, MOE_TILE), lambda b, c, o: (b, 0, c)),
            pl.BlockSpec((MOE_TILE, d), lambda b, c, o: (b * nt + c, 0)),
        ],
        out_specs=pl.BlockSpec((1, N_EXPERTS, cap, d), lambda b, c, o: (b, 0, 0, 0)),
    )
    return pl.pallas_call(
        functools.partial(_gather_kernel, cap=cap, win=win),
        out_shape=jax.ShapeDtypeStruct((batch, N_EXPERTS, cap, d), BF16),
        grid_spec=grid_spec,
        compiler_params=_cparams(("arbitrary", "arbitrary")),
        name="gather",
    )(offs, post, u2)


def _expert_kernel(offs_ref, post_ref, afft_ref, x_ref, wg_ref, wu_ref, wd_ref, o_ref, gs_ref, w_ref, *, cap, win):
    e = pl.program_id(0)
    b = pl.program_id(1)

    @pl.when(e < N_EXPERTS)
    def _():
        for m, src in enumerate((wg_ref, wu_ref, wd_ref)):
            part = src.shape[2]
            w_ref[jnp.bitwise_and(e, 1), m, pl.ds(pl.multiple_of(b * part, part), part), :] = src[0, 0].astype(BF16)

    @pl.when(e == 0)
    def _():
        o_ref[...] = jnp.zeros_like(o_ref)

    @pl.when(e > 0)
    def _():
        _expert_body(offs_ref, post_ref, afft_ref, x_ref, w_ref.at[1 - jnp.bitwise_and(e, 1)], o_ref, gs_ref,
                     e - 1, b, cap, win)


def _expert_body(offs_ref, post_ref, afft_ref, x_ref, w_ref, o_ref, gs_ref, e, b, cap, win):
    seq = post_ref.shape[2]
    nt = seq // MOE_TILE
    gs_ref[...] = jnp.zeros_like(gs_ref)
    slot0 = lax.broadcasted_iota(jnp.int32, (win, MOE_TILE), 0).astype(F32)
    for c in range(nt):
        base = (b * nt + c) * N_EXPERTS + e
        lo0 = _align_down(offs_ref[base], 8)
        end = offs_ref[base + N_EXPERTS] if c + 1 < nt else cap
        prow = post_ref[0, pl.ds(e, 1), c * MOE_TILE:(c + 1) * MOE_TILE]
        arow = afft_ref[0, pl.ds(e, 1), c * MOE_TILE:(c + 1) * MOE_TILE]

        def window(k, carry, lo0=lo0, prow=prow, arow=arow):
            lo = lo0 + k * win
            s = pl.multiple_of(jnp.minimum(lo, cap - win), 8)
            srow = slot0 + s.astype(F32)
            hit = jnp.logical_and(srow == prow, srow >= lo.astype(F32))
            gs_ref[pl.ds(s, win), :] += jnp.sum(jnp.where(hit, arow, 0.0), axis=-1, keepdims=True)
            return carry

        _windows(lo0, end, window, win)
    xb = x_ref[0, 0]
    hg = jnp.dot(xb, w_ref[0], preferred_element_type=F32)
    hu = jnp.dot(xb, w_ref[1], preferred_element_type=F32)
    act = (hg * jax.nn.sigmoid(hg) * hu).astype(BF16)
    eo = jnp.dot(act, w_ref[2], preferred_element_type=F32)
    o_ref[0, 0] = (eo * gs_ref[...]).astype(o_ref.dtype)


def _experts(offs, post, afft, xin, wg, wu, wd, layer, batch, seq, cap, win=128):
    d = xin.shape[3]
    assert wg.shape[2:] == (d, d) and wd.shape[2:] == (d, d) and d % (16 * batch) == 0
    part = d // batch
    wspec = pl.BlockSpec((1, 1, part, d), lambda e, b, o: (layer, jnp.minimum(e, N_EXPERTS - 1), b, 0))
    grid_spec = pltpu.PrefetchScalarGridSpec(
        num_scalar_prefetch=1,
        grid=(N_EXPERTS + 1, batch),
        in_specs=[
            pl.BlockSpec((1, N_EXPERTS, seq), lambda e, b, o: (b, 0, 0)),
            pl.BlockSpec((1, N_EXPERTS, seq), lambda e, b, o: (b, 0, 0)),
            pl.BlockSpec((1, 1, cap, d), lambda e, b, o: (b, jnp.maximum(e - 1, 0), 0, 0)),
            wspec, wspec, wspec,
        ],
        out_specs=pl.BlockSpec((1, 1, cap, d), lambda e, b, o: (b, jnp.where(e == 0, N_EXPERTS, e - 1), 0, 0)),
        scratch_shapes=[pltpu.VMEM((cap, 1), F32), pltpu.VMEM((2, 3, d, d), BF16)],
    )
    return pl.pallas_call(
        functools.partial(_expert_kernel, cap=cap, win=win),
        out_shape=jax.ShapeDtypeStruct((batch, N_EXPERTS + 1, cap, d), BF16),
        grid_spec=grid_spec,
        compiler_params=_cparams(("arbitrary", "arbitrary")),
        name="experts",
    )(offs, post, afft, xin, wg, wu, wd)


def _combine_kernel(offs_ref, pos_ref, x1_ref, weo_ref, o_ref, *, cap, win):
    b = pl.program_id(0)
    t = pl.program_id(1)
    nt = pl.num_programs(1)
    tt = pos_ref.shape[0]
    pos = pos_ref[...]
    slot0 = lax.broadcasted_iota(jnp.int32, (tt, win), 1).astype(F32)
    base = (b * nt + t) * N_EXPERTS
    last = pl.num_programs(0) * nt * N_EXPERTS - 1
    acc = x1_ref[...]
    for e0 in range(0, N_EXPERTS, 2):
        hots, rows = [], []
        for e in (e0, e0 + 1):
            s = pl.multiple_of(jnp.minimum(_align_down(offs_ref[base + e], 16), cap - win), 16)
            hots.append(jnp.where((slot0 + s.astype(F32)) == pos[:, e:e + 1], 1.0, 0.0).astype(BF16))
            rows.append(weo_ref[0, e, pl.ds(s, win), :])
        acc = acc + jnp.dot(jnp.concatenate(hots, axis=1), jnp.concatenate(rows, axis=0),
                            preferred_element_type=F32)
    o_ref[...] = acc
    for e in range(N_EXPERTS):
        lo0 = _align_down(offs_ref[base + e], 16)
        end = jnp.where(t + 1 < nt, offs_ref[jnp.minimum(base + N_EXPERTS + e, last)], cap)

        def window(k, carry, lo0=lo0, e=e):
            lo = lo0 + k * win
            s = pl.multiple_of(jnp.minimum(lo, cap - win), 16)
            srow = slot0 + s.astype(F32)
            hit = jnp.logical_and(srow == pos[:, e:e + 1], srow >= lo.astype(F32))
            o_ref[...] += jnp.dot(jnp.where(hit, 1.0, 0.0).astype(BF16), weo_ref[0, e, pl.ds(s, win), :],
                                  preferred_element_type=F32)
            return carry

        lax.fori_loop(1, _ceil_div(end - lo0, win), window, 0)


def _combine(offs, pos, x1, weo, batch, seq, cap, win=128):
    d = x1.shape[1]
    tt = MOE_TILE
    nt = seq // tt
    grid_spec = pltpu.PrefetchScalarGridSpec(
        num_scalar_prefetch=1,
        grid=(batch, nt),
        in_specs=[
            pl.BlockSpec((tt, LANES), lambda b, t, o: (b * nt + t, 0)),
            pl.BlockSpec((tt, d), lambda b, t, o: (b * nt + t, 0)),
            pl.BlockSpec((1, N_EXPERTS, cap, d), lambda b, t, o: (b, 0, 0, 0)),
        ],
        out_specs=pl.BlockSpec((tt, d), lambda b, t, o: (b * nt + t, 0)),
    )
    return pl.pallas_call(
        functools.partial(_combine_kernel, cap=cap, win=win),
        out_shape=jax.ShapeDtypeStruct(x1.shape, F32),
        grid_spec=grid_spec,
        compiler_params=_cparams(("arbitrary", "arbitrary")),
        name="combine",
    )(offs, pos, x1, weo)


FFT_R = 64
FFT_PITCH = FFT_R + 8


def _dft_tables(seq):
    r = FFT_R
    assert seq == r * r
    n = np.arange(r)
    f = np.exp(-2j * np.pi * np.outer(n, n) / r)
    w = lambda e: np.exp(-2j * np.pi * e / (2 * seq))

    def real_rep(a):
        return np.block([[a.real, -a.imag], [a.imag, a.real]]).astype(np.float32)

    m1 = [f, f * w(r * n)[None, :]]
    m2 = [f * w(n * m)[None, :] for m in range(2 * r)]
    m4 = [f.conj() / (2 * seq), w(-r * n)[:, None] * f.conj() / (2 * seq)]
    t1 = np.stack([real_rep(a) for a in m1])
    t2 = np.stack([real_rep(a) for a in m2])
    t3 = np.stack([real_rep(a.conj().T) for a in m2])
    t4 = np.stack([real_rep(a) for a in m4])
    t1r = np.stack([np.concatenate([a.real, a.imag], axis=0).astype(np.float32) for a in m1])
    return tuple(jnp.asarray(t).astype(BF16) for t in (t1, t2, t3, t4, t1r))


def _slab(r):
    return pl.ds(pl.multiple_of(r * FFT_PITCH, 8), FFT_R)


def _across(r):
    return pl.ds(r, FFT_R, stride=FFT_PITCH)


def _cat_bf16(a, b):
    return jnp.concatenate([a, b], axis=0).astype(BF16)


FFT_UNROLL = 32


def _loop(body, step=1, rows_per_trip=FFT_UNROLL):
    def wrapped(i, c):
        body(i * step)
        return c
    lax.fori_loop(0, FFT_R // step, wrapped, 0, unroll=rows_per_trip // step)


def _pair_bf16(re_ref, im_ref, idx0, idx1):
    return jnp.concatenate([_cat_bf16(re_ref[idx0, :], im_ref[idx0, :]),
                            _cat_bf16(re_ref[idx1, :], im_ref[idx1, :])], axis=1)


def _bf16_pieces(w):
    hi = w.astype(BF16)
    return hi, (w - hi.astype(F32)).astype(BF16)


def _dot3(a, w_ref):
    a_hi, a_lo = _bf16_pieces(a)
    return (jnp.dot(a_hi, w_ref[0], preferred_element_type=F32)
            + (jnp.dot(a_hi, w_ref[1], preferred_element_type=F32)
               + jnp.dot(a_lo, w_ref[0], preferred_element_type=F32)))


def _filter_kernel(bands_ref, w1t_ref, w1c_ref, w1s_ref, b1_ref, w2_ref, b2_ref, w3_ref, b3_ref, fr_ref,
                   wf_ref, wb_ref, dl_ref, hs_ref, hd_ref, *, seq):
    tm = hs_ref.shape[0]
    rows = tm // FILT_GROUPS
    hid = wf_ref.shape[1]
    base = pl.program_id(0) * tm
    lane = lax.broadcasted_iota(jnp.int32, (rows, LANES), 1)
    grp = lax.shift_right_logical(lane, FILTER_BANDS.bit_length() - 1)
    pos = (base + grp * rows + lax.broadcasted_iota(jnp.int32, (rows, LANES), 0)).astype(F32)
    first = jnp.bitwise_and(lane, FILTER_BANDS - 1) == 0
    fr = fr_ref[...]
    tscale = 1.0 / max(seq - 1, 1)

    def mlp(p):
        ang = ((2.0 * math.pi / seq) * p) * bands_ref[...]
        pre = (_dot3(jnp.cos(ang), w1c_ref) + _dot3(-jnp.sin(ang), w1s_ref)
               + _dot3(jnp.where(first, p * tscale, 0.0), w1t_ref) + b1_ref[...])
        h = jnp.sin(fr * pre)
        h = jnp.sin(fr * (_dot3(h, w2_ref) + b2_ref[...]))
        return jnp.sin(fr * (_dot3(h, w3_ref) + b3_ref[...]))

    hf_h = mlp(pos)
    hb_h = mlp(seq - pos)
    dl = dl_ref[...]
    for g in range(FILT_GROUPS):
        pcol = (base + g * rows + lax.broadcasted_iota(jnp.int32, (rows, 1), 0)).astype(F32)
        hf = _dot3(hf_h[:, g * hid:(g + 1) * hid], wf_ref) * jnp.exp(-(pcol * tscale) * dl)
        hb = _dot3(hb_h[:, g * hid:(g + 1) * hid], wb_ref) * jnp.exp(-((seq - pcol) * tscale) * dl)
        hb = jnp.where(pcol > 0.0, hb, 0.0)
        hs_ref[g * rows:(g + 1) * rows, :] = hf + hb
        hd_ref[g * rows:(g + 1) * rows, :] = hf - hb


def _filters(bands, w1t, w1c, w1s, b1, w2, b2, w3, b3, fr, wf, wb, dl, seq, tm=512):
    cols = wf.shape[1]
    pieces = lambda w: jnp.stack(_bf16_pieces(w))
    args = (bands, pieces(w1t), pieces(w1c), pieces(w1s), b1, pieces(w2), b2, pieces(w3), b3, fr,
            pieces(wf), pieces(wb), dl)
    return pl.pallas_call(
        functools.partial(_filter_kernel, seq=seq),
        out_shape=(jax.ShapeDtypeStruct((seq, cols), F32), jax.ShapeDtypeStruct((seq, cols), F32)),
        grid=(seq // tm,),
        in_specs=[pl.BlockSpec(a.shape, lambda i, nd=a.ndim: (0,) * nd) for a in args],
        out_specs=(pl.BlockSpec((tm, cols), lambda i: (i, 0)), pl.BlockSpec((tm, cols), lambda i: (i, 0))),
        compiler_params=_cparams(("arbitrary",)),
        name="hyfilter",
    )(*args)


def _spectra_kernel(hs_ref, hd_ref, t1r_ref, t2_ref, h_ref, xe_ref, xo_ref, per_ref, pei_ref, por_ref, poi_ref):
    xx = (xe_ref, xo_ref)
    pp = ((per_ref, pei_ref), (por_ref, poi_ref))

    def fill(r):
        rows = pl.ds(pl.multiple_of(r * FFT_R, FFT_R), FFT_R)
        xe_ref[_slab(r), :] = hs_ref[rows, :]
        xo_ref[_slab(r), :] = hd_ref[rows, :]
    _loop(fill)

    def first(r):
        for par in (0, 1):
            o = jnp.dot(t1r_ref[par], xx[par][_across(r), :].astype(BF16), preferred_element_type=F32)
            pp[par][0][_slab(r), :] = o[:FFT_R]
            pp[par][1][_slab(r), :] = o[FFT_R:]
    _loop(first)

    def second(r):
        rows = pl.ds(pl.multiple_of(r * FFT_R, FFT_R), FFT_R)
        for par in (0, 1):
            o = jnp.dot(t2_ref[2 * r + par], _cat_bf16(pp[par][0][_across(r), :], pp[par][1][_across(r), :]),
                        preferred_element_type=F32)
            h_ref[0, 2 * par, rows, :] = o[:FFT_R].astype(h_ref.dtype)
            h_ref[0, 2 * par + 1, rows, :] = o[FFT_R:].astype(h_ref.dtype)
    _loop(second)


def _spectra(hs, hd, t1r, t2, seq):
    ncol = hs.shape[1] // LANES
    nct = ncol // HYENA_ORDER
    prow = FFT_R * FFT_PITCH
    const = lambda a: pl.BlockSpec(a.shape, lambda o, c: (0,) * a.ndim, pipeline_mode=pl.Buffered(1))
    return pl.pallas_call(
        _spectra_kernel,
        out_shape=jax.ShapeDtypeStruct((HYENA_ORDER, 4, seq, nct * LANES), BF16),
        grid=(HYENA_ORDER, nct),
        in_specs=[pl.BlockSpec((seq, LANES), lambda o, c: (0, o * nct + c)),
                  pl.BlockSpec((seq, LANES), lambda o, c: (0, o * nct + c)),
                  const(t1r), const(t2)],
        out_specs=pl.BlockSpec((1, 4, seq, LANES), lambda o, c: (o, 0, 0, c)),
        scratch_shapes=[pltpu.VMEM((prow, LANES), F32)] * 6,
        compiler_params=_cparams(("arbitrary", "arbitrary")),
        name="hyspectra",
    )(hs, hd, t1r, t2)


def _short_conv(x, w_ref, b_ref):
    rows = x.shape[0]
    sub = 8
    ri = lax.broadcasted_iota(jnp.int32, (sub, x.shape[1]), 0)
    prev = pltpu.roll(x, 1, 0)
    prev = jnp.concatenate([jnp.where(ri == 0, 0.0, prev[:sub]), prev[sub:]], axis=0)
    nxt = pltpu.roll(x, rows - 1, 0)
    nxt = jnp.concatenate([nxt[:rows - sub], jnp.where(ri == sub - 1, 0.0, nxt[rows - sub:])], axis=0)
    return prev * w_ref[0:1, :] + x * w_ref[1:2, :] + nxt * w_ref[2:3, :] + b_ref[...]


def _hyconv_kernel(z_ref, g_ref, cwz_ref, cbz_ref, cwg_ref, cbg_ref, skip_ref, h_ref,
                   t1_ref, t2_ref, t3_ref, t4_ref, o_ref,
                   xr_ref, xi_ref, per_ref, pei_ref, por_ref, poi_ref, qer_ref, qei_ref, qor_ref, qoi_ref,
                   *, conv_z):
    seq = z_ref.shape[0] // 2
    for half, dst in ((0, xr_ref), (1, xi_ref)):
        z = z_ref[half * seq:(half + 1) * seq, :].astype(F32)
        if conv_z:
            z = _short_conv(z, cwz_ref, cbz_ref)
        for j in range(FFT_R):
            dst[j * FFT_PITCH:j * FFT_PITCH + FFT_R, :] = z[j * FFT_R:(j + 1) * FFT_R]

    pp = ((per_ref, pei_ref), (por_ref, poi_ref))
    qq = ((qer_ref, qei_ref), (qor_ref, qoi_ref))
    yr_ref, yi_ref = qer_ref, qei_ref
    t1 = jnp.concatenate([t1_ref[0], t1_ref[1]], axis=0)
    t4 = jnp.concatenate([t4_ref[0], t4_ref[1]], axis=1)
    halves = (slice(0, LANES), slice(LANES, 2 * LANES))

    def first(r):
        o = jnp.dot(t1, _pair_bf16(xr_ref, xi_ref, _across(r), _across(r + 1)), preferred_element_type=F32)
        for par in (0, 1):
            for k, cols in zip((r, r + 1), halves):
                pp[par][0][_slab(k), :] = o[2 * par * FFT_R:(2 * par + 1) * FFT_R, cols]
                pp[par][1][_slab(k), :] = o[(2 * par + 1) * FFT_R:(2 * par + 2) * FFT_R, cols]
    _loop(first, step=2)

    def second(r):
        rows = pl.ds(pl.multiple_of(r * FFT_R, FFT_R), FFT_R)
        for par in (0, 1):
            o = jnp.dot(t2_ref[2 * r + par], _cat_bf16(pp[par][0][_across(r), :], pp[par][1][_across(r), :]),
                        preferred_element_type=F32)
            hr = h_ref[0, 2 * par, rows, :].astype(F32)
            hi = h_ref[0, 2 * par + 1, rows, :].astype(F32)
            ar, ai = o[:FFT_R], o[FFT_R:]
            qq[par][0][_slab(r), :] = ar * hr - ai * hi
            qq[par][1][_slab(r), :] = ar * hi + ai * hr
    _loop(second)

    def third(r):
        for par in (0, 1):
            o = jnp.dot(t3_ref[2 * r + par], _cat_bf16(qq[par][0][_slab(r), :], qq[par][1][_slab(r), :]),
                        preferred_element_type=F32)
            pp[par][0][_across(r), :] = o[:FFT_R]
            pp[par][1][_across(r), :] = o[FFT_R:]
    _loop(third)

    def fourth(r):
        quad = lambda k: jnp.concatenate([pp[0][0][_slab(k), :], pp[0][1][_slab(k), :],
                                          pp[1][0][_slab(k), :], pp[1][1][_slab(k), :]], axis=0).astype(BF16)
        o = jnp.dot(t4, jnp.concatenate([quad(r), quad(r + 1)], axis=1), preferred_element_type=F32)
        for k, cols in zip((r, r + 1), halves):
            yr_ref[_across(k), :] = o[:FFT_R, cols]
            yi_ref[_across(k), :] = o[FFT_R:, cols]
    _loop(fourth, step=2)

    skip = skip_ref[...]
    for half, (y_ref, x_ref) in enumerate(((yr_ref, xr_ref), (yi_ref, xi_ref))):
        g = _short_conv(g_ref[half * seq:(half + 1) * seq, :].astype(F32), cwg_ref, cbg_ref)
        for j in range(FFT_R):
            src = slice(j * FFT_PITCH, j * FFT_PITCH + FFT_R)
            dst = slice(half * seq + j * FFT_R, half * seq + (j + 1) * FFT_R)
            o_ref[dst, :] = (g[j * FFT_R:(j + 1) * FFT_R] * (y_ref[src, :] + skip * x_ref[src, :])).astype(o_ref.dtype)


def _hyconv(zsrc, zcol0, gcol0, proj, conv_w, conv_b, zpart, gpart, skip, spec, order, tabs, batch, seq, conv_z):
    t1, t2, t3, t4 = tabs
    nct = HYENA_WIDTH // LANES
    prow = FFT_R * FFT_PITCH
    const = lambda a: pl.BlockSpec(a.shape, lambda c, p: (0,) * a.ndim, pipeline_mode=pl.Buffered(1))
    return pl.pallas_call(
        functools.partial(_hyconv_kernel, conv_z=conv_z),
        out_shape=jax.ShapeDtypeStruct((batch * seq, HYENA_WIDTH), BF16),
        grid=(nct, batch // 2),
        in_specs=[
            pl.BlockSpec((2 * seq, LANES), lambda c, p: (p, zcol0 + c)),
            pl.BlockSpec((2 * seq, LANES), lambda c, p: (p, gcol0 + c)),
            pl.BlockSpec((3, LANES), lambda c, p: (0, zpart * nct + c)),
            pl.BlockSpec((1, LANES), lambda c, p: (0, zpart * nct + c)),
            pl.BlockSpec((3, LANES), lambda c, p: (0, gpart * nct + c)),
            pl.BlockSpec((1, LANES), lambda c, p: (0, gpart * nct + c)),
            pl.BlockSpec((1, LANES), lambda c, p: (0, c)),
            pl.BlockSpec((1, 4, seq, LANES), lambda c, p: (order, 0, 0, c), pipeline_mode=pl.Buffered(1)),
            const(t1), const(t2), const(t3), const(t4),
        ],
        out_specs=pl.BlockSpec((2 * seq, LANES), lambda c, p: (p, c)),
        scratch_shapes=[pltpu.VMEM((prow, LANES), F32)] * 10,
        compiler_params=_cparams(("arbitrary", "arbitrary")),
        name=f"hyconv{order}",
    )(zsrc, proj, conv_w, conv_b, conv_w, conv_b, skip[order:order + 1], spec, t1, t2, t3, t4)


def _hyena(proj, hycol0, conv_w, conv_b, w1, b1, w2, b2, w3, b3, w_out, freq, skip, batch, seq):
    hid = w2.shape[0]
    eye = jnp.eye(FILT_GROUPS, dtype=F32)
    bdiag = lambda a: jnp.kron(eye, a)
    tile = lambda a: jnp.tile(a[None], (1, FILT_GROUPS))
    bands = tile(jnp.linspace(1e-4, FILTER_BANDS - 1, FILTER_BANDS, dtype=F32))
    min_decay = math.log(DECAY_TARGET) / SLOW_DECAY_PCT
    max_decay = math.log(DECAY_TARGET) / FAST_DECAY_PCT
    deltas = jnp.abs(jnp.linspace(min_decay, max_decay, HYENA_WIDTH, dtype=F32))
    wo4 = w_out.reshape(hid, HYENA_ORDER, 2, HYENA_WIDTH)
    w1t = jnp.pad(w1[0:1], ((0, FILTER_BANDS - 1), (0, 0)))
    hs, hd = _filters(
        bands, bdiag(w1t), bdiag(w1[1:1 + FILTER_BANDS]), bdiag(w1[1 + FILTER_BANDS:]),
        tile(b1), bdiag(w2), tile(b2), bdiag(w3), tile(b3), tile(freq),
        wo4[:, :, 0].reshape(hid, -1), wo4[:, :, 1].reshape(hid, -1),
        jnp.tile(deltas, HYENA_ORDER)[None], seq)
    t1, t2, t3, t4, t1r = _dft_tables(seq)
    spec = _spectra(hs, hd, t1r, t2, seq)
    c0 = hycol0 // LANES
    nct = HYENA_WIDTH // LANES
    cb = conv_b[None]
    z1 = _hyconv(proj, c0, c0 + nct, proj, conv_w, cb, 0, 1, skip, spec, 0, (t1, t2, t3, t4), batch, seq, True)
    return _hyconv(z1, 0, c0 + 2 * nct, proj, conv_w, cb, 0, 2, skip, spec, 1, (t1, t2, t3, t4), batch, seq, False)


def _rope_tables(seq):
    pos = np.arange(seq, dtype=np.float32)
    inv_freq = (ROPE_THETA ** (-np.arange(0, ROPE_DIM, 2, dtype=np.float32) / ROPE_DIM)).astype(np.float32)
    ang = pos[:, None] * inv_freq[None, :]
    cos, sin = np.cos(ang).astype(np.float32), np.sin(ang).astype(np.float32)
    half = ROPE_DIM // 2
    c = np.ones((seq, QK_DIM), np.float32)
    s = np.zeros((seq, QK_DIM), np.float32)
    c[:, :half] = cos
    c[:, half:ROPE_DIM] = cos
    s[:, :half] = -sin
    s[:, half:ROPE_DIM] = sin
    perm = np.zeros((LANES, LANES), np.float32)
    for j in range(LANES):
        if j % QK_DIM < half:
            perm[j + half, j] = 1.0
        elif j % QK_DIM < ROPE_DIM:
            perm[j - half, j] = 1.0
    rep = LANES // QK_DIM
    return (jnp.asarray(np.tile(c, (1, rep))), jnp.asarray(np.tile(s, (1, rep))), jnp.asarray(perm).astype(BF16))


def kernel(x, norm1_g, w_in, short_conv_w, short_conv_b, q_norm_g, k_norm_g, lambda_q1, lambda_k1, lambda_q2, lambda_k2, subln_g, filt_w1, filt_b1, filt_w2, filt_b2, filt_w3, filt_b3, filt_w_out, filt_freq, hyena_skip, w_branch_attn, w_branch_hyena, w_out, norm2_g, w_router, w_gate, w_up, w_down):
    b, l, d = x.shape
    depth = w_in.shape[0]
    n = b * l
    cap = CAPACITY_FACTOR * l // N_EXPERTS
    q_cols = ATTN_HEADS * 2 * QK_DIM
    ctab, stab, perm = _rope_tables(l)
    xc = x.reshape(n, d)
    for li in range(depth):
        lambda_init = 0.8 - 0.6 * math.exp(-0.3 * li)
        proj = _inproj(xc, norm1_g[li][None], w_in[li].astype(BF16))
        scale = math.log2(math.e) / math.sqrt(QK_DIM)
        gqk = jnp.concatenate([jnp.tile(q_norm_g[li], q_cols // QK_DIM) * scale,
                               jnp.tile(k_norm_g[li], q_cols // QK_DIM)])[None]
        qk = _qkprep(proj, ctab, stab, perm, gqk, l)
        lamv = jnp.stack([lambda_q1[li], lambda_k1[li], lambda_q2[li], lambda_k2[li]])
        attn = _attention(qk, proj, lamv, subln_g[li][None], b, l, lambda_init)

        hyena = _hyena(proj, 3 * q_cols, short_conv_w[li], short_conv_b[li], filt_w1[li], filt_b1[li],
                       filt_w2[li], filt_b2[li], filt_w3[li], filt_b3[li], filt_w_out[li], filt_freq[li],
                       hyena_skip[li], b, l)

        wr_pad = jnp.pad(w_router[li], ((0, 0), (0, LANES - N_EXPERTS))).astype(BF16)
        x1, u2, aff = _merge(attn, hyena, proj, xc, w_branch_attn[li].astype(BF16),
                             w_branch_hyena[li].astype(BF16), w_out[li].astype(BF16),
                             norm2_g[li][None], wr_pad)
        pos, post, afft, offs = _topk(aff, b, l, cap)
        xin = _gather(offs, post, u2, b, l, cap)
        weo = _experts(offs, post, afft, xin, w_gate, w_up, w_down, li, b, l, cap)
        xc = _combine(offs, pos, x1, weo, b, l, cap)
    return xc.reshape(b, l, d)
```

```python
import functools
import math

import jax
import jax.numpy as jnp
import numpy as np
from jax import lax
from jax.experimental import pallas as pl
from jax.experimental.pallas import tpu as pltpu

F32 = jnp.float32
BF16 = jnp.bfloat16

ATTN_HEADS = 4
QK_DIM = 64
V_DIM = 128
ROPE_DIM = 16
ROPE_THETA = 500000.0
HYENA_WIDTH = 512
HYENA_ORDER = 2
FILTER_BANDS = 16
DECAY_TARGET = 1e-2
FAST_DECAY_PCT = 0.3
SLOW_DECAY_PCT = 1.5
N_EXPERTS = 16
CAPACITY_FACTOR = 2
MOE_TILE = 512
EPS = 1e-6
LANES = 128
FILT_GROUPS = LANES // FILTER_BANDS
VMEM_LIMIT = 56 * 1024 * 1024

INPROJ_ROWS, INPROJ_COLS = 2048, 1280
QKPREP_ROWS = 1024
ATTN_Q_ROWS, ATTN_KEY_CHUNK = 1024, 512
MERGE_ROWS = 1024
TOPK_CUMSUM_ROWS = 256
GATHER_WINDOW = 128
COMBINE_WINDOW = 128
FILTER_ROWS = 512


def _cparams(sem):
    return pltpu.CompilerParams(dimension_semantics=sem, vmem_limit_bytes=VMEM_LIMIT)


def _inproj_kernel(x_ref, g_ref, w_ref, o_ref, u_ref):
    @pl.when(pl.program_id(1) == 0)
    def _():
        x = x_ref[...]
        ms = jnp.mean(x * x, axis=-1, keepdims=True)
        u_ref[...] = (x * lax.rsqrt(ms + EPS) * g_ref[...]).astype(BF16)

    o_ref[...] = jnp.dot(u_ref[...], w_ref[...], preferred_element_type=F32).astype(o_ref.dtype)


def _inproj(x2, g, w_bf16, tm=INPROJ_ROWS, tn=INPROJ_COLS):
    n, d = x2.shape
    cols = w_bf16.shape[1]
    return pl.pallas_call(
        _inproj_kernel,
        out_shape=jax.ShapeDtypeStruct((n, cols), BF16),
        grid=(n // tm, cols // tn),
        in_specs=[
            pl.BlockSpec((tm, d), lambda i, j: (i, 0)),
            pl.BlockSpec((1, d), lambda i, j: (0, 0)),
            pl.BlockSpec((d, tn), lambda i, j: (0, j)),
        ],
        out_specs=pl.BlockSpec((tm, tn), lambda i, j: (i, j)),
        scratch_shapes=[pltpu.VMEM((tm, d), BF16)],
        compiler_params=_cparams(("arbitrary", "arbitrary")),
        name="inproj",
    )(x2, g, w_bf16)


def _qkprep_kernel(p_ref, c_ref, s_ref, perm_ref, g_ref, o_ref):
    lane = lax.broadcasted_iota(jnp.int32, (1, LANES), 1)
    lo = lane < QK_DIM
    c = c_ref[...]
    s = s_ref[...]
    perm = perm_ref[...]
    for j in range(p_ref.shape[1] // LANES):
        t = p_ref[:, j * LANES:(j + 1) * LANES].astype(F32)
        sq = t * t
        ss_lo = jnp.sum(jnp.where(lo, sq, 0.0), axis=-1, keepdims=True)
        ss_hi = jnp.sum(jnp.where(lo, 0.0, sq), axis=-1, keepdims=True)
        r = lax.rsqrt(jnp.where(lo, ss_lo, ss_hi) * (1.0 / QK_DIM) + EPS)
        y = t * r * g_ref[:, j * LANES:(j + 1) * LANES]
        y_hi = y.astype(BF16)
        y_lo = (y - y_hi.astype(F32)).astype(BF16)
        partner = (jnp.dot(y_hi, perm, preferred_element_type=F32)
                   + jnp.dot(y_lo, perm, preferred_element_type=F32))
        o_ref[:, j * LANES:(j + 1) * LANES] = (y * c + partner * s).astype(o_ref.dtype)


def _qkprep(proj, ctab, stab, perm, gqk, seq, tm=QKPREP_ROWS):
    n = proj.shape[0]
    w = gqk.shape[1]
    nb = seq // tm
    return pl.pallas_call(
        _qkprep_kernel,
        out_shape=jax.ShapeDtypeStruct((n, w), BF16),
        grid=(n // tm,),
        in_specs=[
            pl.BlockSpec((tm, w), lambda i: (i, 0)),
            pl.BlockSpec((tm, LANES), lambda i: (i % nb, 0)),
            pl.BlockSpec((tm, LANES), lambda i: (i % nb, 0)),
            pl.BlockSpec((LANES, LANES), lambda i: (0, 0)),
            pl.BlockSpec((1, w), lambda i: (0, 0)),
        ],
        out_specs=pl.BlockSpec((tm, w), lambda i: (i, 0)),
        compiler_params=_cparams(("arbitrary",)),
        name="qkprep",
    )(proj, ctab, stab, perm, gqk)


def _attn_kernel(q_ref, k_ref, v_ref, lam_ref, g_ref, o_ref, *, tk, unroll, lambda_init):
    tq = q_ref.shape[0]
    seq = k_ref.shape[0]
    lane = lax.broadcasted_iota(jnp.int32, (1, LANES), 1)
    q = q_ref[...]
    zero = jnp.zeros_like(q)
    qs = jnp.concatenate([jnp.where(lane < QK_DIM, q, zero), jnp.where(lane < QK_DIM, zero, q)], axis=0)
    lv = lam_ref[...]
    lam = (jnp.exp(jnp.sum(lv[0:1] * lv[1:2], axis=-1, keepdims=True))
           - jnp.exp(jnp.sum(lv[2:3] * lv[3:4], axis=-1, keepdims=True)) + lambda_init)

    def step(c, carry):
        m, l, a = carry
        off = pl.multiple_of(c * tk, tk)
        kc = k_ref[pl.ds(off, tk), :]
        vc = v_ref[pl.ds(off, tk), :]
        s = lax.dot_general(qs, kc, (((1,), (1,)), ((), ())), preferred_element_type=F32)
        mn = jnp.maximum(m, jnp.max(s, axis=-1, keepdims=True))
        p = jnp.exp2((s - mn).astype(BF16))
        al = jnp.exp2(m - mn)
        pf = p.astype(F32)
        part = pf[:, 0:LANES]
        for j in range(1, tk // LANES):
            part = part + pf[:, j * LANES:(j + 1) * LANES]
        l = al * l + part
        a = al * a + jnp.dot(p, vc, preferred_element_type=F32)
        return mn, l, a

    init = (jnp.full((2 * tq, 1), -jnp.inf, F32), jnp.zeros((2 * tq, LANES), F32), jnp.zeros((2 * tq, V_DIM), F32))
    _, l, a = lax.fori_loop(0, seq // tk, step, init, unroll=unroll)
    a = a / jnp.sum(l, axis=-1, keepdims=True)
    o = a[:tq] - lam * a[tq:]
    ms = jnp.mean(o * o, axis=-1, keepdims=True)
    o = o * lax.rsqrt(ms + EPS) * g_ref[...] * (1.0 - lambda_init)
    o_ref[...] = o.astype(o_ref.dtype)


def _attention(qk, proj, lamv, subln_g, batch, seq, lambda_init, tq=ATTN_Q_ROWS, tk=ATTN_KEY_CHUNK):
    nq = seq // tq
    vcol0 = (2 * ATTN_HEADS * 2 * QK_DIM) // V_DIM
    return pl.pallas_call(
        functools.partial(_attn_kernel, tk=tk, unroll=seq // tk, lambda_init=lambda_init),
        out_shape=jax.ShapeDtypeStruct((batch * seq, ATTN_HEADS * V_DIM), BF16),
        grid=(batch, ATTN_HEADS, nq),
        in_specs=[
            pl.BlockSpec((tq, LANES), lambda b, h, i: (b * nq + i, h)),
            pl.BlockSpec((seq, LANES), lambda b, h, i: (b, ATTN_HEADS + h)),
            pl.BlockSpec((seq, V_DIM), lambda b, h, i: (b, vcol0 + h)),
            pl.BlockSpec((4, QK_DIM), lambda b, h, i: (0, 0)),
            pl.BlockSpec((1, V_DIM), lambda b, h, i: (0, 0)),
        ],
        out_specs=pl.BlockSpec((tq, V_DIM), lambda b, h, i: (b * nq + i, h)),
        compiler_params=_cparams(("arbitrary", "arbitrary", "arbitrary")),
        name="diffattn",
    )(qk, qk, proj, lamv, subln_g)


def _merge_kernel(at_ref, hy_ref, ga_ref, gh_ref, x_ref, wpa_ref, wph_ref, wo_ref, g2_ref, wr_ref,
                  x1_ref, u2_ref, aff_ref):
    ga = jax.nn.sigmoid(ga_ref[...].astype(F32))
    gh = jax.nn.sigmoid(gh_ref[...].astype(F32))
    merged = (ga * jnp.dot(at_ref[...], wpa_ref[...], preferred_element_type=F32)
              + gh * jnp.dot(hy_ref[...], wph_ref[...], preferred_element_type=F32))
    x1 = x_ref[...] + jnp.dot(merged.astype(BF16), wo_ref[...], preferred_element_type=F32)
    x1_ref[...] = x1
    ms = jnp.mean(x1 * x1, axis=-1, keepdims=True)
    u2 = (x1 * lax.rsqrt(ms + EPS) * g2_ref[...]).astype(BF16)
    u2_ref[...] = u2
    logits = jnp.dot(u2, wr_ref[...], preferred_element_type=F32)
    lane = lax.broadcasted_iota(jnp.int32, logits.shape, 1)
    logits = jnp.where(lane < N_EXPERTS, logits, -jnp.inf)
    e = jnp.exp(logits - jnp.max(logits, axis=-1, keepdims=True))
    aff_ref[...] = e / jnp.sum(e, axis=-1, keepdims=True)


def _merge(attn, hyena, proj, x2, wpa, wph, wo, g2, wr_pad, tm=MERGE_ROWS):
    n, d = x2.shape
    gcol0 = (proj.shape[1] - 2 * d) // d
    full = lambda shape: pl.BlockSpec(shape, lambda i: (0, 0))
    return pl.pallas_call(
        _merge_kernel,
        out_shape=(jax.ShapeDtypeStruct((n, d), F32), jax.ShapeDtypeStruct((n, d), BF16),
                   jax.ShapeDtypeStruct((n, LANES), F32)),
        grid=(n // tm,),
        in_specs=[
            pl.BlockSpec((tm, attn.shape[1]), lambda i: (i, 0)),
            pl.BlockSpec((tm, hyena.shape[1]), lambda i: (i, 0)),
            pl.BlockSpec((tm, d), lambda i: (i, gcol0)),
            pl.BlockSpec((tm, d), lambda i: (i, gcol0 + 1)),
            pl.BlockSpec((tm, d), lambda i: (i, 0)),
            full(wpa.shape), full(wph.shape), full(wo.shape), full(g2.shape), full(wr_pad.shape),
        ],
        out_specs=(pl.BlockSpec((tm, d), lambda i: (i, 0)), pl.BlockSpec((tm, d), lambda i: (i, 0)),
                   pl.BlockSpec((tm, LANES), lambda i: (i, 0))),
        compiler_params=_cparams(("arbitrary",)),
        name="merge",
    )(attn, hyena, proj, proj, x2, wpa, wph, wo, g2, wr_pad)


def _excl_cumsum_rows(mask_f32, tri, blk):
    rows = mask_f32.shape[0]
    carry = jnp.zeros((1, LANES), F32)
    outs = []
    for r in range(rows // blk):
        mb = mask_f32[r * blk:(r + 1) * blk]
        outs.append(jnp.dot(tri, mb.astype(BF16), preferred_element_type=F32) + carry)
        carry = carry + jnp.sum(mb, axis=0, keepdims=True)
    return jnp.concatenate(outs, axis=0)


def _topk_kernel(aff_ref, pos_ref, post_ref, afft_ref, offs_ref, *, cap, blk, batch):
    seq = aff_ref.shape[0] // batch
    aff = aff_ref[0:seq]
    for b in range(1, batch):
        aff = aff + pltpu.roll(aff_ref[b * seq:(b + 1) * seq], b * N_EXPERTS, 1)

    def search(i, prefix):
        cand = prefix | jnp.left_shift(jnp.int32(1), 29 - i)
        cnt = jnp.sum((aff >= pltpu.bitcast(cand, F32)[0:1]).astype(F32), axis=0, keepdims=True)
        return jnp.where(cnt >= cap, cand, prefix)

    lo = pltpu.bitcast(lax.fori_loop(0, 30, search, jnp.zeros((8, LANES), jnp.int32)), F32)[0:1]
    thr = jnp.min(jnp.where(aff >= lo, aff, jnp.inf), axis=0, keepdims=True)
    gt = (aff > thr).astype(F32)
    eq = (aff == thr).astype(F32)
    need = cap - jnp.sum(gt, axis=0, keepdims=True)
    ri = lax.broadcasted_iota(jnp.int32, (blk, blk), 0)
    ci = lax.broadcasted_iota(jnp.int32, (blk, blk), 1)
    tri = (ci < ri).astype(BF16)
    sel = gt + eq * (_excl_cumsum_rows(eq, tri, blk) < need).astype(F32)
    before = _excl_cumsum_rows(sel, tri, blk)
    pos = jnp.where(sel > 0.0, before, -1.0)
    pos_t = pos.T
    aff_t = aff.T
    lane = lax.broadcasted_iota(jnp.int32, (1, LANES), 1)
    for b in range(batch):
        mine = pos if b == 0 else pltpu.roll(pos, LANES - b * N_EXPERTS, 1)
        pos_ref[b * seq:(b + 1) * seq] = jnp.where(lane < N_EXPERTS, mine, -1.0)
        post_ref[b] = pos_t[b * N_EXPERTS:(b + 1) * N_EXPERTS]
        afft_ref[b] = aff_t[b * N_EXPERTS:(b + 1) * N_EXPERTS]
    offs_ref[...] = jnp.concatenate([before[j * MOE_TILE:j * MOE_TILE + 1] for j in range(offs_ref.shape[0])], axis=0)


def _topk(aff, batch, seq, cap, blk=TOPK_CUMSUM_ROWS):
    assert batch * N_EXPERTS <= LANES
    nt = seq // MOE_TILE
    pos, post, afft, offs = pl.pallas_call(
        functools.partial(_topk_kernel, cap=cap, blk=blk, batch=batch),
        out_shape=(jax.ShapeDtypeStruct((batch * seq, LANES), F32),
                   jax.ShapeDtypeStruct((batch, N_EXPERTS, seq), F32),
                   jax.ShapeDtypeStruct((batch, N_EXPERTS, seq), F32),
                   jax.ShapeDtypeStruct((nt, LANES), F32)),
        compiler_params=pltpu.CompilerParams(vmem_limit_bytes=VMEM_LIMIT),
        name="topk",
    )(aff)
    offs = offs[:, :batch * N_EXPERTS].reshape(nt, batch, N_EXPERTS).transpose(1, 0, 2)
    return pos, post, afft, offs.astype(jnp.int32).reshape(-1)


def _align_down(x, m):
    sh = m.bit_length() - 1
    return lax.shift_left(lax.shift_right_logical(x, sh), sh)


def _ceil_div(x, m):
    return lax.shift_right_logical(x + (m - 1), m.bit_length() - 1)


def _windows(lo0, end, body, win):
    body(0, 0)
    lax.fori_loop(1, _ceil_div(end - lo0, win), body, 0)


def _gather_kernel(offs_ref, post_ref, u_ref, x_ref, *, cap, win):
    b = pl.program_id(0)
    c = pl.program_id(1)
    nt = pl.num_programs(1)

    @pl.when(c == 0)
    def _():
        x_ref[...] = jnp.zeros_like(x_ref)

    uc = u_ref[...]
    slot0 = lax.broadcasted_iota(jnp.int32, (win, MOE_TILE), 0).astype(F32)
    base = (b * nt + c) * N_EXPERTS
    last = pl.num_programs(0) * nt * N_EXPERTS - 1
    starts, hots = [], []
    for e in range(N_EXPERTS):
        s = pl.multiple_of(jnp.minimum(_align_down(offs_ref[base + e], 16), cap - win), 16)
        hots.append(jnp.where((slot0 + s.astype(F32)) == post_ref[0, e:e + 1, :], 1.0, 0.0).astype(BF16))
        starts.append(s)
    rows = jnp.dot(jnp.concatenate(hots, axis=0), uc, preferred_element_type=F32)
    for e in range(N_EXPERTS):
        x_ref[0, e, pl.ds(starts[e], win), :] += rows[e * win:(e + 1) * win].astype(BF16)
    for e in range(N_EXPERTS):
        lo0 = _align_down(offs_ref[base + e], 16)
        end = jnp.where(c + 1 < nt, offs_ref[jnp.minimum(base + N_EXPERTS + e, last)], cap)

        def more(k, carry, lo0=lo0, e=e):
            lo = lo0 + k * win
            s = pl.multiple_of(jnp.minimum(lo, cap - win), 16)
            srow = slot0 + s.astype(F32)
            hit = jnp.logical_and(srow == post_ref[0, e:e + 1, :], srow >= lo.astype(F32))
            x_ref[0, e, pl.ds(s, win), :] += jnp.dot(jnp.where(hit, 1.0, 0.0).astype(BF16), uc,
                                                     preferred_element_type=F32).astype(BF16)
            return carry

        lax.fori_loop(1, _ceil_div(end - lo0, win), more, 0)


def _gather(offs, post, u2, batch, seq, cap, win=GATHER_WINDOW):
    d = u2.shape[1]
    nt = seq // MOE_TILE
    grid_spec = pltpu.PrefetchScalarGridSpec(
        num_scalar_prefetch=1,
        grid=(batch, nt),
        in_specs=[
            pl.BlockSpec((1, N_EXPERTS, MOE_TILE), lambda b, c, o: (b, 0, c)),
            pl.BlockSpec((MOE_TILE, d), lambda b, c, o: (b * nt + c, 0)),
        ],
        out_specs=pl.BlockSpec((1, N_EXPERTS, cap, d), lambda b, c, o: (b, 0, 0, 0)),
    )
    return pl.pallas_call(
        functools.partial(_gather_kernel, cap=cap, win=win),
        out_shape=jax.ShapeDtypeStruct((batch, N_EXPERTS, cap, d), BF16),
        grid_spec=grid_spec,
        compiler_params=_cparams(("arbitrary", "arbitrary")),
        name="gather",
    )(offs, post, u2)


def _expert_kernel(offs_ref, post_ref, afft_ref, x_ref, wg_ref, wu_ref, wd_ref, o_ref, gs_ref, w_ref, *, cap, win):
    e = pl.program_id(0)
    b = pl.program_id(1)

    @pl.when(e < N_EXPERTS)
    def _():
        for m, src in enumerate((wg_ref, wu_ref, wd_ref)):
            part = src.shape[2]
            w_ref[jnp.bitwise_and(e, 1), m, pl.ds(pl.multiple_of(b * part, part), part), :] = src[0, 0].astype(BF16)

    @pl.when(e == 0)
    def _():
        o_ref[...] = jnp.zeros_like(o_ref)

    @pl.when(e > 0)
    def _():
        _expert_body(offs_ref, post_ref, afft_ref, x_ref, w_ref.at[1 - jnp.bitwise_and(e, 1)], o_ref, gs_ref,
                     e - 1, b, cap, win)


def _expert_body(offs_ref, post_ref, afft_ref, x_ref, w_ref, o_ref, gs_ref, e, b, cap, win):
    seq = post_ref.shape[2]
    nt = seq // MOE_TILE
    gs_ref[...] = jnp.zeros_like(gs_ref)
    slot0 = lax.broadcasted_iota(jnp.int32, (win, MOE_TILE), 0).astype(F32)
    for c in range(nt):
        base = (b * nt + c) * N_EXPERTS + e
        lo0 = _align_down(offs_ref[base], 8)
        end = offs_ref[base + N_EXPERTS] if c + 1 < nt else cap
        prow = post_ref[0, pl.ds(e, 1), c * MOE_TILE:(c + 1) * MOE_TILE]
        arow = afft_ref[0, pl.ds(e, 1), c * MOE_TILE:(c + 1) * MOE_TILE]

        def window(k, carry, lo0=lo0, prow=prow, arow=arow):
            lo = lo0 + k * win
            s = pl.multiple_of(jnp.minimum(lo, cap - win), 8)
            srow = slot0 + s.astype(F32)
            hit = jnp.logical_and(srow == prow, srow >= lo.astype(F32))
            gs_ref[pl.ds(s, win), :] += jnp.sum(jnp.where(hit, arow, 0.0), axis=-1, keepdims=True)
            return carry

        _windows(lo0, end, window, win)
    xb = x_ref[0, 0]
    hg = jnp.dot(xb, w_ref[0], preferred_element_type=F32)
    hu = jnp.dot(xb, w_ref[1], preferred_element_type=F32)
    act = (hg * jax.nn.sigmoid(hg) * hu).astype(BF16)
    eo = jnp.dot(act, w_ref[2], preferred_element_type=F32)
    o_ref[0, 0] = (eo * gs_ref[...]).astype(o_ref.dtype)


def _experts(offs, post, afft, xin, wg, wu, wd, layer, batch, seq, cap, win=GATHER_WINDOW):
    d = xin.shape[3]
    assert wg.shape[2:] == (d, d) and wd.shape[2:] == (d, d) and d % (16 * batch) == 0
    part = d // batch
    wspec = pl.BlockSpec((1, 1, part, d), lambda e, b, o: (layer, jnp.minimum(e, N_EXPERTS - 1), b, 0))
    grid_spec = pltpu.PrefetchScalarGridSpec(
        num_scalar_prefetch=1,
        grid=(N_EXPERTS + 1, batch),
        in_specs=[
            pl.BlockSpec((1, N_EXPERTS, seq), lambda e, b, o: (b, 0, 0)),
            pl.BlockSpec((1, N_EXPERTS, seq), lambda e, b, o: (b, 0, 0)),
            pl.BlockSpec((1, 1, cap, d), lambda e, b, o: (b, jnp.maximum(e - 1, 0), 0, 0)),
            wspec, wspec, wspec,
        ],
        out_specs=pl.BlockSpec((1, 1, cap, d), lambda e, b, o: (b, jnp.where(e == 0, N_EXPERTS, e - 1), 0, 0)),
        scratch_shapes=[pltpu.VMEM((cap, 1), F32), pltpu.VMEM((2, 3, d, d), BF16)],
    )
    return pl.pallas_call(
        functools.partial(_expert_kernel, cap=cap, win=win),
        out_shape=jax.ShapeDtypeStruct((batch, N_EXPERTS + 1, cap, d), BF16),
        grid_spec=grid_spec,
        compiler_params=_cparams(("arbitrary", "arbitrary")),
        name="experts",
    )(offs, post, afft, xin, wg, wu, wd)


def _combine_kernel(offs_ref, pos_ref, x1_ref, weo_ref, o_ref, *, cap, win):
    b = pl.program_id(0)
    t = pl.program_id(1)
    nt = pl.num_programs(1)
    tt = pos_ref.shape[0]
    pos = pos_ref[...]
    slot0 = lax.broadcasted_iota(jnp.int32, (tt, win), 1).astype(F32)
    base = (b * nt + t) * N_EXPERTS
    last = pl.num_programs(0) * nt * N_EXPERTS - 1
    acc = x1_ref[...]
    for e0 in range(0, N_EXPERTS, 2):
        hots, rows = [], []
        for e in (e0, e0 + 1):
            s = pl.multiple_of(jnp.minimum(_align_down(offs_ref[base + e], 16), cap - win), 16)
            hots.append(jnp.where((slot0 + s.astype(F32)) == pos[:, e:e + 1], 1.0, 0.0).astype(BF16))
            rows.append(weo_ref[0, e, pl.ds(s, win), :])
        acc = acc + jnp.dot(jnp.concatenate(hots, axis=1), jnp.concatenate(rows, axis=0),
                            preferred_element_type=F32)
    o_ref[...] = acc
    for e in range(N_EXPERTS):
        lo0 = _align_down(offs_ref[base + e], 16)
        end = jnp.where(t + 1 < nt, offs_ref[jnp.minimum(base + N_EXPERTS + e, last)], cap)

        def window(k, carry, lo0=lo0, e=e):
            lo = lo0 + k * win
            s = pl.multiple_of(jnp.minimum(lo, cap - win), 16)
            srow = slot0 + s.astype(F32)
            hit = jnp.logical_and(srow == pos[:, e:e + 1], srow >= lo.astype(F32))
            o_ref[...] += jnp.dot(jnp.where(hit, 1.0, 0.0).astype(BF16), weo_ref[0, e, pl.ds(s, win), :],
                                  preferred_element_type=F32)
            return carry

        lax.fori_loop(1, _ceil_div(end - lo0, win), window, 0)


def _combine(offs, pos, x1, weo, batch, seq, cap, win=COMBINE_WINDOW):
    d = x1.shape[1]
    tt = MOE_TILE
    nt = seq // tt
    grid_spec = pltpu.PrefetchScalarGridSpec(
        num_scalar_prefetch=1,
        grid=(batch, nt),
        in_specs=[
            pl.BlockSpec((tt, LANES), lambda b, t, o: (b * nt + t, 0)),
            pl.BlockSpec((tt, d), lambda b, t, o: (b * nt + t, 0)),
            pl.BlockSpec((1, N_EXPERTS, cap, d), lambda b, t, o: (b, 0, 0, 0)),
        ],
        out_specs=pl.BlockSpec((tt, d), lambda b, t, o: (b * nt + t, 0)),
    )
    return pl.pallas_call(
        functools.partial(_combine_kernel, cap=cap, win=win),
        out_shape=jax.ShapeDtypeStruct(x1.shape, F32),
        grid_spec=grid_spec,
        compiler_params=_cparams(("arbitrary", "arbitrary")),
        name="combine",
    )(offs, pos, x1, weo)


FFT_R = 64
FFT_PITCH = FFT_R + 8


def _dft_tables(seq):
    r = FFT_R
    assert seq == r * r
    n = np.arange(r)
    f = np.exp(-2j * np.pi * np.outer(n, n) / r)
    w = lambda e: np.exp(-2j * np.pi * e / (2 * seq))

    def real_rep(a):
        return np.block([[a.real, -a.imag], [a.imag, a.real]]).astype(np.float32)

    m1 = [f, f * w(r * n)[None, :]]
    m2 = [f * w(n * m)[None, :] for m in range(2 * r)]
    m4 = [f.conj() / (2 * seq), w(-r * n)[:, None] * f.conj() / (2 * seq)]
    t1 = np.stack([real_rep(a) for a in m1])
    t2 = np.stack([real_rep(a) for a in m2])
    t3 = np.stack([real_rep(a.conj().T) for a in m2])
    t4 = np.stack([real_rep(a) for a in m4])
    t1r = np.stack([np.concatenate([a.real, a.imag], axis=0).astype(np.float32) for a in m1])
    return tuple(jnp.asarray(t).astype(BF16) for t in (t1, t2, t3, t4, t1r))


def _slab(r):
    return pl.ds(pl.multiple_of(r * FFT_PITCH, 8), FFT_R)


def _across(r):
    return pl.ds(r, FFT_R, stride=FFT_PITCH)


def _cat_bf16(a, b):
    return jnp.concatenate([a, b], axis=0).astype(BF16)


FFT_UNROLL = 32


def _loop(body, step=1, rows_per_trip=FFT_UNROLL):
    def wrapped(i, c):
        body(i * step)
        return c
    lax.fori_loop(0, FFT_R // step, wrapped, 0, unroll=rows_per_trip // step)


def _pair_bf16(re_ref, im_ref, idx0, idx1):
    return jnp.concatenate([_cat_bf16(re_ref[idx0, :], im_ref[idx0, :]),
                            _cat_bf16(re_ref[idx1, :], im_ref[idx1, :])], axis=1)


def _bf16_pieces(w):
    hi = w.astype(BF16)
    return hi, (w - hi.astype(F32)).astype(BF16)


def _dot3(a, w_ref):
    a_hi, a_lo = _bf16_pieces(a)
    return (jnp.dot(a_hi, w_ref[0], preferred_element_type=F32)
            + (jnp.dot(a_hi, w_ref[1], preferred_element_type=F32)
               + jnp.dot(a_lo, w_ref[0], preferred_element_type=F32)))


def _filter_kernel(bands_ref, w1t_ref, w1c_ref, w1s_ref, b1_ref, w2_ref, b2_ref, w3_ref, b3_ref, fr_ref,
                   wf_ref, wb_ref, dl_ref, hs_ref, hd_ref, *, seq):
    tm = hs_ref.shape[0]
    rows = tm // FILT_GROUPS
    hid = wf_ref.shape[1]
    base = pl.program_id(0) * tm
    lane = lax.broadcasted_iota(jnp.int32, (rows, LANES), 1)
    grp = lax.shift_right_logical(lane, FILTER_BANDS.bit_length() - 1)
    pos = (base + grp * rows + lax.broadcasted_iota(jnp.int32, (rows, LANES), 0)).astype(F32)
    first = jnp.bitwise_and(lane, FILTER_BANDS - 1) == 0
    fr = fr_ref[...]
    tscale = 1.0 / max(seq - 1, 1)

    def mlp(p):
        ang = ((2.0 * math.pi / seq) * p) * bands_ref[...]
        pre = (_dot3(jnp.cos(ang), w1c_ref) + _dot3(-jnp.sin(ang), w1s_ref)
               + _dot3(jnp.where(first, p * tscale, 0.0), w1t_ref) + b1_ref[...])
        h = jnp.sin(fr * pre)
        h = jnp.sin(fr * (_dot3(h, w2_ref) + b2_ref[...]))
        return jnp.sin(fr * (_dot3(h, w3_ref) + b3_ref[...]))

    hf_h = mlp(pos)
    hb_h = mlp(seq - pos)
    dl = dl_ref[...]
    for g in range(FILT_GROUPS):
        pcol = (base + g * rows + lax.broadcasted_iota(jnp.int32, (rows, 1), 0)).astype(F32)
        hf = _dot3(hf_h[:, g * hid:(g + 1) * hid], wf_ref) * jnp.exp(-(pcol * tscale) * dl)
        hb = _dot3(hb_h[:, g * hid:(g + 1) * hid], wb_ref) * jnp.exp(-((seq - pcol) * tscale) * dl)
        hb = jnp.where(pcol > 0.0, hb, 0.0)
        hs_ref[g * rows:(g + 1) * rows, :] = (hf + hb).astype(hs_ref.dtype)
        hd_ref[g * rows:(g + 1) * rows, :] = (hf - hb).astype(hd_ref.dtype)


def _filters(bands, w1t, w1c, w1s, b1, w2, b2, w3, b3, fr, wf, wb, dl, seq, tm=FILTER_ROWS):
    cols = wf.shape[1]
    pieces = lambda w: jnp.stack(_bf16_pieces(w))
    args = (bands, pieces(w1t), pieces(w1c), pieces(w1s), b1, pieces(w2), b2, pieces(w3), b3, fr,
            pieces(wf), pieces(wb), dl)
    return pl.pallas_call(
        functools.partial(_filter_kernel, seq=seq),
        out_shape=(jax.ShapeDtypeStruct((seq, cols), BF16), jax.ShapeDtypeStruct((seq, cols), BF16)),
        grid=(seq // tm,),
        in_specs=[pl.BlockSpec(a.shape, lambda i, nd=a.ndim: (0,) * nd) for a in args],
        out_specs=(pl.BlockSpec((tm, cols), lambda i: (i, 0)), pl.BlockSpec((tm, cols), lambda i: (i, 0))),
        compiler_params=_cparams(("arbitrary",)),
        name="hyfilter",
    )(*args)


def _spectra_kernel(hs_ref, hd_ref, t1r_ref, t2_ref, h_ref, xe_ref, xo_ref, per_ref, pei_ref, por_ref, poi_ref):
    xx = (xe_ref, xo_ref)
    pp = ((per_ref, pei_ref), (por_ref, poi_ref))

    def fill(r):
        rows = pl.ds(pl.multiple_of(r * FFT_R, FFT_R), FFT_R)
        xe_ref[_slab(r), :] = hs_ref[rows, :].astype(F32)
        xo_ref[_slab(r), :] = hd_ref[rows, :].astype(F32)
    _loop(fill)

    def first(r):
        for par in (0, 1):
            o = jnp.dot(t1r_ref[par], xx[par][_across(r), :].astype(BF16), preferred_element_type=F32)
            pp[par][0][_slab(r), :] = o[:FFT_R]
            pp[par][1][_slab(r), :] = o[FFT_R:]
    _loop(first)

    def second(r):
        rows = pl.ds(pl.multiple_of(r * FFT_R, FFT_R), FFT_R)
        for par in (0, 1):
            o = jnp.dot(t2_ref[2 * r + par], _cat_bf16(pp[par][0][_across(r), :], pp[par][1][_across(r), :]),
                        preferred_element_type=F32)
            h_ref[0, 2 * par, rows, :] = o[:FFT_R].astype(h_ref.dtype)
            h_ref[0, 2 * par + 1, rows, :] = o[FFT_R:].astype(h_ref.dtype)
    _loop(second)


def _spectra(hs, hd, t1r, t2, seq):
    ncol = hs.shape[1] // LANES
    nct = ncol // HYENA_ORDER
    prow = FFT_R * FFT_PITCH
    const = lambda a: pl.BlockSpec(a.shape, lambda o, c: (0,) * a.ndim, pipeline_mode=pl.Buffered(1))
    return pl.pallas_call(
        _spectra_kernel,
        out_shape=jax.ShapeDtypeStruct((HYENA_ORDER, 4, seq, nct * LANES), BF16),
        grid=(HYENA_ORDER, nct),
        in_specs=[pl.BlockSpec((seq, LANES), lambda o, c: (0, o * nct + c)),
                  pl.BlockSpec((seq, LANES), lambda o, c: (0, o * nct + c)),
                  const(t1r), const(t2)],
        out_specs=pl.BlockSpec((1, 4, seq, LANES), lambda o, c: (o, 0, 0, c)),
        scratch_shapes=[pltpu.VMEM((prow, LANES), F32)] * 6,
        compiler_params=_cparams(("arbitrary", "arbitrary")),
        name="hyspectra",
    )(hs, hd, t1r, t2)


def _short_conv(x, w_ref, b_ref):
    rows = x.shape[0]
    sub = 8
    ri = lax.broadcasted_iota(jnp.int32, (sub, x.shape[1]), 0)
    prev = pltpu.roll(x, 1, 0)
    prev = jnp.concatenate([jnp.where(ri == 0, 0.0, prev[:sub]), prev[sub:]], axis=0)
    nxt = pltpu.roll(x, rows - 1, 0)
    nxt = jnp.concatenate([nxt[:rows - sub], jnp.where(ri == sub - 1, 0.0, nxt[rows - sub:])], axis=0)
    return prev * w_ref[0:1, :] + x * w_ref[1:2, :] + nxt * w_ref[2:3, :] + b_ref[...]


def _hyconv_kernel(z_ref, g_ref, cwz_ref, cbz_ref, cwg_ref, cbg_ref, skip_ref, h_ref,
                   t1_ref, t2_ref, t3_ref, t4_ref, o_ref,
                   xr_ref, xi_ref, per_ref, pei_ref, por_ref, poi_ref, qer_ref, qei_ref, qor_ref, qoi_ref,
                   *, conv_z):
    seq = z_ref.shape[0] // 2
    for half, dst in ((0, xr_ref), (1, xi_ref)):
        z = z_ref[half * seq:(half + 1) * seq, :].astype(F32)
        if conv_z:
            z = _short_conv(z, cwz_ref, cbz_ref)
        for j in range(FFT_R):
            dst[j * FFT_PITCH:j * FFT_PITCH + FFT_R, :] = z[j * FFT_R:(j + 1) * FFT_R]

    pp = ((per_ref, pei_ref), (por_ref, poi_ref))
    qq = ((qer_ref, qei_ref), (qor_ref, qoi_ref))
    yr_ref, yi_ref = qer_ref, qei_ref
    t1 = jnp.concatenate([t1_ref[0], t1_ref[1]], axis=0)
    t4 = jnp.concatenate([t4_ref[0], t4_ref[1]], axis=1)
    halves = (slice(0, LANES), slice(LANES, 2 * LANES))

    def first(r):
        o = jnp.dot(t1, _pair_bf16(xr_ref, xi_ref, _across(r), _across(r + 1)), preferred_element_type=F32)
        for par in (0, 1):
            for k, cols in zip((r, r + 1), halves):
                pp[par][0][_slab(k), :] = o[2 * par * FFT_R:(2 * par + 1) * FFT_R, cols]
                pp[par][1][_slab(k), :] = o[(2 * par + 1) * FFT_R:(2 * par + 2) * FFT_R, cols]
    _loop(first, step=2)

    def second(r):
        rows = pl.ds(pl.multiple_of(r * FFT_R, FFT_R), FFT_R)
        for par in (0, 1):
            o = jnp.dot(t2_ref[2 * r + par], _cat_bf16(pp[par][0][_across(r), :], pp[par][1][_across(r), :]),
                        preferred_element_type=F32)
            hr = h_ref[0, 2 * par, rows, :].astype(F32)
            hi = h_ref[0, 2 * par + 1, rows, :].astype(F32)
            ar, ai = o[:FFT_R], o[FFT_R:]
            qq[par][0][_slab(r), :] = ar * hr - ai * hi
            qq[par][1][_slab(r), :] = ar * hi + ai * hr
    _loop(second)

    def third(r):
        for par in (0, 1):
            o = jnp.dot(t3_ref[2 * r + par], _cat_bf16(qq[par][0][_slab(r), :], qq[par][1][_slab(r), :]),
                        preferred_element_type=F32)
            pp[par][0][_across(r), :] = o[:FFT_R]
            pp[par][1][_across(r), :] = o[FFT_R:]
    _loop(third)

    def fourth(r):
        quad = lambda k: jnp.concatenate([pp[0][0][_slab(k), :], pp[0][1][_slab(k), :],
                                          pp[1][0][_slab(k), :], pp[1][1][_slab(k), :]], axis=0).astype(BF16)
        o = jnp.dot(t4, jnp.concatenate([quad(r), quad(r + 1)], axis=1), preferred_element_type=F32)
        for k, cols in zip((r, r + 1), halves):
            yr_ref[_across(k), :] = o[:FFT_R, cols]
            yi_ref[_across(k), :] = o[FFT_R:, cols]
    _loop(fourth, step=2)

    skip = skip_ref[...]
    for half, (y_ref, x_ref) in enumerate(((yr_ref, xr_ref), (yi_ref, xi_ref))):
        g = _short_conv(g_ref[half * seq:(half + 1) * seq, :].astype(F32), cwg_ref, cbg_ref)
        for j in range(FFT_R):
            src = slice(j * FFT_PITCH, j * FFT_PITCH + FFT_R)
            dst = slice(half * seq + j * FFT_R, half * seq + (j + 1) * FFT_R)
            o_ref[dst, :] = (g[j * FFT_R:(j + 1) * FFT_R] * (y_ref[src, :] + skip * x_ref[src, :])).astype(o_ref.dtype)


def _hyconv(zsrc, zcol0, gcol0, proj, conv_w, conv_b, zpart, gpart, skip, spec, order, tabs, batch, seq, conv_z):
    t1, t2, t3, t4 = tabs
    nct = HYENA_WIDTH // LANES
    prow = FFT_R * FFT_PITCH
    const = lambda a: pl.BlockSpec(a.shape, lambda c, p: (0,) * a.ndim, pipeline_mode=pl.Buffered(1))
    return pl.pallas_call(
        functools.partial(_hyconv_kernel, conv_z=conv_z),
        out_shape=jax.ShapeDtypeStruct((batch * seq, HYENA_WIDTH), BF16),
        grid=(nct, batch // 2),
        in_specs=[
            pl.BlockSpec((2 * seq, LANES), lambda c, p: (p, zcol0 + c)),
            pl.BlockSpec((2 * seq, LANES), lambda c, p: (p, gcol0 + c)),
            pl.BlockSpec((3, LANES), lambda c, p: (0, zpart * nct + c)),
            pl.BlockSpec((1, LANES), lambda c, p: (0, zpart * nct + c)),
            pl.BlockSpec((3, LANES), lambda c, p: (0, gpart * nct + c)),
            pl.BlockSpec((1, LANES), lambda c, p: (0, gpart * nct + c)),
            pl.BlockSpec((1, LANES), lambda c, p: (0, c)),
            pl.BlockSpec((1, 4, seq, LANES), lambda c, p: (order, 0, 0, c), pipeline_mode=pl.Buffered(1)),
            const(t1), const(t2), const(t3), const(t4),
        ],
        out_specs=pl.BlockSpec((2 * seq, LANES), lambda c, p: (p, c)),
        scratch_shapes=[pltpu.VMEM((prow, LANES), F32)] * 10,
        compiler_params=_cparams(("arbitrary", "arbitrary")),
        name=f"hyconv{order}",
    )(zsrc, proj, conv_w, conv_b, conv_w, conv_b, skip[order:order + 1], spec, t1, t2, t3, t4)


def _hyena(proj, hycol0, conv_w, conv_b, w1, b1, w2, b2, w3, b3, w_out, freq, skip, batch, seq):
    hid = w2.shape[0]
    eye = jnp.eye(FILT_GROUPS, dtype=F32)
    bdiag = lambda a: jnp.kron(eye, a)
    tile = lambda a: jnp.tile(a[None], (1, FILT_GROUPS))
    bands = tile(jnp.linspace(1e-4, FILTER_BANDS - 1, FILTER_BANDS, dtype=F32))
    min_decay = math.log(DECAY_TARGET) / SLOW_DECAY_PCT
    max_decay = math.log(DECAY_TARGET) / FAST_DECAY_PCT
    deltas = jnp.abs(jnp.linspace(min_decay, max_decay, HYENA_WIDTH, dtype=F32))
    wo4 = w_out.reshape(hid, HYENA_ORDER, 2, HYENA_WIDTH)
    w1t = jnp.pad(w1[0:1], ((0, FILTER_BANDS - 1), (0, 0)))
    hs, hd = _filters(
        bands, bdiag(w1t), bdiag(w1[1:1 + FILTER_BANDS]), bdiag(w1[1 + FILTER_BANDS:]),
        tile(b1), bdiag(w2), tile(b2), bdiag(w3), tile(b3), tile(freq),
        wo4[:, :, 0].reshape(hid, -1), wo4[:, :, 1].reshape(hid, -1),
        jnp.tile(deltas, HYENA_ORDER)[None], seq)
    t1, t2, t3, t4, t1r = _dft_tables(seq)
    spec = _spectra(hs, hd, t1r, t2, seq)
    c0 = hycol0 // LANES
    nct = HYENA_WIDTH // LANES
    cb = conv_b[None]
    z1 = _hyconv(proj, c0, c0 + nct, proj, conv_w, cb, 0, 1, skip, spec, 0, (t1, t2, t3, t4), batch, seq, True)
    return _hyconv(z1, 0, c0 + 2 * nct, proj, conv_w, cb, 0, 2, skip, spec, 1, (t1, t2, t3, t4), batch, seq, False)


def _rope_tables(seq):
    pos = np.arange(seq, dtype=np.float32)
    inv_freq = (ROPE_THETA ** (-np.arange(0, ROPE_DIM, 2, dtype=np.float32) / ROPE_DIM)).astype(np.float32)
    ang = pos[:, None] * inv_freq[None, :]
    cos, sin = np.cos(ang).astype(np.float32), np.sin(ang).astype(np.float32)
    half = ROPE_DIM // 2
    c = np.ones((seq, QK_DIM), np.float32)
    s = np.zeros((seq, QK_DIM), np.float32)
    c[:, :half] = cos
    c[:, half:ROPE_DIM] = cos
    s[:, :half] = -sin
    s[:, half:ROPE_DIM] = sin
    perm = np.zeros((LANES, LANES), np.float32)
    for j in range(LANES):
        if j % QK_DIM < half:
            perm[j + half, j] = 1.0
        elif j % QK_DIM < ROPE_DIM:
            perm[j - half, j] = 1.0
    rep = LANES // QK_DIM
    return (jnp.asarray(np.tile(c, (1, rep))), jnp.asarray(np.tile(s, (1, rep))), jnp.asarray(perm).astype(BF16))


def kernel(x, norm1_g, w_in, short_conv_w, short_conv_b, q_norm_g, k_norm_g, lambda_q1, lambda_k1, lambda_q2, lambda_k2, subln_g, filt_w1, filt_b1, filt_w2, filt_b2, filt_w3, filt_b3, filt_w_out, filt_freq, hyena_skip, w_branch_attn, w_branch_hyena, w_out, norm2_g, w_router, w_gate, w_up, w_down):
    b, l, d = x.shape
    depth = w_in.shape[0]
    n = b * l
    cap = CAPACITY_FACTOR * l // N_EXPERTS
    q_cols = ATTN_HEADS * 2 * QK_DIM
    ctab, stab, perm = _rope_tables(l)
    xc = x.reshape(n, d)
    for li in range(depth):
        lambda_init = 0.8 - 0.6 * math.exp(-0.3 * li)
        proj = _inproj(xc, norm1_g[li][None], w_in[li].astype(BF16))
        scale = math.log2(math.e) / math.sqrt(QK_DIM)
        gqk = jnp.concatenate([jnp.tile(q_norm_g[li], q_cols // QK_DIM) * scale,
                               jnp.tile(k_norm_g[li], q_cols // QK_DIM)])[None]
        qk = _qkprep(proj, ctab, stab, perm, gqk, l)
        lamv = jnp.stack([lambda_q1[li], lambda_k1[li], lambda_q2[li], lambda_k2[li]])
        attn = _attention(qk, proj, lamv, subln_g[li][None], b, l, lambda_init)

        hyena = _hyena(proj, 3 * q_cols, short_conv_w[li], short_conv_b[li], filt_w1[li], filt_b1[li],
                       filt_w2[li], filt_b2[li], filt_w3[li], filt_b3[li], filt_w_out[li], filt_freq[li],
                       hyena_skip[li], b, l)

        wr_pad = jnp.pad(w_router[li], ((0, 0), (0, LANES - N_EXPERTS))).astype(BF16)
        x1, u2, aff = _merge(attn, hyena, proj, xc, w_branch_attn[li].astype(BF16),
                             w_branch_hyena[li].astype(BF16), w_out[li].astype(BF16),
                             norm2_g[li][None], wr_pad)
        pos, post, afft, offs = _topk(aff, b, l, cap)
        xin = _gather(offs, post, u2, b, l, cap)
        weo = _experts(offs, post, afft, xin, w_gate, w_up, w_down, li, b, l, cap)
        xc = _combine(offs, pos, x1, weo, b, l, cap)
    return xc.reshape(b, l, d)
```

```python
import functools
import math

import jax
import jax.numpy as jnp
import numpy as np
from jax import lax
from jax.experimental import pallas as pl
from jax.experimental.pallas import tpu as pltpu

F32 = jnp.float32
BF16 = jnp.bfloat16

ATTN_HEADS = 4
QK_DIM = 64
V_DIM = 128
ROPE_DIM = 16
ROPE_THETA = 500000.0
HYENA_WIDTH = 512
HYENA_ORDER = 2
FILTER_BANDS = 16
DECAY_TARGET = 1e-2
FAST_DECAY_PCT = 0.3
SLOW_DECAY_PCT = 1.5
N_EXPERTS = 16
CAPACITY_FACTOR = 2
MOE_TILE = 512
EPS = 1e-6
LANES = 128
FILT_GROUPS = LANES // FILTER_BANDS
VMEM_LIMIT = 56 * 1024 * 1024

INPROJ_ROWS, INPROJ_COLS = 2048, 1280
QKPREP_ROWS = 1024
ATTN_Q_ROWS, ATTN_KEY_CHUNK = 1024, 512
MERGE_ROWS = 1024
TOPK_CUMSUM_ROWS = 256
GATHER_WINDOW = 128
COMBINE_WINDOW = 128
FILTER_ROWS = 512


def _cparams(sem):
    return pltpu.CompilerParams(dimension_semantics=sem, vmem_limit_bytes=VMEM_LIMIT)


def _inproj_kernel(x_ref, g_ref, w_ref, o_ref, u_ref):
    j = pl.program_id(1)

    @pl.when(j == 0)
    def _():
        x = x_ref[...]
        ms = jnp.mean(x * x, axis=-1, keepdims=True)
        u = (x * lax.rsqrt(ms + EPS) * g_ref[...]).astype(BF16)
        u_ref[...] = u
        o_ref[...] = jnp.dot(u, w_ref[...], preferred_element_type=F32).astype(o_ref.dtype)

    @pl.when(j > 0)
    def _():
        o_ref[...] = jnp.dot(u_ref[...], w_ref[...], preferred_element_type=F32).astype(o_ref.dtype)


def _inproj(x2, g, w_bf16, tm=INPROJ_ROWS, tn=INPROJ_COLS):
    n, d = x2.shape
    cols = w_bf16.shape[1]
    return pl.pallas_call(
        _inproj_kernel,
        out_shape=jax.ShapeDtypeStruct((n, cols), BF16),
        grid=(n // tm, cols // tn),
        in_specs=[
            pl.BlockSpec((tm, d), lambda i, j: (i, 0)),
            pl.BlockSpec((1, d), lambda i, j: (0, 0)),
            pl.BlockSpec((d, tn), lambda i, j: (0, j)),
        ],
        out_specs=pl.BlockSpec((tm, tn), lambda i, j: (i, j)),
        scratch_shapes=[pltpu.VMEM((tm, d), BF16)],
        compiler_params=_cparams(("arbitrary", "arbitrary")),
        name="inproj",
    )(x2, g, w_bf16)


def _qkprep_kernel(p_ref, c_ref, s_ref, perm_ref, g_ref, o_ref):
    lane = lax.broadcasted_iota(jnp.int32, (1, LANES), 1)
    lo = lane < QK_DIM
    c = c_ref[...]
    s = s_ref[...]
    perm = perm_ref[...]
    for j in range(p_ref.shape[1] // LANES):
        t = p_ref[:, j * LANES:(j + 1) * LANES].astype(F32)
        sq = t * t
        ss_lo = jnp.sum(jnp.where(lo, sq, 0.0), axis=-1, keepdims=True)
        ss_hi = jnp.sum(jnp.where(lo, 0.0, sq), axis=-1, keepdims=True)
        r = lax.rsqrt(jnp.where(lo, ss_lo, ss_hi) * (1.0 / QK_DIM) + EPS)
        y = t * r * g_ref[:, j * LANES:(j + 1) * LANES]
        y_hi = y.astype(BF16)
        y_lo = (y - y_hi.astype(F32)).astype(BF16)
        partner = (jnp.dot(y_hi, perm, preferred_element_type=F32)
                   + jnp.dot(y_lo, perm, preferred_element_type=F32))
        o_ref[:, j * LANES:(j + 1) * LANES] = (y * c + partner * s).astype(o_ref.dtype)


def _qkprep(proj, ctab, stab, perm, gqk, seq, tm=QKPREP_ROWS):
    n = proj.shape[0]
    w = gqk.shape[1]
    nb = seq // tm
    return pl.pallas_call(
        _qkprep_kernel,
        out_shape=jax.ShapeDtypeStruct((n, w), BF16),
        grid=(n // tm,),
        in_specs=[
            pl.BlockSpec((tm, w), lambda i: (i, 0)),
            pl.BlockSpec((tm, LANES), lambda i: (i % nb, 0)),
            pl.BlockSpec((tm, LANES), lambda i: (i % nb, 0)),
            pl.BlockSpec((LANES, LANES), lambda i: (0, 0)),
            pl.BlockSpec((1, w), lambda i: (0, 0)),
        ],
        out_specs=pl.BlockSpec((tm, w), lambda i: (i, 0)),
        compiler_params=_cparams(("arbitrary",)),
        name="qkprep",
    )(proj, ctab, stab, perm, gqk)


def _attn_kernel(q_ref, k_ref, v_ref, lam_ref, g_ref, o_ref, *, tk, unroll, lambda_init):
    tq = q_ref.shape[0]
    seq = k_ref.shape[0]
    lane = lax.broadcasted_iota(jnp.int32, (1, LANES), 1)
    q = q_ref[...]
    zero = jnp.zeros_like(q)
    qs = jnp.concatenate([jnp.where(lane < QK_DIM, q, zero), jnp.where(lane < QK_DIM, zero, q)], axis=0)
    lv = lam_ref[...]
    lam = (jnp.exp(jnp.sum(lv[0:1] * lv[1:2], axis=-1, keepdims=True))
           - jnp.exp(jnp.sum(lv[2:3] * lv[3:4], axis=-1, keepdims=True)) + lambda_init)

    def step(c, carry):
        m, l, a = carry
        off = pl.multiple_of(c * tk, tk)
        kc = k_ref[pl.ds(off, tk), :]
        vc = v_ref[pl.ds(off, tk), :]
        s = lax.dot_general(qs, kc, (((1,), (1,)), ((), ())), preferred_element_type=F32)
        mn = jnp.maximum(m, jnp.max(s, axis=-1, keepdims=True))
        p = jnp.exp2((s - mn).astype(BF16))
        al = jnp.exp2(m - mn)
        pf = p.astype(F32)
        part = pf[:, 0:LANES]
        for j in range(1, tk // LANES):
            part = part + pf[:, j * LANES:(j + 1) * LANES]
        l = al * l + part
        a = al * a + jnp.dot(p, vc, preferred_element_type=F32)
        return mn, l, a

    init = (jnp.full((2 * tq, 1), -jnp.inf, F32), jnp.zeros((2 * tq, LANES), F32), jnp.zeros((2 * tq, V_DIM), F32))
    _, l, a = lax.fori_loop(0, seq // tk, step, init, unroll=unroll)
    a = a / jnp.sum(l, axis=-1, keepdims=True)
    o = a[:tq] - lam * a[tq:]
    ms = jnp.mean(o * o, axis=-1, keepdims=True)
    o = o * lax.rsqrt(ms + EPS) * g_ref[...] * (1.0 - lambda_init)
    o_ref[...] = o.astype(o_ref.dtype)


def _attention(qk, proj, lamv, subln_g, batch, seq, lambda_init, tq=ATTN_Q_ROWS, tk=ATTN_KEY_CHUNK):
    nq = seq // tq
    vcol0 = (2 * ATTN_HEADS * 2 * QK_DIM) // V_DIM
    return pl.pallas_call(
        functools.partial(_attn_kernel, tk=tk, unroll=seq // tk, lambda_init=lambda_init),
        out_shape=jax.ShapeDtypeStruct((batch * seq, ATTN_HEADS * V_DIM), BF16),
        grid=(batch, ATTN_HEADS, nq),
        in_specs=[
            pl.BlockSpec((tq, LANES), lambda b, h, i: (b * nq + i, h)),
            pl.BlockSpec((seq, LANES), lambda b, h, i: (b, ATTN_HEADS + h)),
            pl.BlockSpec((seq, V_DIM), lambda b, h, i: (b, vcol0 + h)),
            pl.BlockSpec((4, QK_DIM), lambda b, h, i: (0, 0)),
            pl.BlockSpec((1, V_DIM), lambda b, h, i: (0, 0)),
        ],
        out_specs=pl.BlockSpec((tq, V_DIM), lambda b, h, i: (b * nq + i, h)),
        compiler_params=_cparams(("arbitrary", "arbitrary", "arbitrary")),
        name="diffattn",
    )(qk, qk, proj, lamv, subln_g)


def _merge_kernel(at_ref, hy_ref, ga_ref, gh_ref, x_ref, wpa_ref, wph_ref, wo_ref, g2_ref, wr_ref,
                  x1_ref, u2_ref, aff_ref):
    ga = jax.nn.sigmoid(ga_ref[...].astype(F32))
    gh = jax.nn.sigmoid(gh_ref[...].astype(F32))
    merged = (ga * jnp.dot(at_ref[...], wpa_ref[...], preferred_element_type=F32)
              + gh * jnp.dot(hy_ref[...], wph_ref[...], preferred_element_type=F32))
    x1 = x_ref[...] + jnp.dot(merged.astype(BF16), wo_ref[...], preferred_element_type=F32)
    x1_ref[...] = x1
    ms = jnp.mean(x1 * x1, axis=-1, keepdims=True)
    u2 = (x1 * lax.rsqrt(ms + EPS) * g2_ref[...]).astype(BF16)
    u2_ref[...] = u2
    logits = jnp.dot(u2, wr_ref[...], preferred_element_type=F32)
    lane = lax.broadcasted_iota(jnp.int32, logits.shape, 1)
    logits = jnp.where(lane < N_EXPERTS, logits, -jnp.inf)
    e = jnp.exp(logits - jnp.max(logits, axis=-1, keepdims=True))
    aff_ref[...] = e / jnp.sum(e, axis=-1, keepdims=True)


def _merge(attn, hyena, proj, x2, wpa, wph, wo, g2, wr_pad, tm=MERGE_ROWS):
    n, d = x2.shape
    gcol0 = (proj.shape[1] - 2 * d) // d
    full = lambda shape: pl.BlockSpec(shape, lambda i: (0, 0))
    return pl.pallas_call(
        _merge_kernel,
        out_shape=(jax.ShapeDtypeStruct((n, d), F32), jax.ShapeDtypeStruct((n, d), BF16),
                   jax.ShapeDtypeStruct((n, LANES), F32)),
        grid=(n // tm,),
        in_specs=[
            pl.BlockSpec((tm, attn.shape[1]), lambda i: (i, 0)),
            pl.BlockSpec((tm, hyena.shape[1]), lambda i: (i, 0)),
            pl.BlockSpec((tm, d), lambda i: (i, gcol0)),
            pl.BlockSpec((tm, d), lambda i: (i, gcol0 + 1)),
            pl.BlockSpec((tm, d), lambda i: (i, 0)),
            full(wpa.shape), full(wph.shape), full(wo.shape), full(g2.shape), full(wr_pad.shape),
        ],
        out_specs=(pl.BlockSpec((tm, d), lambda i: (i, 0)), pl.BlockSpec((tm, d), lambda i: (i, 0)),
                   pl.BlockSpec((tm, LANES), lambda i: (i, 0))),
        compiler_params=_cparams(("arbitrary",)),
        name="merge",
    )(attn, hyena, proj, proj, x2, wpa, wph, wo, g2, wr_pad)


def _excl_cumsum_rows(mask_f32, tri, blk):
    rows = mask_f32.shape[0]
    carry = jnp.zeros((1, LANES), F32)
    outs = []
    for r in range(rows // blk):
        mb = mask_f32[r * blk:(r + 1) * blk]
        outs.append(jnp.dot(tri, mb.astype(BF16), preferred_element_type=F32) + carry)
        carry = carry + jnp.sum(mb, axis=0, keepdims=True)
    return jnp.concatenate(outs, axis=0)


def _topk_kernel(aff_ref, pos_ref, post_ref, afft_ref, offs_ref, *, cap, blk, batch):
    seq = aff_ref.shape[0] // batch
    aff = aff_ref[0:seq]
    for b in range(1, batch):
        aff = aff + pltpu.roll(aff_ref[b * seq:(b + 1) * seq], b * N_EXPERTS, 1)

    def search(i, prefix):
        cand = prefix | jnp.left_shift(jnp.int32(1), 29 - i)
        cnt = jnp.sum((aff >= pltpu.bitcast(cand, F32)[0:1]).astype(F32), axis=0, keepdims=True)
        return jnp.where(cnt >= cap, cand, prefix)

    lo = pltpu.bitcast(lax.fori_loop(0, 30, search, jnp.zeros((8, LANES), jnp.int32)), F32)[0:1]
    thr = jnp.min(jnp.where(aff >= lo, aff, jnp.inf), axis=0, keepdims=True)
    gt = (aff > thr).astype(F32)
    eq = (aff == thr).astype(F32)
    need = cap - jnp.sum(gt, axis=0, keepdims=True)
    ri = lax.broadcasted_iota(jnp.int32, (blk, blk), 0)
    ci = lax.broadcasted_iota(jnp.int32, (blk, blk), 1)
    tri = (ci < ri).astype(BF16)
    sel = gt + eq * (_excl_cumsum_rows(eq, tri, blk) < need).astype(F32)
    before = _excl_cumsum_rows(sel, tri, blk)
    pos = jnp.where(sel > 0.0, before, -1.0)
    pos_t = pos.T
    aff_t = aff.T
    lane = lax.broadcasted_iota(jnp.int32, (1, LANES), 1)
    for b in range(batch):
        mine = pos if b == 0 else pltpu.roll(pos, LANES - b * N_EXPERTS, 1)
        pos_ref[b * seq:(b + 1) * seq] = jnp.where(lane < N_EXPERTS, mine, -1.0)
        post_ref[b] = pos_t[b * N_EXPERTS:(b + 1) * N_EXPERTS]
        afft_ref[b] = aff_t[b * N_EXPERTS:(b + 1) * N_EXPERTS]
    offs_ref[...] = jnp.concatenate([before[j * MOE_TILE:j * MOE_TILE + 1] for j in range(offs_ref.shape[0])], axis=0)


def _topk(aff, batch, seq, cap, blk=TOPK_CUMSUM_ROWS):
    assert batch * N_EXPERTS <= LANES
    nt = seq // MOE_TILE
    pos, post, afft, offs = pl.pallas_call(
        functools.partial(_topk_kernel, cap=cap, blk=blk, batch=batch),
        out_shape=(jax.ShapeDtypeStruct((batch * seq, LANES), F32),
                   jax.ShapeDtypeStruct((batch, N_EXPERTS, seq), F32),
                   jax.ShapeDtypeStruct((batch, N_EXPERTS, seq), F32),
                   jax.ShapeDtypeStruct((nt, LANES), F32)),
        compiler_params=pltpu.CompilerParams(vmem_limit_bytes=VMEM_LIMIT),
        name="topk",
    )(aff)
    offs = offs[:, :batch * N_EXPERTS].reshape(nt, batch, N_EXPERTS).transpose(1, 0, 2)
    return pos, post, afft, offs.astype(jnp.int32).reshape(-1)


def _align_down(x, m):
    sh = m.bit_length() - 1
    return lax.shift_left(lax.shift_right_logical(x, sh), sh)


def _ceil_div(x, m):
    return lax.shift_right_logical(x + (m - 1), m.bit_length() - 1)


def _gather_kernel(offs_ref, post_ref, u_ref, x_ref, *, cap, win):
    b = pl.program_id(0)
    c = pl.program_id(1)
    nt = pl.num_programs(1)

    @pl.when(c == 0)
    def _():
        x_ref[...] = jnp.zeros_like(x_ref)

    uc = u_ref[...]
    slot0 = lax.broadcasted_iota(jnp.int32, (win, MOE_TILE), 0).astype(F32)
    base = (b * nt + c) * N_EXPERTS
    last = pl.num_programs(0) * nt * N_EXPERTS - 1
    starts, hots = [], []
    for e in range(N_EXPERTS):
        s = pl.multiple_of(jnp.minimum(_align_down(offs_ref[base + e], 16), cap - win), 16)
        hots.append(jnp.where((slot0 + s.astype(F32)) == post_ref[0, e:e + 1, :], 1.0, 0.0).astype(BF16))
        starts.append(s)
    rows = jnp.dot(jnp.concatenate(hots, axis=0), uc, preferred_element_type=F32)
    for e in range(N_EXPERTS):
        x_ref[0, e, pl.ds(starts[e], win), :] += rows[e * win:(e + 1) * win].astype(BF16)
    for e in range(N_EXPERTS):
        lo0 = _align_down(offs_ref[base + e], 16)
        end = jnp.where(c + 1 < nt, offs_ref[jnp.minimum(base + N_EXPERTS + e, last)], cap)

        def more(k, carry, lo0=lo0, e=e):
            lo = lo0 + k * win
            s = pl.multiple_of(jnp.minimum(lo, cap - win), 16)
            srow = slot0 + s.astype(F32)
            hit = jnp.logical_and(srow == post_ref[0, e:e + 1, :], srow >= lo.astype(F32))
            x_ref[0, e, pl.ds(s, win), :] += jnp.dot(jnp.where(hit, 1.0, 0.0).astype(BF16), uc,
                                                     preferred_element_type=F32).astype(BF16)
            return carry

        lax.fori_loop(1, _ceil_div(end - lo0, win), more, 0)


def _gather(offs, post, u2, batch, seq, cap, win=GATHER_WINDOW):
    d = u2.shape[1]
    nt = seq // MOE_TILE
    grid_spec = pltpu.PrefetchScalarGridSpec(
        num_scalar_prefetch=1,
        grid=(batch, nt),
        in_specs=[
            pl.BlockSpec((1, N_EXPERTS, MOE_TILE), lambda b, c, o: (b, 0, c)),
            pl.BlockSpec((MOE_TILE, d), lambda b, c, o: (b * nt + c, 0)),
        ],
        out_specs=pl.BlockSpec((1, N_EXPERTS, cap, d), lambda b, c, o: (b, 0, 0, 0)),
    )
    return pl.pallas_call(
        functools.partial(_gather_kernel, cap=cap, win=win),
        out_shape=jax.ShapeDtypeStruct((batch, N_EXPERTS, cap, d), BF16),
        grid_spec=grid_spec,
        compiler_params=_cparams(("arbitrary", "arbitrary")),
        name="gather",
    )(offs, post, u2)


def _expert_kernel(offs_ref, post_ref, afft_ref, x_ref, wg_ref, wu_ref, wd_ref, o_ref, gs_ref, w_ref, eo_ref,
                   *, cap, win):
    e = pl.program_id(0)
    b = pl.program_id(1)

    def cast():
        for m, src in enumerate((wg_ref, wu_ref, wd_ref)):
            part = src.shape[2]
            w_ref[jnp.bitwise_and(e, 1), m, pl.ds(pl.multiple_of(b * part, part), part), :] = src[0, 0].astype(BF16)

    def run():
        _expert_body(offs_ref, post_ref, afft_ref, x_ref, w_ref.at[1 - jnp.bitwise_and(e, 1)], o_ref, gs_ref, eo_ref,
                     e - 1, b, cap, win)

    @pl.when(e == 0)
    def _():
        cast()
        o_ref[...] = jnp.zeros_like(o_ref)

    @pl.when(jnp.logical_and(e > 0, e < N_EXPERTS))
    def _():
        cast()
        run()

    @pl.when(e == N_EXPERTS)
    def _():
        run()


def _expert_body(offs_ref, post_ref, afft_ref, x_ref, w_ref, o_ref, gs_ref, eo_ref, e, b, cap, win):
    seq = post_ref.shape[2]
    nt = seq // MOE_TILE
    gs_ref[...] = jnp.zeros_like(gs_ref)
    slot0 = lax.broadcasted_iota(jnp.int32, (win, MOE_TILE), 0).astype(F32)
    windows = []
    for c in range(nt):
        base = (b * nt + c) * N_EXPERTS + e
        lo0 = _align_down(offs_ref[base], 8)
        end = offs_ref[base + N_EXPERTS] if c + 1 < nt else cap
        prow = post_ref[0, pl.ds(e, 1), c * MOE_TILE:(c + 1) * MOE_TILE]
        arow = afft_ref[0, pl.ds(e, 1), c * MOE_TILE:(c + 1) * MOE_TILE]

        def window(k, carry, lo0=lo0, prow=prow, arow=arow):
            lo = lo0 + k * win
            s = pl.multiple_of(jnp.minimum(lo, cap - win), 8)
            srow = slot0 + s.astype(F32)
            hit = jnp.logical_and(srow == prow, srow >= lo.astype(F32))
            gs_ref[pl.ds(s, win), :] += jnp.sum(jnp.where(hit, arow, 0.0), axis=-1, keepdims=True)
            return carry

        window(0, 0)
        windows.append((lo0, end, window))
    xb = x_ref[0, 0]
    hg = jnp.dot(xb, w_ref[0], preferred_element_type=F32)
    hu = jnp.dot(xb, w_ref[1], preferred_element_type=F32)
    act = (hg * jax.nn.sigmoid(hg) * hu).astype(BF16)
    eo_ref[...] = jnp.dot(act, w_ref[2], preferred_element_type=F32)
    for lo0, end, window in windows:
        lax.fori_loop(1, _ceil_div(end - lo0, win), window, 0)
    o_ref[0, 0] = (eo_ref[...] * gs_ref[...]).astype(o_ref.dtype)


def _experts(offs, post, afft, xin, wg, wu, wd, layer, batch, seq, cap, win=GATHER_WINDOW):
    d = xin.shape[3]
    assert wg.shape[2:] == (d, d) and wd.shape[2:] == (d, d) and d % (16 * batch) == 0
    part = d // batch
    wspec = pl.BlockSpec((1, 1, part, d), lambda e, b, o: (layer, jnp.minimum(e, N_EXPERTS - 1), b, 0))
    grid_spec = pltpu.PrefetchScalarGridSpec(
        num_scalar_prefetch=1,
        grid=(N_EXPERTS + 1, batch),
        in_specs=[
            pl.BlockSpec((1, N_EXPERTS, seq), lambda e, b, o: (b, 0, 0)),
            pl.BlockSpec((1, N_EXPERTS, seq), lambda e, b, o: (b, 0, 0)),
            pl.BlockSpec((1, 1, cap, d), lambda e, b, o: (b, jnp.maximum(e - 1, 0), 0, 0)),
            wspec, wspec, wspec,
        ],
        out_specs=pl.BlockSpec((1, 1, cap, d), lambda e, b, o: (b, jnp.where(e == 0, N_EXPERTS, e - 1), 0, 0)),
        scratch_shapes=[pltpu.VMEM((cap, 1), F32), pltpu.VMEM((2, 3, d, d), BF16), pltpu.VMEM((cap, d), F32)],
    )
    return pl.pallas_call(
        functools.partial(_expert_kernel, cap=cap, win=win),
        out_shape=jax.ShapeDtypeStruct((batch, N_EXPERTS + 1, cap, d), BF16),
        grid_spec=grid_spec,
        compiler_params=_cparams(("arbitrary", "arbitrary")),
        name="experts",
    )(offs, post, afft, xin, wg, wu, wd)


def _combine_kernel(offs_ref, pos_ref, x1_ref, weo_ref, o_ref, *, cap, win):
    b = pl.program_id(0)
    t = pl.program_id(1)
    nt = pl.num_programs(1)
    tt = pos_ref.shape[0]
    pos = pos_ref[...]
    slot0 = lax.broadcasted_iota(jnp.int32, (tt, win), 1).astype(F32)
    base = (b * nt + t) * N_EXPERTS
    last = pl.num_programs(0) * nt * N_EXPERTS - 1
    acc = x1_ref[...]
    for e0 in range(0, N_EXPERTS, 2):
        hots, rows = [], []
        for e in (e0, e0 + 1):
            s = pl.multiple_of(jnp.minimum(_align_down(offs_ref[base + e], 16), cap - win), 16)
            hots.append(jnp.where((slot0 + s.astype(F32)) == pos[:, e:e + 1], 1.0, 0.0).astype(BF16))
            rows.append(weo_ref[0, e, pl.ds(s, win), :])
        acc = acc + jnp.dot(jnp.concatenate(hots, axis=1), jnp.concatenate(rows, axis=0),
                            preferred_element_type=F32)
    o_ref[...] = acc
    for e in range(N_EXPERTS):
        lo0 = _align_down(offs_ref[base + e], 16)
        end = jnp.where(t + 1 < nt, offs_ref[jnp.minimum(base + N_EXPERTS + e, last)], cap)

        def window(k, carry, lo0=lo0, e=e):
            lo = lo0 + k * win
            s = pl.multiple_of(jnp.minimum(lo, cap - win), 16)
            srow = slot0 + s.astype(F32)
            hit = jnp.logical_and(srow == pos[:, e:e + 1], srow >= lo.astype(F32))
            o_ref[...] += jnp.dot(jnp.where(hit, 1.0, 0.0).astype(BF16), weo_ref[0, e, pl.ds(s, win), :],
                                  preferred_element_type=F32)
            return carry

        lax.fori_loop(1, _ceil_div(end - lo0, win), window, 0)


def _combine(offs, pos, x1, weo, batch, seq, cap, win=COMBINE_WINDOW):
    d = x1.shape[1]
    tt = MOE_TILE
    nt = seq // tt
    grid_spec = pltpu.PrefetchScalarGridSpec(
        num_scalar_prefetch=1,
        grid=(batch, nt),
        in_specs=[
            pl.BlockSpec((tt, LANES), lambda b, t, o: (b * nt + t, 0)),
            pl.BlockSpec((tt, d), lambda b, t, o: (b * nt + t, 0)),
            pl.BlockSpec((1, N_EXPERTS, cap, d), lambda b, t, o: (b, 0, 0, 0)),
        ],
        out_specs=pl.BlockSpec((tt, d), lambda b, t, o: (b * nt + t, 0)),
    )
    return pl.pallas_call(
        functools.partial(_combine_kernel, cap=cap, win=win),
        out_shape=jax.ShapeDtypeStruct(x1.shape, F32),
        grid_spec=grid_spec,
        compiler_params=_cparams(("arbitrary", "arbitrary")),
        name="combine",
    )(offs, pos, x1, weo)


FFT_R = 64
FFT_PITCH = FFT_R + 8


def _dft_tables(seq):
    r = FFT_R
    assert seq == r * r
    n = np.arange(r)
    f = np.exp(-2j * np.pi * np.outer(n, n) / r)
    w = lambda e: np.exp(-2j * np.pi * e / (2 * seq))

    def real_rep(a):
        return np.block([[a.real, -a.imag], [a.imag, a.real]]).astype(np.float32)

    m1 = [f, f * w(r * n)[None, :]]
    m2 = [f * w(n * m)[None, :] for m in range(2 * r)]
    m4 = [f.conj() / (2 * seq), w(-r * n)[:, None] * f.conj() / (2 * seq)]
    t1 = np.stack([real_rep(a) for a in m1])
    t2 = np.stack([real_rep(a) for a in m2])
    t3 = np.stack([real_rep(a.conj().T) for a in m2])
    t4 = np.stack([real_rep(a) for a in m4])
    t1r = np.stack([np.concatenate([a.real, a.imag], axis=0).astype(np.float32) for a in m1])
    return tuple(jnp.asarray(t).astype(BF16) for t in (t1, t2, t3, t4, t1r))


def _slab(r):
    return pl.ds(pl.multiple_of(r * FFT_PITCH, 8), FFT_R)


def _across(r):
    return pl.ds(r, FFT_R, stride=FFT_PITCH)


def _cat_bf16(a, b):
    return jnp.concatenate([a, b], axis=0).astype(BF16)


FFT_UNROLL = 32


def _loop(body, step=1, rows_per_trip=FFT_UNROLL):
    def wrapped(i, c):
        body(i * step)
        return c
    lax.fori_loop(0, FFT_R // step, wrapped, 0, unroll=rows_per_trip // step)


def _pair_bf16(re_ref, im_ref, idx0, idx1):
    return jnp.concatenate([_cat_bf16(re_ref[idx0, :], im_ref[idx0, :]),
                            _cat_bf16(re_ref[idx1, :], im_ref[idx1, :])], axis=1)


def _bf16_pieces(w):
    hi = w.astype(BF16)
    return hi, (w - hi.astype(F32)).astype(BF16)


def _dot3(a, w_ref):
    a_hi, a_lo = _bf16_pieces(a)
    return (jnp.dot(a_hi, w_ref[0], preferred_element_type=F32)
            + (jnp.dot(a_hi, w_ref[1], preferred_element_type=F32)
               + jnp.dot(a_lo, w_ref[0], preferred_element_type=F32)))


def _filter_kernel(bands_ref, w1t_ref, w1c_ref, w1s_ref, b1_ref, w2_ref, b2_ref, w3_ref, b3_ref, fr_ref,
                   wf_ref, wb_ref, dl_ref, hs_ref, hd_ref, *, seq):
    tm = hs_ref.shape[0]
    rows = tm // FILT_GROUPS
    hid = wf_ref.shape[1]
    base = pl.program_id(0) * tm
    lane = lax.broadcasted_iota(jnp.int32, (rows, LANES), 1)
    grp = lax.shift_right_logical(lane, FILTER_BANDS.bit_length() - 1)
    pos = (base + grp * rows + lax.broadcasted_iota(jnp.int32, (rows, LANES), 0)).astype(F32)
    first = jnp.bitwise_and(lane, FILTER_BANDS - 1) == 0
    fr = fr_ref[...]
    tscale = 1.0 / max(seq - 1, 1)

    def mlp(p):
        ang = ((2.0 * math.pi / seq) * p) * bands_ref[...]
        pre = (_dot3(jnp.cos(ang), w1c_ref) + _dot3(-jnp.sin(ang), w1s_ref)
               + _dot3(jnp.where(first, p * tscale, 0.0), w1t_ref) + b1_ref[...])
        h = jnp.sin(fr * pre)
        h = jnp.sin(fr * (_dot3(h, w2_ref) + b2_ref[...]))
        return jnp.sin(fr * (_dot3(h, w3_ref) + b3_ref[...]))

    hf_h = mlp(pos)
    hb_h = mlp(seq - pos)
    dl = dl_ref[...]
    for g in range(FILT_GROUPS):
        pcol = (base + g * rows + lax.broadcasted_iota(jnp.int32, (rows, 1), 0)).astype(F32)
        hf = _dot3(hf_h[:, g * hid:(g + 1) * hid], wf_ref) * jnp.exp(-(pcol * tscale) * dl)
        hb = _dot3(hb_h[:, g * hid:(g + 1) * hid], wb_ref) * jnp.exp(-((seq - pcol) * tscale) * dl)
        hb = jnp.where(pcol > 0.0, hb, 0.0)
        hs_ref[g * rows:(g + 1) * rows, :] = (hf + hb).astype(hs_ref.dtype)
        hd_ref[g * rows:(g + 1) * rows, :] = (hf - hb).astype(hd_ref.dtype)


def _filters(bands, w1t, w1c, w1s, b1, w2, b2, w3, b3, fr, wf, wb, dl, seq, tm=FILTER_ROWS):
    cols = wf.shape[1]
    pieces = lambda w: jnp.stack(_bf16_pieces(w))
    args = (bands, pieces(w1t), pieces(w1c), pieces(w1s), b1, pieces(w2), b2, pieces(w3), b3, fr,
            pieces(wf), pieces(wb), dl)
    return pl.pallas_call(
        functools.partial(_filter_kernel, seq=seq),
        out_shape=(jax.ShapeDtypeStruct((seq, cols), BF16), jax.ShapeDtypeStruct((seq, cols), BF16)),
        grid=(seq // tm,),
        in_specs=[pl.BlockSpec(a.shape, lambda i, nd=a.ndim: (0,) * nd) for a in args],
        out_specs=(pl.BlockSpec((tm, cols), lambda i: (i, 0)), pl.BlockSpec((tm, cols), lambda i: (i, 0))),
        compiler_params=_cparams(("arbitrary",)),
        name="hyfilter",
    )(*args)


def _spectra_kernel(hs_ref, hd_ref, t1r_ref, t2_ref, h_ref, xe_ref, xo_ref, per_ref, pei_ref, por_ref, poi_ref):
    xx = (xe_ref, xo_ref)
    pp = ((per_ref, pei_ref), (por_ref, poi_ref))

    def fill(r):
        rows = pl.ds(pl.multiple_of(r * FFT_R, FFT_R), FFT_R)
        xe_ref[_slab(r), :] = hs_ref[rows, :].astype(F32)
        xo_ref[_slab(r), :] = hd_ref[rows, :].astype(F32)
    _loop(fill)

    def first(r):
        for par in (0, 1):
            o = jnp.dot(t1r_ref[par], xx[par][_across(r), :].astype(BF16), preferred_element_type=F32)
            pp[par][0][_slab(r), :] = o[:FFT_R]
            pp[par][1][_slab(r), :] = o[FFT_R:]
    _loop(first)

    def second(r):
        rows = pl.ds(pl.multiple_of(r * FFT_R, FFT_R), FFT_R)
        for par in (0, 1):
            o = jnp.dot(t2_ref[2 * r + par], _cat_bf16(pp[par][0][_across(r), :], pp[par][1][_across(r), :]),
                        preferred_element_type=F32)
            h_ref[0, 2 * par, rows, :] = o[:FFT_R].astype(h_ref.dtype)
            h_ref[0, 2 * par + 1, rows, :] = o[FFT_R:].astype(h_ref.dtype)
    _loop(second)


def _spectra(hs, hd, t1r, t2, seq):
    ncol = hs.shape[1] // LANES
    nct = ncol // HYENA_ORDER
    prow = FFT_R * FFT_PITCH
    const = lambda a: pl.BlockSpec(a.shape, lambda o, c: (0,) * a.ndim, pipeline_mode=pl.Buffered(1))
    return pl.pallas_call(
        _spectra_kernel,
        out_shape=jax.ShapeDtypeStruct((HYENA_ORDER, 4, seq, nct * LANES), BF16),
        grid=(HYENA_ORDER, nct),
        in_specs=[pl.BlockSpec((seq, LANES), lambda o, c: (0, o * nct + c)),
                  pl.BlockSpec((seq, LANES), lambda o, c: (0, o * nct + c)),
                  const(t1r), const(t2)],
        out_specs=pl.BlockSpec((1, 4, seq, LANES), lambda o, c: (o, 0, 0, c)),
        scratch_shapes=[pltpu.VMEM((prow, LANES), F32)] * 6,
        compiler_params=_cparams(("arbitrary", "arbitrary")),
        name="hyspectra",
    )(hs, hd, t1r, t2)


def _short_conv(x, w_ref, b_ref):
    rows = x.shape[0]
    sub = 8
    ri = lax.broadcasted_iota(jnp.int32, (sub, x.shape[1]), 0)
    prev = pltpu.roll(x, 1, 0)
    prev = jnp.concatenate([jnp.where(ri == 0, 0.0, prev[:sub]), prev[sub:]], axis=0)
    nxt = pltpu.roll(x, rows - 1, 0)
    nxt = jnp.concatenate([nxt[:rows - sub], jnp.where(ri == sub - 1, 0.0, nxt[rows - sub:])], axis=0)
    return prev * w_ref[0:1, :] + x * w_ref[1:2, :] + nxt * w_ref[2:3, :] + b_ref[...]


def _hyconv_kernel(z_ref, g_ref, cwz_ref, cbz_ref, cwg_ref, cbg_ref, skip_ref, h_ref,
                   t1_ref, t2_ref, t3_ref, t4_ref, o_ref,
                   xr_ref, xi_ref, per_ref, pei_ref, por_ref, poi_ref, qer_ref, qei_ref, qor_ref, qoi_ref,
                   *, conv_z):
    seq = z_ref.shape[0] // 2
    for half, dst in ((0, xr_ref), (1, xi_ref)):
        z = z_ref[half * seq:(half + 1) * seq, :].astype(F32)
        if conv_z:
            z = _short_conv(z, cwz_ref, cbz_ref)
        for j in range(FFT_R):
            dst[j * FFT_PITCH:j * FFT_PITCH + FFT_R, :] = z[j * FFT_R:(j + 1) * FFT_R]

    pp = ((per_ref, pei_ref), (por_ref, poi_ref))
    qq = ((qer_ref, qei_ref), (qor_ref, qoi_ref))
    yr_ref, yi_ref = qer_ref, qei_ref
    t1 = jnp.concatenate([t1_ref[0], t1_ref[1]], axis=0)
    t4 = jnp.concatenate([t4_ref[0], t4_ref[1]], axis=1)
    halves = (slice(0, LANES), slice(LANES, 2 * LANES))

    def first(r):
        o = jnp.dot(t1, _pair_bf16(xr_ref, xi_ref, _across(r), _across(r + 1)), preferred_element_type=F32)
        for par in (0, 1):
            for k, cols in zip((r, r + 1), halves):
                pp[par][0][_slab(k), :] = o[2 * par * FFT_R:(2 * par + 1) * FFT_R, cols]
                pp[par][1][_slab(k), :] = o[(2 * par + 1) * FFT_R:(2 * par + 2) * FFT_R, cols]
    _loop(first, step=2)

    def second(r):
        rows = pl.ds(pl.multiple_of(r * FFT_R, FFT_R), FFT_R)
        for par in (0, 1):
            o = jnp.dot(t2_ref[2 * r + par], _cat_bf16(pp[par][0][_across(r), :], pp[par][1][_across(r), :]),
                        preferred_element_type=F32)
            hr = h_ref[0, 2 * par, rows, :].astype(F32)
            hi = h_ref[0, 2 * par + 1, rows, :].astype(F32)
            ar, ai = o[:FFT_R], o[FFT_R:]
            qq[par][0][_slab(r), :] = ar * hr - ai * hi
            qq[par][1][_slab(r), :] = ar * hi + ai * hr
    _loop(second)

    def third(r):
        for par in (0, 1):
            o = jnp.dot(t3_ref[2 * r + par], _cat_bf16(qq[par][0][_slab(r), :], qq[par][1][_slab(r), :]),
                        preferred_element_type=F32)
            pp[par][0][_across(r), :] = o[:FFT_R]
            pp[par][1][_across(r), :] = o[FFT_R:]
    _loop(third)

    def fourth(r):
        quad = lambda k: jnp.concatenate([pp[0][0][_slab(k), :], pp[0][1][_slab(k), :],
                                          pp[1][0][_slab(k), :], pp[1][1][_slab(k), :]], axis=0).astype(BF16)
        o = jnp.dot(t4, jnp.concatenate([quad(r), quad(r + 1)], axis=1), preferred_element_type=F32)
        for k, cols in zip((r, r + 1), halves):
            yr_ref[_across(k), :] = o[:FFT_R, cols]
            yi_ref[_across(k), :] = o[FFT_R:, cols]
    _loop(fourth, step=2)

    skip = skip_ref[...]
    for half, (y_ref, x_ref) in enumerate(((yr_ref, xr_ref), (yi_ref, xi_ref))):
        g = _short_conv(g_ref[half * seq:(half + 1) * seq, :].astype(F32), cwg_ref, cbg_ref)
        for j in range(FFT_R):
            src = slice(j * FFT_PITCH, j * FFT_PITCH + FFT_R)
            dst = slice(half * seq + j * FFT_R, half * seq + (j + 1) * FFT_R)
            o_ref[dst, :] = (g[j * FFT_R:(j + 1) * FFT_R] * (y_ref[src, :] + skip * x_ref[src, :])).astype(o_ref.dtype)


def _hyconv(zsrc, zcol0, gcol0, proj, conv_w, conv_b, zpart, gpart, skip, spec, order, tabs, batch, seq, conv_z):
    t1, t2, t3, t4 = tabs
    nct = HYENA_WIDTH // LANES
    prow = FFT_R * FFT_PITCH
    const = lambda a: pl.BlockSpec(a.shape, lambda c, p: (0,) * a.ndim, pipeline_mode=pl.Buffered(1))
    return pl.pallas_call(
        functools.partial(_hyconv_kernel, conv_z=conv_z),
        out_shape=jax.ShapeDtypeStruct((batch * seq, HYENA_WIDTH), BF16),
        grid=(nct, batch // 2),
        in_specs=[
            pl.BlockSpec((2 * seq, LANES), lambda c, p: (p, zcol0 + c)),
            pl.BlockSpec((2 * seq, LANES), lambda c, p: (p, gcol0 + c)),
            pl.BlockSpec((3, LANES), lambda c, p: (0, zpart * nct + c)),
            pl.BlockSpec((1, LANES), lambda c, p: (0, zpart * nct + c)),
            pl.BlockSpec((3, LANES), lambda c, p: (0, gpart * nct + c)),
            pl.BlockSpec((1, LANES), lambda c, p: (0, gpart * nct + c)),
            pl.BlockSpec((1, LANES), lambda c, p: (0, c)),
            pl.BlockSpec((1, 4, seq, LANES), lambda c, p: (order, 0, 0, c), pipeline_mode=pl.Buffered(1)),
            const(t1), const(t2), const(t3), const(t4),
        ],
        out_specs=pl.BlockSpec((2 * seq, LANES), lambda c, p: (p, c)),
        scratch_shapes=[pltpu.VMEM((prow, LANES), F32)] * 10,
        compiler_params=_cparams(("arbitrary", "arbitrary")),
        name=f"hyconv{order}",
    )(zsrc, proj, conv_w, conv_b, conv_w, conv_b, skip[order:order + 1], spec, t1, t2, t3, t4)


def _hyena(proj, hycol0, conv_w, conv_b, w1, b1, w2, b2, w3, b3, w_out, freq, skip, batch, seq):
    hid = w2.shape[0]
    eye = jnp.eye(FILT_GROUPS, dtype=F32)
    bdiag = lambda a: jnp.kron(eye, a)
    tile = lambda a: jnp.tile(a[None], (1, FILT_GROUPS))
    bands = tile(jnp.linspace(1e-4, FILTER_BANDS - 1, FILTER_BANDS, dtype=F32))
    min_decay = math.log(DECAY_TARGET) / SLOW_DECAY_PCT
    max_decay = math.log(DECAY_TARGET) / FAST_DECAY_PCT
    deltas = jnp.abs(jnp.linspace(min_decay, max_decay, HYENA_WIDTH, dtype=F32))
    wo4 = w_out.reshape(hid, HYENA_ORDER, 2, HYENA_WIDTH)
    w1t = jnp.pad(w1[0:1], ((0, FILTER_BANDS - 1), (0, 0)))
    hs, hd = _filters(
        bands, bdiag(w1t), bdiag(w1[1:1 + FILTER_BANDS]), bdiag(w1[1 + FILTER_BANDS:]),
        tile(b1), bdiag(w2), tile(b2), bdiag(w3), tile(b3), tile(freq),
        wo4[:, :, 0].reshape(hid, -1), wo4[:, :, 1].reshape(hid, -1),
        jnp.tile(deltas, HYENA_ORDER)[None], seq)
    t1, t2, t3, t4, t1r = _dft_tables(seq)
    spec = _spectra(hs, hd, t1r, t2, seq)
    c0 = hycol0 // LANES
    nct = HYENA_WIDTH // LANES
    cb = conv_b[None]
    z1 = _hyconv(proj, c0, c0 + nct, proj, conv_w, cb, 0, 1, skip, spec, 0, (t1, t2, t3, t4), batch, seq, True)
    return _hyconv(z1, 0, c0 + 2 * nct, proj, conv_w, cb, 0, 2, skip, spec, 1, (t1, t2, t3, t4), batch, seq, False)


def _rope_tables(seq):
    pos = np.arange(seq, dtype=np.float32)
    inv_freq = (ROPE_THETA ** (-np.arange(0, ROPE_DIM, 2, dtype=np.float32) / ROPE_DIM)).astype(np.float32)
    ang = pos[:, None] * inv_freq[None, :]
    cos, sin = np.cos(ang).astype(np.float32), np.sin(ang).astype(np.float32)
    half = ROPE_DIM // 2
    c = np.ones((seq, QK_DIM), np.float32)
    s = np.zeros((seq, QK_DIM), np.float32)
    c[:, :half] = cos
    c[:, half:ROPE_DIM] = cos
    s[:, :half] = -sin
    s[:, half:ROPE_DIM] = sin
    perm = np.zeros((LANES, LANES), np.float32)
    for j in range(LANES):
        if j % QK_DIM < half:
            perm[j + half, j] = 1.0
        elif j % QK_DIM < ROPE_DIM:
            perm[j - half, j] = 1.0
    rep = LANES // QK_DIM
    return (jnp.asarray(np.tile(c, (1, rep))), jnp.asarray(np.tile(s, (1, rep))), jnp.asarray(perm).astype(BF16))


def kernel(x, norm1_g, w_in, short_conv_w, short_conv_b, q_norm_g, k_norm_g, lambda_q1, lambda_k1, lambda_q2, lambda_k2, subln_g, filt_w1, filt_b1, filt_w2, filt_b2, filt_w3, filt_b3, filt_w_out, filt_freq, hyena_skip, w_branch_attn, w_branch_hyena, w_out, norm2_g, w_router, w_gate, w_up, w_down):
    b, l, d = x.shape
    depth = w_in.shape[0]
    n = b * l
    cap = CAPACITY_FACTOR * l // N_EXPERTS
    q_cols = ATTN_HEADS * 2 * QK_DIM
    ctab, stab, perm = _rope_tables(l)
    xc = x.reshape(n, d)
    for li in range(depth):
        lambda_init = 0.8 - 0.6 * math.exp(-0.3 * li)
        proj = _inproj(xc, norm1_g[li][None], w_in[li].astype(BF16))
        scale = math.log2(math.e) / math.sqrt(QK_DIM)
        gqk = jnp.concatenate([jnp.tile(q_norm_g[li], q_cols // QK_DIM) * scale,
                               jnp.tile(k_norm_g[li], q_cols // QK_DIM)])[None]
        qk = _qkprep(proj, ctab, stab, perm, gqk, l)
        lamv = jnp.stack([lambda_q1[li], lambda_k1[li], lambda_q2[li], lambda_k2[li]])
        attn = _attention(qk, proj, lamv, subln_g[li][None], b, l, lambda_init)

        hyena = _hyena(proj, 3 * q_cols, short_conv_w[li], short_conv_b[li], filt_w1[li], filt_b1[li],
                       filt_w2[li], filt_b2[li], filt_w3[li], filt_b3[li], filt_w_out[li], filt_freq[li],
                       hyena_skip[li], b, l)

        wr_pad = jnp.pad(w_router[li], ((0, 0), (0, LANES - N_EXPERTS))).astype(BF16)
        x1, u2, aff = _merge(attn, hyena, proj, xc, w_branch_attn[li].astype(BF16),
                             w_branch_hyena[li].astype(BF16), w_out[li].astype(BF16),
                             norm2_g[li][None], wr_pad)
        pos, post, afft, offs = _topk(aff, b, l, cap)
        xin = _gather(offs, post, u2, b, l, cap)
        weo = _experts(offs, post, afft, xin, w_gate, w_up, w_down, li, b, l, cap)
        xc = _combine(offs, pos, x1, weo, b, l, cap)
    return xc.reshape(b, l, d)
```

```python
import functools
import math

import jax
import jax.numpy as jnp
import numpy as np
from jax import lax
from jax.experimental import pallas as pl
from jax.experimental.pallas import tpu as pltpu

F32 = jnp.float32
BF16 = jnp.bfloat16

ATTN_HEADS = 4
QK_DIM = 64
V_DIM = 128
ROPE_DIM = 16
ROPE_THETA = 500000.0
HYENA_WIDTH = 512
HYENA_ORDER = 2
FILTER_BANDS = 16
DECAY_TARGET = 1e-2
FAST_DECAY_PCT = 0.3
SLOW_DECAY_PCT = 1.5
N_EXPERTS = 16
CAPACITY_FACTOR = 2
MOE_TILE = 512
EPS = 1e-6
LANES = 128
FILT_GROUPS = LANES // FILTER_BANDS
VMEM_LIMIT = 56 * 1024 * 1024

INPROJ_ROWS, INPROJ_COLS = 2048, 1280
QKPREP_ROWS = 1024
ATTN_Q_ROWS, ATTN_KEY_CHUNK = 1024, 512
MERGE_ROWS = 1024
TOPK_CUMSUM_ROWS = 256
GATHER_WINDOW = 128
COMBINE_WINDOW = 128
FILTER_ROWS = 512


def _cparams(sem):
    return pltpu.CompilerParams(dimension_semantics=sem, vmem_limit_bytes=VMEM_LIMIT)


def _inproj_kernel(x_ref, g_ref, w_ref, o_ref, u_ref):
    j = pl.program_id(1)

    @pl.when(j == 0)
    def _():
        x = x_ref[...]
        ms = jnp.mean(x * x, axis=-1, keepdims=True)
        u = (x * lax.rsqrt(ms + EPS) * g_ref[...]).astype(BF16)
        u_ref[...] = u
        o_ref[...] = jnp.dot(u, w_ref[...], preferred_element_type=F32).astype(o_ref.dtype)

    @pl.when(j > 0)
    def _():
        o_ref[...] = jnp.dot(u_ref[...], w_ref[...], preferred_element_type=F32).astype(o_ref.dtype)


def _inproj(x2, g, w_bf16, tm=INPROJ_ROWS, tn=INPROJ_COLS):
    n, d = x2.shape
    cols = w_bf16.shape[1]
    return pl.pallas_call(
        _inproj_kernel,
        out_shape=jax.ShapeDtypeStruct((n, cols), BF16),
        grid=(n // tm, cols // tn),
        in_specs=[
            pl.BlockSpec((tm, d), lambda i, j: (i, 0)),
            pl.BlockSpec((1, d), lambda i, j: (0, 0)),
            pl.BlockSpec((d, tn), lambda i, j: (0, j)),
        ],
        out_specs=pl.BlockSpec((tm, tn), lambda i, j: (i, j)),
        scratch_shapes=[pltpu.VMEM((tm, d), BF16)],
        compiler_params=_cparams(("arbitrary", "arbitrary")),
        name="inproj",
    )(x2, g, w_bf16)


def _qkprep_kernel(p_ref, c_ref, s_ref, perm_ref, g_ref, o_ref):
    lane = lax.broadcasted_iota(jnp.int32, (1, LANES), 1)
    lo = lane < QK_DIM
    c = c_ref[...]
    s = s_ref[...]
    perm = perm_ref[...]
    for j in range(p_ref.shape[1] // LANES):
        t = p_ref[:, j * LANES:(j + 1) * LANES].astype(F32)
        sq = t * t
        ss_lo = jnp.sum(jnp.where(lo, sq, 0.0), axis=-1, keepdims=True)
        ss_hi = jnp.sum(jnp.where(lo, 0.0, sq), axis=-1, keepdims=True)
        r = lax.rsqrt(jnp.where(lo, ss_lo, ss_hi) * (1.0 / QK_DIM) + EPS)
        y = t * r * g_ref[:, j * LANES:(j + 1) * LANES]
        y_hi = y.astype(BF16)
        y_lo = (y - y_hi.astype(F32)).astype(BF16)
        partner = (jnp.dot(y_hi, perm, preferred_element_type=F32)
                   + jnp.dot(y_lo, perm, preferred_element_type=F32))
        o_ref[:, j * LANES:(j + 1) * LANES] = (y * c + partner * s).astype(o_ref.dtype)


def _qkprep(proj, ctab, stab, perm, gqk, seq, tm=QKPREP_ROWS):
    n = proj.shape[0]
    w = gqk.shape[1]
    nb = seq // tm
    return pl.pallas_call(
        _qkprep_kernel,
        out_shape=jax.ShapeDtypeStruct((n, w), BF16),
        grid=(n // tm,),
        in_specs=[
            pl.BlockSpec((tm, w), lambda i: (i, 0)),
            pl.BlockSpec((tm, LANES), lambda i: (i % nb, 0)),
            pl.BlockSpec((tm, LANES), lambda i: (i % nb, 0)),
            pl.BlockSpec((LANES, LANES), lambda i: (0, 0)),
            pl.BlockSpec((1, w), lambda i: (0, 0)),
        ],
        out_specs=pl.BlockSpec((tm, w), lambda i: (i, 0)),
        compiler_params=_cparams(("arbitrary",)),
        name="qkprep",
    )(proj, ctab, stab, perm, gqk)


def _attn_kernel(q_ref, k_ref, v_ref, lam_ref, g_ref, o_ref, *, tk, unroll, lambda_init):
    tq = q_ref.shape[0]
    seq = k_ref.shape[0]
    lane = lax.broadcasted_iota(jnp.int32, (1, LANES), 1)
    q = q_ref[...]
    zero = jnp.zeros_like(q)
    qs = jnp.concatenate([jnp.where(lane < QK_DIM, q, zero), jnp.where(lane < QK_DIM, zero, q)], axis=0)
    lv = lam_ref[...]
    lam = (jnp.exp(jnp.sum(lv[0:1] * lv[1:2], axis=-1, keepdims=True))
           - jnp.exp(jnp.sum(lv[2:3] * lv[3:4], axis=-1, keepdims=True)) + lambda_init)

    def step(c, carry):
        m, l, a = carry
        off = pl.multiple_of(c * tk, tk)
        kc = k_ref[pl.ds(off, tk), :]
        vc = v_ref[pl.ds(off, tk), :]
        s = lax.dot_general(qs, kc, (((1,), (1,)), ((), ())), preferred_element_type=F32)
        mn = jnp.maximum(m, jnp.max(s, axis=-1, keepdims=True))
        p = jnp.exp2((s - mn).astype(BF16))
        al = jnp.exp2(m - mn)
        pf = p.astype(F32)
        part = pf[:, 0:LANES]
        for j in range(1, tk // LANES):
            part = part + pf[:, j * LANES:(j + 1) * LANES]
        l = al * l + part
        a = al * a + jnp.dot(p, vc, preferred_element_type=F32)
        return mn, l, a

    init = (jnp.full((2 * tq, 1), -jnp.inf, F32), jnp.zeros((2 * tq, LANES), F32), jnp.zeros((2 * tq, V_DIM), F32))
    _, l, a = lax.fori_loop(0, seq // tk, step, init, unroll=unroll)
    a = a / jnp.sum(l, axis=-1, keepdims=True)
    o = a[:tq] - lam * a[tq:]
    ms = jnp.mean(o * o, axis=-1, keepdims=True)
    o = o * lax.rsqrt(ms + EPS) * g_ref[...] * (1.0 - lambda_init)
    o_ref[...] = o.astype(o_ref.dtype)


def _attention(qk, proj, lamv, subln_g, batch, seq, lambda_init, tq=ATTN_Q_ROWS, tk=ATTN_KEY_CHUNK):
    nq = seq // tq
    vcol0 = (2 * ATTN_HEADS * 2 * QK_DIM) // V_DIM
    return pl.pallas_call(
        functools.partial(_attn_kernel, tk=tk, unroll=seq // tk, lambda_init=lambda_init),
        out_shape=jax.ShapeDtypeStruct((batch * seq, ATTN_HEADS * V_DIM), BF16),
        grid=(batch, ATTN_HEADS, nq),
        in_specs=[
            pl.BlockSpec((tq, LANES), lambda b, h, i: (b * nq + i, h)),
            pl.BlockSpec((seq, LANES), lambda b, h, i: (b, ATTN_HEADS + h)),
            pl.BlockSpec((seq, V_DIM), lambda b, h, i: (b, vcol0 + h)),
            pl.BlockSpec((4, QK_DIM), lambda b, h, i: (0, 0)),
            pl.BlockSpec((1, V_DIM), lambda b, h, i: (0, 0)),
        ],
        out_specs=pl.BlockSpec((tq, V_DIM), lambda b, h, i: (b * nq + i, h)),
        compiler_params=_cparams(("arbitrary", "arbitrary", "arbitrary")),
        name="diffattn",
    )(qk, qk, proj, lamv, subln_g)


def _merge_kernel(at_ref, hy_ref, ga_ref, gh_ref, x_ref, wpa_ref, wph_ref, wo_ref, g2_ref, wr_ref,
                  x1_ref, u2_ref, aff_ref):
    tm = x_ref.shape[0]
    parts = 2
    for rows in (slice(k * tm // parts, (k + 1) * tm // parts) for k in range(parts)):
        ga = jax.nn.sigmoid(ga_ref[rows, :].astype(F32))
        gh = jax.nn.sigmoid(gh_ref[rows, :].astype(F32))
        merged = (ga * jnp.dot(at_ref[rows, :], wpa_ref[...], preferred_element_type=F32)
                  + gh * jnp.dot(hy_ref[rows, :], wph_ref[...], preferred_element_type=F32))
        x1 = x_ref[rows, :] + jnp.dot(merged.astype(BF16), wo_ref[...], preferred_element_type=F32)
        x1_ref[rows, :] = x1
        ms = jnp.mean(x1 * x1, axis=-1, keepdims=True)
        u2 = (x1 * lax.rsqrt(ms + EPS) * g2_ref[...]).astype(BF16)
        u2_ref[rows, :] = u2
        logits = jnp.dot(u2, wr_ref[...], preferred_element_type=F32)
        lane = lax.broadcasted_iota(jnp.int32, logits.shape, 1)
        logits = jnp.where(lane < N_EXPERTS, logits, -jnp.inf)
        e = jnp.exp(logits - jnp.max(logits, axis=-1, keepdims=True))
        aff_ref[rows, :] = e / jnp.sum(e, axis=-1, keepdims=True)


def _merge(attn, hyena, proj, x2, wpa, wph, wo, g2, wr_pad, tm=MERGE_ROWS):
    n, d = x2.shape
    gcol0 = (proj.shape[1] - 2 * d) // d
    full = lambda shape: pl.BlockSpec(shape, lambda i: (0, 0))
    return pl.pallas_call(
        _merge_kernel,
        out_shape=(jax.ShapeDtypeStruct((n, d), F32), jax.ShapeDtypeStruct((n, d), BF16),
                   jax.ShapeDtypeStruct((n, LANES), F32)),
        grid=(n // tm,),
        in_specs=[
            pl.BlockSpec((tm, attn.shape[1]), lambda i: (i, 0)),
            pl.BlockSpec((tm, hyena.shape[1]), lambda i: (i, 0)),
            pl.BlockSpec((tm, d), lambda i: (i, gcol0)),
            pl.BlockSpec((tm, d), lambda i: (i, gcol0 + 1)),
            pl.BlockSpec((tm, d), lambda i: (i, 0)),
            full(wpa.shape), full(wph.shape), full(wo.shape), full(g2.shape), full(wr_pad.shape),
        ],
        out_specs=(pl.BlockSpec((tm, d), lambda i: (i, 0)), pl.BlockSpec((tm, d), lambda i: (i, 0)),
                   pl.BlockSpec((tm, LANES), lambda i: (i, 0))),
        compiler_params=_cparams(("arbitrary",)),
        name="merge",
    )(attn, hyena, proj, proj, x2, wpa, wph, wo, g2, wr_pad)


def _excl_cumsum_rows(mask_f32, tri, blk):
    rows = mask_f32.shape[0]
    carry = jnp.zeros((1, LANES), F32)
    outs = []
    for r in range(rows // blk):
        mb = mask_f32[r * blk:(r + 1) * blk]
        outs.append(jnp.dot(tri, mb.astype(BF16), preferred_element_type=F32) + carry)
        carry = carry + jnp.sum(mb, axis=0, keepdims=True)
    return jnp.concatenate(outs, axis=0)


def _topk_kernel(aff_ref, pos_ref, post_ref, afft_ref, offs_ref, *, cap, blk, batch):
    seq = aff_ref.shape[0] // batch
    aff = aff_ref[0:seq]
    for b in range(1, batch):
        aff = aff + pltpu.roll(aff_ref[b * seq:(b + 1) * seq], b * N_EXPERTS, 1)

    def search(i, prefix):
        cand = prefix | jnp.left_shift(jnp.int32(1), 29 - i)
        cnt = jnp.sum((aff >= pltpu.bitcast(cand, F32)[0:1]).astype(F32), axis=0, keepdims=True)
        return jnp.where(cnt >= cap, cand, prefix)

    lo = pltpu.bitcast(lax.fori_loop(0, 30, search, jnp.zeros((8, LANES), jnp.int32)), F32)[0:1]
    thr = jnp.min(jnp.where(aff >= lo, aff, jnp.inf), axis=0, keepdims=True)
    gt = (aff > thr).astype(F32)
    eq = (aff == thr).astype(F32)
    need = cap - jnp.sum(gt, axis=0, keepdims=True)
    ri = lax.broadcasted_iota(jnp.int32, (blk, blk), 0)
    ci = lax.broadcasted_iota(jnp.int32, (blk, blk), 1)
    tri = (ci < ri).astype(BF16)
    sel = gt + eq * (_excl_cumsum_rows(eq, tri, blk) < need).astype(F32)
    before = _excl_cumsum_rows(sel, tri, blk)
    pos = jnp.where(sel > 0.0, before, -1.0)
    pos_t = pos.T
    aff_t = aff.T
    lane = lax.broadcasted_iota(jnp.int32, (1, LANES), 1)
    for b in range(batch):
        mine = pos if b == 0 else pltpu.roll(pos, LANES - b * N_EXPERTS, 1)
        pos_ref[b * seq:(b + 1) * seq] = jnp.where(lane < N_EXPERTS, mine, -1.0)
        post_ref[b] = pos_t[b * N_EXPERTS:(b + 1) * N_EXPERTS]
        afft_ref[b] = aff_t[b * N_EXPERTS:(b + 1) * N_EXPERTS]
    offs_ref[...] = jnp.concatenate([before[j * MOE_TILE:j * MOE_TILE + 1] for j in range(offs_ref.shape[0])], axis=0)


def _topk(aff, batch, seq, cap, blk=TOPK_CUMSUM_ROWS):
    assert batch * N_EXPERTS <= LANES
    nt = seq // MOE_TILE
    pos, post, afft, offs = pl.pallas_call(
        functools.partial(_topk_kernel, cap=cap, blk=blk, batch=batch),
        out_shape=(jax.ShapeDtypeStruct((batch * seq, LANES), F32),
                   jax.ShapeDtypeStruct((batch, N_EXPERTS, seq), F32),
                   jax.ShapeDtypeStruct((batch, N_EXPERTS, seq), F32),
                   jax.ShapeDtypeStruct((nt, LANES), F32)),
        compiler_params=pltpu.CompilerParams(vmem_limit_bytes=VMEM_LIMIT),
        name="topk",
    )(aff)
    offs = offs[:, :batch * N_EXPERTS].reshape(nt, batch, N_EXPERTS).transpose(1, 0, 2)
    return pos, post, afft, offs.astype(jnp.int32).reshape(-1)


def _align_down(x, m):
    sh = m.bit_length() - 1
    return lax.shift_left(lax.shift_right_logical(x, sh), sh)


def _ceil_div(x, m):
    return lax.shift_right_logical(x + (m - 1), m.bit_length() - 1)


def _gather_kernel(offs_ref, post_ref, u_ref, x_ref, *, cap, win):
    b = pl.program_id(0)
    c = pl.program_id(1)
    nt = pl.num_programs(1)

    @pl.when(c == 0)
    def _():
        x_ref[...] = jnp.zeros_like(x_ref)

    uc = u_ref[...]
    slot0 = lax.broadcasted_iota(jnp.int32, (win, MOE_TILE), 0).astype(F32)
    base = (b * nt + c) * N_EXPERTS
    last = pl.num_programs(0) * nt * N_EXPERTS - 1
    starts, hots = [], []
    for e in range(N_EXPERTS):
        s = pl.multiple_of(jnp.minimum(_align_down(offs_ref[base + e], 16), cap - win), 16)
        hots.append(jnp.where((slot0 + s.astype(F32)) == post_ref[0, e:e + 1, :], 1.0, 0.0).astype(BF16))
        starts.append(s)
    rows = jnp.dot(jnp.concatenate(hots, axis=0), uc, preferred_element_type=F32)
    for e in range(N_EXPERTS):
        x_ref[0, e, pl.ds(starts[e], win), :] += rows[e * win:(e + 1) * win].astype(BF16)
    for e in range(N_EXPERTS):
        lo0 = _align_down(offs_ref[base + e], 16)
        end = jnp.where(c + 1 < nt, offs_ref[jnp.minimum(base + N_EXPERTS + e, last)], cap)

        def more(k, carry, lo0=lo0, e=e):
            lo = lo0 + k * win
            s = pl.multiple_of(jnp.minimum(lo, cap - win), 16)
            srow = slot0 + s.astype(F32)
            hit = jnp.logical_and(srow == post_ref[0, e:e + 1, :], srow >= lo.astype(F32))
            x_ref[0, e, pl.ds(s, win), :] += jnp.dot(jnp.where(hit, 1.0, 0.0).astype(BF16), uc,
                                                     preferred_element_type=F32).astype(BF16)
            return carry

        lax.fori_loop(1, _ceil_div(end - lo0, win), more, 0)


def _gather(offs, post, u2, batch, seq, cap, win=GATHER_WINDOW):
    d = u2.shape[1]
    nt = seq // MOE_TILE
    grid_spec = pltpu.PrefetchScalarGridSpec(
        num_scalar_prefetch=1,
        grid=(batch, nt),
        in_specs=[
            pl.BlockSpec((1, N_EXPERTS, MOE_TILE), lambda b, c, o: (b, 0, c)),
            pl.BlockSpec((MOE_TILE, d), lambda b, c, o: (b * nt + c, 0)),
        ],
        out_specs=pl.BlockSpec((1, N_EXPERTS, cap, d), lambda b, c, o: (b, 0, 0, 0)),
    )
    return pl.pallas_call(
        functools.partial(_gather_kernel, cap=cap, win=win),
        out_shape=jax.ShapeDtypeStruct((batch, N_EXPERTS, cap, d), BF16),
        grid_spec=grid_spec,
        compiler_params=_cparams(("arbitrary", "arbitrary")),
        name="gather",
    )(offs, post, u2)


def _expert_kernel(offs_ref, post_ref, afft_ref, x_ref, wg_ref, wu_ref, wd_ref, o_ref, gs_ref, w_ref, eo_ref,
                   *, cap, win):
    e = pl.program_id(0)
    b = pl.program_id(1)

    def cast():
        for m, src in enumerate((wg_ref, wu_ref, wd_ref)):
            part = src.shape[2]
            w_ref[jnp.bitwise_and(e, 1), m, pl.ds(pl.multiple_of(b * part, part), part), :] = src[0, 0].astype(BF16)

    def run():
        _expert_body(offs_ref, post_ref, afft_ref, x_ref, w_ref.at[1 - jnp.bitwise_and(e, 1)], o_ref, gs_ref, eo_ref,
                     e - 1, b, cap, win)

    @pl.when(e == 0)
    def _():
        cast()
        o_ref[...] = jnp.zeros_like(o_ref)

    @pl.when(jnp.logical_and(e > 0, e < N_EXPERTS))
    def _():
        cast()
        run()

    @pl.when(e == N_EXPERTS)
    def _():
        run()


def _expert_body(offs_ref, post_ref, afft_ref, x_ref, w_ref, o_ref, gs_ref, eo_ref, e, b, cap, win):
    seq = post_ref.shape[2]
    nt = seq // MOE_TILE
    gs_ref[...] = jnp.zeros_like(gs_ref)
    slot0 = lax.broadcasted_iota(jnp.int32, (win, MOE_TILE), 0).astype(F32)
    windows = []
    for c in range(nt):
        base = (b * nt + c) * N_EXPERTS + e
        lo0 = _align_down(offs_ref[base], 8)
        end = offs_ref[base + N_EXPERTS] if c + 1 < nt else cap
        prow = post_ref[0, pl.ds(e, 1), c * MOE_TILE:(c + 1) * MOE_TILE]
        arow = afft_ref[0, pl.ds(e, 1), c * MOE_TILE:(c + 1) * MOE_TILE]

        def window(k, carry, lo0=lo0, prow=prow, arow=arow):
            lo = lo0 + k * win
            s = pl.multiple_of(jnp.minimum(lo, cap - win), 8)
            srow = slot0 + s.astype(F32)
            hit = jnp.logical_and(srow == prow, srow >= lo.astype(F32))
            gs_ref[pl.ds(s, win), :] += jnp.sum(jnp.where(hit, arow, 0.0), axis=-1, keepdims=True)
            return carry

        window(0, 0)
        windows.append((lo0, end, window))
    xb = x_ref[0, 0]
    hg = jnp.dot(xb, w_ref[0], preferred_element_type=F32)
    hu = jnp.dot(xb, w_ref[1], preferred_element_type=F32)
    act = (hg * jax.nn.sigmoid(hg) * hu).astype(BF16)
    eo_ref[...] = jnp.dot(act, w_ref[2], preferred_element_type=F32)
    for lo0, end, window in windows:
        lax.fori_loop(1, _ceil_div(end - lo0, win), window, 0)
    o_ref[0, 0] = (eo_ref[...] * gs_ref[...]).astype(o_ref.dtype)


def _experts(offs, post, afft, xin, wg, wu, wd, layer, batch, seq, cap, win=GATHER_WINDOW):
    d = xin.shape[3]
    assert wg.shape[2:] == (d, d) and wd.shape[2:] == (d, d) and d % (16 * batch) == 0
    part = d // batch
    wspec = pl.BlockSpec((1, 1, part, d), lambda e, b, o: (layer, jnp.minimum(e, N_EXPERTS - 1), b, 0))
    grid_spec = pltpu.PrefetchScalarGridSpec(
        num_scalar_prefetch=1,
        grid=(N_EXPERTS + 1, batch),
        in_specs=[
            pl.BlockSpec((1, N_EXPERTS, seq), lambda e, b, o: (b, 0, 0)),
            pl.BlockSpec((1, N_EXPERTS, seq), lambda e, b, o: (b, 0, 0)),
            pl.BlockSpec((1, 1, cap, d), lambda e, b, o: (b, jnp.maximum(e - 1, 0), 0, 0)),
            wspec, wspec, wspec,
        ],
        out_specs=pl.BlockSpec((1, 1, cap, d), lambda e, b, o: (b, jnp.where(e == 0, N_EXPERTS, e - 1), 0, 0)),
        scratch_shapes=[pltpu.VMEM((cap, 1), F32), pltpu.VMEM((2, 3, d, d), BF16), pltpu.VMEM((cap, d), F32)],
    )
    return pl.pallas_call(
        functools.partial(_expert_kernel, cap=cap, win=win),
        out_shape=jax.ShapeDtypeStruct((batch, N_EXPERTS + 1, cap, d), BF16),
        grid_spec=grid_spec,
        compiler_params=_cparams(("arbitrary", "arbitrary")),
        name="experts",
    )(offs, post, afft, xin, wg, wu, wd)


def _combine_kernel(offs_ref, pos_ref, x1_ref, weo_ref, o_ref, *, cap, win):
    b = pl.program_id(0)
    t = pl.program_id(1)
    nt = pl.num_programs(1)
    tt = pos_ref.shape[0]
    pos = pos_ref[...]
    slot0 = lax.broadcasted_iota(jnp.int32, (tt, win), 1).astype(F32)
    base = (b * nt + t) * N_EXPERTS
    last = pl.num_programs(0) * nt * N_EXPERTS - 1
    acc = x1_ref[...]
    for e0 in range(0, N_EXPERTS, 2):
        hots, rows = [], []
        for e in (e0, e0 + 1):
            s = pl.multiple_of(jnp.minimum(_align_down(offs_ref[base + e], 16), cap - win), 16)
            hots.append(jnp.where((slot0 + s.astype(F32)) == pos[:, e:e + 1], 1.0, 0.0).astype(BF16))
            rows.append(weo_ref[0, e, pl.ds(s, win), :])
        acc = acc + jnp.dot(jnp.concatenate(hots, axis=1), jnp.concatenate(rows, axis=0),
                            preferred_element_type=F32)
    o_ref[...] = acc
    for e in range(N_EXPERTS):
        lo0 = _align_down(offs_ref[base + e], 16)
        end = jnp.where(t + 1 < nt, offs_ref[jnp.minimum(base + N_EXPERTS + e, last)], cap)

        def window(k, carry, lo0=lo0, e=e):
            lo = lo0 + k * win
            s = pl.multiple_of(jnp.minimum(lo, cap - win), 16)
            srow = slot0 + s.astype(F32)
            hit = jnp.logical_and(srow == pos[:, e:e + 1], srow >= lo.astype(F32))
            o_ref[...] += jnp.dot(jnp.where(hit, 1.0, 0.0).astype(BF16), weo_ref[0, e, pl.ds(s, win), :],
                                  preferred_element_type=F32)
            return carry

        lax.fori_loop(1, _ceil_div(end - lo0, win), window, 0)


def _combine(offs, pos, x1, weo, batch, seq, cap, win=COMBINE_WINDOW):
    d = x1.shape[1]
    tt = MOE_TILE
    nt = seq // tt
    grid_spec = pltpu.PrefetchScalarGridSpec(
        num_scalar_prefetch=1,
        grid=(batch, nt),
        in_specs=[
            pl.BlockSpec((tt, LANES), lambda b, t, o: (b * nt + t, 0)),
            pl.BlockSpec((tt, d), lambda b, t, o: (b * nt + t, 0)),
            pl.BlockSpec((1, N_EXPERTS, cap, d), lambda b, t, o: (b, 0, 0, 0)),
        ],
        out_specs=pl.BlockSpec((tt, d), lambda b, t, o: (b * nt + t, 0)),
    )
    return pl.pallas_call(
        functools.partial(_combine_kernel, cap=cap, win=win),
        out_shape=jax.ShapeDtypeStruct(x1.shape, F32),
        grid_spec=grid_spec,
        compiler_params=_cparams(("arbitrary", "arbitrary")),
        name="combine",
    )(offs, pos, x1, weo)


FFT_R = 64
FFT_PITCH = FFT_R + 8


def _dft_tables(seq):
    r = FFT_R
    assert seq == r * r
    n = np.arange(r)
    f = np.exp(-2j * np.pi * np.outer(n, n) / r)
    w = lambda e: np.exp(-2j * np.pi * e / (2 * seq))

    def real_rep(a):
        return np.block([[a.real, -a.imag], [a.imag, a.real]]).astype(np.float32)

    m1 = [f, f * w(r * n)[None, :]]
    m2 = [f * w(n * m)[None, :] for m in range(2 * r)]
    m4 = [f.conj() / (2 * seq), w(-r * n)[:, None] * f.conj() / (2 * seq)]
    t1 = np.stack([real_rep(a) for a in m1])
    t2 = np.stack([real_rep(a) for a in m2])
    t3 = np.stack([real_rep(a.conj().T) for a in m2])
    t4 = np.stack([real_rep(a) for a in m4])
    t1r = np.stack([np.concatenate([a.real, a.imag], axis=0).astype(np.float32) for a in m1])
    return tuple(jnp.asarray(t).astype(BF16) for t in (t1, t2, t3, t4, t1r))


def _slab(r):
    return pl.ds(pl.multiple_of(r * FFT_PITCH, 8), FFT_R)


def _across(r):
    return pl.ds(r, FFT_R, stride=FFT_PITCH)


def _cat_bf16(a, b):
    return jnp.concatenate([a, b], axis=0).astype(BF16)


FFT_UNROLL = 32


def _loop(body, step=1, rows_per_trip=FFT_UNROLL):
    def wrapped(i, c):
        body(i * step)
        return c
    lax.fori_loop(0, FFT_R // step, wrapped, 0, unroll=rows_per_trip // step)


def _pair_bf16(re_ref, im_ref, idx0, idx1):
    return jnp.concatenate([_cat_bf16(re_ref[idx0, :], im_ref[idx0, :]),
                            _cat_bf16(re_ref[idx1, :], im_ref[idx1, :])], axis=1)


def _bf16_pieces(w):
    hi = w.astype(BF16)
    return hi, (w - hi.astype(F32)).astype(BF16)


def _dot3(a, w_ref):
    a_hi, a_lo = _bf16_pieces(a)
    return (jnp.dot(a_hi, w_ref[0], preferred_element_type=F32)
            + (jnp.dot(a_hi, w_ref[1], preferred_element_type=F32)
               + jnp.dot(a_lo, w_ref[0], preferred_element_type=F32)))


def _filter_kernel(bands_ref, w1t_ref, w1c_ref, w1s_ref, b1_ref, w2_ref, b2_ref, w3_ref, b3_ref, fr_ref,
                   wf_ref, wb_ref, dl_ref, hs_ref, hd_ref, *, seq):
    tm = hs_ref.shape[0]
    rows = tm // FILT_GROUPS
    hid = wf_ref.shape[1]
    base = pl.program_id(0) * tm
    lane = lax.broadcasted_iota(jnp.int32, (rows, LANES), 1)
    grp = lax.shift_right_logical(lane, FILTER_BANDS.bit_length() - 1)
    pos = (base + grp * rows + lax.broadcasted_iota(jnp.int32, (rows, LANES), 0)).astype(F32)
    first = jnp.bitwise_and(lane, FILTER_BANDS - 1) == 0
    fr = fr_ref[...]
    tscale = 1.0 / max(seq - 1, 1)

    def mlp(p):
        ang = ((2.0 * math.pi / seq) * p) * bands_ref[...]
        pre = (_dot3(jnp.cos(ang), w1c_ref) + _dot3(-jnp.sin(ang), w1s_ref)
               + _dot3(jnp.where(first, p * tscale, 0.0), w1t_ref) + b1_ref[...])
        h = jnp.sin(fr * pre)
        h = jnp.sin(fr * (_dot3(h, w2_ref) + b2_ref[...]))
        return jnp.sin(fr * (_dot3(h, w3_ref) + b3_ref[...]))

    hf_h = mlp(pos)
    hb_h = mlp(seq - pos)
    dl = dl_ref[...]
    for g in range(FILT_GROUPS):
        pcol = (base + g * rows + lax.broadcasted_iota(jnp.int32, (rows, 1), 0)).astype(F32)
        hf = _dot3(hf_h[:, g * hid:(g + 1) * hid], wf_ref) * jnp.exp(-(pcol * tscale) * dl)
        hb = _dot3(hb_h[:, g * hid:(g + 1) * hid], wb_ref) * jnp.exp(-((seq - pcol) * tscale) * dl)
        hb = jnp.where(pcol > 0.0, hb, 0.0)
        hs_ref[g * rows:(g + 1) * rows, :] = (hf + hb).astype(hs_ref.dtype)
        hd_ref[g * rows:(g + 1) * rows, :] = (hf - hb).astype(hd_ref.dtype)


def _filters(bands, w1t, w1c, w1s, b1, w2, b2, w3, b3, fr, wf, wb, dl, seq, tm=FILTER_ROWS):
    cols = wf.shape[1]
    pieces = lambda w: jnp.stack(_bf16_pieces(w))
    args = (bands, pieces(w1t), pieces(w1c), pieces(w1s), b1, pieces(w2), b2, pieces(w3), b3, fr,
            pieces(wf), pieces(wb), dl)
    return pl.pallas_call(
        functools.partial(_filter_kernel, seq=seq),
        out_shape=(jax.ShapeDtypeStruct((seq, cols), BF16), jax.ShapeDtypeStruct((seq, cols), BF16)),
        grid=(seq // tm,),
        in_specs=[pl.BlockSpec(a.shape, lambda i, nd=a.ndim: (0,) * nd) for a in args],
        out_specs=(pl.BlockSpec((tm, cols), lambda i: (i, 0)), pl.BlockSpec((tm, cols), lambda i: (i, 0))),
        compiler_params=_cparams(("arbitrary",)),
        name="hyfilter",
    )(*args)


def _spectra_kernel(hs_ref, hd_ref, t1r_ref, t2_ref, h_ref, xe_ref, xo_ref, per_ref, pei_ref, por_ref, poi_ref):
    xx = (xe_ref, xo_ref)
    pp = ((per_ref, pei_ref), (por_ref, poi_ref))

    def fill(r):
        rows = pl.ds(pl.multiple_of(r * FFT_R, FFT_R), FFT_R)
        xe_ref[_slab(r), :] = hs_ref[rows, :].astype(F32)
        xo_ref[_slab(r), :] = hd_ref[rows, :].astype(F32)
    _loop(fill)

    def first(r):
        for par in (0, 1):
            o = jnp.dot(t1r_ref[par], xx[par][_across(r), :].astype(BF16), preferred_element_type=F32)
            pp[par][0][_slab(r), :] = o[:FFT_R]
            pp[par][1][_slab(r), :] = o[FFT_R:]
    _loop(first)

    def second(r):
        rows = pl.ds(pl.multiple_of(r * FFT_R, FFT_R), FFT_R)
        for par in (0, 1):
            o = jnp.dot(t2_ref[2 * r + par], _cat_bf16(pp[par][0][_across(r), :], pp[par][1][_across(r), :]),
                        preferred_element_type=F32)
            h_ref[0, 2 * par, rows, :] = o[:FFT_R].astype(h_ref.dtype)
            h_ref[0, 2 * par + 1, rows, :] = o[FFT_R:].astype(h_ref.dtype)
    _loop(second)


def _spectra(hs, hd, t1r, t2, seq):
    ncol = hs.shape[1] // LANES
    nct = ncol // HYENA_ORDER
    prow = FFT_R * FFT_PITCH
    const = lambda a: pl.BlockSpec(a.shape, lambda o, c: (0,) * a.ndim, pipeline_mode=pl.Buffered(1))
    return pl.pallas_call(
        _spectra_kernel,
        out_shape=jax.ShapeDtypeStruct((HYENA_ORDER, 4, seq, nct * LANES), BF16),
        grid=(HYENA_ORDER, nct),
        in_specs=[pl.BlockSpec((seq, LANES), lambda o, c: (0, o * nct + c)),
                  pl.BlockSpec((seq, LANES), lambda o, c: (0, o * nct + c)),
                  const(t1r), const(t2)],
        out_specs=pl.BlockSpec((1, 4, seq, LANES), lambda o, c: (o, 0, 0, c)),
        scratch_shapes=[pltpu.VMEM((prow, LANES), F32)] * 6,
        compiler_params=_cparams(("arbitrary", "arbitrary")),
        name="hyspectra",
    )(hs, hd, t1r, t2)


def _short_conv(x, w_ref, b_ref):
    rows = x.shape[0]
    sub = 8
    ri = lax.broadcasted_iota(jnp.int32, (sub, x.shape[1]), 0)
    prev = pltpu.roll(x, 1, 0)
    prev = jnp.concatenate([jnp.where(ri == 0, 0.0, prev[:sub]), prev[sub:]], axis=0)
    nxt = pltpu.roll(x, rows - 1, 0)
    nxt = jnp.concatenate([nxt[:rows - sub], jnp.where(ri == sub - 1, 0.0, nxt[rows - sub:])], axis=0)
    return prev * w_ref[0:1, :] + x * w_ref[1:2, :] + nxt * w_ref[2:3, :] + b_ref[...]


def _hyconv_kernel(z_ref, g_ref, cwz_ref, cbz_ref, cwg_ref, cbg_ref, skip_ref, h_ref,
                   t1_ref, t2_ref, t3_ref, t4_ref, o_ref,
                   xr_ref, xi_ref, per_ref, pei_ref, por_ref, poi_ref, qer_ref, qei_ref, qor_ref, qoi_ref,
                   *, conv_z):
    seq = z_ref.shape[0] // 2
    for half, dst in ((0, xr_ref), (1, xi_ref)):
        z = z_ref[half * seq:(half + 1) * seq, :].astype(F32)
        if conv_z:
            z = _short_conv(z, cwz_ref, cbz_ref)
        for j in range(FFT_R):
            dst[j * FFT_PITCH:j * FFT_PITCH + FFT_R, :] = z[j * FFT_R:(j + 1) * FFT_R]

    pp = ((per_ref, pei_ref), (por_ref, poi_ref))
    qq = ((qer_ref, qei_ref), (qor_ref, qoi_ref))
    yr_ref, yi_ref = qer_ref, qei_ref
    t1 = jnp.concatenate([t1_ref[0], t1_ref[1]], axis=0)
    t4 = jnp.concatenate([t4_ref[0], t4_ref[1]], axis=1)
    halves = (slice(0, LANES), slice(LANES, 2 * LANES))

    def first(r):
        o = jnp.dot(t1, _pair_bf16(xr_ref, xi_ref, _across(r), _across(r + 1)), preferred_element_type=F32)
        for par in (0, 1):
            for k, cols in zip((r, r + 1), halves):
                pp[par][0][_slab(k), :] = o[2 * par * FFT_R:(2 * par + 1) * FFT_R, cols]
                pp[par][1][_slab(k), :] = o[(2 * par + 1) * FFT_R:(2 * par + 2) * FFT_R, cols]
    _loop(first, step=2)

    def second(r):
        rows = pl.ds(pl.multiple_of(r * FFT_R, FFT_R), FFT_R)
        for par in (0, 1):
            o = jnp.dot(t2_ref[2 * r + par], _cat_bf16(pp[par][0][_across(r), :], pp[par][1][_across(r), :]),
                        preferred_element_type=F32)
            hr = h_ref[0, 2 * par, rows, :].astype(F32)
            hi = h_ref[0, 2 * par + 1, rows, :].astype(F32)
            ar, ai = o[:FFT_R], o[FFT_R:]
            qq[par][0][_slab(r), :] = ar * hr - ai * hi
            qq[par][1][_slab(r), :] = ar * hi + ai * hr
    _loop(second)

    def third(r):
        for par in (0, 1):
            o = jnp.dot(t3_ref[2 * r + par], _cat_bf16(qq[par][0][_slab(r), :], qq[par][1][_slab(r), :]),
                        preferred_element_type=F32)
            pp[par][0][_across(r), :] = o[:FFT_R]
            pp[par][1][_across(r), :] = o[FFT_R:]
    _loop(third)

    def fourth(r):
        quad = lambda k: jnp.concatenate([pp[0][0][_slab(k), :], pp[0][1][_slab(k), :],
                                          pp[1][0][_slab(k), :], pp[1][1][_slab(k), :]], axis=0).astype(BF16)
        o = jnp.dot(t4, jnp.concatenate([quad(r), quad(r + 1)], axis=1), preferred_element_type=F32)
        for k, cols in zip((r, r + 1), halves):
            yr_ref[_across(k), :] = o[:FFT_R, cols]
            yi_ref[_across(k), :] = o[FFT_R:, cols]
    _loop(fourth, step=2)

    skip = skip_ref[...]
    for half, (y_ref, x_ref) in enumerate(((yr_ref, xr_ref), (yi_ref, xi_ref))):
        g = _short_conv(g_ref[half * seq:(half + 1) * seq, :].astype(F32), cwg_ref, cbg_ref)
        for j in range(FFT_R):
            src = slice(j * FFT_PITCH, j * FFT_PITCH + FFT_R)
            dst = slice(half * seq + j * FFT_R, half * seq + (j + 1) * FFT_R)
            o_ref[dst, :] = (g[j * FFT_R:(j + 1) * FFT_R] * (y_ref[src, :] + skip * x_ref[src, :])).astype(o_ref.dtype)


def _hyconv(zsrc, zcol0, gcol0, proj, conv_w, conv_b, zpart, gpart, skip, spec, order, tabs, batch, seq, conv_z):
    t1, t2, t3, t4 = tabs
    nct = HYENA_WIDTH // LANES
    prow = FFT_R * FFT_PITCH
    const = lambda a: pl.BlockSpec(a.shape, lambda c, p: (0,) * a.ndim, pipeline_mode=pl.Buffered(1))
    return pl.pallas_call(
        functools.partial(_hyconv_kernel, conv_z=conv_z),
        out_shape=jax.ShapeDtypeStruct((batch * seq, HYENA_WIDTH), BF16),
        grid=(nct, batch // 2),
        in_specs=[
            pl.BlockSpec((2 * seq, LANES), lambda c, p: (p, zcol0 + c)),
            pl.BlockSpec((2 * seq, LANES), lambda c, p: (p, gcol0 + c)),
            pl.BlockSpec((3, LANES), lambda c, p: (0, zpart * nct + c)),
            pl.BlockSpec((1, LANES), lambda c, p: (0, zpart * nct + c)),
            pl.BlockSpec((3, LANES), lambda c, p: (0, gpart * nct + c)),
            pl.BlockSpec((1, LANES), lambda c, p: (0, gpart * nct + c)),
            pl.BlockSpec((1, LANES), lambda c, p: (0, c)),
            pl.BlockSpec((1, 4, seq, LANES), lambda c, p: (order, 0, 0, c), pipeline_mode=pl.Buffered(1)),
            const(t1), const(t2), const(t3), const(t4),
        ],
        out_specs=pl.BlockSpec((2 * seq, LANES), lambda c, p: (p, c)),
        scratch_shapes=[pltpu.VMEM((prow, LANES), F32)] * 10,
        compiler_params=_cparams(("arbitrary", "arbitrary")),
        name=f"hyconv{order}",
    )(zsrc, proj, conv_w, conv_b, conv_w, conv_b, skip[order:order + 1], spec, t1, t2, t3, t4)


def _hyena(proj, hycol0, conv_w, conv_b, w1, b1, w2, b2, w3, b3, w_out, freq, skip, batch, seq):
    hid = w2.shape[0]
    eye = jnp.eye(FILT_GROUPS, dtype=F32)
    bdiag = lambda a: jnp.kron(eye, a)
    tile = lambda a: jnp.tile(a[None], (1, FILT_GROUPS))
    bands = tile(jnp.linspace(1e-4, FILTER_BANDS - 1, FILTER_BANDS, dtype=F32))
    min_decay = math.log(DECAY_TARGET) / SLOW_DECAY_PCT
    max_decay = math.log(DECAY_TARGET) / FAST_DECAY_PCT
    deltas = jnp.abs(jnp.linspace(min_decay, max_decay, HYENA_WIDTH, dtype=F32))
    wo4 = w_out.reshape(hid, HYENA_ORDER, 2, HYENA_WIDTH)
    w1t = jnp.pad(w1[0:1], ((0, FILTER_BANDS - 1), (0, 0)))
    hs, hd = _filters(
        bands, bdiag(w1t), bdiag(w1[1:1 + FILTER_BANDS]), bdiag(w1[1 + FILTER_BANDS:]),
        tile(b1), bdiag(w2), tile(b2), bdiag(w3), tile(b3), tile(freq),
        wo4[:, :, 0].reshape(hid, -1), wo4[:, :, 1].reshape(hid, -1),
        jnp.tile(deltas, HYENA_ORDER)[None], seq)
    t1, t2, t3, t4, t1r = _dft_tables(seq)
    spec = _spectra(hs, hd, t1r, t2, seq)
    c0 = hycol0 // LANES
    nct = HYENA_WIDTH // LANES
    cb = conv_b[None]
    z1 = _hyconv(proj, c0, c0 + nct, proj, conv_w, cb, 0, 1, skip, spec, 0, (t1, t2, t3, t4), batch, seq, True)
    return _hyconv(z1, 0, c0 + 2 * nct, proj, conv_w, cb, 0, 2, skip, spec, 1, (t1, t2, t3, t4), batch, seq, False)


def _rope_tables(seq):
    pos = np.arange(seq, dtype=np.float32)
    inv_freq = (ROPE_THETA ** (-np.arange(0, ROPE_DIM, 2, dtype=np.float32) / ROPE_DIM)).astype(np.float32)
    ang = pos[:, None] * inv_freq[None, :]
    cos, sin = np.cos(ang).astype(np.float32), np.sin(ang).astype(np.float32)
    half = ROPE_DIM // 2
    c = np.ones((seq, QK_DIM), np.float32)
    s = np.zeros((seq, QK_DIM), np.float32)
    c[:, :half] = cos
    c[:, half:ROPE_DIM] = cos
    s[:, :half] = -sin
    s[:, half:ROPE_DIM] = sin
    perm = np.zeros((LANES, LANES), np.float32)
    for j in range(LANES):
        if j % QK_DIM < half:
            perm[j + half, j] = 1.0
        elif j % QK_DIM < ROPE_DIM:
            perm[j - half, j] = 1.0
    rep = LANES // QK_DIM
    return (jnp.asarray(np.tile(c, (1, rep))), jnp.asarray(np.tile(s, (1, rep))), jnp.asarray(perm).astype(BF16))


def kernel(x, norm1_g, w_in, short_conv_w, short_conv_b, q_norm_g, k_norm_g, lambda_q1, lambda_k1, lambda_q2, lambda_k2, subln_g, filt_w1, filt_b1, filt_w2, filt_b2, filt_w3, filt_b3, filt_w_out, filt_freq, hyena_skip, w_branch_attn, w_branch_hyena, w_out, norm2_g, w_router, w_gate, w_up, w_down):
    b, l, d = x.shape
    depth = w_in.shape[0]
    n = b * l
    cap = CAPACITY_FACTOR * l // N_EXPERTS
    q_cols = ATTN_HEADS * 2 * QK_DIM
    ctab, stab, perm = _rope_tables(l)
    xc = x.reshape(n, d)
    for li in range(depth):
        lambda_init = 0.8 - 0.6 * math.exp(-0.3 * li)
        proj = _inproj(xc, norm1_g[li][None], w_in[li].astype(BF16))
        scale = math.log2(math.e) / math.sqrt(QK_DIM)
        gqk = jnp.concatenate([jnp.tile(q_norm_g[li], q_cols // QK_DIM) * scale,
                               jnp.tile(k_norm_g[li], q_cols // QK_DIM)])[None]
        qk = _qkprep(proj, ctab, stab, perm, gqk, l)
        lamv = jnp.stack([lambda_q1[li], lambda_k1[li], lambda_q2[li], lambda_k2[li]])
        attn = _attention(qk, proj, lamv, subln_g[li][None], b, l, lambda_init)

        hyena = _hyena(proj, 3 * q_cols, short_conv_w[li], short_conv_b[li], filt_w1[li], filt_b1[li],
                       filt_w2[li], filt_b2[li], filt_w3[li], filt_b3[li], filt_w_out[li], filt_freq[li],
                       hyena_skip[li], b, l)

        wr_pad = jnp.pad(w_router[li], ((0, 0), (0, LANES - N_EXPERTS))).astype(BF16)
        x1, u2, aff = _merge(attn, hyena, proj, xc, w_branch_attn[li].astype(BF16),
                             w_branch_hyena[li].astype(BF16), w_out[li].astype(BF16),
                             norm2_g[li][None], wr_pad)
        pos, post, afft, offs = _topk(aff, b, l, cap)
        xin = _gather(offs, post, u2, b, l, cap)
        weo = _experts(offs, post, afft, xin, w_gate, w_up, w_down, li, b, l, cap)
        xc = _combine(offs, pos, x1, weo, b, l, cap)
    return xc.reshape(b, l, d)
```

```python
import functools
import math

import jax
import jax.numpy as jnp
import numpy as np
from jax import lax
from jax.experimental import pallas as pl
from jax.experimental.pallas import tpu as pltpu

F32 = jnp.float32
BF16 = jnp.bfloat16

ATTN_HEADS = 4
QK_DIM = 64
V_DIM = 128
ROPE_DIM = 16
ROPE_THETA = 500000.0
HYENA_WIDTH = 512
HYENA_ORDER = 2
FILTER_BANDS = 16
DECAY_TARGET = 1e-2
FAST_DECAY_PCT = 0.3
SLOW_DECAY_PCT = 1.5
N_EXPERTS = 16
CAPACITY_FACTOR = 2
MOE_TILE = 512
EPS = 1e-6
LANES = 128
FILT_GROUPS = LANES // FILTER_BANDS
VMEM_LIMIT = 56 * 1024 * 1024

INPROJ_ROWS, INPROJ_COLS = 2048, 1280
QKPREP_ROWS = 1024
ATTN_Q_ROWS, ATTN_KEY_CHUNK = 1024, 512
MERGE_ROWS = 1024
TOPK_CUMSUM_ROWS = 256
GATHER_WINDOW = 96
AFFINITY_WINDOW = 128
COMBINE_WINDOW = 128
FILTER_ROWS = 512


def _cparams(sem):
    return pltpu.CompilerParams(dimension_semantics=sem, vmem_limit_bytes=VMEM_LIMIT)


def _inproj_kernel(x_ref, g_ref, w_ref, o_ref, u_ref):
    j = pl.program_id(1)

    @pl.when(j == 0)
    def _():
        x = x_ref[...]
        ms = jnp.mean(x * x, axis=-1, keepdims=True)
        u = (x * lax.rsqrt(ms + EPS) * g_ref[...]).astype(BF16)
        u_ref[...] = u
        o_ref[...] = jnp.dot(u, w_ref[...], preferred_element_type=F32).astype(o_ref.dtype)

    @pl.when(j > 0)
    def _():
        o_ref[...] = jnp.dot(u_ref[...], w_ref[...], preferred_element_type=F32).astype(o_ref.dtype)


def _inproj(x2, g, w_bf16, tm=INPROJ_ROWS, tn=INPROJ_COLS):
    n, d = x2.shape
    cols = w_bf16.shape[1]
    return pl.pallas_call(
        _inproj_kernel,
        out_shape=jax.ShapeDtypeStruct((n, cols), BF16),
        grid=(n // tm, cols // tn),
        in_specs=[
            pl.BlockSpec((tm, d), lambda i, j: (i, 0)),
            pl.BlockSpec((1, d), lambda i, j: (0, 0)),
            pl.BlockSpec((d, tn), lambda i, j: (0, j)),
        ],
        out_specs=pl.BlockSpec((tm, tn), lambda i, j: (i, j)),
        scratch_shapes=[pltpu.VMEM((tm, d), BF16)],
        compiler_params=_cparams(("arbitrary", "arbitrary")),
        name="inproj",
    )(x2, g, w_bf16)


def _qkprep_kernel(p_ref, c_ref, s_ref, perm_ref, g_ref, o_ref):
    lane = lax.broadcasted_iota(jnp.int32, (1, LANES), 1)
    lo = lane < QK_DIM
    c = c_ref[...]
    s = s_ref[...]
    perm = perm_ref[...]
    for j in range(p_ref.shape[1] // LANES):
        t = p_ref[:, j * LANES:(j + 1) * LANES].astype(F32)
        sq = t * t
        ss_lo = jnp.sum(jnp.where(lo, sq, 0.0), axis=-1, keepdims=True)
        ss_hi = jnp.sum(jnp.where(lo, 0.0, sq), axis=-1, keepdims=True)
        r = lax.rsqrt(jnp.where(lo, ss_lo, ss_hi) * (1.0 / QK_DIM) + EPS)
        y = t * r * g_ref[:, j * LANES:(j + 1) * LANES]
        y_hi = y.astype(BF16)
        y_lo = (y - y_hi.astype(F32)).astype(BF16)
        partner = (jnp.dot(y_hi, perm, preferred_element_type=F32)
                   + jnp.dot(y_lo, perm, preferred_element_type=F32))
        o_ref[:, j * LANES:(j + 1) * LANES] = (y * c + partner * s).astype(o_ref.dtype)


def _qkprep(proj, ctab, stab, perm, gqk, seq, tm=QKPREP_ROWS):
    n = proj.shape[0]
    w = gqk.shape[1]
    nb = seq // tm
    return pl.pallas_call(
        _qkprep_kernel,
        out_shape=jax.ShapeDtypeStruct((n, w), BF16),
        grid=(n // tm,),
        in_specs=[
            pl.BlockSpec((tm, w), lambda i: (i, 0)),
            pl.BlockSpec((tm, LANES), lambda i: (i % nb, 0)),
            pl.BlockSpec((tm, LANES), lambda i: (i % nb, 0)),
            pl.BlockSpec((LANES, LANES), lambda i: (0, 0)),
            pl.BlockSpec((1, w), lambda i: (0, 0)),
        ],
        out_specs=pl.BlockSpec((tm, w), lambda i: (i, 0)),
        compiler_params=_cparams(("arbitrary",)),
        name="qkprep",
    )(proj, ctab, stab, perm, gqk)


def _attn_kernel(q_ref, k_ref, v_ref, lam_ref, g_ref, o_ref, *, tk, unroll, lambda_init):
    tq = q_ref.shape[0]
    seq = k_ref.shape[0]
    lane = lax.broadcasted_iota(jnp.int32, (1, LANES), 1)
    q = q_ref[...]
    zero = jnp.zeros_like(q)
    qs = jnp.concatenate([jnp.where(lane < QK_DIM, q, zero), jnp.where(lane < QK_DIM, zero, q)], axis=0)
    lv = lam_ref[...]
    lam = (jnp.exp(jnp.sum(lv[0:1] * lv[1:2], axis=-1, keepdims=True))
           - jnp.exp(jnp.sum(lv[2:3] * lv[3:4], axis=-1, keepdims=True)) + lambda_init)

    def step(c, carry):
        m, l, a = carry
        off = pl.multiple_of(c * tk, tk)
        kc = k_ref[pl.ds(off, tk), :]
        vc = v_ref[pl.ds(off, tk), :]
        s = lax.dot_general(qs, kc, (((1,), (1,)), ((), ())), preferred_element_type=F32)
        mn = jnp.maximum(m, jnp.max(s, axis=-1, keepdims=True))
        p = jnp.exp2((s - mn).astype(BF16))
        al = jnp.exp2(m - mn)
        pf = p.astype(F32)
        part = pf[:, 0:LANES]
        for j in range(1, tk // LANES):
            part = part + pf[:, j * LANES:(j + 1) * LANES]
        l = al * l + part
        a = al * a + jnp.dot(p, vc, preferred_element_type=F32)
        return mn, l, a

    init = (jnp.full((2 * tq, 1), -jnp.inf, F32), jnp.zeros((2 * tq, LANES), F32), jnp.zeros((2 * tq, V_DIM), F32))
    _, l, a = lax.fori_loop(0, seq // tk, step, init, unroll=unroll)
    a = a / jnp.sum(l, axis=-1, keepdims=True)
    o = a[:tq] - lam * a[tq:]
    ms = jnp.mean(o * o, axis=-1, keepdims=True)
    o = o * lax.rsqrt(ms + EPS) * g_ref[...] * (1.0 - lambda_init)
    o_ref[...] = o.astype(o_ref.dtype)


def _attention(qk, proj, lamv, subln_g, batch, seq, lambda_init, tq=ATTN_Q_ROWS, tk=ATTN_KEY_CHUNK):
    nq = seq // tq
    vcol0 = (2 * ATTN_HEADS * 2 * QK_DIM) // V_DIM
    return pl.pallas_call(
        functools.partial(_attn_kernel, tk=tk, unroll=seq // tk, lambda_init=lambda_init),
        out_shape=jax.ShapeDtypeStruct((batch * seq, ATTN_HEADS * V_DIM), BF16),
        grid=(batch, ATTN_HEADS, nq),
        in_specs=[
            pl.BlockSpec((tq, LANES), lambda b, h, i: (b * nq + i, h)),
            pl.BlockSpec((seq, LANES), lambda b, h, i: (b, ATTN_HEADS + h)),
            pl.BlockSpec((seq, V_DIM), lambda b, h, i: (b, vcol0 + h)),
            pl.BlockSpec((4, QK_DIM), lambda b, h, i: (0, 0)),
            pl.BlockSpec((1, V_DIM), lambda b, h, i: (0, 0)),
        ],
        out_specs=pl.BlockSpec((tq, V_DIM), lambda b, h, i: (b * nq + i, h)),
        compiler_params=_cparams(("arbitrary", "arbitrary", "arbitrary")),
        name="diffattn",
    )(qk, qk, proj, lamv, subln_g)


def _merge_kernel(at_ref, hy_ref, ga_ref, gh_ref, x_ref, wpa_ref, wph_ref, wo_ref, g2_ref, wr_ref,
                  x1_ref, u2_ref, aff_ref):
    tm = x_ref.shape[0]
    parts = 2
    for rows in (slice(k * tm // parts, (k + 1) * tm // parts) for k in range(parts)):
        ga = jax.nn.sigmoid(ga_ref[rows, :].astype(F32))
        gh = jax.nn.sigmoid(gh_ref[rows, :].astype(F32))
        merged = (ga * jnp.dot(at_ref[rows, :], wpa_ref[...], preferred_element_type=F32)
                  + gh * jnp.dot(hy_ref[rows, :], wph_ref[...], preferred_element_type=F32))
        x1 = x_ref[rows, :] + jnp.dot(merged.astype(BF16), wo_ref[...], preferred_element_type=F32)
        x1_ref[rows, :] = x1
        ms = jnp.mean(x1 * x1, axis=-1, keepdims=True)
        u2 = (x1 * lax.rsqrt(ms + EPS) * g2_ref[...]).astype(BF16)
        u2_ref[rows, :] = u2
        logits = jnp.dot(u2, wr_ref[...], preferred_element_type=F32)
        lane = lax.broadcasted_iota(jnp.int32, logits.shape, 1)
        logits = jnp.where(lane < N_EXPERTS, logits, -jnp.inf)
        e = jnp.exp(logits - jnp.max(logits, axis=-1, keepdims=True))
        aff_ref[rows, :] = e / jnp.sum(e, axis=-1, keepdims=True)


def _merge(attn, hyena, proj, x2, wpa, wph, wo, g2, wr_pad, tm=MERGE_ROWS):
    n, d = x2.shape
    gcol0 = (proj.shape[1] - 2 * d) // d
    full = lambda shape: pl.BlockSpec(shape, lambda i: (0, 0))
    return pl.pallas_call(
        _merge_kernel,
        out_shape=(jax.ShapeDtypeStruct((n, d), F32), jax.ShapeDtypeStruct((n, d), BF16),
                   jax.ShapeDtypeStruct((n, LANES), F32)),
        grid=(n // tm,),
        in_specs=[
            pl.BlockSpec((tm, attn.shape[1]), lambda i: (i, 0)),
            pl.BlockSpec((tm, hyena.shape[1]), lambda i: (i, 0)),
            pl.BlockSpec((tm, d), lambda i: (i, gcol0)),
            pl.BlockSpec((tm, d), lambda i: (i, gcol0 + 1)),
            pl.BlockSpec((tm, d), lambda i: (i, 0)),
            full(wpa.shape), full(wph.shape), full(wo.shape), full(g2.shape), full(wr_pad.shape),
        ],
        out_specs=(pl.BlockSpec((tm, d), lambda i: (i, 0)), pl.BlockSpec((tm, d), lambda i: (i, 0)),
                   pl.BlockSpec((tm, LANES), lambda i: (i, 0))),
        compiler_params=_cparams(("arbitrary",)),
        name="merge",
    )(attn, hyena, proj, proj, x2, wpa, wph, wo, g2, wr_pad)


def _excl_cumsum_rows(mask_f32, tri, blk):
    rows = mask_f32.shape[0]
    carry = jnp.zeros((1, LANES), F32)
    outs = []
    for r in range(rows // blk):
        mb = mask_f32[r * blk:(r + 1) * blk]
        outs.append(jnp.dot(tri, mb.astype(BF16), preferred_element_type=F32) + carry)
        carry = carry + jnp.sum(mb, axis=0, keepdims=True)
    return jnp.concatenate(outs, axis=0)


def _topk_kernel(aff_ref, pos_ref, post_ref, afft_ref, offs_ref, *, cap, blk, batch):
    seq = aff_ref.shape[0] // batch
    aff = aff_ref[0:seq]
    for b in range(1, batch):
        aff = aff + pltpu.roll(aff_ref[b * seq:(b + 1) * seq], b * N_EXPERTS, 1)

    def search(i, prefix):
        cand = prefix | jnp.left_shift(jnp.int32(1), 29 - i)
        cnt = jnp.sum((aff >= pltpu.bitcast(cand, F32)[0:1]).astype(F32), axis=0, keepdims=True)
        return jnp.where(cnt >= cap, cand, prefix)

    lo = pltpu.bitcast(lax.fori_loop(0, 30, search, jnp.zeros((8, LANES), jnp.int32)), F32)[0:1]
    thr = jnp.min(jnp.where(aff >= lo, aff, jnp.inf), axis=0, keepdims=True)
    gt = (aff > thr).astype(F32)
    eq = (aff == thr).astype(F32)
    need = cap - jnp.sum(gt, axis=0, keepdims=True)
    ri = lax.broadcasted_iota(jnp.int32, (blk, blk), 0)
    ci = lax.broadcasted_iota(jnp.int32, (blk, blk), 1)
    tri = (ci < ri).astype(BF16)
    sel = gt + eq * (_excl_cumsum_rows(eq, tri, blk) < need).astype(F32)
    before = _excl_cumsum_rows(sel, tri, blk)
    pos = jnp.where(sel > 0.0, before, -1.0)
    pos_t = pos.T
    aff_t = aff.T
    lane = lax.broadcasted_iota(jnp.int32, (1, LANES), 1)
    for b in range(batch):
        mine = pos if b == 0 else pltpu.roll(pos, LANES - b * N_EXPERTS, 1)
        pos_ref[b * seq:(b + 1) * seq] = jnp.where(lane < N_EXPERTS, mine, -1.0)
        post_ref[b] = pos_t[b * N_EXPERTS:(b + 1) * N_EXPERTS]
        afft_ref[b] = aff_t[b * N_EXPERTS:(b + 1) * N_EXPERTS]
    offs_ref[...] = jnp.concatenate([before[j * MOE_TILE:j * MOE_TILE + 1] for j in range(offs_ref.shape[0])], axis=0)


def _topk(aff, batch, seq, cap, blk=TOPK_CUMSUM_ROWS):
    assert batch * N_EXPERTS <= LANES
    nt = seq // MOE_TILE
    pos, post, afft, offs = pl.pallas_call(
        functools.partial(_topk_kernel, cap=cap, blk=blk, batch=batch),
        out_shape=(jax.ShapeDtypeStruct((batch * seq, LANES), F32),
                   jax.ShapeDtypeStruct((batch, N_EXPERTS, seq), F32),
                   jax.ShapeDtypeStruct((batch, N_EXPERTS, seq), F32),
                   jax.ShapeDtypeStruct((nt, LANES), F32)),
        compiler_params=pltpu.CompilerParams(vmem_limit_bytes=VMEM_LIMIT),
        name="topk",
    )(aff)
    offs = offs[:, :batch * N_EXPERTS].reshape(nt, batch, N_EXPERTS).transpose(1, 0, 2)
    return pos, post, afft, offs.astype(jnp.int32).reshape(-1)


def _align_down(x, m):
    sh = m.bit_length() - 1
    return lax.shift_left(lax.shift_right_logical(x, sh), sh)


def _ceil_div(x, m, x_max):
    if m & (m - 1) == 0:
        return lax.shift_right_logical(x + (m - 1), m.bit_length() - 1)
    n = jnp.int32(0)
    for k in range(-(-x_max // m)):
        n = n + (x > k * m).astype(jnp.int32)
    return n


def _gather_kernel(offs_ref, post_ref, u_ref, x_ref, *, cap, win):
    b = pl.program_id(0)
    c = pl.program_id(1)
    nt = pl.num_programs(1)

    @pl.when(c == 0)
    def _():
        x_ref[...] = jnp.zeros_like(x_ref)

    uc = u_ref[...]
    slot0 = lax.broadcasted_iota(jnp.int32, (win, MOE_TILE), 0).astype(F32)
    base = (b * nt + c) * N_EXPERTS
    last = pl.num_programs(0) * nt * N_EXPERTS - 1
    starts, hots = [], []
    for e in range(N_EXPERTS):
        s = pl.multiple_of(jnp.minimum(_align_down(offs_ref[base + e], 16), cap - win), 16)
        hots.append(jnp.where((slot0 + s.astype(F32)) == post_ref[0, e:e + 1, :], 1.0, 0.0).astype(BF16))
        starts.append(s)
    rows = jnp.dot(jnp.concatenate(hots, axis=0), uc, preferred_element_type=F32)
    for e in range(N_EXPERTS):
        x_ref[0, e, pl.ds(starts[e], win), :] += rows[e * win:(e + 1) * win].astype(BF16)
    for e in range(N_EXPERTS):
        lo0 = _align_down(offs_ref[base + e], 16)
        end = jnp.where(c + 1 < nt, offs_ref[jnp.minimum(base + N_EXPERTS + e, last)], cap)

        def more(k, carry, lo0=lo0, e=e):
            lo = lo0 + k * win
            s = pl.multiple_of(jnp.minimum(lo, cap - win), 16)
            srow = slot0 + s.astype(F32)
            hit = jnp.logical_and(srow == post_ref[0, e:e + 1, :], srow >= lo.astype(F32))
            x_ref[0, e, pl.ds(s, win), :] += jnp.dot(jnp.where(hit, 1.0, 0.0).astype(BF16), uc,
                                                     preferred_element_type=F32).astype(BF16)
            return carry

        lax.fori_loop(1, _ceil_div(end - lo0, win, cap + 16), more, 0)


def _gather(offs, post, u2, batch, seq, cap, win=GATHER_WINDOW):
    d = u2.shape[1]
    nt = seq // MOE_TILE
    grid_spec = pltpu.PrefetchScalarGridSpec(
        num_scalar_prefetch=1,
        grid=(batch, nt),
        in_specs=[
            pl.BlockSpec((1, N_EXPERTS, MOE_TILE), lambda b, c, o: (b, 0, c)),
            pl.BlockSpec((MOE_TILE, d), lambda b, c, o: (b * nt + c, 0)),
        ],
        out_specs=pl.BlockSpec((1, N_EXPERTS, cap, d), lambda b, c, o: (b, 0, 0, 0)),
    )
    return pl.pallas_call(
        functools.partial(_gather_kernel, cap=cap, win=win),
        out_shape=jax.ShapeDtypeStruct((batch, N_EXPERTS, cap, d), BF16),
        grid_spec=grid_spec,
        compiler_params=_cparams(("arbitrary", "arbitrary")),
        name="gather",
    )(offs, post, u2)


def _expert_kernel(offs_ref, post_ref, afft_ref, x_ref, wg_ref, wu_ref, wd_ref, o_ref, gs_ref, w_ref, eo_ref,
                   *, cap, win):
    e = pl.program_id(0)
    b = pl.program_id(1)

    def cast():
        for m, src in enumerate((wg_ref, wu_ref, wd_ref)):
            part = src.shape[2]
            w_ref[jnp.bitwise_and(e, 1), m, pl.ds(pl.multiple_of(b * part, part), part), :] = src[0, 0].astype(BF16)

    def run():
        _expert_body(offs_ref, post_ref, afft_ref, x_ref, w_ref.at[1 - jnp.bitwise_and(e, 1)], o_ref, gs_ref, eo_ref,
                     e - 1, b, cap, win)

    @pl.when(e == 0)
    def _():
        cast()
        o_ref[...] = jnp.zeros_like(o_ref)

    @pl.when(jnp.logical_and(e > 0, e < N_EXPERTS))
    def _():
        cast()
        run()

    @pl.when(e == N_EXPERTS)
    def _():
        run()


def _expert_body(offs_ref, post_ref, afft_ref, x_ref, w_ref, o_ref, gs_ref, eo_ref, e, b, cap, win):
    seq = post_ref.shape[2]
    nt = seq // MOE_TILE
    gs_ref[...] = jnp.zeros_like(gs_ref)
    slot0 = lax.broadcasted_iota(jnp.int32, (win, MOE_TILE), 0).astype(F32)
    windows = []
    for c in range(nt):
        base = (b * nt + c) * N_EXPERTS + e
        lo0 = _align_down(offs_ref[base], 8)
        end = offs_ref[base + N_EXPERTS] if c + 1 < nt else cap
        prow = post_ref[0, pl.ds(e, 1), c * MOE_TILE:(c + 1) * MOE_TILE]
        arow = afft_ref[0, pl.ds(e, 1), c * MOE_TILE:(c + 1) * MOE_TILE]

        def window(k, carry, lo0=lo0, prow=prow, arow=arow):
            lo = lo0 + k * win
            s = pl.multiple_of(jnp.minimum(lo, cap - win), 8)
            srow = slot0 + s.astype(F32)
            hit = jnp.logical_and(srow == prow, srow >= lo.astype(F32))
            gs_ref[pl.ds(s, win), :] += jnp.sum(jnp.where(hit, arow, 0.0), axis=-1, keepdims=True)
            return carry

        window(0, 0)
        windows.append((lo0, end, window))
    xb = x_ref[0, 0]
    hg = jnp.dot(xb, w_ref[0], preferred_element_type=F32)
    hu = jnp.dot(xb, w_ref[1], preferred_element_type=F32)
    act = (hg * jax.nn.sigmoid(hg) * hu).astype(BF16)
    eo_ref[...] = jnp.dot(act, w_ref[2], preferred_element_type=F32)
    for lo0, end, window in windows:
        lax.fori_loop(1, _ceil_div(end - lo0, win, cap + 16), window, 0)
    o_ref[0, 0] = (eo_ref[...] * gs_ref[...]).astype(o_ref.dtype)


def _experts(offs, post, afft, xin, wg, wu, wd, layer, batch, seq, cap, win=AFFINITY_WINDOW):
    d = xin.shape[3]
    assert wg.shape[2:] == (d, d) and wd.shape[2:] == (d, d) and d % (16 * batch) == 0
    part = d // batch
    wspec = pl.BlockSpec((1, 1, part, d), lambda e, b, o: (layer, jnp.minimum(e, N_EXPERTS - 1), b, 0))
    grid_spec = pltpu.PrefetchScalarGridSpec(
        num_scalar_prefetch=1,
        grid=(N_EXPERTS + 1, batch),
        in_specs=[
            pl.BlockSpec((1, N_EXPERTS, seq), lambda e, b, o: (b, 0, 0)),
            pl.BlockSpec((1, N_EXPERTS, seq), lambda e, b, o: (b, 0, 0)),
            pl.BlockSpec((1, 1, cap, d), lambda e, b, o: (b, jnp.maximum(e - 1, 0), 0, 0)),
            wspec, wspec, wspec,
        ],
        out_specs=pl.BlockSpec((1, 1, cap, d), lambda e, b, o: (b, jnp.where(e == 0, N_EXPERTS, e - 1), 0, 0)),
        scratch_shapes=[pltpu.VMEM((cap, 1), F32), pltpu.VMEM((2, 3, d, d), BF16), pltpu.VMEM((cap, d), F32)],
    )
    return pl.pallas_call(
        functools.partial(_expert_kernel, cap=cap, win=win),
        out_shape=jax.ShapeDtypeStruct((batch, N_EXPERTS + 1, cap, d), BF16),
        grid_spec=grid_spec,
        compiler_params=_cparams(("arbitrary", "arbitrary")),
        name="experts",
    )(offs, post, afft, xin, wg, wu, wd)


def _combine_kernel(offs_ref, pos_ref, x1_ref, weo_ref, o_ref, *, cap, win):
    b = pl.program_id(0)
    t = pl.program_id(1)
    nt = pl.num_programs(1)
    tt = pos_ref.shape[0]
    pos = pos_ref[...]
    slot0 = lax.broadcasted_iota(jnp.int32, (tt, win), 1).astype(F32)
    base = (b * nt + t) * N_EXPERTS
    last = pl.num_programs(0) * nt * N_EXPERTS - 1
    acc = x1_ref[...]
    for e0 in range(0, N_EXPERTS, 2):
        hots, rows = [], []
        for e in (e0, e0 + 1):
            s = pl.multiple_of(jnp.minimum(_align_down(offs_ref[base + e], 16), cap - win), 16)
            hots.append(jnp.where((slot0 + s.astype(F32)) == pos[:, e:e + 1], 1.0, 0.0).astype(BF16))
            rows.append(weo_ref[0, e, pl.ds(s, win), :])
        acc = acc + jnp.dot(jnp.concatenate(hots, axis=1), jnp.concatenate(rows, axis=0),
                            preferred_element_type=F32)
    o_ref[...] = acc
    for e in range(N_EXPERTS):
        lo0 = _align_down(offs_ref[base + e], 16)
        end = jnp.where(t + 1 < nt, offs_ref[jnp.minimum(base + N_EXPERTS + e, last)], cap)

        def window(k, carry, lo0=lo0, e=e):
            lo = lo0 + k * win
            s = pl.multiple_of(jnp.minimum(lo, cap - win), 16)
            srow = slot0 + s.astype(F32)
            hit = jnp.logical_and(srow == pos[:, e:e + 1], srow >= lo.astype(F32))
            o_ref[...] += jnp.dot(jnp.where(hit, 1.0, 0.0).astype(BF16), weo_ref[0, e, pl.ds(s, win), :],
                                  preferred_element_type=F32)
            return carry

        lax.fori_loop(1, _ceil_div(end - lo0, win, cap + 16), window, 0)


def _combine(offs, pos, x1, weo, batch, seq, cap, win=COMBINE_WINDOW):
    d = x1.shape[1]
    tt = MOE_TILE
    nt = seq // tt
    grid_spec = pltpu.PrefetchScalarGridSpec(
        num_scalar_prefetch=1,
        grid=(batch, nt),
        in_specs=[
            pl.BlockSpec((tt, LANES), lambda b, t, o: (b * nt + t, 0)),
            pl.BlockSpec((tt, d), lambda b, t, o: (b * nt + t, 0)),
            pl.BlockSpec((1, N_EXPERTS, cap, d), lambda b, t, o: (b, 0, 0, 0)),
        ],
        out_specs=pl.BlockSpec((tt, d), lambda b, t, o: (b * nt + t, 0)),
    )
    return pl.pallas_call(
        functools.partial(_combine_kernel, cap=cap, win=win),
        out_shape=jax.ShapeDtypeStruct(x1.shape, F32),
        grid_spec=grid_spec,
        compiler_params=_cparams(("arbitrary", "arbitrary")),
        name="combine",
    )(offs, pos, x1, weo)


FFT_R = 64
FFT_PITCH = FFT_R + 8


def _dft_tables(seq):
    r = FFT_R
    assert seq == r * r
    n = np.arange(r)
    f = np.exp(-2j * np.pi * np.outer(n, n) / r)
    w = lambda e: np.exp(-2j * np.pi * e / (2 * seq))

    def real_rep(a):
        return np.block([[a.real, -a.imag], [a.imag, a.real]]).astype(np.float32)

    m1 = [f, f * w(r * n)[None, :]]
    m2 = [f * w(n * m)[None, :] for m in range(2 * r)]
    m4 = [f.conj() / (2 * seq), w(-r * n)[:, None] * f.conj() / (2 * seq)]
    t1 = np.stack([real_rep(a) for a in m1])
    t2 = np.stack([real_rep(a) for a in m2])
    t3 = np.stack([real_rep(a.conj().T) for a in m2])
    t4 = np.stack([real_rep(a) for a in m4])
    t1r = np.stack([np.concatenate([a.real, a.imag], axis=0).astype(np.float32) for a in m1])
    return tuple(jnp.asarray(t).astype(BF16) for t in (t1, t2, t3, t4, t1r))


def _slab(r):
    return pl.ds(pl.multiple_of(r * FFT_PITCH, 8), FFT_R)


def _across(r):
    return pl.ds(r, FFT_R, stride=FFT_PITCH)


def _cat_bf16(a, b):
    return jnp.concatenate([a, b], axis=0).astype(BF16)


FFT_UNROLL = 32


def _loop(body, step=1, rows_per_trip=FFT_UNROLL):
    def wrapped(i, c):
        body(i * step)
        return c
    lax.fori_loop(0, FFT_R // step, wrapped, 0, unroll=rows_per_trip // step)


def _pair_bf16(re_ref, im_ref, idx0, idx1):
    return jnp.concatenate([_cat_bf16(re_ref[idx0, :], im_ref[idx0, :]),
                            _cat_bf16(re_ref[idx1, :], im_ref[idx1, :])], axis=1)


def _bf16_pieces(w):
    hi = w.astype(BF16)
    return hi, (w - hi.astype(F32)).astype(BF16)


def _dot3(a, w_ref):
    a_hi, a_lo = _bf16_pieces(a)
    return (jnp.dot(a_hi, w_ref[0], preferred_element_type=F32)
            + (jnp.dot(a_hi, w_ref[1], preferred_element_type=F32)
               + jnp.dot(a_lo, w_ref[0], preferred_element_type=F32)))


def _filter_kernel(bands_ref, w1t_ref, w1c_ref, w1s_ref, b1_ref, w2_ref, b2_ref, w3_ref, b3_ref, fr_ref,
                   wf_ref, wb_ref, dl_ref, hs_ref, hd_ref, *, seq):
    tm = hs_ref.shape[0]
    rows = tm // FILT_GROUPS
    hid = wf_ref.shape[1]
    base = pl.program_id(0) * tm
    lane = lax.broadcasted_iota(jnp.int32, (rows, LANES), 1)
    grp = lax.shift_right_logical(lane, FILTER_BANDS.bit_length() - 1)
    pos = (base + grp * rows + lax.broadcasted_iota(jnp.int32, (rows, LANES), 0)).astype(F32)
    first = jnp.bitwise_and(lane, FILTER_BANDS - 1) == 0
    fr = fr_ref[...]
    tscale = 1.0 / max(seq - 1, 1)

    def mlp(p):
        ang = ((2.0 * math.pi / seq) * p) * bands_ref[...]
        pre = (_dot3(jnp.cos(ang), w1c_ref) + _dot3(-jnp.sin(ang), w1s_ref)
               + _dot3(jnp.where(first, p * tscale, 0.0), w1t_ref) + b1_ref[...])
        h = jnp.sin(fr * pre)
        h = jnp.sin(fr * (_dot3(h, w2_ref) + b2_ref[...]))
        return jnp.sin(fr * (_dot3(h, w3_ref) + b3_ref[...]))

    hf_h = mlp(pos)
    hb_h = mlp(seq - pos)
    dl = dl_ref[...]
    for g in range(FILT_GROUPS):
        pcol = (base + g * rows + lax.broadcasted_iota(jnp.int32, (rows, 1), 0)).astype(F32)
        hf = _dot3(hf_h[:, g * hid:(g + 1) * hid], wf_ref) * jnp.exp(-(pcol * tscale) * dl)
        hb = _dot3(hb_h[:, g * hid:(g + 1) * hid], wb_ref) * jnp.exp(-((seq - pcol) * tscale) * dl)
        hb = jnp.where(pcol > 0.0, hb, 0.0)
        hs_ref[g * rows:(g + 1) * rows, :] = (hf + hb).astype(hs_ref.dtype)
        hd_ref[g * rows:(g + 1) * rows, :] = (hf - hb).astype(hd_ref.dtype)


def _filters(bands, w1t, w1c, w1s, b1, w2, b2, w3, b3, fr, wf, wb, dl, seq, tm=FILTER_ROWS):
    cols = wf.shape[1]
    pieces = lambda w: jnp.stack(_bf16_pieces(w))
    args = (bands, pieces(w1t), pieces(w1c), pieces(w1s), b1, pieces(w2), b2, pieces(w3), b3, fr,
            pieces(wf), pieces(wb), dl)
    return pl.pallas_call(
        functools.partial(_filter_kernel, seq=seq),
        out_shape=(jax.ShapeDtypeStruct((seq, cols), BF16), jax.ShapeDtypeStruct((seq, cols), BF16)),
        grid=(seq // tm,),
        in_specs=[pl.BlockSpec(a.shape, lambda i, nd=a.ndim: (0,) * nd) for a in args],
        out_specs=(pl.BlockSpec((tm, cols), lambda i: (i, 0)), pl.BlockSpec((tm, cols), lambda i: (i, 0))),
        compiler_params=_cparams(("arbitrary",)),
        name="hyfilter",
    )(*args)


def _spectra_kernel(hs_ref, hd_ref, t1r_ref, t2_ref, h_ref, xe_ref, xo_ref, per_ref, pei_ref, por_ref, poi_ref):
    xx = (xe_ref, xo_ref)
    pp = ((per_ref, pei_ref), (por_ref, poi_ref))

    def fill(r):
        rows = pl.ds(pl.multiple_of(r * FFT_R, FFT_R), FFT_R)
        xe_ref[_slab(r), :] = hs_ref[rows, :].astype(F32)
        xo_ref[_slab(r), :] = hd_ref[rows, :].astype(F32)
    _loop(fill)

    def first(r):
        for par in (0, 1):
            o = jnp.dot(t1r_ref[par], xx[par][_across(r), :].astype(BF16), preferred_element_type=F32)
            pp[par][0][_slab(r), :] = o[:FFT_R]
            pp[par][1][_slab(r), :] = o[FFT_R:]
    _loop(first)

    def second(r):
        rows = pl.ds(pl.multiple_of(r * FFT_R, FFT_R), FFT_R)
        for par in (0, 1):
            o = jnp.dot(t2_ref[2 * r + par], _cat_bf16(pp[par][0][_across(r), :], pp[par][1][_across(r), :]),
                        preferred_element_type=F32)
            h_ref[0, 2 * par, rows, :] = o[:FFT_R].astype(h_ref.dtype)
            h_ref[0, 2 * par + 1, rows, :] = o[FFT_R:].astype(h_ref.dtype)
    _loop(second)


def _spectra(hs, hd, t1r, t2, seq):
    ncol = hs.shape[1] // LANES
    nct = ncol // HYENA_ORDER
    prow = FFT_R * FFT_PITCH
    const = lambda a: pl.BlockSpec(a.shape, lambda o, c: (0,) * a.ndim, pipeline_mode=pl.Buffered(1))
    return pl.pallas_call(
        _spectra_kernel,
        out_shape=jax.ShapeDtypeStruct((HYENA_ORDER, 4, seq, nct * LANES), BF16),
        grid=(HYENA_ORDER, nct),
        in_specs=[pl.BlockSpec((seq, LANES), lambda o, c: (0, o * nct + c)),
                  pl.BlockSpec((seq, LANES), lambda o, c: (0, o * nct + c)),
                  const(t1r), const(t2)],
        out_specs=pl.BlockSpec((1, 4, seq, LANES), lambda o, c: (o, 0, 0, c)),
        scratch_shapes=[pltpu.VMEM((prow, LANES), F32)] * 6,
        compiler_params=_cparams(("arbitrary", "arbitrary")),
        name="hyspectra",
    )(hs, hd, t1r, t2)


def _short_conv(x, w_ref, b_ref):
    rows = x.shape[0]
    sub = 8
    ri = lax.broadcasted_iota(jnp.int32, (sub, x.shape[1]), 0)
    prev = pltpu.roll(x, 1, 0)
    prev = jnp.concatenate([jnp.where(ri == 0, 0.0, prev[:sub]), prev[sub:]], axis=0)
    nxt = pltpu.roll(x, rows - 1, 0)
    nxt = jnp.concatenate([nxt[:rows - sub], jnp.where(ri == sub - 1, 0.0, nxt[rows - sub:])], axis=0)
    return prev * w_ref[0:1, :] + x * w_ref[1:2, :] + nxt * w_ref[2:3, :] + b_ref[...]


def _hyconv_kernel(z_ref, g_ref, cwz_ref, cbz_ref, cwg_ref, cbg_ref, skip_ref, h_ref,
                   t1_ref, t2_ref, t3_ref, t4_ref, o_ref,
                   xr_ref, xi_ref, per_ref, pei_ref, por_ref, poi_ref, qer_ref, qei_ref, qor_ref, qoi_ref,
                   *, conv_z):
    seq = z_ref.shape[0] // 2
    for half, dst in ((0, xr_ref), (1, xi_ref)):
        z = z_ref[half * seq:(half + 1) * seq, :].astype(F32)
        if conv_z:
            z = _short_conv(z, cwz_ref, cbz_ref)
        for j in range(FFT_R):
            dst[j * FFT_PITCH:j * FFT_PITCH + FFT_R, :] = z[j * FFT_R:(j + 1) * FFT_R]

    pp = ((per_ref, pei_ref), (por_ref, poi_ref))
    qq = ((qer_ref, qei_ref), (qor_ref, qoi_ref))
    yr_ref, yi_ref = qer_ref, qei_ref
    t1 = jnp.concatenate([t1_ref[0], t1_ref[1]], axis=0)
    t4 = jnp.concatenate([t4_ref[0], t4_ref[1]], axis=1)
    halves = (slice(0, LANES), slice(LANES, 2 * LANES))

    def first(r):
        o = jnp.dot(t1, _pair_bf16(xr_ref, xi_ref, _across(r), _across(r + 1)), preferred_element_type=F32)
        for par in (0, 1):
            for k, cols in zip((r, r + 1), halves):
                pp[par][0][_slab(k), :] = o[2 * par * FFT_R:(2 * par + 1) * FFT_R, cols]
                pp[par][1][_slab(k), :] = o[(2 * par + 1) * FFT_R:(2 * par + 2) * FFT_R, cols]
    _loop(first, step=2)

    def second(r):
        rows = pl.ds(pl.multiple_of(r * FFT_R, FFT_R), FFT_R)
        for par in (0, 1):
            o = jnp.dot(t2_ref[2 * r + par], _cat_bf16(pp[par][0][_across(r), :], pp[par][1][_across(r), :]),
                        preferred_element_type=F32)
            hr = h_ref[0, 2 * par, rows, :].astype(F32)
            hi = h_ref[0, 2 * par + 1, rows, :].astype(F32)
            ar, ai = o[:FFT_R], o[FFT_R:]
            qq[par][0][_slab(r), :] = ar * hr - ai * hi
            qq[par][1][_slab(r), :] = ar * hi + ai * hr
    _loop(second)

    def third(r):
        for par in (0, 1):
            o = jnp.dot(t3_ref[2 * r + par], _cat_bf16(qq[par][0][_slab(r), :], qq[par][1][_slab(r), :]),
                        preferred_element_type=F32)
            pp[par][0][_across(r), :] = o[:FFT_R]
            pp[par][1][_across(r), :] = o[FFT_R:]
    _loop(third)

    def fourth(r):
        quad = lambda k: jnp.concatenate([pp[0][0][_slab(k), :], pp[0][1][_slab(k), :],
                                          pp[1][0][_slab(k), :], pp[1][1][_slab(k), :]], axis=0).astype(BF16)
        o = jnp.dot(t4, jnp.concatenate([quad(r), quad(r + 1)], axis=1), preferred_element_type=F32)
        for k, cols in zip((r, r + 1), halves):
            yr_ref[_across(k), :] = o[:FFT_R, cols]
            yi_ref[_across(k), :] = o[FFT_R:, cols]
    _loop(fourth, step=2)

    skip = skip_ref[...]
    for half, (y_ref, x_ref) in enumerate(((yr_ref, xr_ref), (yi_ref, xi_ref))):
        g = _short_conv(g_ref[half * seq:(half + 1) * seq, :].astype(F32), cwg_ref, cbg_ref)
        for j in range(FFT_R):
            src = slice(j * FFT_PITCH, j * FFT_PITCH + FFT_R)
            dst = slice(half * seq + j * FFT_R, half * seq + (j + 1) * FFT_R)
            o_ref[dst, :] = (g[j * FFT_R:(j + 1) * FFT_R] * (y_ref[src, :] + skip * x_ref[src, :])).astype(o_ref.dtype)


def _hyconv(zsrc, zcol0, gcol0, proj, conv_w, conv_b, zpart, gpart, skip, spec, order, tabs, batch, seq, conv_z):
    t1, t2, t3, t4 = tabs
    nct = HYENA_WIDTH // LANES
    prow = FFT_R * FFT_PITCH
    const = lambda a: pl.BlockSpec(a.shape, lambda c, p: (0,) * a.ndim, pipeline_mode=pl.Buffered(1))
    return pl.pallas_call(
        functools.partial(_hyconv_kernel, conv_z=conv_z),
        out_shape=jax.ShapeDtypeStruct((batch * seq, HYENA_WIDTH), BF16),
        grid=(nct, batch // 2),
        in_specs=[
            pl.BlockSpec((2 * seq, LANES), lambda c, p: (p, zcol0 + c)),
            pl.BlockSpec((2 * seq, LANES), lambda c, p: (p, gcol0 + c)),
            pl.BlockSpec((3, LANES), lambda c, p: (0, zpart * nct + c)),
            pl.BlockSpec((1, LANES), lambda c, p: (0, zpart * nct + c)),
            pl.BlockSpec((3, LANES), lambda c, p: (0, gpart * nct + c)),
            pl.BlockSpec((1, LANES), lambda c, p: (0, gpart * nct + c)),
            pl.BlockSpec((1, LANES), lambda c, p: (0, c)),
            pl.BlockSpec((1, 4, seq, LANES), lambda c, p: (order, 0, 0, c), pipeline_mode=pl.Buffered(1)),
            const(t1), const(t2), const(t3), const(t4),
        ],
        out_specs=pl.BlockSpec((2 * seq, LANES), lambda c, p: (p, c)),
        scratch_shapes=[pltpu.VMEM((prow, LANES), F32)] * 10,
        compiler_params=_cparams(("arbitrary", "arbitrary")),
        name=f"hyconv{order}",
    )(zsrc, proj, conv_w, conv_b, conv_w, conv_b, skip[order:order + 1], spec, t1, t2, t3, t4)


def _hyena(proj, hycol0, conv_w, conv_b, w1, b1, w2, b2, w3, b3, w_out, freq, skip, batch, seq):
    hid = w2.shape[0]
    eye = jnp.eye(FILT_GROUPS, dtype=F32)
    bdiag = lambda a: jnp.kron(eye, a)
    tile = lambda a: jnp.tile(a[None], (1, FILT_GROUPS))
    bands = tile(jnp.linspace(1e-4, FILTER_BANDS - 1, FILTER_BANDS, dtype=F32))
    min_decay = math.log(DECAY_TARGET) / SLOW_DECAY_PCT
    max_decay = math.log(DECAY_TARGET) / FAST_DECAY_PCT
    deltas = jnp.abs(jnp.linspace(min_decay, max_decay, HYENA_WIDTH, dtype=F32))
    wo4 = w_out.reshape(hid, HYENA_ORDER, 2, HYENA_WIDTH)
    w1t = jnp.pad(w1[0:1], ((0, FILTER_BANDS - 1), (0, 0)))
    hs, hd = _filters(
        bands, bdiag(w1t), bdiag(w1[1:1 + FILTER_BANDS]), bdiag(w1[1 + FILTER_BANDS:]),
        tile(b1), bdiag(w2), tile(b2), bdiag(w3), tile(b3), tile(freq),
        wo4[:, :, 0].reshape(hid, -1), wo4[:, :, 1].reshape(hid, -1),
        jnp.tile(deltas, HYENA_ORDER)[None], seq)
    t1, t2, t3, t4, t1r = _dft_tables(seq)
    spec = _spectra(hs, hd, t1r, t2, seq)
    c0 = hycol0 // LANES
    nct = HYENA_WIDTH // LANES
    cb = conv_b[None]
    z1 = _hyconv(proj, c0, c0 + nct, proj, conv_w, cb, 0, 1, skip, spec, 0, (t1, t2, t3, t4), batch, seq, True)
    return _hyconv(z1, 0, c0 + 2 * nct, proj, conv_w, cb, 0, 2, skip, spec, 1, (t1, t2, t3, t4), batch, seq, False)


def _rope_tables(seq):
    pos = np.arange(seq, dtype=np.float32)
    inv_freq = (ROPE_THETA ** (-np.arange(0, ROPE_DIM, 2, dtype=np.float32) / ROPE_DIM)).astype(np.float32)
    ang = pos[:, None] * inv_freq[None, :]
    cos, sin = np.cos(ang).astype(np.float32), np.sin(ang).astype(np.float32)
    half = ROPE_DIM // 2
    c = np.ones((seq, QK_DIM), np.float32)
    s = np.zeros((seq, QK_DIM), np.float32)
    c[:, :half] = cos
    c[:, half:ROPE_DIM] = cos
    s[:, :half] = -sin
    s[:, half:ROPE_DIM] = sin
    perm = np.zeros((LANES, LANES), np.float32)
    for j in range(LANES):
        if j % QK_DIM < half:
            perm[j + half, j] = 1.0
        elif j % QK_DIM < ROPE_DIM:
            perm[j - half, j] = 1.0
    rep = LANES // QK_DIM
    return (jnp.asarray(np.tile(c, (1, rep))), jnp.asarray(np.tile(s, (1, rep))), jnp.asarray(perm).astype(BF16))


def kernel(x, norm1_g, w_in, short_conv_w, short_conv_b, q_norm_g, k_norm_g, lambda_q1, lambda_k1, lambda_q2, lambda_k2, subln_g, filt_w1, filt_b1, filt_w2, filt_b2, filt_w3, filt_b3, filt_w_out, filt_freq, hyena_skip, w_branch_attn, w_branch_hyena, w_out, norm2_g, w_router, w_gate, w_up, w_down):
    b, l, d = x.shape
    depth = w_in.shape[0]
    n = b * l
    cap = CAPACITY_FACTOR * l // N_EXPERTS
    q_cols = ATTN_HEADS * 2 * QK_DIM
    ctab, stab, perm = _rope_tables(l)
    xc = x.reshape(n, d)
    for li in range(depth):
        lambda_init = 0.8 - 0.6 * math.exp(-0.3 * li)
        proj = _inproj(xc, norm1_g[li][None], w_in[li].astype(BF16))
        scale = math.log2(math.e) / math.sqrt(QK_DIM)
        gqk = jnp.concatenate([jnp.tile(q_norm_g[li], q_cols // QK_DIM) * scale,
                               jnp.tile(k_norm_g[li], q_cols // QK_DIM)])[None]
        qk = _qkprep(proj, ctab, stab, perm, gqk, l)
        lamv = jnp.stack([lambda_q1[li], lambda_k1[li], lambda_q2[li], lambda_k2[li]])
        attn = _attention(qk, proj, lamv, subln_g[li][None], b, l, lambda_init)

        hyena = _hyena(proj, 3 * q_cols, short_conv_w[li], short_conv_b[li], filt_w1[li], filt_b1[li],
                       filt_w2[li], filt_b2[li], filt_w3[li], filt_b3[li], filt_w_out[li], filt_freq[li],
                       hyena_skip[li], b, l)

        wr_pad = jnp.pad(w_router[li], ((0, 0), (0, LANES - N_EXPERTS))).astype(BF16)
        x1, u2, aff = _merge(attn, hyena, proj, xc, w_branch_attn[li].astype(BF16),
                             w_branch_hyena[li].astype(BF16), w_out[li].astype(BF16),
                             norm2_g[li][None], wr_pad)
        pos, post, afft, offs = _topk(aff, b, l, cap)
        xin = _gather(offs, post, u2, b, l, cap)
        weo = _experts(offs, post, afft, xin, w_gate, w_up, w_down, li, b, l, cap)
        xc = _combine(offs, pos, x1, weo, b, l, cap)
    return xc.reshape(b, l, d)
```

```python
import functools
import math

import jax
import jax.numpy as jnp
import numpy as np
from jax import lax
from jax.experimental import pallas as pl
from jax.experimental.pallas import tpu as pltpu

F32 = jnp.float32
BF16 = jnp.bfloat16

ATTN_HEADS = 4
QK_DIM = 64
V_DIM = 128
ROPE_DIM = 16
ROPE_THETA = 500000.0
HYENA_WIDTH = 512
HYENA_ORDER = 2
FILTER_BANDS = 16
DECAY_TARGET = 1e-2
FAST_DECAY_PCT = 0.3
SLOW_DECAY_PCT = 1.5
N_EXPERTS = 16
CAPACITY_FACTOR = 2
MOE_TILE = 512
EPS = 1e-6
LANES = 128
FILT_GROUPS = LANES // FILTER_BANDS
VMEM_LIMIT = 56 * 1024 * 1024

INPROJ_ROWS, INPROJ_COLS = 2048, 1280
QKPREP_ROWS = 1024
ATTN_Q_ROWS, ATTN_KEY_CHUNK = 1024, 256
MERGE_ROWS = 1024
TOPK_CUMSUM_ROWS = 256
GATHER_WINDOW = 96
AFFINITY_WINDOW = 128
COMBINE_WINDOW = 128
FILTER_ROWS = 512


def _cparams(sem):
    return pltpu.CompilerParams(dimension_semantics=sem, vmem_limit_bytes=VMEM_LIMIT)


def _inproj_kernel(x_ref, g_ref, w_ref, o_ref, u_ref):
    j = pl.program_id(1)

    @pl.when(j == 0)
    def _():
        x = x_ref[...]
        ms = jnp.mean(x * x, axis=-1, keepdims=True)
        u = (x * lax.rsqrt(ms + EPS) * g_ref[...]).astype(BF16)
        u_ref[...] = u
        o_ref[...] = jnp.dot(u, w_ref[...], preferred_element_type=F32).astype(o_ref.dtype)

    @pl.when(j > 0)
    def _():
        o_ref[...] = jnp.dot(u_ref[...], w_ref[...], preferred_element_type=F32).astype(o_ref.dtype)


def _inproj(x2, g, w_bf16, tm=INPROJ_ROWS, tn=INPROJ_COLS):
    n, d = x2.shape
    cols = w_bf16.shape[1]
    return pl.pallas_call(
        _inproj_kernel,
        out_shape=jax.ShapeDtypeStruct((n, cols), BF16),
        grid=(n // tm, cols // tn),
        in_specs=[
            pl.BlockSpec((tm, d), lambda i, j: (i, 0)),
            pl.BlockSpec((1, d), lambda i, j: (0, 0)),
            pl.BlockSpec((d, tn), lambda i, j: (0, j)),
        ],
        out_specs=pl.BlockSpec((tm, tn), lambda i, j: (i, j)),
        scratch_shapes=[pltpu.VMEM((tm, d), BF16)],
        compiler_params=_cparams(("arbitrary", "arbitrary")),
        name="inproj",
    )(x2, g, w_bf16)


def _qkprep_kernel(p_ref, c_ref, s_ref, perm_ref, g_ref, o_ref):
    lane = lax.broadcasted_iota(jnp.int32, (1, LANES), 1)
    lo = lane < QK_DIM
    c = c_ref[...]
    s = s_ref[...]
    perm = perm_ref[...]
    for j in range(p_ref.shape[1] // LANES):
        t = p_ref[:, j * LANES:(j + 1) * LANES].astype(F32)
        sq = t * t
        ss_lo = jnp.sum(jnp.where(lo, sq, 0.0), axis=-1, keepdims=True)
        ss_hi = jnp.sum(jnp.where(lo, 0.0, sq), axis=-1, keepdims=True)
        r = lax.rsqrt(jnp.where(lo, ss_lo, ss_hi) * (1.0 / QK_DIM) + EPS)
        y = t * r * g_ref[:, j * LANES:(j + 1) * LANES]
        y_hi = y.astype(BF16)
        y_lo = (y - y_hi.astype(F32)).astype(BF16)
        partner = (jnp.dot(y_hi, perm, preferred_element_type=F32)
                   + jnp.dot(y_lo, perm, preferred_element_type=F32))
        o_ref[:, j * LANES:(j + 1) * LANES] = (y * c + partner * s).astype(o_ref.dtype)


def _qkprep(proj, ctab, stab, perm, gqk, seq, tm=QKPREP_ROWS):
    n = proj.shape[0]
    w = gqk.shape[1]
    nb = seq // tm
    return pl.pallas_call(
        _qkprep_kernel,
        out_shape=jax.ShapeDtypeStruct((n, w), BF16),
        grid=(n // tm,),
        in_specs=[
            pl.BlockSpec((tm, w), lambda i: (i, 0)),
            pl.BlockSpec((tm, LANES), lambda i: (i % nb, 0)),
            pl.BlockSpec((tm, LANES), lambda i: (i % nb, 0)),
            pl.BlockSpec((LANES, LANES), lambda i: (0, 0)),
            pl.BlockSpec((1, w), lambda i: (0, 0)),
        ],
        out_specs=pl.BlockSpec((tm, w), lambda i: (i, 0)),
        compiler_params=_cparams(("arbitrary",)),
        name="qkprep",
    )(proj, ctab, stab, perm, gqk)


def _attn_kernel(q_ref, k_ref, v_ref, lam_ref, g_ref, o_ref, *, tk, unroll, lambda_init):
    tq = q_ref.shape[0]
    seq = k_ref.shape[0]
    lane = lax.broadcasted_iota(jnp.int32, (1, LANES), 1)
    q = q_ref[...]
    zero = jnp.zeros_like(q)
    qs = jnp.concatenate([jnp.where(lane < QK_DIM, q, zero), jnp.where(lane < QK_DIM, zero, q)], axis=0)
    lv = lam_ref[...]
    lam = (jnp.exp(jnp.sum(lv[0:1] * lv[1:2], axis=-1, keepdims=True))
           - jnp.exp(jnp.sum(lv[2:3] * lv[3:4], axis=-1, keepdims=True)) + lambda_init)

    def step(c, carry):
        m, l, a = carry
        off = pl.multiple_of(c * tk, tk)
        kc = k_ref[pl.ds(off, tk), :]
        vc = v_ref[pl.ds(off, tk), :]
        s = lax.dot_general(qs, kc, (((1,), (1,)), ((), ())), preferred_element_type=F32)
        mn = jnp.maximum(m, jnp.max(s, axis=-1, keepdims=True))
        p = jnp.exp2((s - mn).astype(BF16))
        al = jnp.exp2(m - mn)
        pf = p.astype(F32)
        part = pf[:, 0:LANES]
        for j in range(1, tk // LANES):
            part = part + pf[:, j * LANES:(j + 1) * LANES]
        l = al * l + part
        a = al * a + jnp.dot(p, vc, preferred_element_type=F32)
        return mn, l, a

    init = (jnp.full((2 * tq, 1), -jnp.inf, F32), jnp.zeros((2 * tq, LANES), F32), jnp.zeros((2 * tq, V_DIM), F32))
    _, l, a = lax.fori_loop(0, seq // tk, step, init, unroll=unroll)
    a = a / jnp.sum(l, axis=-1, keepdims=True)
    o = a[:tq] - lam * a[tq:]
    ms = jnp.mean(o * o, axis=-1, keepdims=True)
    o = o * lax.rsqrt(ms + EPS) * g_ref[...] * (1.0 - lambda_init)
    o_ref[...] = o.astype(o_ref.dtype)


def _attention(qk, proj, lamv, subln_g, batch, seq, lambda_init, tq=ATTN_Q_ROWS, tk=ATTN_KEY_CHUNK):
    nq = seq // tq
    vcol0 = (2 * ATTN_HEADS * 2 * QK_DIM) // V_DIM
    return pl.pallas_call(
        functools.partial(_attn_kernel, tk=tk, unroll=seq // tk, lambda_init=lambda_init),
        out_shape=jax.ShapeDtypeStruct((batch * seq, ATTN_HEADS * V_DIM), BF16),
        grid=(batch, ATTN_HEADS, nq),
        in_specs=[
            pl.BlockSpec((tq, LANES), lambda b, h, i: (b * nq + i, h)),
            pl.BlockSpec((seq, LANES), lambda b, h, i: (b, ATTN_HEADS + h)),
            pl.BlockSpec((seq, V_DIM), lambda b, h, i: (b, vcol0 + h)),
            pl.BlockSpec((4, QK_DIM), lambda b, h, i: (0, 0)),
            pl.BlockSpec((1, V_DIM), lambda b, h, i: (0, 0)),
        ],
        out_specs=pl.BlockSpec((tq, V_DIM), lambda b, h, i: (b * nq + i, h)),
        compiler_params=_cparams(("arbitrary", "arbitrary", "arbitrary")),
        name="diffattn",
    )(qk, qk, proj, lamv, subln_g)


def _merge_kernel(at_ref, hy_ref, ga_ref, gh_ref, x_ref, wpa_ref, wph_ref, wo_ref, g2_ref, wr_ref,
                  x1_ref, u2_ref, aff_ref):
    tm = x_ref.shape[0]
    parts = 2
    for rows in (slice(k * tm // parts, (k + 1) * tm // parts) for k in range(parts)):
        ga = jax.nn.sigmoid(ga_ref[rows, :].astype(F32))
        gh = jax.nn.sigmoid(gh_ref[rows, :].astype(F32))
        merged = (ga * jnp.dot(at_ref[rows, :], wpa_ref[...], preferred_element_type=F32)
                  + gh * jnp.dot(hy_ref[rows, :], wph_ref[...], preferred_element_type=F32))
        x1 = x_ref[rows, :] + jnp.dot(merged.astype(BF16), wo_ref[...], preferred_element_type=F32)
        x1_ref[rows, :] = x1
        ms = jnp.mean(x1 * x1, axis=-1, keepdims=True)
        u2 = (x1 * lax.rsqrt(ms + EPS) * g2_ref[...]).astype(BF16)
        u2_ref[rows, :] = u2
        logits = jnp.dot(u2, wr_ref[...], preferred_element_type=F32)
        lane = lax.broadcasted_iota(jnp.int32, logits.shape, 1)
        logits = jnp.where(lane < N_EXPERTS, logits, -jnp.inf)
        e = jnp.exp(logits - jnp.max(logits, axis=-1, keepdims=True))
        aff_ref[rows, :] = e / jnp.sum(e, axis=-1, keepdims=True)


def _merge(attn, hyena, proj, x2, wpa, wph, wo, g2, wr_pad, tm=MERGE_ROWS):
    n, d = x2.shape
    gcol0 = (proj.shape[1] - 2 * d) // d
    full = lambda shape: pl.BlockSpec(shape, lambda i: (0, 0))
    return pl.pallas_call(
        _merge_kernel,
        out_shape=(jax.ShapeDtypeStruct((n, d), F32), jax.ShapeDtypeStruct((n, d), BF16),
                   jax.ShapeDtypeStruct((n, LANES), F32)),
        grid=(n // tm,),
        in_specs=[
            pl.BlockSpec((tm, attn.shape[1]), lambda i: (i, 0)),
            pl.BlockSpec((tm, hyena.shape[1]), lambda i: (i, 0)),
            pl.BlockSpec((tm, d), lambda i: (i, gcol0)),
            pl.BlockSpec((tm, d), lambda i: (i, gcol0 + 1)),
            pl.BlockSpec((tm, d), lambda i: (i, 0)),
            full(wpa.shape), full(wph.shape), full(wo.shape), full(g2.shape), full(wr_pad.shape),
        ],
        out_specs=(pl.BlockSpec((tm, d), lambda i: (i, 0)), pl.BlockSpec((tm, d), lambda i: (i, 0)),
                   pl.BlockSpec((tm, LANES), lambda i: (i, 0))),
        compiler_params=_cparams(("arbitrary",)),
        name="merge",
    )(attn, hyena, proj, proj, x2, wpa, wph, wo, g2, wr_pad)


def _excl_cumsum_rows(mask_f32, tri, blk):
    rows = mask_f32.shape[0]
    carry = jnp.zeros((1, LANES), F32)
    outs = []
    for r in range(rows // blk):
        mb = mask_f32[r * blk:(r + 1) * blk]
        outs.append(jnp.dot(tri, mb.astype(BF16), preferred_element_type=F32) + carry)
        carry = carry + jnp.sum(mb, axis=0, keepdims=True)
    return jnp.concatenate(outs, axis=0)


def _topk_kernel(aff_ref, pos_ref, post_ref, afft_ref, offs_ref, *, cap, blk, batch):
    seq = aff_ref.shape[0] // batch
    aff = aff_ref[0:seq]
    for b in range(1, batch):
        aff = aff + pltpu.roll(aff_ref[b * seq:(b + 1) * seq], b * N_EXPERTS, 1)

    def search(i, prefix):
        cand = prefix | jnp.left_shift(jnp.int32(1), 29 - i)
        cnt = jnp.sum((aff >= pltpu.bitcast(cand, F32)[0:1]).astype(F32), axis=0, keepdims=True)
        return jnp.where(cnt >= cap, cand, prefix)

    lo = pltpu.bitcast(lax.fori_loop(0, 30, search, jnp.zeros((8, LANES), jnp.int32)), F32)[0:1]
    thr = jnp.min(jnp.where(aff >= lo, aff, jnp.inf), axis=0, keepdims=True)
    gt = (aff > thr).astype(F32)
    eq = (aff == thr).astype(F32)
    need = cap - jnp.sum(gt, axis=0, keepdims=True)
    ri = lax.broadcasted_iota(jnp.int32, (blk, blk), 0)
    ci = lax.broadcasted_iota(jnp.int32, (blk, blk), 1)
    tri = (ci < ri).astype(BF16)
    sel = gt + eq * (_excl_cumsum_rows(eq, tri, blk) < need).astype(F32)
    before = _excl_cumsum_rows(sel, tri, blk)
    pos = jnp.where(sel > 0.0, before, -1.0)
    pos_t = pos.T
    aff_t = aff.T
    lane = lax.broadcasted_iota(jnp.int32, (1, LANES), 1)
    for b in range(batch):
        mine = pos if b == 0 else pltpu.roll(pos, LANES - b * N_EXPERTS, 1)
        pos_ref[b * seq:(b + 1) * seq] = jnp.where(lane < N_EXPERTS, mine, -1.0)
        post_ref[b] = pos_t[b * N_EXPERTS:(b + 1) * N_EXPERTS]
        afft_ref[b] = aff_t[b * N_EXPERTS:(b + 1) * N_EXPERTS]
    offs_ref[...] = jnp.concatenate([before[j * MOE_TILE:j * MOE_TILE + 1] for j in range(offs_ref.shape[0])], axis=0)


def _topk(aff, batch, seq, cap, blk=TOPK_CUMSUM_ROWS):
    assert batch * N_EXPERTS <= LANES
    nt = seq // MOE_TILE
    pos, post, afft, offs = pl.pallas_call(
        functools.partial(_topk_kernel, cap=cap, blk=blk, batch=batch),
        out_shape=(jax.ShapeDtypeStruct((batch * seq, LANES), F32),
                   jax.ShapeDtypeStruct((batch, N_EXPERTS, seq), F32),
                   jax.ShapeDtypeStruct((batch, N_EXPERTS, seq), F32),
                   jax.ShapeDtypeStruct((nt, LANES), F32)),
        compiler_params=pltpu.CompilerParams(vmem_limit_bytes=VMEM_LIMIT),
        name="topk",
    )(aff)
    offs = offs[:, :batch * N_EXPERTS].reshape(nt, batch, N_EXPERTS).transpose(1, 0, 2)
    return pos, post, afft, offs.astype(jnp.int32).reshape(-1)


def _align_down(x, m):
    sh = m.bit_length() - 1
    return lax.shift_left(lax.shift_right_logical(x, sh), sh)


def _ceil_div(x, m, x_max):
    if m & (m - 1) == 0:
        return lax.shift_right_logical(x + (m - 1), m.bit_length() - 1)
    n = jnp.int32(0)
    for k in range(-(-x_max // m)):
        n = n + (x > k * m).astype(jnp.int32)
    return n


def _gather_kernel(offs_ref, post_ref, u_ref, x_ref, *, cap, win):
    b = pl.program_id(0)
    c = pl.program_id(1)
    nt = pl.num_programs(1)

    @pl.when(c == 0)
    def _():
        x_ref[...] = jnp.zeros_like(x_ref)

    uc = u_ref[...]
    slot0 = lax.broadcasted_iota(jnp.int32, (win, MOE_TILE), 0).astype(F32)
    base = (b * nt + c) * N_EXPERTS
    last = pl.num_programs(0) * nt * N_EXPERTS - 1
    starts, hots = [], []
    for e in range(N_EXPERTS):
        s = pl.multiple_of(jnp.minimum(_align_down(offs_ref[base + e], 16), cap - win), 16)
        hots.append(jnp.where((slot0 + s.astype(F32)) == post_ref[0, e:e + 1, :], 1.0, 0.0).astype(BF16))
        starts.append(s)
    rows = jnp.dot(jnp.concatenate(hots, axis=0), uc, preferred_element_type=F32)
    for e in range(N_EXPERTS):
        x_ref[0, e, pl.ds(starts[e], win), :] += rows[e * win:(e + 1) * win].astype(BF16)
    for e in range(N_EXPERTS):
        lo0 = _align_down(offs_ref[base + e], 16)
        end = jnp.where(c + 1 < nt, offs_ref[jnp.minimum(base + N_EXPERTS + e, last)], cap)

        def more(k, carry, lo0=lo0, e=e):
            lo = lo0 + k * win
            s = pl.multiple_of(jnp.minimum(lo, cap - win), 16)
            srow = slot0 + s.astype(F32)
            hit = jnp.logical_and(srow == post_ref[0, e:e + 1, :], srow >= lo.astype(F32))
            x_ref[0, e, pl.ds(s, win), :] += jnp.dot(jnp.where(hit, 1.0, 0.0).astype(BF16), uc,
                                                     preferred_element_type=F32).astype(BF16)
            return carry

        lax.fori_loop(1, _ceil_div(end - lo0, win, cap + 16), more, 0)


def _gather(offs, post, u2, batch, seq, cap, win=GATHER_WINDOW):
    d = u2.shape[1]
    nt = seq // MOE_TILE
    grid_spec = pltpu.PrefetchScalarGridSpec(
        num_scalar_prefetch=1,
        grid=(batch, nt),
        in_specs=[
            pl.BlockSpec((1, N_EXPERTS, MOE_TILE), lambda b, c, o: (b, 0, c)),
            pl.BlockSpec((MOE_TILE, d), lambda b, c, o: (b * nt + c, 0)),
        ],
        out_specs=pl.BlockSpec((1, N_EXPERTS, cap, d), lambda b, c, o: (b, 0, 0, 0)),
    )
    return pl.pallas_call(
        functools.partial(_gather_kernel, cap=cap, win=win),
        out_shape=jax.ShapeDtypeStruct((batch, N_EXPERTS, cap, d), BF16),
        grid_spec=grid_spec,
        compiler_params=_cparams(("arbitrary", "arbitrary")),
        name="gather",
    )(offs, post, u2)


def _expert_kernel(offs_ref, post_ref, afft_ref, x_ref, wg_ref, wu_ref, wd_ref, o_ref, gs_ref, w_ref, eo_ref,
                   *, cap, win):
    e = pl.program_id(0)
    b = pl.program_id(1)

    def cast():
        for m, src in enumerate((wg_ref, wu_ref, wd_ref)):
            part = src.shape[2]
            w_ref[jnp.bitwise_and(e, 1), m, pl.ds(pl.multiple_of(b * part, part), part), :] = src[0, 0].astype(BF16)

    def run():
        _expert_body(offs_ref, post_ref, afft_ref, x_ref, w_ref.at[1 - jnp.bitwise_and(e, 1)], o_ref, gs_ref, eo_ref,
                     e - 1, b, cap, win)

    @pl.when(e == 0)
    def _():
        cast()
        o_ref[...] = jnp.zeros_like(o_ref)

    @pl.when(jnp.logical_and(e > 0, e < N_EXPERTS))
    def _():
        cast()
        run()

    @pl.when(e == N_EXPERTS)
    def _():
        run()


def _expert_body(offs_ref, post_ref, afft_ref, x_ref, w_ref, o_ref, gs_ref, eo_ref, e, b, cap, win):
    seq = post_ref.shape[2]
    nt = seq // MOE_TILE
    gs_ref[...] = jnp.zeros_like(gs_ref)
    slot0 = lax.broadcasted_iota(jnp.int32, (win, MOE_TILE), 0).astype(F32)
    windows = []
    for c in range(nt):
        base = (b * nt + c) * N_EXPERTS + e
        lo0 = _align_down(offs_ref[base], 8)
        end = offs_ref[base + N_EXPERTS] if c + 1 < nt else cap
        prow = post_ref[0, pl.ds(e, 1), c * MOE_TILE:(c + 1) * MOE_TILE]
        arow = afft_ref[0, pl.ds(e, 1), c * MOE_TILE:(c + 1) * MOE_TILE]

        def window(k, carry, lo0=lo0, prow=prow, arow=arow):
            lo = lo0 + k * win
            s = pl.multiple_of(jnp.minimum(lo, cap - win), 8)
            srow = slot0 + s.astype(F32)
            hit = jnp.logical_and(srow == prow, srow >= lo.astype(F32))
            gs_ref[pl.ds(s, win), :] += jnp.sum(jnp.where(hit, arow, 0.0), axis=-1, keepdims=True)
            return carry

        window(0, 0)
        windows.append((lo0, end, window))
    xb = x_ref[0, 0]
    hg = jnp.dot(xb, w_ref[0], preferred_element_type=F32)
    hu = jnp.dot(xb, w_ref[1], preferred_element_type=F32)
    act = (hg * jax.nn.sigmoid(hg) * hu).astype(BF16)
    eo_ref[...] = jnp.dot(act, w_ref[2], preferred_element_type=F32)
    for lo0, end, window in windows:
        lax.fori_loop(1, _ceil_div(end - lo0, win, cap + 16), window, 0)
    o_ref[0, 0] = (eo_ref[...] * gs_ref[...]).astype(o_ref.dtype)


def _experts(offs, post, afft, xin, wg, wu, wd, layer, batch, seq, cap, win=AFFINITY_WINDOW):
    d = xin.shape[3]
    assert wg.shape[2:] == (d, d) and wd.shape[2:] == (d, d) and d % (16 * batch) == 0
    part = d // batch
    wspec = pl.BlockSpec((1, 1, part, d), lambda e, b, o: (layer, jnp.minimum(e, N_EXPERTS - 1), b, 0))
    grid_spec = pltpu.PrefetchScalarGridSpec(
        num_scalar_prefetch=1,
        grid=(N_EXPERTS + 1, batch),
        in_specs=[
            pl.BlockSpec((1, N_EXPERTS, seq), lambda e, b, o: (b, 0, 0)),
            pl.BlockSpec((1, N_EXPERTS, seq), lambda e, b, o: (b, 0, 0)),
            pl.BlockSpec((1, 1, cap, d), lambda e, b, o: (b, jnp.maximum(e - 1, 0), 0, 0)),
            wspec, wspec, wspec,
        ],
        out_specs=pl.BlockSpec((1, 1, cap, d), lambda e, b, o: (b, jnp.where(e == 0, N_EXPERTS, e - 1), 0, 0)),
        scratch_shapes=[pltpu.VMEM((cap, 1), F32), pltpu.VMEM((2, 3, d, d), BF16), pltpu.VMEM((cap, d), F32)],
    )
    return pl.pallas_call(
        functools.partial(_expert_kernel, cap=cap, win=win),
        out_shape=jax.ShapeDtypeStruct((batch, N_EXPERTS + 1, cap, d), BF16),
        grid_spec=grid_spec,
        compiler_params=_cparams(("arbitrary", "arbitrary")),
        name="experts",
    )(offs, post, afft, xin, wg, wu, wd)


def _combine_kernel(offs_ref, pos_ref, x1_ref, weo_ref, o_ref, *, cap, win):
    b = pl.program_id(0)
    t = pl.program_id(1)
    nt = pl.num_programs(1)
    tt = pos_ref.shape[0]
    pos = pos_ref[...]
    slot0 = lax.broadcasted_iota(jnp.int32, (tt, win), 1).astype(F32)
    base = (b * nt + t) * N_EXPERTS
    last = pl.num_programs(0) * nt * N_EXPERTS - 1
    acc = x1_ref[...]
    for e0 in range(0, N_EXPERTS, 2):
        hots, rows = [], []
        for e in (e0, e0 + 1):
            s = pl.multiple_of(jnp.minimum(_align_down(offs_ref[base + e], 16), cap - win), 16)
            hots.append(jnp.where((slot0 + s.astype(F32)) == pos[:, e:e + 1], 1.0, 0.0).astype(BF16))
            rows.append(weo_ref[0, e, pl.ds(s, win), :])
        acc = acc + jnp.dot(jnp.concatenate(hots, axis=1), jnp.concatenate(rows, axis=0),
                            preferred_element_type=F32)
    o_ref[...] = acc
    for e in range(N_EXPERTS):
        lo0 = _align_down(offs_ref[base + e], 16)
        end = jnp.where(t + 1 < nt, offs_ref[jnp.minimum(base + N_EXPERTS + e, last)], cap)

        def window(k, carry, lo0=lo0, e=e):
            lo = lo0 + k * win
            s = pl.multiple_of(jnp.minimum(lo, cap - win), 16)
            srow = slot0 + s.astype(F32)
            hit = jnp.logical_and(srow == pos[:, e:e + 1], srow >= lo.astype(F32))
            o_ref[...] += jnp.dot(jnp.where(hit, 1.0, 0.0).astype(BF16), weo_ref[0, e, pl.ds(s, win), :],
                                  preferred_element_type=F32)
            return carry

        lax.fori_loop(1, _ceil_div(end - lo0, win, cap + 16), window, 0)


def _combine(offs, pos, x1, weo, batch, seq, cap, win=COMBINE_WINDOW):
    d = x1.shape[1]
    tt = MOE_TILE
    nt = seq // tt
    grid_spec = pltpu.PrefetchScalarGridSpec(
        num_scalar_prefetch=1,
        grid=(batch, nt),
        in_specs=[
            pl.BlockSpec((tt, LANES), lambda b, t, o: (b * nt + t, 0)),
            pl.BlockSpec((tt, d), lambda b, t, o: (b * nt + t, 0)),
            pl.BlockSpec((1, N_EXPERTS, cap, d), lambda b, t, o: (b, 0, 0, 0)),
        ],
        out_specs=pl.BlockSpec((tt, d), lambda b, t, o: (b * nt + t, 0)),
    )
    return pl.pallas_call(
        functools.partial(_combine_kernel, cap=cap, win=win),
        out_shape=jax.ShapeDtypeStruct(x1.shape, F32),
        grid_spec=grid_spec,
        compiler_params=_cparams(("arbitrary", "arbitrary")),
        name="combine",
    )(offs, pos, x1, weo)


FFT_R = 64
FFT_PITCH = FFT_R + 8


def _dft_tables(seq):
    r = FFT_R
    assert seq == r * r
    n = np.arange(r)
    f = np.exp(-2j * np.pi * np.outer(n, n) / r)
    w = lambda e: np.exp(-2j * np.pi * e / (2 * seq))

    def real_rep(a):
        return np.block([[a.real, -a.imag], [a.imag, a.real]]).astype(np.float32)

    m1 = [f, f * w(r * n)[None, :]]
    m2 = [f * w(n * m)[None, :] for m in range(2 * r)]
    m4 = [f.conj() / (2 * seq), w(-r * n)[:, None] * f.conj() / (2 * seq)]
    t1 = np.stack([real_rep(a) for a in m1])
    t2 = np.stack([real_rep(a) for a in m2])
    t3 = np.stack([real_rep(a.conj().T) for a in m2])
    t4 = np.stack([real_rep(a) for a in m4])
    t1r = np.stack([np.concatenate([a.real, a.imag], axis=0).astype(np.float32) for a in m1])
    return tuple(jnp.asarray(t).astype(BF16) for t in (t1, t2, t3, t4, t1r))


def _slab(r):
    return pl.ds(pl.multiple_of(r * FFT_PITCH, 8), FFT_R)


def _across(r):
    return pl.ds(r, FFT_R, stride=FFT_PITCH)


def _cat_bf16(a, b):
    return jnp.concatenate([a, b], axis=0).astype(BF16)


FFT_UNROLL = 32


def _loop(body, step=1, rows_per_trip=FFT_UNROLL):
    def wrapped(i, c):
        body(i * step)
        return c
    lax.fori_loop(0, FFT_R // step, wrapped, 0, unroll=rows_per_trip // step)


def _pair_bf16(re_ref, im_ref, idx0, idx1):
    return jnp.concatenate([_cat_bf16(re_ref[idx0, :], im_ref[idx0, :]),
                            _cat_bf16(re_ref[idx1, :], im_ref[idx1, :])], axis=1)


def _bf16_pieces(w):
    hi = w.astype(BF16)
    return hi, (w - hi.astype(F32)).astype(BF16)


def _dot3(a, w_ref):
    a_hi, a_lo = _bf16_pieces(a)
    return (jnp.dot(a_hi, w_ref[0], preferred_element_type=F32)
            + (jnp.dot(a_hi, w_ref[1], preferred_element_type=F32)
               + jnp.dot(a_lo, w_ref[0], preferred_element_type=F32)))


def _filter_kernel(bands_ref, w1t_ref, w1c_ref, w1s_ref, b1_ref, w2_ref, b2_ref, w3_ref, b3_ref, fr_ref,
                   wf_ref, wb_ref, dl_ref, hs_ref, hd_ref, *, seq):
    tm = hs_ref.shape[0]
    rows = tm // FILT_GROUPS
    hid = wf_ref.shape[1]
    base = pl.program_id(0) * tm
    lane = lax.broadcasted_iota(jnp.int32, (rows, LANES), 1)
    grp = lax.shift_right_logical(lane, FILTER_BANDS.bit_length() - 1)
    pos = (base + grp * rows + lax.broadcasted_iota(jnp.int32, (rows, LANES), 0)).astype(F32)
    first = jnp.bitwise_and(lane, FILTER_BANDS - 1) == 0
    fr = fr_ref[...]
    tscale = 1.0 / max(seq - 1, 1)

    def mlp(p):
        ang = ((2.0 * math.pi / seq) * p) * bands_ref[...]
        pre = (_dot3(jnp.cos(ang), w1c_ref) + _dot3(-jnp.sin(ang), w1s_ref)
               + _dot3(jnp.where(first, p * tscale, 0.0), w1t_ref) + b1_ref[...])
        h = jnp.sin(fr * pre)
        h = jnp.sin(fr * (_dot3(h, w2_ref) + b2_ref[...]))
        return jnp.sin(fr * (_dot3(h, w3_ref) + b3_ref[...]))

    hf_h = mlp(pos)
    hb_h = mlp(seq - pos)
    dl = dl_ref[...]
    for g in range(FILT_GROUPS):
        pcol = (base + g * rows + lax.broadcasted_iota(jnp.int32, (rows, 1), 0)).astype(F32)
        hf = _dot3(hf_h[:, g * hid:(g + 1) * hid], wf_ref) * jnp.exp(-(pcol * tscale) * dl)
        hb = _dot3(hb_h[:, g * hid:(g + 1) * hid], wb_ref) * jnp.exp(-((seq - pcol) * tscale) * dl)
        hb = jnp.where(pcol > 0.0, hb, 0.0)
        hs_ref[g * rows:(g + 1) * rows, :] = (hf + hb).astype(hs_ref.dtype)
        hd_ref[g * rows:(g + 1) * rows, :] = (hf - hb).astype(hd_ref.dtype)


def _filters(bands, w1t, w1c, w1s, b1, w2, b2, w3, b3, fr, wf, wb, dl, seq, tm=FILTER_ROWS):
    cols = wf.shape[1]
    pieces = lambda w: jnp.stack(_bf16_pieces(w))
    args = (bands, pieces(w1t), pieces(w1c), pieces(w1s), b1, pieces(w2), b2, pieces(w3), b3, fr,
            pieces(wf), pieces(wb), dl)
    return pl.pallas_call(
        functools.partial(_filter_kernel, seq=seq),
        out_shape=(jax.ShapeDtypeStruct((seq, cols), BF16), jax.ShapeDtypeStruct((seq, cols), BF16)),
        grid=(seq // tm,),
        in_specs=[pl.BlockSpec(a.shape, lambda i, nd=a.ndim: (0,) * nd) for a in args],
        out_specs=(pl.BlockSpec((tm, cols), lambda i: (i, 0)), pl.BlockSpec((tm, cols), lambda i: (i, 0))),
        compiler_params=_cparams(("arbitrary",)),
        name="hyfilter",
    )(*args)


def _spectra_kernel(hs_ref, hd_ref, t1r_ref, t2_ref, h_ref, xe_ref, xo_ref, per_ref, pei_ref, por_ref, poi_ref):
    xx = (xe_ref, xo_ref)
    pp = ((per_ref, pei_ref), (por_ref, poi_ref))

    def fill(r):
        rows = pl.ds(pl.multiple_of(r * FFT_R, FFT_R), FFT_R)
        xe_ref[_slab(r), :] = hs_ref[rows, :].astype(F32)
        xo_ref[_slab(r), :] = hd_ref[rows, :].astype(F32)
    _loop(fill)

    def first(r):
        for par in (0, 1):
            o = jnp.dot(t1r_ref[par], xx[par][_across(r), :].astype(BF16), preferred_element_type=F32)
            pp[par][0][_slab(r), :] = o[:FFT_R]
            pp[par][1][_slab(r), :] = o[FFT_R:]
    _loop(first)

    def second(r):
        rows = pl.ds(pl.multiple_of(r * FFT_R, FFT_R), FFT_R)
        for par in (0, 1):
            o = jnp.dot(t2_ref[2 * r + par], _cat_bf16(pp[par][0][_across(r), :], pp[par][1][_across(r), :]),
                        preferred_element_type=F32)
            h_ref[0, 2 * par, rows, :] = o[:FFT_R].astype(h_ref.dtype)
            h_ref[0, 2 * par + 1, rows, :] = o[FFT_R:].astype(h_ref.dtype)
    _loop(second)


def _spectra(hs, hd, t1r, t2, seq):
    ncol = hs.shape[1] // LANES
    nct = ncol // HYENA_ORDER
    prow = FFT_R * FFT_PITCH
    const = lambda a: pl.BlockSpec(a.shape, lambda o, c: (0,) * a.ndim, pipeline_mode=pl.Buffered(1))
    return pl.pallas_call(
        _spectra_kernel,
        out_shape=jax.ShapeDtypeStruct((HYENA_ORDER, 4, seq, nct * LANES), BF16),
        grid=(HYENA_ORDER, nct),
        in_specs=[pl.BlockSpec((seq, LANES), lambda o, c: (0, o * nct + c)),
                  pl.BlockSpec((seq, LANES), lambda o, c: (0, o * nct + c)),
                  const(t1r), const(t2)],
        out_specs=pl.BlockSpec((1, 4, seq, LANES), lambda o, c: (o, 0, 0, c)),
        scratch_shapes=[pltpu.VMEM((prow, LANES), F32)] * 6,
        compiler_params=_cparams(("arbitrary", "arbitrary")),
        name="hyspectra",
    )(hs, hd, t1r, t2)


def _short_conv(x, w_ref, b_ref):
    rows = x.shape[0]
    sub = 8
    ri = lax.broadcasted_iota(jnp.int32, (sub, x.shape[1]), 0)
    prev = pltpu.roll(x, 1, 0)
    prev = jnp.concatenate([jnp.where(ri == 0, 0.0, prev[:sub]), prev[sub:]], axis=0)
    nxt = pltpu.roll(x, rows - 1, 0)
    nxt = jnp.concatenate([nxt[:rows - sub], jnp.where(ri == sub - 1, 0.0, nxt[rows - sub:])], axis=0)
    return prev * w_ref[0:1, :] + x * w_ref[1:2, :] + nxt * w_ref[2:3, :] + b_ref[...]


def _hyconv_kernel(z_ref, g_ref, cwz_ref, cbz_ref, cwg_ref, cbg_ref, skip_ref, h_ref,
                   t1_ref, t2_ref, t3_ref, t4_ref, o_ref,
                   xr_ref, xi_ref, per_ref, pei_ref, por_ref, poi_ref, qer_ref, qei_ref, qor_ref, qoi_ref,
                   *, conv_z):
    seq = z_ref.shape[0] // 2
    for half, dst in ((0, xr_ref), (1, xi_ref)):
        z = z_ref[half * seq:(half + 1) * seq, :].astype(F32)
        if conv_z:
            z = _short_conv(z, cwz_ref, cbz_ref)
        for j in range(FFT_R):
            dst[j * FFT_PITCH:j * FFT_PITCH + FFT_R, :] = z[j * FFT_R:(j + 1) * FFT_R]

    pp = ((per_ref, pei_ref), (por_ref, poi_ref))
    qq = ((qer_ref, qei_ref), (qor_ref, qoi_ref))
    yr_ref, yi_ref = qer_ref, qei_ref
    t1 = jnp.concatenate([t1_ref[0], t1_ref[1]], axis=0)
    t4 = jnp.concatenate([t4_ref[0], t4_ref[1]], axis=1)
    halves = (slice(0, LANES), slice(LANES, 2 * LANES))

    def first(r):
        o = jnp.dot(t1, _pair_bf16(xr_ref, xi_ref, _across(r), _across(r + 1)), preferred_element_type=F32)
        for par in (0, 1):
            for k, cols in zip((r, r + 1), halves):
                pp[par][0][_slab(k), :] = o[2 * par * FFT_R:(2 * par + 1) * FFT_R, cols]
                pp[par][1][_slab(k), :] = o[(2 * par + 1) * FFT_R:(2 * par + 2) * FFT_R, cols]
    _loop(first, step=2)

    def second(r):
        rows = pl.ds(pl.multiple_of(r * FFT_R, FFT_R), FFT_R)
        for par in (0, 1):
            o = jnp.dot(t2_ref[2 * r + par], _cat_bf16(pp[par][0][_across(r), :], pp[par][1][_across(r), :]),
                        preferred_element_type=F32)
            hr = h_ref[0, 2 * par, rows, :].astype(F32)
            hi = h_ref[0, 2 * par + 1, rows, :].astype(F32)
            ar, ai = o[:FFT_R], o[FFT_R:]
            qq[par][0][_slab(r), :] = ar * hr - ai * hi
            qq[par][1][_slab(r), :] = ar * hi + ai * hr
    _loop(second)

    def third(r):
        for par in (0, 1):
            o = jnp.dot(t3_ref[2 * r + par], _cat_bf16(qq[par][0][_slab(r), :], qq[par][1][_slab(r), :]),
                        preferred_element_type=F32)
            pp[par][0][_across(r), :] = o[:FFT_R]
            pp[par][1][_across(r), :] = o[FFT_R:]
    _loop(third)

    def fourth(r):
        quad = lambda k: jnp.concatenate([pp[0][0][_slab(k), :], pp[0][1][_slab(k), :],
                                          pp[1][0][_slab(k), :], pp[1][1][_slab(k), :]], axis=0).astype(BF16)
        o = jnp.dot(t4, jnp.concatenate([quad(r), quad(r + 1)], axis=1), preferred_element_type=F32)
        for k, cols in zip((r, r + 1), halves):
            yr_ref[_across(k), :] = o[:FFT_R, cols]
            yi_ref[_across(k), :] = o[FFT_R:, cols]
    _loop(fourth, step=2)

    skip = skip_ref[...]
    for half, (y_ref, x_ref) in enumerate(((yr_ref, xr_ref), (yi_ref, xi_ref))):
        g = _short_conv(g_ref[half * seq:(half + 1) * seq, :].astype(F32), cwg_ref, cbg_ref)
        for j in range(FFT_R):
            src = slice(j * FFT_PITCH, j * FFT_PITCH + FFT_R)
            dst = slice(half * seq + j * FFT_R, half * seq + (j + 1) * FFT_R)
            o_ref[dst, :] = (g[j * FFT_R:(j + 1) * FFT_R] * (y_ref[src, :] + skip * x_ref[src, :])).astype(o_ref.dtype)


def _hyconv(zsrc, zcol0, gcol0, proj, conv_w, conv_b, zpart, gpart, skip, spec, order, tabs, batch, seq, conv_z):
    t1, t2, t3, t4 = tabs
    nct = HYENA_WIDTH // LANES
    prow = FFT_R * FFT_PITCH
    const = lambda a: pl.BlockSpec(a.shape, lambda c, p: (0,) * a.ndim, pipeline_mode=pl.Buffered(1))
    return pl.pallas_call(
        functools.partial(_hyconv_kernel, conv_z=conv_z),
        out_shape=jax.ShapeDtypeStruct((batch * seq, HYENA_WIDTH), BF16),
        grid=(nct, batch // 2),
        in_specs=[
            pl.BlockSpec((2 * seq, LANES), lambda c, p: (p, zcol0 + c)),
            pl.BlockSpec((2 * seq, LANES), lambda c, p: (p, gcol0 + c)),
            pl.BlockSpec((3, LANES), lambda c, p: (0, zpart * nct + c)),
            pl.BlockSpec((1, LANES), lambda c, p: (0, zpart * nct + c)),
            pl.BlockSpec((3, LANES), lambda c, p: (0, gpart * nct + c)),
            pl.BlockSpec((1, LANES), lambda c, p: (0, gpart * nct + c)),
            pl.BlockSpec((1, LANES), lambda c, p: (0, c)),
            pl.BlockSpec((1, 4, seq, LANES), lambda c, p: (order, 0, 0, c), pipeline_mode=pl.Buffered(1)),
            const(t1), const(t2), const(t3), const(t4),
        ],
        out_specs=pl.BlockSpec((2 * seq, LANES), lambda c, p: (p, c)),
        scratch_shapes=[pltpu.VMEM((prow, LANES), F32)] * 10,
        compiler_params=_cparams(("arbitrary", "arbitrary")),
        name=f"hyconv{order}",
    )(zsrc, proj, conv_w, conv_b, conv_w, conv_b, skip[order:order + 1], spec, t1, t2, t3, t4)


def _hyena(proj, hycol0, conv_w, conv_b, w1, b1, w2, b2, w3, b3, w_out, freq, skip, batch, seq):
    hid = w2.shape[0]
    eye = jnp.eye(FILT_GROUPS, dtype=F32)
    bdiag = lambda a: jnp.kron(eye, a)
    tile = lambda a: jnp.tile(a[None], (1, FILT_GROUPS))
    bands = tile(jnp.linspace(1e-4, FILTER_BANDS - 1, FILTER_BANDS, dtype=F32))
    min_decay = math.log(DECAY_TARGET) / SLOW_DECAY_PCT
    max_decay = math.log(DECAY_TARGET) / FAST_DECAY_PCT
    deltas = jnp.abs(jnp.linspace(min_decay, max_decay, HYENA_WIDTH, dtype=F32))
    wo4 = w_out.reshape(hid, HYENA_ORDER, 2, HYENA_WIDTH)
    w1t = jnp.pad(w1[0:1], ((0, FILTER_BANDS - 1), (0, 0)))
    hs, hd = _filters(
        bands, bdiag(w1t), bdiag(w1[1:1 + FILTER_BANDS]), bdiag(w1[1 + FILTER_BANDS:]),
        tile(b1), bdiag(w2), tile(b2), bdiag(w3), tile(b3), tile(freq),
        wo4[:, :, 0].reshape(hid, -1), wo4[:, :, 1].reshape(hid, -1),
        jnp.tile(deltas, HYENA_ORDER)[None], seq)
    t1, t2, t3, t4, t1r = _dft_tables(seq)
    spec = _spectra(hs, hd, t1r, t2, seq)
    c0 = hycol0 // LANES
    nct = HYENA_WIDTH // LANES
    cb = conv_b[None]
    z1 = _hyconv(proj, c0, c0 + nct, proj, conv_w, cb, 0, 1, skip, spec, 0, (t1, t2, t3, t4), batch, seq, True)
    return _hyconv(z1, 0, c0 + 2 * nct, proj, conv_w, cb, 0, 2, skip, spec, 1, (t1, t2, t3, t4), batch, seq, False)


def _rope_tables(seq):
    pos = np.arange(seq, dtype=np.float32)
    inv_freq = (ROPE_THETA ** (-np.arange(0, ROPE_DIM, 2, dtype=np.float32) / ROPE_DIM)).astype(np.float32)
    ang = pos[:, None] * inv_freq[None, :]
    cos, sin = np.cos(ang).astype(np.float32), np.sin(ang).astype(np.float32)
    half = ROPE_DIM // 2
    c = np.ones((seq, QK_DIM), np.float32)
    s = np.zeros((seq, QK_DIM), np.float32)
    c[:, :half] = cos
    c[:, half:ROPE_DIM] = cos
    s[:, :half] = -sin
    s[:, half:ROPE_DIM] = sin
    perm = np.zeros((LANES, LANES), np.float32)
    for j in range(LANES):
        if j % QK_DIM < half:
            perm[j + half, j] = 1.0
        elif j % QK_DIM < ROPE_DIM:
            perm[j - half, j] = 1.0
    rep = LANES // QK_DIM
    return (jnp.asarray(np.tile(c, (1, rep))), jnp.asarray(np.tile(s, (1, rep))), jnp.asarray(perm).astype(BF16))


def kernel(x, norm1_g, w_in, short_conv_w, short_conv_b, q_norm_g, k_norm_g, lambda_q1, lambda_k1, lambda_q2, lambda_k2, subln_g, filt_w1, filt_b1, filt_w2, filt_b2, filt_w3, filt_b3, filt_w_out, filt_freq, hyena_skip, w_branch_attn, w_branch_hyena, w_out, norm2_g, w_router, w_gate, w_up, w_down):
    b, l, d = x.shape
    depth = w_in.shape[0]
    n = b * l
    cap = CAPACITY_FACTOR * l // N_EXPERTS
    q_cols = ATTN_HEADS * 2 * QK_DIM
    ctab, stab, perm = _rope_tables(l)
    xc = x.reshape(n, d)
    for li in range(depth):
        lambda_init = 0.8 - 0.6 * math.exp(-0.3 * li)
        proj = _inproj(xc, norm1_g[li][None], w_in[li].astype(BF16))
        scale = math.log2(math.e) / math.sqrt(QK_DIM)
        gqk = jnp.concatenate([jnp.tile(q_norm_g[li], q_cols // QK_DIM) * scale,
                               jnp.tile(k_norm_g[li], q_cols // QK_DIM)])[None]
        qk = _qkprep(proj, ctab, stab, perm, gqk, l)
        lamv = jnp.stack([lambda_q1[li], lambda_k1[li], lambda_q2[li], lambda_k2[li]])
        attn = _attention(qk, proj, lamv, subln_g[li][None], b, l, lambda_init)

        hyena = _hyena(proj, 3 * q_cols, short_conv_w[li], short_conv_b[li], filt_w1[li], filt_b1[li],
                       filt_w2[li], filt_b2[li], filt_w3[li], filt_b3[li], filt_w_out[li], filt_freq[li],
                       hyena_skip[li], b, l)

        wr_pad = jnp.pad(w_router[li], ((0, 0), (0, LANES - N_EXPERTS))).astype(BF16)
        x1, u2, aff = _merge(attn, hyena, proj, xc, w_branch_attn[li].astype(BF16),
                             w_branch_hyena[li].astype(BF16), w_out[li].astype(BF16),
                             norm2_g[li][None], wr_pad)
        pos, post, afft, offs = _topk(aff, b, l, cap)
        xin = _gather(offs, post, u2, b, l, cap)
        weo = _experts(offs, post, afft, xin, w_gate, w_up, w_down, li, b, l, cap)
        xc = _combine(offs, pos, x1, weo, b, l, cap)
    return xc.reshape(b, l, d)
```

```python
import functools
import math

import jax
import jax.numpy as jnp
import numpy as np
from jax import lax
from jax.experimental import pallas as pl
from jax.experimental.pallas import tpu as pltpu

F32 = jnp.float32
BF16 = jnp.bfloat16

ATTN_HEADS = 4
QK_DIM = 64
V_DIM = 128
ROPE_DIM = 16
ROPE_THETA = 500000.0
HYENA_WIDTH = 512
HYENA_ORDER = 2
FILTER_BANDS = 16
DECAY_TARGET = 1e-2
FAST_DECAY_PCT = 0.3
SLOW_DECAY_PCT = 1.5
N_EXPERTS = 16
CAPACITY_FACTOR = 2
MOE_TILE = 512
EPS = 1e-6
LANES = 128
FILT_GROUPS = LANES // FILTER_BANDS
VMEM_LIMIT = 56 * 1024 * 1024

INPROJ_ROWS, INPROJ_COLS = 2048, 1280
QKPREP_ROWS = 2048
ATTN_Q_ROWS, ATTN_KEY_CHUNK = 1024, 256
MERGE_ROWS = 1024
TOPK_CUMSUM_ROWS = 256
GATHER_WINDOW = 96
AFFINITY_WINDOW = 128
COMBINE_WINDOW = 128
FILTER_ROWS = 1024


def _cparams(sem):
    return pltpu.CompilerParams(dimension_semantics=sem, vmem_limit_bytes=VMEM_LIMIT)


def _inproj_kernel(x_ref, g_ref, w_ref, o_ref, u_ref):
    j = pl.program_id(1)

    @pl.when(j == 0)
    def _():
        x = x_ref[...]
        ms = jnp.mean(x * x, axis=-1, keepdims=True)
        u = (x * lax.rsqrt(ms + EPS) * g_ref[...]).astype(BF16)
        u_ref[...] = u
        o_ref[...] = jnp.dot(u, w_ref[...], preferred_element_type=F32).astype(o_ref.dtype)

    @pl.when(j > 0)
    def _():
        o_ref[...] = jnp.dot(u_ref[...], w_ref[...], preferred_element_type=F32).astype(o_ref.dtype)


def _inproj(x2, g, w_bf16, tm=INPROJ_ROWS, tn=INPROJ_COLS):
    n, d = x2.shape
    cols = w_bf16.shape[1]
    return pl.pallas_call(
        _inproj_kernel,
        out_shape=jax.ShapeDtypeStruct((n, cols), BF16),
        grid=(n // tm, cols // tn),
        in_specs=[
            pl.BlockSpec((tm, d), lambda i, j: (i, 0)),
            pl.BlockSpec((1, d), lambda i, j: (0, 0)),
            pl.BlockSpec((d, tn), lambda i, j: (0, j)),
        ],
        out_specs=pl.BlockSpec((tm, tn), lambda i, j: (i, j)),
        scratch_shapes=[pltpu.VMEM((tm, d), BF16)],
        compiler_params=_cparams(("arbitrary", "arbitrary")),
        name="inproj",
    )(x2, g, w_bf16)


def _qkprep_kernel(p_ref, c_ref, s_ref, perm_ref, g_ref, o_ref):
    lane = lax.broadcasted_iota(jnp.int32, (1, LANES), 1)
    lo = lane < QK_DIM
    c = c_ref[...]
    s = s_ref[...]
    perm = perm_ref[...]
    for j in range(p_ref.shape[1] // LANES):
        t = p_ref[:, j * LANES:(j + 1) * LANES].astype(F32)
        sq = t * t
        ss_lo = jnp.sum(jnp.where(lo, sq, 0.0), axis=-1, keepdims=True)
        ss_hi = jnp.sum(jnp.where(lo, 0.0, sq), axis=-1, keepdims=True)
        r = lax.rsqrt(jnp.where(lo, ss_lo, ss_hi) * (1.0 / QK_DIM) + EPS)
        y = t * r * g_ref[:, j * LANES:(j + 1) * LANES]
        y_hi = y.astype(BF16)
        y_lo = (y - y_hi.astype(F32)).astype(BF16)
        partner = (jnp.dot(y_hi, perm, preferred_element_type=F32)
                   + jnp.dot(y_lo, perm, preferred_element_type=F32))
        o_ref[:, j * LANES:(j + 1) * LANES] = (y * c + partner * s).astype(o_ref.dtype)


def _qkprep(proj, ctab, stab, perm, gqk, seq, tm=QKPREP_ROWS):
    n = proj.shape[0]
    w = gqk.shape[1]
    nb = seq // tm
    return pl.pallas_call(
        _qkprep_kernel,
        out_shape=jax.ShapeDtypeStruct((n, w), BF16),
        grid=(n // tm,),
        in_specs=[
            pl.BlockSpec((tm, w), lambda i: (i, 0)),
            pl.BlockSpec((tm, LANES), lambda i: (i % nb, 0)),
            pl.BlockSpec((tm, LANES), lambda i: (i % nb, 0)),
            pl.BlockSpec((LANES, LANES), lambda i: (0, 0)),
            pl.BlockSpec((1, w), lambda i: (0, 0)),
        ],
        out_specs=pl.BlockSpec((tm, w), lambda i: (i, 0)),
        compiler_params=_cparams(("arbitrary",)),
        name="qkprep",
    )(proj, ctab, stab, perm, gqk)


def _attn_kernel(q_ref, k_ref, v_ref, lam_ref, g_ref, o_ref, *, tk, unroll, lambda_init):
    tq = q_ref.shape[0]
    seq = k_ref.shape[0]
    lane = lax.broadcasted_iota(jnp.int32, (1, LANES), 1)
    q = q_ref[...]
    zero = jnp.zeros_like(q)
    qs = jnp.concatenate([jnp.where(lane < QK_DIM, q, zero), jnp.where(lane < QK_DIM, zero, q)], axis=0)
    lv = lam_ref[...]
    lam = (jnp.exp(jnp.sum(lv[0:1] * lv[1:2], axis=-1, keepdims=True))
           - jnp.exp(jnp.sum(lv[2:3] * lv[3:4], axis=-1, keepdims=True)) + lambda_init)

    def step(c, carry):
        m, l, a = carry
        off = pl.multiple_of(c * tk, tk)
        kc = k_ref[pl.ds(off, tk), :]
        vc = v_ref[pl.ds(off, tk), :]
        s = lax.dot_general(qs, kc, (((1,), (1,)), ((), ())), preferred_element_type=F32)
        mn = jnp.maximum(m, jnp.max(s, axis=-1, keepdims=True))
        p = jnp.exp2((s - mn).astype(BF16))
        al = jnp.exp2(m - mn)
        pf = p.astype(F32)
        part = pf[:, 0:LANES]
        for j in range(1, tk // LANES):
            part = part + pf[:, j * LANES:(j + 1) * LANES]
        l = al * l + part
        a = al * a + jnp.dot(p, vc, preferred_element_type=F32)
        return mn, l, a

    init = (jnp.full((2 * tq, 1), -jnp.inf, F32), jnp.zeros((2 * tq, LANES), F32), jnp.zeros((2 * tq, V_DIM), F32))
    _, l, a = lax.fori_loop(0, seq // tk, step, init, unroll=unroll)
    a = a / jnp.sum(l, axis=-1, keepdims=True)
    o = a[:tq] - lam * a[tq:]
    ms = jnp.mean(o * o, axis=-1, keepdims=True)
    o = o * lax.rsqrt(ms + EPS) * g_ref[...] * (1.0 - lambda_init)
    o_ref[...] = o.astype(o_ref.dtype)


def _attention(qk, proj, lamv, subln_g, batch, seq, lambda_init, tq=ATTN_Q_ROWS, tk=ATTN_KEY_CHUNK):
    nq = seq // tq
    vcol0 = (2 * ATTN_HEADS * 2 * QK_DIM) // V_DIM
    return pl.pallas_call(
        functools.partial(_attn_kernel, tk=tk, unroll=seq // tk, lambda_init=lambda_init),
        out_shape=jax.ShapeDtypeStruct((batch * seq, ATTN_HEADS * V_DIM), BF16),
        grid=(batch, ATTN_HEADS, nq),
        in_specs=[
            pl.BlockSpec((tq, LANES), lambda b, h, i: (b * nq + i, h)),
            pl.BlockSpec((seq, LANES), lambda b, h, i: (b, ATTN_HEADS + h)),
            pl.BlockSpec((seq, V_DIM), lambda b, h, i: (b, vcol0 + h)),
            pl.BlockSpec((4, QK_DIM), lambda b, h, i: (0, 0)),
            pl.BlockSpec((1, V_DIM), lambda b, h, i: (0, 0)),
        ],
        out_specs=pl.BlockSpec((tq, V_DIM), lambda b, h, i: (b * nq + i, h)),
        compiler_params=_cparams(("arbitrary", "arbitrary", "arbitrary")),
        name="diffattn",
    )(qk, qk, proj, lamv, subln_g)


def _merge_kernel(at_ref, hy_ref, ga_ref, gh_ref, x_ref, wpa_ref, wph_ref, wo_ref, g2_ref, wr_ref,
                  x1_ref, u2_ref, aff_ref):
    tm = x_ref.shape[0]
    parts = 2
    for rows in (slice(k * tm // parts, (k + 1) * tm // parts) for k in range(parts)):
        ga = jax.nn.sigmoid(ga_ref[rows, :].astype(F32))
        gh = jax.nn.sigmoid(gh_ref[rows, :].astype(F32))
        merged = (ga * jnp.dot(at_ref[rows, :], wpa_ref[...], preferred_element_type=F32)
                  + gh * jnp.dot(hy_ref[rows, :], wph_ref[...], preferred_element_type=F32))
        x1 = x_ref[rows, :] + jnp.dot(merged.astype(BF16), wo_ref[...], preferred_element_type=F32)
        x1_ref[rows, :] = x1
        ms = jnp.mean(x1 * x1, axis=-1, keepdims=True)
        u2 = (x1 * lax.rsqrt(ms + EPS) * g2_ref[...]).astype(BF16)
        u2_ref[rows, :] = u2
        logits = jnp.dot(u2, wr_ref[...], preferred_element_type=F32)
        lane = lax.broadcasted_iota(jnp.int32, logits.shape, 1)
        logits = jnp.where(lane < N_EXPERTS, logits, -jnp.inf)
        e = jnp.exp(logits - jnp.max(logits, axis=-1, keepdims=True))
        aff_ref[rows, :] = e / jnp.sum(e, axis=-1, keepdims=True)


def _merge(attn, hyena, proj, x2, wpa, wph, wo, g2, wr_pad, tm=MERGE_ROWS):
    n, d = x2.shape
    gcol0 = (proj.shape[1] - 2 * d) // d
    full = lambda shape: pl.BlockSpec(shape, lambda i: (0, 0))
    return pl.pallas_call(
        _merge_kernel,
        out_shape=(jax.ShapeDtypeStruct((n, d), F32), jax.ShapeDtypeStruct((n, d), BF16),
                   jax.ShapeDtypeStruct((n, LANES), F32)),
        grid=(n // tm,),
        in_specs=[
            pl.BlockSpec((tm, attn.shape[1]), lambda i: (i, 0)),
            pl.BlockSpec((tm, hyena.shape[1]), lambda i: (i, 0)),
            pl.BlockSpec((tm, d), lambda i: (i, gcol0)),
            pl.BlockSpec((tm, d), lambda i: (i, gcol0 + 1)),
            pl.BlockSpec((tm, d), lambda i: (i, 0)),
            full(wpa.shape), full(wph.shape), full(wo.shape), full(g2.shape), full(wr_pad.shape),
        ],
        out_specs=(pl.BlockSpec((tm, d), lambda i: (i, 0)), pl.BlockSpec((tm, d), lambda i: (i, 0)),
                   pl.BlockSpec((tm, LANES), lambda i: (i, 0))),
        compiler_params=_cparams(("arbitrary",)),
        name="merge",
    )(attn, hyena, proj, proj, x2, wpa, wph, wo, g2, wr_pad)


def _excl_cumsum_rows(mask_f32, tri, blk):
    rows = mask_f32.shape[0]
    carry = jnp.zeros((1, LANES), F32)
    outs = []
    for r in range(rows // blk):
        mb = mask_f32[r * blk:(r + 1) * blk]
        outs.append(jnp.dot(tri, mb.astype(BF16), preferred_element_type=F32) + carry)
        carry = carry + jnp.sum(mb, axis=0, keepdims=True)
    return jnp.concatenate(outs, axis=0)


def _topk_kernel(aff_ref, pos_ref, post_ref, afft_ref, offs_ref, *, cap, blk, batch):
    seq = aff_ref.shape[0] // batch
    aff = aff_ref[0:seq]
    for b in range(1, batch):
        aff = aff + pltpu.roll(aff_ref[b * seq:(b + 1) * seq], b * N_EXPERTS, 1)

    def search(i, prefix):
        cand = prefix | jnp.left_shift(jnp.int32(1), 29 - i)
        cnt = jnp.sum((aff >= pltpu.bitcast(cand, F32)[0:1]).astype(F32), axis=0, keepdims=True)
        return jnp.where(cnt >= cap, cand, prefix)

    lo = pltpu.bitcast(lax.fori_loop(0, 30, search, jnp.zeros((8, LANES), jnp.int32)), F32)[0:1]
    thr = jnp.min(jnp.where(aff >= lo, aff, jnp.inf), axis=0, keepdims=True)
    gt = (aff > thr).astype(F32)
    eq = (aff == thr).astype(F32)
    need = cap - jnp.sum(gt, axis=0, keepdims=True)
    ri = lax.broadcasted_iota(jnp.int32, (blk, blk), 0)
    ci = lax.broadcasted_iota(jnp.int32, (blk, blk), 1)
    tri = (ci < ri).astype(BF16)
    sel = gt + eq * (_excl_cumsum_rows(eq, tri, blk) < need).astype(F32)
    before = _excl_cumsum_rows(sel, tri, blk)
    pos = jnp.where(sel > 0.0, before, -1.0)
    pos_t = pos.T
    aff_t = aff.T
    lane = lax.broadcasted_iota(jnp.int32, (1, LANES), 1)
    for b in range(batch):
        mine = pos if b == 0 else pltpu.roll(pos, LANES - b * N_EXPERTS, 1)
        pos_ref[b * seq:(b + 1) * seq] = jnp.where(lane < N_EXPERTS, mine, -1.0)
        post_ref[b] = pos_t[b * N_EXPERTS:(b + 1) * N_EXPERTS]
        afft_ref[b] = aff_t[b * N_EXPERTS:(b + 1) * N_EXPERTS]
    offs_ref[...] = jnp.concatenate([before[j * MOE_TILE:j * MOE_TILE + 1] for j in range(offs_ref.shape[0])], axis=0)


def _topk(aff, batch, seq, cap, blk=TOPK_CUMSUM_ROWS):
    assert batch * N_EXPERTS <= LANES
    nt = seq // MOE_TILE
    pos, post, afft, offs = pl.pallas_call(
        functools.partial(_topk_kernel, cap=cap, blk=blk, batch=batch),
        out_shape=(jax.ShapeDtypeStruct((batch * seq, LANES), F32),
                   jax.ShapeDtypeStruct((batch, N_EXPERTS, seq), F32),
                   jax.ShapeDtypeStruct((batch, N_EXPERTS, seq), F32),
                   jax.ShapeDtypeStruct((nt, LANES), F32)),
        compiler_params=pltpu.CompilerParams(vmem_limit_bytes=VMEM_LIMIT),
        name="topk",
    )(aff)
    offs = offs[:, :batch * N_EXPERTS].reshape(nt, batch, N_EXPERTS).transpose(1, 0, 2)
    return pos, post, afft, offs.astype(jnp.int32).reshape(-1)


def _align_down(x, m):
    sh = m.bit_length() - 1
    return lax.shift_left(lax.shift_right_logical(x, sh), sh)


def _ceil_div(x, m, x_max):
    if m & (m - 1) == 0:
        return lax.shift_right_logical(x + (m - 1), m.bit_length() - 1)
    n = jnp.int32(0)
    for k in range(-(-x_max // m)):
        n = n + (x > k * m).astype(jnp.int32)
    return n


def _gather_kernel(offs_ref, post_ref, u_ref, x_ref, *, cap, win):
    b = pl.program_id(0)
    c = pl.program_id(1)
    nt = pl.num_programs(1)

    @pl.when(c == 0)
    def _():
        x_ref[...] = jnp.zeros_like(x_ref)

    uc = u_ref[...]
    slot0 = lax.broadcasted_iota(jnp.int32, (win, MOE_TILE), 0).astype(F32)
    base = (b * nt + c) * N_EXPERTS
    last = pl.num_programs(0) * nt * N_EXPERTS - 1
    starts, hots = [], []
    for e in range(N_EXPERTS):
        s = pl.multiple_of(jnp.minimum(_align_down(offs_ref[base + e], 16), cap - win), 16)
        hots.append(jnp.where((slot0 + s.astype(F32)) == post_ref[0, e:e + 1, :], 1.0, 0.0).astype(BF16))
        starts.append(s)
    rows = jnp.dot(jnp.concatenate(hots, axis=0), uc, preferred_element_type=F32)
    for e in range(N_EXPERTS):
        x_ref[0, e, pl.ds(starts[e], win), :] += rows[e * win:(e + 1) * win].astype(BF16)
    for e in range(N_EXPERTS):
        lo0 = _align_down(offs_ref[base + e], 16)
        end = jnp.where(c + 1 < nt, offs_ref[jnp.minimum(base + N_EXPERTS + e, last)], cap)

        def more(k, carry, lo0=lo0, e=e):
            lo = lo0 + k * win
            s = pl.multiple_of(jnp.minimum(lo, cap - win), 16)
            srow = slot0 + s.astype(F32)
            hit = jnp.logical_and(srow == post_ref[0, e:e + 1, :], srow >= lo.astype(F32))
            x_ref[0, e, pl.ds(s, win), :] += jnp.dot(jnp.where(hit, 1.0, 0.0).astype(BF16), uc,
                                                     preferred_element_type=F32).astype(BF16)
            return carry

        lax.fori_loop(1, _ceil_div(end - lo0, win, cap + 16), more, 0)


def _gather(offs, post, u2, batch, seq, cap, win=GATHER_WINDOW):
    d = u2.shape[1]
    nt = seq // MOE_TILE
    grid_spec = pltpu.PrefetchScalarGridSpec(
        num_scalar_prefetch=1,
        grid=(batch, nt),
        in_specs=[
            pl.BlockSpec((1, N_EXPERTS, MOE_TILE), lambda b, c, o: (b, 0, c)),
            pl.BlockSpec((MOE_TILE, d), lambda b, c, o: (b * nt + c, 0)),
        ],
        out_specs=pl.BlockSpec((1, N_EXPERTS, cap, d), lambda b, c, o: (b, 0, 0, 0)),
    )
    return pl.pallas_call(
        functools.partial(_gather_kernel, cap=cap, win=win),
        out_shape=jax.ShapeDtypeStruct((batch, N_EXPERTS, cap, d), BF16),
        grid_spec=grid_spec,
        compiler_params=_cparams(("arbitrary", "arbitrary")),
        name="gather",
    )(offs, post, u2)


def _expert_kernel(offs_ref, post_ref, afft_ref, x_ref, wg_ref, wu_ref, wd_ref, o_ref, gs_ref, w_ref, eo_ref,
                   *, cap, win):
    e = pl.program_id(0)
    b = pl.program_id(1)

    def cast():
        for m, src in enumerate((wg_ref, wu_ref, wd_ref)):
            part = src.shape[2]
            w_ref[jnp.bitwise_and(e, 1), m, pl.ds(pl.multiple_of(b * part, part), part), :] = src[0, 0].astype(BF16)

    def run():
        _expert_body(offs_ref, post_ref, afft_ref, x_ref, w_ref.at[1 - jnp.bitwise_and(e, 1)], o_ref, gs_ref, eo_ref,
                     e - 1, b, cap, win)

    @pl.when(e == 0)
    def _():
        cast()
        o_ref[...] = jnp.zeros_like(o_ref)

    @pl.when(jnp.logical_and(e > 0, e < N_EXPERTS))
    def _():
        cast()
        run()

    @pl.when(e == N_EXPERTS)
    def _():
        run()


def _expert_body(offs_ref, post_ref, afft_ref, x_ref, w_ref, o_ref, gs_ref, eo_ref, e, b, cap, win):
    seq = post_ref.shape[2]
    nt = seq // MOE_TILE
    gs_ref[...] = jnp.zeros_like(gs_ref)
    slot0 = lax.broadcasted_iota(jnp.int32, (win, MOE_TILE), 0).astype(F32)
    windows = []
    for c in range(nt):
        base = (b * nt + c) * N_EXPERTS + e
        lo0 = _align_down(offs_ref[base], 8)
        end = offs_ref[base + N_EXPERTS] if c + 1 < nt else cap
        prow = post_ref[0, pl.ds(e, 1), c * MOE_TILE:(c + 1) * MOE_TILE]
        arow = afft_ref[0, pl.ds(e, 1), c * MOE_TILE:(c + 1) * MOE_TILE]

        def window(k, carry, lo0=lo0, prow=prow, arow=arow):
            lo = lo0 + k * win
            s = pl.multiple_of(jnp.minimum(lo, cap - win), 8)
            srow = slot0 + s.astype(F32)
            hit = jnp.logical_and(srow == prow, srow >= lo.astype(F32))
            gs_ref[pl.ds(s, win), :] += jnp.sum(jnp.where(hit, arow, 0.0), axis=-1, keepdims=True)
            return carry

        window(0, 0)
        windows.append((lo0, end, window))
    xb = x_ref[0, 0]
    hg = jnp.dot(xb, w_ref[0], preferred_element_type=F32)
    hu = jnp.dot(xb, w_ref[1], preferred_element_type=F32)
    act = (hg * jax.nn.sigmoid(hg) * hu).astype(BF16)
    eo_ref[...] = jnp.dot(act, w_ref[2], preferred_element_type=F32)
    for lo0, end, window in windows:
        lax.fori_loop(1, _ceil_div(end - lo0, win, cap + 16), window, 0)
    o_ref[0, 0] = (eo_ref[...] * gs_ref[...]).astype(o_ref.dtype)


def _experts(offs, post, afft, xin, wg, wu, wd, layer, batch, seq, cap, win=AFFINITY_WINDOW):
    d = xin.shape[3]
    assert wg.shape[2:] == (d, d) and wd.shape[2:] == (d, d) and d % (16 * batch) == 0
    part = d // batch
    wspec = pl.BlockSpec((1, 1, part, d), lambda e, b, o: (layer, jnp.minimum(e, N_EXPERTS - 1), b, 0))
    grid_spec = pltpu.PrefetchScalarGridSpec(
        num_scalar_prefetch=1,
        grid=(N_EXPERTS + 1, batch),
        in_specs=[
            pl.BlockSpec((1, N_EXPERTS, seq), lambda e, b, o: (b, 0, 0)),
            pl.BlockSpec((1, N_EXPERTS, seq), lambda e, b, o: (b, 0, 0)),
            pl.BlockSpec((1, 1, cap, d), lambda e, b, o: (b, jnp.maximum(e - 1, 0), 0, 0)),
            wspec, wspec, wspec,
        ],
        out_specs=pl.BlockSpec((1, 1, cap, d), lambda e, b, o: (b, jnp.where(e == 0, N_EXPERTS, e - 1), 0, 0)),
        scratch_shapes=[pltpu.VMEM((cap, 1), F32), pltpu.VMEM((2, 3, d, d), BF16), pltpu.VMEM((cap, d), F32)],
    )
    return pl.pallas_call(
        functools.partial(_expert_kernel, cap=cap, win=win),
        out_shape=jax.ShapeDtypeStruct((batch, N_EXPERTS + 1, cap, d), BF16),
        grid_spec=grid_spec,
        compiler_params=_cparams(("arbitrary", "arbitrary")),
        name="experts",
    )(offs, post, afft, xin, wg, wu, wd)


def _combine_kernel(offs_ref, pos_ref, x1_ref, weo_ref, o_ref, *, cap, win):
    b = pl.program_id(0)
    t = pl.program_id(1)
    nt = pl.num_programs(1)
    tt = pos_ref.shape[0]
    pos = pos_ref[...]
    slot0 = lax.broadcasted_iota(jnp.int32, (tt, win), 1).astype(F32)
    base = (b * nt + t) * N_EXPERTS
    last = pl.num_programs(0) * nt * N_EXPERTS - 1
    acc = x1_ref[...]
    for e0 in range(0, N_EXPERTS, 2):
        hots, rows = [], []
        for e in (e0, e0 + 1):
            s = pl.multiple_of(jnp.minimum(_align_down(offs_ref[base + e], 16), cap - win), 16)
            hots.append(jnp.where((slot0 + s.astype(F32)) == pos[:, e:e + 1], 1.0, 0.0).astype(BF16))
            rows.append(weo_ref[0, e, pl.ds(s, win), :])
        acc = acc + jnp.dot(jnp.concatenate(hots, axis=1), jnp.concatenate(rows, axis=0),
                            preferred_element_type=F32)
    o_ref[...] = acc
    for e in range(N_EXPERTS):
        lo0 = _align_down(offs_ref[base + e], 16)
        end = jnp.where(t + 1 < nt, offs_ref[jnp.minimum(base + N_EXPERTS + e, last)], cap)

        def window(k, carry, lo0=lo0, e=e):
            lo = lo0 + k * win
            s = pl.multiple_of(jnp.minimum(lo, cap - win), 16)
            srow = slot0 + s.astype(F32)
            hit = jnp.logical_and(srow == pos[:, e:e + 1], srow >= lo.astype(F32))
            o_ref[...] += jnp.dot(jnp.where(hit, 1.0, 0.0).astype(BF16), weo_ref[0, e, pl.ds(s, win), :],
                                  preferred_element_type=F32)
            return carry

        lax.fori_loop(1, _ceil_div(end - lo0, win, cap + 16), window, 0)


def _combine(offs, pos, x1, weo, batch, seq, cap, win=COMBINE_WINDOW):
    d = x1.shape[1]
    tt = MOE_TILE
    nt = seq // tt
    grid_spec = pltpu.PrefetchScalarGridSpec(
        num_scalar_prefetch=1,
        grid=(batch, nt),
        in_specs=[
            pl.BlockSpec((tt, LANES), lambda b, t, o: (b * nt + t, 0)),
            pl.BlockSpec((tt, d), lambda b, t, o: (b * nt + t, 0)),
            pl.BlockSpec((1, N_EXPERTS, cap, d), lambda b, t, o: (b, 0, 0, 0)),
        ],
        out_specs=pl.BlockSpec((tt, d), lambda b, t, o: (b * nt + t, 0)),
    )
    return pl.pallas_call(
        functools.partial(_combine_kernel, cap=cap, win=win),
        out_shape=jax.ShapeDtypeStruct(x1.shape, F32),
        grid_spec=grid_spec,
        compiler_params=_cparams(("arbitrary", "arbitrary")),
        name="combine",
    )(offs, pos, x1, weo)


FFT_R = 64
FFT_PITCH = FFT_R + 8


def _dft_tables(seq):
    r = FFT_R
    assert seq == r * r
    n = np.arange(r)
    f = np.exp(-2j * np.pi * np.outer(n, n) / r)
    w = lambda e: np.exp(-2j * np.pi * e / (2 * seq))

    def real_rep(a):
        return np.block([[a.real, -a.imag], [a.imag, a.real]]).astype(np.float32)

    m1 = [f, f * w(r * n)[None, :]]
    m2 = [f * w(n * m)[None, :] for m in range(2 * r)]
    m4 = [f.conj() / (2 * seq), w(-r * n)[:, None] * f.conj() / (2 * seq)]
    t1 = np.stack([real_rep(a) for a in m1])
    t2 = np.stack([real_rep(a) for a in m2])
    t3 = np.stack([real_rep(a.conj().T) for a in m2])
    t4 = np.stack([real_rep(a) for a in m4])
    t1r = np.stack([np.concatenate([a.real, a.imag], axis=0).astype(np.float32) for a in m1])
    return tuple(jnp.asarray(t).astype(BF16) for t in (t1, t2, t3, t4, t1r))


def _slab(r):
    return pl.ds(pl.multiple_of(r * FFT_PITCH, 8), FFT_R)


def _across(r):
    return pl.ds(r, FFT_R, stride=FFT_PITCH)


def _cat_bf16(a, b):
    return jnp.concatenate([a, b], axis=0).astype(BF16)


FFT_UNROLL = 32


def _loop(body, step=1, rows_per_trip=FFT_UNROLL):
    def wrapped(i, c):
        body(i * step)
        return c
    lax.fori_loop(0, FFT_R // step, wrapped, 0, unroll=rows_per_trip // step)


def _pair_bf16(re_ref, im_ref, idx0, idx1):
    return jnp.concatenate([_cat_bf16(re_ref[idx0, :], im_ref[idx0, :]),
                            _cat_bf16(re_ref[idx1, :], im_ref[idx1, :])], axis=1)


def _bf16_pieces(w):
    hi = w.astype(BF16)
    return hi, (w - hi.astype(F32)).astype(BF16)


def _dot3(a, w_ref):
    a_hi, a_lo = _bf16_pieces(a)
    return (jnp.dot(a_hi, w_ref[0], preferred_element_type=F32)
            + (jnp.dot(a_hi, w_ref[1], preferred_element_type=F32)
               + jnp.dot(a_lo, w_ref[0], preferred_element_type=F32)))


def _filter_kernel(bands_ref, w1t_ref, w1c_ref, w1s_ref, b1_ref, w2_ref, b2_ref, w3_ref, b3_ref, fr_ref,
                   wf_ref, wb_ref, dl_ref, hs_ref, hd_ref, *, seq):
    tm = hs_ref.shape[0]
    rows = tm // FILT_GROUPS
    hid = wf_ref.shape[1]
    base = pl.program_id(0) * tm
    lane = lax.broadcasted_iota(jnp.int32, (rows, LANES), 1)
    grp = lax.shift_right_logical(lane, FILTER_BANDS.bit_length() - 1)
    pos = (base + grp * rows + lax.broadcasted_iota(jnp.int32, (rows, LANES), 0)).astype(F32)
    first = jnp.bitwise_and(lane, FILTER_BANDS - 1) == 0
    fr = fr_ref[...]
    tscale = 1.0 / max(seq - 1, 1)

    def mlp(p):
        ang = ((2.0 * math.pi / seq) * p) * bands_ref[...]
        pre = (_dot3(jnp.cos(ang), w1c_ref) + _dot3(-jnp.sin(ang), w1s_ref)
               + _dot3(jnp.where(first, p * tscale, 0.0), w1t_ref) + b1_ref[...])
        h = jnp.sin(fr * pre)
        h = jnp.sin(fr * (_dot3(h, w2_ref) + b2_ref[...]))
        return jnp.sin(fr * (_dot3(h, w3_ref) + b3_ref[...]))

    hf_h = mlp(pos)
    hb_h = mlp(seq - pos)
    dl = dl_ref[...]
    for g in range(FILT_GROUPS):
        pcol = (base + g * rows + lax.broadcasted_iota(jnp.int32, (rows, 1), 0)).astype(F32)
        hf = _dot3(hf_h[:, g * hid:(g + 1) * hid], wf_ref) * jnp.exp(-(pcol * tscale) * dl)
        hb = _dot3(hb_h[:, g * hid:(g + 1) * hid], wb_ref) * jnp.exp(-((seq - pcol) * tscale) * dl)
        hb = jnp.where(pcol > 0.0, hb, 0.0)
        hs_ref[g * rows:(g + 1) * rows, :] = (hf + hb).astype(hs_ref.dtype)
        hd_ref[g * rows:(g + 1) * rows, :] = (hf - hb).astype(hd_ref.dtype)


def _filters(bands, w1t, w1c, w1s, b1, w2, b2, w3, b3, fr, wf, wb, dl, seq, tm=FILTER_ROWS):
    cols = wf.shape[1]
    pieces = lambda w: jnp.stack(_bf16_pieces(w))
    args = (bands, pieces(w1t), pieces(w1c), pieces(w1s), b1, pieces(w2), b2, pieces(w3), b3, fr,
            pieces(wf), pieces(wb), dl)
    return pl.pallas_call(
        functools.partial(_filter_kernel, seq=seq),
        out_shape=(jax.ShapeDtypeStruct((seq, cols), BF16), jax.ShapeDtypeStruct((seq, cols), BF16)),
        grid=(seq // tm,),
        in_specs=[pl.BlockSpec(a.shape, lambda i, nd=a.ndim: (0,) * nd) for a in args],
        out_specs=(pl.BlockSpec((tm, cols), lambda i: (i, 0)), pl.BlockSpec((tm, cols), lambda i: (i, 0))),
        compiler_params=_cparams(("arbitrary",)),
        name="hyfilter",
    )(*args)


def _spectra_kernel(hs_ref, hd_ref, t1r_ref, t2_ref, h_ref, xe_ref, xo_ref, per_ref, pei_ref, por_ref, poi_ref):
    xx = (xe_ref, xo_ref)
    pp = ((per_ref, pei_ref), (por_ref, poi_ref))

    def fill(r):
        rows = pl.ds(pl.multiple_of(r * FFT_R, FFT_R), FFT_R)
        xe_ref[_slab(r), :] = hs_ref[rows, :].astype(F32)
        xo_ref[_slab(r), :] = hd_ref[rows, :].astype(F32)
    _loop(fill)

    def first(r):
        for par in (0, 1):
            o = jnp.dot(t1r_ref[par], xx[par][_across(r), :].astype(BF16), preferred_element_type=F32)
            pp[par][0][_slab(r), :] = o[:FFT_R]
            pp[par][1][_slab(r), :] = o[FFT_R:]
    _loop(first)

    def second(r):
        rows = pl.ds(pl.multiple_of(r * FFT_R, FFT_R), FFT_R)
        for par in (0, 1):
            o = jnp.dot(t2_ref[2 * r + par], _cat_bf16(pp[par][0][_across(r), :], pp[par][1][_across(r), :]),
                        preferred_element_type=F32)
            h_ref[0, 2 * par, rows, :] = o[:FFT_R].astype(h_ref.dtype)
            h_ref[0, 2 * par + 1, rows, :] = o[FFT_R:].astype(h_ref.dtype)
    _loop(second)


def _spectra(hs, hd, t1r, t2, seq):
    ncol = hs.shape[1] // LANES
    nct = ncol // HYENA_ORDER
    prow = FFT_R * FFT_PITCH
    const = lambda a: pl.BlockSpec(a.shape, lambda o, c: (0,) * a.ndim, pipeline_mode=pl.Buffered(1))
    return pl.pallas_call(
        _spectra_kernel,
        out_shape=jax.ShapeDtypeStruct((HYENA_ORDER, 4, seq, nct * LANES), BF16),
        grid=(HYENA_ORDER, nct),
        in_specs=[pl.BlockSpec((seq, LANES), lambda o, c: (0, o * nct + c)),
                  pl.BlockSpec((seq, LANES), lambda o, c: (0, o * nct + c)),
                  const(t1r), const(t2)],
        out_specs=pl.BlockSpec((1, 4, seq, LANES), lambda o, c: (o, 0, 0, c)),
        scratch_shapes=[pltpu.VMEM((prow, LANES), F32)] * 6,
        compiler_params=_cparams(("arbitrary", "arbitrary")),
        name="hyspectra",
    )(hs, hd, t1r, t2)


def _short_conv(x, w_ref, b_ref):
    rows = x.shape[0]
    sub = 8
    ri = lax.broadcasted_iota(jnp.int32, (sub, x.shape[1]), 0)
    prev = pltpu.roll(x, 1, 0)
    prev = jnp.concatenate([jnp.where(ri == 0, 0.0, prev[:sub]), prev[sub:]], axis=0)
    nxt = pltpu.roll(x, rows - 1, 0)
    nxt = jnp.concatenate([nxt[:rows - sub], jnp.where(ri == sub - 1, 0.0, nxt[rows - sub:])], axis=0)
    return prev * w_ref[0:1, :] + x * w_ref[1:2, :] + nxt * w_ref[2:3, :] + b_ref[...]


def _hyconv_kernel(z_ref, g_ref, cwz_ref, cbz_ref, cwg_ref, cbg_ref, skip_ref, h_ref,
                   t1_ref, t2_ref, t3_ref, t4_ref, o_ref,
                   xr_ref, xi_ref, per_ref, pei_ref, por_ref, poi_ref, qer_ref, qei_ref, qor_ref, qoi_ref,
                   *, conv_z):
    seq = z_ref.shape[0] // 2
    for half, dst in ((0, xr_ref), (1, xi_ref)):
        z = z_ref[half * seq:(half + 1) * seq, :].astype(F32)
        if conv_z:
            z = _short_conv(z, cwz_ref, cbz_ref)
        for j in range(FFT_R):
            dst[j * FFT_PITCH:j * FFT_PITCH + FFT_R, :] = z[j * FFT_R:(j + 1) * FFT_R]

    pp = ((per_ref, pei_ref), (por_ref, poi_ref))
    qq = ((qer_ref, qei_ref), (qor_ref, qoi_ref))
    yr_ref, yi_ref = qer_ref, qei_ref
    t1 = jnp.concatenate([t1_ref[0], t1_ref[1]], axis=0)
    t4 = jnp.concatenate([t4_ref[0], t4_ref[1]], axis=1)
    halves = (slice(0, LANES), slice(LANES, 2 * LANES))

    def first(r):
        o = jnp.dot(t1, _pair_bf16(xr_ref, xi_ref, _across(r), _across(r + 1)), preferred_element_type=F32)
        for par in (0, 1):
            for k, cols in zip((r, r + 1), halves):
                pp[par][0][_slab(k), :] = o[2 * par * FFT_R:(2 * par + 1) * FFT_R, cols]
                pp[par][1][_slab(k), :] = o[(2 * par + 1) * FFT_R:(2 * par + 2) * FFT_R, cols]
    _loop(first, step=2)

    def second(r):
        rows = pl.ds(pl.multiple_of(r * FFT_R, FFT_R), FFT_R)
        for par in (0, 1):
            o = jnp.dot(t2_ref[2 * r + par], _cat_bf16(pp[par][0][_across(r), :], pp[par][1][_across(r), :]),
                        preferred_element_type=F32)
            hr = h_ref[0, 2 * par, rows, :].astype(F32)
            hi = h_ref[0, 2 * par + 1, rows, :].astype(F32)
            ar, ai = o[:FFT_R], o[FFT_R:]
            qq[par][0][_slab(r), :] = ar * hr - ai * hi
            qq[par][1][_slab(r), :] = ar * hi + ai * hr
    _loop(second)

    def third(r):
        for par in (0, 1):
            o = jnp.dot(t3_ref[2 * r + par], _cat_bf16(qq[par][0][_slab(r), :], qq[par][1][_slab(r), :]),
                        preferred_element_type=F32)
            pp[par][0][_across(r), :] = o[:FFT_R]
            pp[par][1][_across(r), :] = o[FFT_R:]
    _loop(third)

    def fourth(r):
        quad = lambda k: jnp.concatenate([pp[0][0][_slab(k), :], pp[0][1][_slab(k), :],
                                          pp[1][0][_slab(k), :], pp[1][1][_slab(k), :]], axis=0).astype(BF16)
        o = jnp.dot(t4, jnp.concatenate([quad(r), quad(r + 1)], axis=1), preferred_element_type=F32)
        for k, cols in zip((r, r + 1), halves):
            yr_ref[_across(k), :] = o[:FFT_R, cols]
            yi_ref[_across(k), :] = o[FFT_R:, cols]
    _loop(fourth, step=2)

    skip = skip_ref[...]
    for half, (y_ref, x_ref) in enumerate(((yr_ref, xr_ref), (yi_ref, xi_ref))):
        g = _short_conv(g_ref[half * seq:(half + 1) * seq, :].astype(F32), cwg_ref, cbg_ref)
        for j in range(FFT_R):
            src = slice(j * FFT_PITCH, j * FFT_PITCH + FFT_R)
            dst = slice(half * seq + j * FFT_R, half * seq + (j + 1) * FFT_R)
            o_ref[dst, :] = (g[j * FFT_R:(j + 1) * FFT_R] * (y_ref[src, :] + skip * x_ref[src, :])).astype(o_ref.dtype)


def _hyconv(zsrc, zcol0, gcol0, proj, conv_w, conv_b, zpart, gpart, skip, spec, order, tabs, batch, seq, conv_z):
    t1, t2, t3, t4 = tabs
    nct = HYENA_WIDTH // LANES
    prow = FFT_R * FFT_PITCH
    const = lambda a: pl.BlockSpec(a.shape, lambda c, p: (0,) * a.ndim, pipeline_mode=pl.Buffered(1))
    return pl.pallas_call(
        functools.partial(_hyconv_kernel, conv_z=conv_z),
        out_shape=jax.ShapeDtypeStruct((batch * seq, HYENA_WIDTH), BF16),
        grid=(nct, batch // 2),
        in_specs=[
            pl.BlockSpec((2 * seq, LANES), lambda c, p: (p, zcol0 + c)),
            pl.BlockSpec((2 * seq, LANES), lambda c, p: (p, gcol0 + c)),
            pl.BlockSpec((3, LANES), lambda c, p: (0, zpart * nct + c)),
            pl.BlockSpec((1, LANES), lambda c, p: (0, zpart * nct + c)),
            pl.BlockSpec((3, LANES), lambda c, p: (0, gpart * nct + c)),
            pl.BlockSpec((1, LANES), lambda c, p: (0, gpart * nct + c)),
            pl.BlockSpec((1, LANES), lambda c, p: (0, c)),
            pl.BlockSpec((1, 4, seq, LANES), lambda c, p: (order, 0, 0, c), pipeline_mode=pl.Buffered(1)),
            const(t1), const(t2), const(t3), const(t4),
        ],
        out_specs=pl.BlockSpec((2 * seq, LANES), lambda c, p: (p, c)),
        scratch_shapes=[pltpu.VMEM((prow, LANES), F32)] * 10,
        compiler_params=_cparams(("arbitrary", "arbitrary")),
        name=f"hyconv{order}",
    )(zsrc, proj, conv_w, conv_b, conv_w, conv_b, skip[order:order + 1], spec, t1, t2, t3, t4)


def _hyena(proj, hycol0, conv_w, conv_b, w1, b1, w2, b2, w3, b3, w_out, freq, skip, batch, seq):
    hid = w2.shape[0]
    eye = jnp.eye(FILT_GROUPS, dtype=F32)
    bdiag = lambda a: jnp.kron(eye, a)
    tile = lambda a: jnp.tile(a[None], (1, FILT_GROUPS))
    bands = tile(jnp.linspace(1e-4, FILTER_BANDS - 1, FILTER_BANDS, dtype=F32))
    min_decay = math.log(DECAY_TARGET) / SLOW_DECAY_PCT
    max_decay = math.log(DECAY_TARGET) / FAST_DECAY_PCT
    deltas = jnp.abs(jnp.linspace(min_decay, max_decay, HYENA_WIDTH, dtype=F32))
    wo4 = w_out.reshape(hid, HYENA_ORDER, 2, HYENA_WIDTH)
    w1t = jnp.pad(w1[0:1], ((0, FILTER_BANDS - 1), (0, 0)))
    hs, hd = _filters(
        bands, bdiag(w1t), bdiag(w1[1:1 + FILTER_BANDS]), bdiag(w1[1 + FILTER_BANDS:]),
        tile(b1), bdiag(w2), tile(b2), bdiag(w3), tile(b3), tile(freq),
        wo4[:, :, 0].reshape(hid, -1), wo4[:, :, 1].reshape(hid, -1),
        jnp.tile(deltas, HYENA_ORDER)[None], seq)
    t1, t2, t3, t4, t1r = _dft_tables(seq)
    spec = _spectra(hs, hd, t1r, t2, seq)
    c0 = hycol0 // LANES
    nct = HYENA_WIDTH // LANES
    cb = conv_b[None]
    z1 = _hyconv(proj, c0, c0 + nct, proj, conv_w, cb, 0, 1, skip, spec, 0, (t1, t2, t3, t4), batch, seq, True)
    return _hyconv(z1, 0, c0 + 2 * nct, proj, conv_w, cb, 0, 2, skip, spec, 1, (t1, t2, t3, t4), batch, seq, False)


def _rope_tables(seq):
    pos = np.arange(seq, dtype=np.float32)
    inv_freq = (ROPE_THETA ** (-np.arange(0, ROPE_DIM, 2, dtype=np.float32) / ROPE_DIM)).astype(np.float32)
    ang = pos[:, None] * inv_freq[None, :]
    cos, sin = np.cos(ang).astype(np.float32), np.sin(ang).astype(np.float32)
    half = ROPE_DIM // 2
    c = np.ones((seq, QK_DIM), np.float32)
    s = np.zeros((seq, QK_DIM), np.float32)
    c[:, :half] = cos
    c[:, half:ROPE_DIM] = cos
    s[:, :half] = -sin
    s[:, half:ROPE_DIM] = sin
    perm = np.zeros((LANES, LANES), np.float32)
    for j in range(LANES):
        if j % QK_DIM < half:
            perm[j + half, j] = 1.0
        elif j % QK_DIM < ROPE_DIM:
            perm[j - half, j] = 1.0
    rep = LANES // QK_DIM
    return (jnp.asarray(np.tile(c, (1, rep))), jnp.asarray(np.tile(s, (1, rep))), jnp.asarray(perm).astype(BF16))


def kernel(x, norm1_g, w_in, short_conv_w, short_conv_b, q_norm_g, k_norm_g, lambda_q1, lambda_k1, lambda_q2, lambda_k2, subln_g, filt_w1, filt_b1, filt_w2, filt_b2, filt_w3, filt_b3, filt_w_out, filt_freq, hyena_skip, w_branch_attn, w_branch_hyena, w_out, norm2_g, w_router, w_gate, w_up, w_down):
    b, l, d = x.shape
    depth = w_in.shape[0]
    n = b * l
    cap = CAPACITY_FACTOR * l // N_EXPERTS
    q_cols = ATTN_HEADS * 2 * QK_DIM
    ctab, stab, perm = _rope_tables(l)
    xc = x.reshape(n, d)
    for li in range(depth):
        lambda_init = 0.8 - 0.6 * math.exp(-0.3 * li)
        proj = _inproj(xc, norm1_g[li][None], w_in[li].astype(BF16))
        scale = math.log2(math.e) / math.sqrt(QK_DIM)
        gqk = jnp.concatenate([jnp.tile(q_norm_g[li], q_cols // QK_DIM) * scale,
                               jnp.tile(k_norm_g[li], q_cols // QK_DIM)])[None]
        qk = _qkprep(proj, ctab, stab, perm, gqk, l)
        lamv = jnp.stack([lambda_q1[li], lambda_k1[li], lambda_q2[li], lambda_k2[li]])
        attn = _attention(qk, proj, lamv, subln_g[li][None], b, l, lambda_init)

        hyena = _hyena(proj, 3 * q_cols, short_conv_w[li], short_conv_b[li], filt_w1[li], filt_b1[li],
                       filt_w2[li], filt_b2[li], filt_w3[li], filt_b3[li], filt_w_out[li], filt_freq[li],
                       hyena_skip[li], b, l)

        wr_pad = jnp.pad(w_router[li], ((0, 0), (0, LANES - N_EXPERTS))).astype(BF16)
        x1, u2, aff = _merge(attn, hyena, proj, xc, w_branch_attn[li].astype(BF16),
                             w_branch_hyena[li].astype(BF16), w_out[li].astype(BF16),
                             norm2_g[li][None], wr_pad)
        pos, post, afft, offs = _topk(aff, b, l, cap)
        xin = _gather(offs, post, u2, b, l, cap)
        weo = _experts(offs, post, afft, xin, w_gate, w_up, w_down, li, b, l, cap)
        xc = _combine(offs, pos, x1, weo, b, l, cap)
    return xc.reshape(b, l, d)
```

```python
import functools
import math

import jax
import jax.numpy as jnp
import numpy as np
from jax import lax
from jax.experimental import pallas as pl
from jax.experimental.pallas import tpu as pltpu

F32 = jnp.float32
BF16 = jnp.bfloat16

ATTN_HEADS = 4
QK_DIM = 64
V_DIM = 128
ROPE_DIM = 16
ROPE_THETA = 500000.0
HYENA_WIDTH = 512
HYENA_ORDER = 2
FILTER_BANDS = 16
DECAY_TARGET = 1e-2
FAST_DECAY_PCT = 0.3
SLOW_DECAY_PCT = 1.5
N_EXPERTS = 16
CAPACITY_FACTOR = 2
MOE_TILE = 512
EPS = 1e-6
LANES = 128
FILT_GROUPS = LANES // FILTER_BANDS
VMEM_LIMIT = 56 * 1024 * 1024

INPROJ_ROWS, INPROJ_COLS = 2048, 1280
QKPREP_ROWS = 2048
ATTN_Q_ROWS, ATTN_KEY_CHUNK = 1024, 256
MERGE_ROWS = 1024
TOPK_CUMSUM_ROWS = 256
GATHER_WINDOW = 96
AFFINITY_WINDOW = 128
EXPERT_SEQS_PER_STEP = 2
COMBINE_WINDOW = 128
FILTER_ROWS = 1024


def _cparams(sem):
    return pltpu.CompilerParams(dimension_semantics=sem, vmem_limit_bytes=VMEM_LIMIT)


def _inproj_kernel(x_ref, g_ref, w_ref, o_ref, u_ref):
    j = pl.program_id(1)

    @pl.when(j == 0)
    def _():
        x = x_ref[...]
        ms = jnp.mean(x * x, axis=-1, keepdims=True)
        u = (x * lax.rsqrt(ms + EPS) * g_ref[...]).astype(BF16)
        u_ref[...] = u
        o_ref[...] = jnp.dot(u, w_ref[...], preferred_element_type=F32).astype(o_ref.dtype)

    @pl.when(j > 0)
    def _():
        o_ref[...] = jnp.dot(u_ref[...], w_ref[...], preferred_element_type=F32).astype(o_ref.dtype)


def _inproj(x2, g, w_bf16, tm=INPROJ_ROWS, tn=INPROJ_COLS):
    n, d = x2.shape
    cols = w_bf16.shape[1]
    return pl.pallas_call(
        _inproj_kernel,
        out_shape=jax.ShapeDtypeStruct((n, cols), BF16),
        grid=(n // tm, cols // tn),
        in_specs=[
            pl.BlockSpec((tm, d), lambda i, j: (i, 0)),
            pl.BlockSpec((1, d), lambda i, j: (0, 0)),
            pl.BlockSpec((d, tn), lambda i, j: (0, j)),
        ],
        out_specs=pl.BlockSpec((tm, tn), lambda i, j: (i, j)),
        scratch_shapes=[pltpu.VMEM((tm, d), BF16)],
        compiler_params=_cparams(("arbitrary", "arbitrary")),
        name="inproj",
    )(x2, g, w_bf16)


def _qkprep_kernel(p_ref, c_ref, s_ref, perm_ref, g_ref, o_ref):
    lane = lax.broadcasted_iota(jnp.int32, (1, LANES), 1)
    lo = lane < QK_DIM
    c = c_ref[...]
    s = s_ref[...]
    perm = perm_ref[...]
    for j in range(p_ref.shape[1] // LANES):
        t = p_ref[:, j * LANES:(j + 1) * LANES].astype(F32)
        sq = t * t
        ss_lo = jnp.sum(jnp.where(lo, sq, 0.0), axis=-1, keepdims=True)
        ss_hi = jnp.sum(jnp.where(lo, 0.0, sq), axis=-1, keepdims=True)
        r = lax.rsqrt(jnp.where(lo, ss_lo, ss_hi) * (1.0 / QK_DIM) + EPS)
        y = t * r * g_ref[:, j * LANES:(j + 1) * LANES]
        y_hi = y.astype(BF16)
        y_lo = (y - y_hi.astype(F32)).astype(BF16)
        partner = (jnp.dot(y_hi, perm, preferred_element_type=F32)
                   + jnp.dot(y_lo, perm, preferred_element_type=F32))
        o_ref[:, j * LANES:(j + 1) * LANES] = (y * c + partner * s).astype(o_ref.dtype)


def _qkprep(proj, ctab, stab, perm, gqk, seq, tm=QKPREP_ROWS):
    n = proj.shape[0]
    w = gqk.shape[1]
    nb = seq // tm
    return pl.pallas_call(
        _qkprep_kernel,
        out_shape=jax.ShapeDtypeStruct((n, w), BF16),
        grid=(n // tm,),
        in_specs=[
            pl.BlockSpec((tm, w), lambda i: (i, 0)),
            pl.BlockSpec((tm, LANES), lambda i: (i % nb, 0)),
            pl.BlockSpec((tm, LANES), lambda i: (i % nb, 0)),
            pl.BlockSpec((LANES, LANES), lambda i: (0, 0)),
            pl.BlockSpec((1, w), lambda i: (0, 0)),
        ],
        out_specs=pl.BlockSpec((tm, w), lambda i: (i, 0)),
        compiler_params=_cparams(("arbitrary",)),
        name="qkprep",
    )(proj, ctab, stab, perm, gqk)


def _attn_kernel(q_ref, k_ref, v_ref, lam_ref, g_ref, o_ref, *, tk, unroll, lambda_init):
    tq = q_ref.shape[0]
    seq = k_ref.shape[0]
    lane = lax.broadcasted_iota(jnp.int32, (1, LANES), 1)
    q = q_ref[...]
    zero = jnp.zeros_like(q)
    qs = jnp.concatenate([jnp.where(lane < QK_DIM, q, zero), jnp.where(lane < QK_DIM, zero, q)], axis=0)
    lv = lam_ref[...]
    lam = (jnp.exp(jnp.sum(lv[0:1] * lv[1:2], axis=-1, keepdims=True))
           - jnp.exp(jnp.sum(lv[2:3] * lv[3:4], axis=-1, keepdims=True)) + lambda_init)

    def step(c, carry):
        m, l, a = carry
        off = pl.multiple_of(c * tk, tk)
        kc = k_ref[pl.ds(off, tk), :]
        vc = v_ref[pl.ds(off, tk), :]
        s = lax.dot_general(qs, kc, (((1,), (1,)), ((), ())), preferred_element_type=F32)
        mn = jnp.maximum(m, jnp.max(s, axis=-1, keepdims=True))
        p = jnp.exp2((s - mn).astype(BF16))
        al = jnp.exp2(m - mn)
        pf = p.astype(F32)
        part = pf[:, 0:LANES]
        for j in range(1, tk // LANES):
            part = part + pf[:, j * LANES:(j + 1) * LANES]
        l = al * l + part
        a = al * a + jnp.dot(p, vc, preferred_element_type=F32)
        return mn, l, a

    init = (jnp.full((2 * tq, 1), -jnp.inf, F32), jnp.zeros((2 * tq, LANES), F32), jnp.zeros((2 * tq, V_DIM), F32))
    _, l, a = lax.fori_loop(0, seq // tk, step, init, unroll=unroll)
    a = a / jnp.sum(l, axis=-1, keepdims=True)
    o = a[:tq] - lam * a[tq:]
    ms = jnp.mean(o * o, axis=-1, keepdims=True)
    o = o * lax.rsqrt(ms + EPS) * g_ref[...] * (1.0 - lambda_init)
    o_ref[...] = o.astype(o_ref.dtype)


def _attention(qk, proj, lamv, subln_g, batch, seq, lambda_init, tq=ATTN_Q_ROWS, tk=ATTN_KEY_CHUNK):
    nq = seq // tq
    vcol0 = (2 * ATTN_HEADS * 2 * QK_DIM) // V_DIM
    return pl.pallas_call(
        functools.partial(_attn_kernel, tk=tk, unroll=seq // tk, lambda_init=lambda_init),
        out_shape=jax.ShapeDtypeStruct((batch * seq, ATTN_HEADS * V_DIM), BF16),
        grid=(batch, ATTN_HEADS, nq),
        in_specs=[
            pl.BlockSpec((tq, LANES), lambda b, h, i: (b * nq + i, h)),
            pl.BlockSpec((seq, LANES), lambda b, h, i: (b, ATTN_HEADS + h)),
            pl.BlockSpec((seq, V_DIM), lambda b, h, i: (b, vcol0 + h)),
            pl.BlockSpec((4, QK_DIM), lambda b, h, i: (0, 0)),
            pl.BlockSpec((1, V_DIM), lambda b, h, i: (0, 0)),
        ],
        out_specs=pl.BlockSpec((tq, V_DIM), lambda b, h, i: (b * nq + i, h)),
        compiler_params=_cparams(("arbitrary", "arbitrary", "arbitrary")),
        name="diffattn",
    )(qk, qk, proj, lamv, subln_g)


def _merge_kernel(at_ref, hy_ref, ga_ref, gh_ref, x_ref, wpa_ref, wph_ref, wo_ref, g2_ref, wr_ref,
                  x1_ref, u2_ref, aff_ref):
    tm = x_ref.shape[0]
    parts = 2
    for rows in (slice(k * tm // parts, (k + 1) * tm // parts) for k in range(parts)):
        ga = jax.nn.sigmoid(ga_ref[rows, :].astype(F32))
        gh = jax.nn.sigmoid(gh_ref[rows, :].astype(F32))
        merged = (ga * jnp.dot(at_ref[rows, :], wpa_ref[...], preferred_element_type=F32)
                  + gh * jnp.dot(hy_ref[rows, :], wph_ref[...], preferred_element_type=F32))
        x1 = x_ref[rows, :] + jnp.dot(merged.astype(BF16), wo_ref[...], preferred_element_type=F32)
        x1_ref[rows, :] = x1
        ms = jnp.mean(x1 * x1, axis=-1, keepdims=True)
        u2 = (x1 * lax.rsqrt(ms + EPS) * g2_ref[...]).astype(BF16)
        u2_ref[rows, :] = u2
        logits = jnp.dot(u2, wr_ref[...], preferred_element_type=F32)
        lane = lax.broadcasted_iota(jnp.int32, logits.shape, 1)
        logits = jnp.where(lane < N_EXPERTS, logits, -jnp.inf)
        e = jnp.exp(logits - jnp.max(logits, axis=-1, keepdims=True))
        aff_ref[rows, :] = e / jnp.sum(e, axis=-1, keepdims=True)


def _merge(attn, hyena, proj, x2, wpa, wph, wo, g2, wr_pad, tm=MERGE_ROWS):
    n, d = x2.shape
    gcol0 = (proj.shape[1] - 2 * d) // d
    full = lambda shape: pl.BlockSpec(shape, lambda i: (0, 0))
    return pl.pallas_call(
        _merge_kernel,
        out_shape=(jax.ShapeDtypeStruct((n, d), F32), jax.ShapeDtypeStruct((n, d), BF16),
                   jax.ShapeDtypeStruct((n, LANES), F32)),
        grid=(n // tm,),
        in_specs=[
            pl.BlockSpec((tm, attn.shape[1]), lambda i: (i, 0)),
            pl.BlockSpec((tm, hyena.shape[1]), lambda i: (i, 0)),
            pl.BlockSpec((tm, d), lambda i: (i, gcol0)),
            pl.BlockSpec((tm, d), lambda i: (i, gcol0 + 1)),
            pl.BlockSpec((tm, d), lambda i: (i, 0)),
            full(wpa.shape), full(wph.shape), full(wo.shape), full(g2.shape), full(wr_pad.shape),
        ],
        out_specs=(pl.BlockSpec((tm, d), lambda i: (i, 0)), pl.BlockSpec((tm, d), lambda i: (i, 0)),
                   pl.BlockSpec((tm, LANES), lambda i: (i, 0))),
        compiler_params=_cparams(("arbitrary",)),
        name="merge",
    )(attn, hyena, proj, proj, x2, wpa, wph, wo, g2, wr_pad)


def _excl_cumsum_rows(mask_f32, tri, blk):
    rows = mask_f32.shape[0]
    carry = jnp.zeros((1, LANES), F32)
    outs = []
    for r in range(rows // blk):
        mb = mask_f32[r * blk:(r + 1) * blk]
        outs.append(jnp.dot(tri, mb.astype(BF16), preferred_element_type=F32) + carry)
        carry = carry + jnp.sum(mb, axis=0, keepdims=True)
    return jnp.concatenate(outs, axis=0)


def _topk_kernel(aff_ref, pos_ref, post_ref, afft_ref, offs_ref, *, cap, blk, batch):
    seq = aff_ref.shape[0] // batch
    aff = aff_ref[0:seq]
    for b in range(1, batch):
        aff = aff + pltpu.roll(aff_ref[b * seq:(b + 1) * seq], b * N_EXPERTS, 1)

    def search(i, prefix):
        cand = prefix | jnp.left_shift(jnp.int32(1), 29 - i)
        cnt = jnp.sum((aff >= pltpu.bitcast(cand, F32)[0:1]).astype(F32), axis=0, keepdims=True)
        return jnp.where(cnt >= cap, cand, prefix)

    lo = pltpu.bitcast(lax.fori_loop(0, 30, search, jnp.zeros((8, LANES), jnp.int32)), F32)[0:1]
    thr = jnp.min(jnp.where(aff >= lo, aff, jnp.inf), axis=0, keepdims=True)
    gt = (aff > thr).astype(F32)
    eq = (aff == thr).astype(F32)
    need = cap - jnp.sum(gt, axis=0, keepdims=True)
    ri = lax.broadcasted_iota(jnp.int32, (blk, blk), 0)
    ci = lax.broadcasted_iota(jnp.int32, (blk, blk), 1)
    tri = (ci < ri).astype(BF16)
    sel = gt + eq * (_excl_cumsum_rows(eq, tri, blk) < need).astype(F32)
    before = _excl_cumsum_rows(sel, tri, blk)
    pos = jnp.where(sel > 0.0, before, -1.0)
    pos_t = pos.T
    aff_t = aff.T
    lane = lax.broadcasted_iota(jnp.int32, (1, LANES), 1)
    for b in range(batch):
        mine = pos if b == 0 else pltpu.roll(pos, LANES - b * N_EXPERTS, 1)
        pos_ref[b * seq:(b + 1) * seq] = jnp.where(lane < N_EXPERTS, mine, -1.0)
        post_ref[b] = pos_t[b * N_EXPERTS:(b + 1) * N_EXPERTS]
        afft_ref[b] = aff_t[b * N_EXPERTS:(b + 1) * N_EXPERTS]
    offs_ref[...] = jnp.concatenate([before[j * MOE_TILE:j * MOE_TILE + 1] for j in range(offs_ref.shape[0])], axis=0)


def _topk(aff, batch, seq, cap, blk=TOPK_CUMSUM_ROWS):
    assert batch * N_EXPERTS <= LANES
    nt = seq // MOE_TILE
    pos, post, afft, offs = pl.pallas_call(
        functools.partial(_topk_kernel, cap=cap, blk=blk, batch=batch),
        out_shape=(jax.ShapeDtypeStruct((batch * seq, LANES), F32),
                   jax.ShapeDtypeStruct((batch, N_EXPERTS, seq), F32),
                   jax.ShapeDtypeStruct((batch, N_EXPERTS, seq), F32),
                   jax.ShapeDtypeStruct((nt, LANES), F32)),
        compiler_params=pltpu.CompilerParams(vmem_limit_bytes=VMEM_LIMIT),
        name="topk",
    )(aff)
    offs = offs[:, :batch * N_EXPERTS].reshape(nt, batch, N_EXPERTS).transpose(1, 0, 2)
    return pos, post, afft, offs.astype(jnp.int32).reshape(-1)


def _align_down(x, m):
    sh = m.bit_length() - 1
    return lax.shift_left(lax.shift_right_logical(x, sh), sh)


def _ceil_div(x, m, x_max):
    if m & (m - 1) == 0:
        return lax.shift_right_logical(x + (m - 1), m.bit_length() - 1)
    n = jnp.int32(0)
    for k in range(-(-x_max // m)):
        n = n + (x > k * m).astype(jnp.int32)
    return n


def _gather_kernel(offs_ref, post_ref, u_ref, x_ref, *, cap, win):
    b = pl.program_id(0)
    c = pl.program_id(1)
    nt = pl.num_programs(1)

    @pl.when(c == 0)
    def _():
        x_ref[...] = jnp.zeros_like(x_ref)

    uc = u_ref[...]
    slot0 = lax.broadcasted_iota(jnp.int32, (win, MOE_TILE), 0).astype(F32)
    base = (b * nt + c) * N_EXPERTS
    last = pl.num_programs(0) * nt * N_EXPERTS - 1
    starts, hots = [], []
    for e in range(N_EXPERTS):
        s = pl.multiple_of(jnp.minimum(_align_down(offs_ref[base + e], 16), cap - win), 16)
        hots.append(jnp.where((slot0 + s.astype(F32)) == post_ref[0, e:e + 1, :], 1.0, 0.0).astype(BF16))
        starts.append(s)
    rows = jnp.dot(jnp.concatenate(hots, axis=0), uc, preferred_element_type=F32)
    for e in range(N_EXPERTS):
        x_ref[0, e, pl.ds(starts[e], win), :] += rows[e * win:(e + 1) * win].astype(BF16)
    for e in range(N_EXPERTS):
        lo0 = _align_down(offs_ref[base + e], 16)
        end = jnp.where(c + 1 < nt, offs_ref[jnp.minimum(base + N_EXPERTS + e, last)], cap)

        def more(k, carry, lo0=lo0, e=e):
            lo = lo0 + k * win
            s = pl.multiple_of(jnp.minimum(lo, cap - win), 16)
            srow = slot0 + s.astype(F32)
            hit = jnp.logical_and(srow == post_ref[0, e:e + 1, :], srow >= lo.astype(F32))
            x_ref[0, e, pl.ds(s, win), :] += jnp.dot(jnp.where(hit, 1.0, 0.0).astype(BF16), uc,
                                                     preferred_element_type=F32).astype(BF16)
            return carry

        lax.fori_loop(1, _ceil_div(end - lo0, win, cap + 16), more, 0)


def _gather(offs, post, u2, batch, seq, cap, win=GATHER_WINDOW):
    d = u2.shape[1]
    nt = seq // MOE_TILE
    grid_spec = pltpu.PrefetchScalarGridSpec(
        num_scalar_prefetch=1,
        grid=(batch, nt),
        in_specs=[
            pl.BlockSpec((1, N_EXPERTS, MOE_TILE), lambda b, c, o: (b, 0, c)),
            pl.BlockSpec((MOE_TILE, d), lambda b, c, o: (b * nt + c, 0)),
        ],
        out_specs=pl.BlockSpec((1, N_EXPERTS, cap, d), lambda b, c, o: (b, 0, 0, 0)),
    )
    return pl.pallas_call(
        functools.partial(_gather_kernel, cap=cap, win=win),
        out_shape=jax.ShapeDtypeStruct((batch, N_EXPERTS, cap, d), BF16),
        grid_spec=grid_spec,
        compiler_params=_cparams(("arbitrary", "arbitrary")),
        name="gather",
    )(offs, post, u2)


def _expert_kernel(offs_ref, post_ref, afft_ref, x_ref, wg_ref, wu_ref, wd_ref, o_ref, gs_ref, w_ref, eo_ref,
                   *, cap, win):
    e = pl.program_id(0)
    b = pl.program_id(1)

    def cast():
        for m, src in enumerate((wg_ref, wu_ref, wd_ref)):
            part = src.shape[2]
            w_ref[jnp.bitwise_and(e, 1), m, pl.ds(pl.multiple_of(b * part, part), part), :] = src[0, 0].astype(BF16)

    def run():
        _expert_body(offs_ref, post_ref, afft_ref, x_ref, w_ref.at[1 - jnp.bitwise_and(e, 1)], o_ref, gs_ref, eo_ref,
                     e - 1, b, cap, win)

    @pl.when(e == 0)
    def _():
        cast()
        o_ref[...] = jnp.zeros_like(o_ref)

    @pl.when(jnp.logical_and(e > 0, e < N_EXPERTS))
    def _():
        cast()
        run()

    @pl.when(e == N_EXPERTS)
    def _():
        run()


def _expert_body(offs_ref, post_ref, afft_ref, x_ref, w_ref, o_ref, gs_ref, eo_ref, e, g, cap, win):
    group = x_ref.shape[0]
    seq = post_ref.shape[2]
    nt = seq // MOE_TILE
    gs_ref[...] = jnp.zeros_like(gs_ref)
    slot0 = lax.broadcasted_iota(jnp.int32, (win, MOE_TILE), 0).astype(F32)
    windows = []
    for q in range(group):
        for c in range(nt):
            base = ((g * group + q) * nt + c) * N_EXPERTS + e
            lo0 = _align_down(offs_ref[base], 8)
            end = offs_ref[base + N_EXPERTS] if c + 1 < nt else cap
            prow = post_ref[q, pl.ds(e, 1), c * MOE_TILE:(c + 1) * MOE_TILE]
            arow = afft_ref[q, pl.ds(e, 1), c * MOE_TILE:(c + 1) * MOE_TILE]

            def window(k, carry, lo0=lo0, prow=prow, arow=arow, q=q):
                lo = lo0 + k * win
                s = jnp.minimum(lo, cap - win)
                srow = slot0 + s.astype(F32)
                hit = jnp.logical_and(srow == prow, srow >= lo.astype(F32))
                rows = pl.ds(pl.multiple_of(q * cap + s, 8), win)
                gs_ref[rows, :] += jnp.sum(jnp.where(hit, arow, 0.0), axis=-1, keepdims=True)
                return carry

            window(0, 0)
            windows.append((lo0, end, window))
    xb = jnp.concatenate([x_ref[q, 0] for q in range(group)], axis=0)
    hg = jnp.dot(xb, w_ref[0], preferred_element_type=F32)
    hu = jnp.dot(xb, w_ref[1], preferred_element_type=F32)
    act = (hg * jax.nn.sigmoid(hg) * hu).astype(BF16)
    eo_ref[...] = jnp.dot(act, w_ref[2], preferred_element_type=F32)
    for lo0, end, window in windows:
        lax.fori_loop(1, _ceil_div(end - lo0, win, cap + 16), window, 0)
    for q in range(group):
        rows = slice(q * cap, (q + 1) * cap)
        o_ref[q, 0] = (eo_ref[rows, :] * gs_ref[rows, :]).astype(o_ref.dtype)


def _experts(offs, post, afft, xin, wg, wu, wd, layer, batch, seq, cap, win=AFFINITY_WINDOW):
    d = xin.shape[3]
    group = EXPERT_SEQS_PER_STEP
    steps = batch // group
    assert wg.shape[2:] == (d, d) and wd.shape[2:] == (d, d) and batch % group == 0 and d % (16 * steps) == 0
    part = d // steps
    wspec = pl.BlockSpec((1, 1, part, d), lambda e, b, o: (layer, jnp.minimum(e, N_EXPERTS - 1), b, 0))
    grid_spec = pltpu.PrefetchScalarGridSpec(
        num_scalar_prefetch=1,
        grid=(N_EXPERTS + 1, steps),
        in_specs=[
            pl.BlockSpec((group, N_EXPERTS, seq), lambda e, b, o: (b, 0, 0)),
            pl.BlockSpec((group, N_EXPERTS, seq), lambda e, b, o: (b, 0, 0)),
            pl.BlockSpec((group, 1, cap, d), lambda e, b, o: (b, jnp.maximum(e - 1, 0), 0, 0)),
            wspec, wspec, wspec,
        ],
        out_specs=pl.BlockSpec((group, 1, cap, d), lambda e, b, o: (b, jnp.where(e == 0, N_EXPERTS, e - 1), 0, 0)),
        scratch_shapes=[pltpu.VMEM((group * cap, 1), F32), pltpu.VMEM((2, 3, d, d), BF16),
                        pltpu.VMEM((group * cap, d), F32)],
    )
    return pl.pallas_call(
        functools.partial(_expert_kernel, cap=cap, win=win),
        out_shape=jax.ShapeDtypeStruct((batch, N_EXPERTS + 1, cap, d), BF16),
        grid_spec=grid_spec,
        compiler_params=_cparams(("arbitrary", "arbitrary")),
        name="experts",
    )(offs, post, afft, xin, wg, wu, wd)


def _combine_kernel(offs_ref, pos_ref, x1_ref, weo_ref, o_ref, *, cap, win):
    b = pl.program_id(0)
    t = pl.program_id(1)
    nt = pl.num_programs(1)
    tt = pos_ref.shape[0]
    pos = pos_ref[...]
    slot0 = lax.broadcasted_iota(jnp.int32, (tt, win), 1).astype(F32)
    base = (b * nt + t) * N_EXPERTS
    last = pl.num_programs(0) * nt * N_EXPERTS - 1
    acc = x1_ref[...]
    for e0 in range(0, N_EXPERTS, 2):
        hots, rows = [], []
        for e in (e0, e0 + 1):
            s = pl.multiple_of(jnp.minimum(_align_down(offs_ref[base + e], 16), cap - win), 16)
            hots.append(jnp.where((slot0 + s.astype(F32)) == pos[:, e:e + 1], 1.0, 0.0).astype(BF16))
            rows.append(weo_ref[0, e, pl.ds(s, win), :])
        acc = acc + jnp.dot(jnp.concatenate(hots, axis=1), jnp.concatenate(rows, axis=0),
                            preferred_element_type=F32)
    o_ref[...] = acc
    for e in range(N_EXPERTS):
        lo0 = _align_down(offs_ref[base + e], 16)
        end = jnp.where(t + 1 < nt, offs_ref[jnp.minimum(base + N_EXPERTS + e, last)], cap)

        def window(k, carry, lo0=lo0, e=e):
            lo = lo0 + k * win
            s = pl.multiple_of(jnp.minimum(lo, cap - win), 16)
            srow = slot0 + s.astype(F32)
            hit = jnp.logical_and(srow == pos[:, e:e + 1], srow >= lo.astype(F32))
            o_ref[...] += jnp.dot(jnp.where(hit, 1.0, 0.0).astype(BF16), weo_ref[0, e, pl.ds(s, win), :],
                                  preferred_element_type=F32)
            return carry

        lax.fori_loop(1, _ceil_div(end - lo0, win, cap + 16), window, 0)


def _combine(offs, pos, x1, weo, batch, seq, cap, win=COMBINE_WINDOW):
    d = x1.shape[1]
    tt = MOE_TILE
    nt = seq // tt
    grid_spec = pltpu.PrefetchScalarGridSpec(
        num_scalar_prefetch=1,
        grid=(batch, nt),
        in_specs=[
            pl.BlockSpec((tt, LANES), lambda b, t, o: (b * nt + t, 0)),
            pl.BlockSpec((tt, d), lambda b, t, o: (b * nt + t, 0)),
            pl.BlockSpec((1, N_EXPERTS, cap, d), lambda b, t, o: (b, 0, 0, 0)),
        ],
        out_specs=pl.BlockSpec((tt, d), lambda b, t, o: (b * nt + t, 0)),
    )
    return pl.pallas_call(
        functools.partial(_combine_kernel, cap=cap, win=win),
        out_shape=jax.ShapeDtypeStruct(x1.shape, F32),
        grid_spec=grid_spec,
        compiler_params=_cparams(("arbitrary", "arbitrary")),
        name="combine",
    )(offs, pos, x1, weo)


FFT_R = 64
FFT_PITCH = FFT_R + 8


def _dft_tables(seq):
    r = FFT_R
    assert seq == r * r
    n = np.arange(r)
    f = np.exp(-2j * np.pi * np.outer(n, n) / r)
    w = lambda e: np.exp(-2j * np.pi * e / (2 * seq))

    def real_rep(a):
        return np.block([[a.real, -a.imag], [a.imag, a.real]]).astype(np.float32)

    m1 = [f, f * w(r * n)[None, :]]
    m2 = [f * w(n * m)[None, :] for m in range(2 * r)]
    m4 = [f.conj() / (2 * seq), w(-r * n)[:, None] * f.conj() / (2 * seq)]
    t1 = np.stack([real_rep(a) for a in m1])
    t2 = np.stack([real_rep(a) for a in m2])
    t3 = np.stack([real_rep(a.conj().T) for a in m2])
    t4 = np.stack([real_rep(a) for a in m4])
    t1r = np.stack([np.concatenate([a.real, a.imag], axis=0).astype(np.float32) for a in m1])
    return tuple(jnp.asarray(t).astype(BF16) for t in (t1, t2, t3, t4, t1r))


def _slab(r):
    return pl.ds(pl.multiple_of(r * FFT_PITCH, 8), FFT_R)


def _across(r):
    return pl.ds(r, FFT_R, stride=FFT_PITCH)


def _cat_bf16(a, b):
    return jnp.concatenate([a, b], axis=0).astype(BF16)


FFT_UNROLL = 32


def _loop(body, step=1, rows_per_trip=FFT_UNROLL):
    def wrapped(i, c):
        body(i * step)
        return c
    lax.fori_loop(0, FFT_R // step, wrapped, 0, unroll=rows_per_trip // step)


def _pair_bf16(re_ref, im_ref, idx0, idx1):
    return jnp.concatenate([_cat_bf16(re_ref[idx0, :], im_ref[idx0, :]),
                            _cat_bf16(re_ref[idx1, :], im_ref[idx1, :])], axis=1)


def _bf16_pieces(w):
    hi = w.astype(BF16)
    return hi, (w - hi.astype(F32)).astype(BF16)


def _dot3(a, w_ref):
    a_hi, a_lo = _bf16_pieces(a)
    return (jnp.dot(a_hi, w_ref[0], preferred_element_type=F32)
            + (jnp.dot(a_hi, w_ref[1], preferred_element_type=F32)
               + jnp.dot(a_lo, w_ref[0], preferred_element_type=F32)))


def _filter_kernel(bands_ref, w1t_ref, w1c_ref, w1s_ref, b1_ref, w2_ref, b2_ref, w3_ref, b3_ref, fr_ref,
                   wf_ref, wb_ref, dl_ref, hs_ref, hd_ref, *, seq):
    tm = hs_ref.shape[0]
    rows = tm // FILT_GROUPS
    hid = wf_ref.shape[1]
    base = pl.program_id(0) * tm
    lane = lax.broadcasted_iota(jnp.int32, (rows, LANES), 1)
    grp = lax.shift_right_logical(lane, FILTER_BANDS.bit_length() - 1)
    pos = (base + grp * rows + lax.broadcasted_iota(jnp.int32, (rows, LANES), 0)).astype(F32)
    first = jnp.bitwise_and(lane, FILTER_BANDS - 1) == 0
    fr = fr_ref[...]
    tscale = 1.0 / max(seq - 1, 1)

    def mlp(p):
        ang = ((2.0 * math.pi / seq) * p) * bands_ref[...]
        pre = (_dot3(jnp.cos(ang), w1c_ref) + _dot3(-jnp.sin(ang), w1s_ref)
               + _dot3(jnp.where(first, p * tscale, 0.0), w1t_ref) + b1_ref[...])
        h = jnp.sin(fr * pre)
        h = jnp.sin(fr * (_dot3(h, w2_ref) + b2_ref[...]))
        return jnp.sin(fr * (_dot3(h, w3_ref) + b3_ref[...]))

    hf_h = mlp(pos)
    hb_h = mlp(seq - pos)
    dl = dl_ref[...]
    for g in range(FILT_GROUPS):
        pcol = (base + g * rows + lax.broadcasted_iota(jnp.int32, (rows, 1), 0)).astype(F32)
        hf = _dot3(hf_h[:, g * hid:(g + 1) * hid], wf_ref) * jnp.exp(-(pcol * tscale) * dl)
        hb = _dot3(hb_h[:, g * hid:(g + 1) * hid], wb_ref) * jnp.exp(-((seq - pcol) * tscale) * dl)
        hb = jnp.where(pcol > 0.0, hb, 0.0)
        hs_ref[g * rows:(g + 1) * rows, :] = (hf + hb).astype(hs_ref.dtype)
        hd_ref[g * rows:(g + 1) * rows, :] = (hf - hb).astype(hd_ref.dtype)


def _filters(bands, w1t, w1c, w1s, b1, w2, b2, w3, b3, fr, wf, wb, dl, seq, tm=FILTER_ROWS):
    cols = wf.shape[1]
    pieces = lambda w: jnp.stack(_bf16_pieces(w))
    args = (bands, pieces(w1t), pieces(w1c), pieces(w1s), b1, pieces(w2), b2, pieces(w3), b3, fr,
            pieces(wf), pieces(wb), dl)
    return pl.pallas_call(
        functools.partial(_filter_kernel, seq=seq),
        out_shape=(jax.ShapeDtypeStruct((seq, cols), BF16), jax.ShapeDtypeStruct((seq, cols), BF16)),
        grid=(seq // tm,),
        in_specs=[pl.BlockSpec(a.shape, lambda i, nd=a.ndim: (0,) * nd) for a in args],
        out_specs=(pl.BlockSpec((tm, cols), lambda i: (i, 0)), pl.BlockSpec((tm, cols), lambda i: (i, 0))),
        compiler_params=_cparams(("arbitrary",)),
        name="hyfilter",
    )(*args)


def _spectra_kernel(hs_ref, hd_ref, t1r_ref, t2_ref, h_ref, xe_ref, xo_ref, per_ref, pei_ref, por_ref, poi_ref):
    xx = (xe_ref, xo_ref)
    pp = ((per_ref, pei_ref), (por_ref, poi_ref))

    def fill(r):
        rows = pl.ds(pl.multiple_of(r * FFT_R, FFT_R), FFT_R)
        xe_ref[_slab(r), :] = hs_ref[rows, :].astype(F32)
        xo_ref[_slab(r), :] = hd_ref[rows, :].astype(F32)
    _loop(fill)

    def first(r):
        for par in (0, 1):
            o = jnp.dot(t1r_ref[par], xx[par][_across(r), :].astype(BF16), preferred_element_type=F32)
            pp[par][0][_slab(r), :] = o[:FFT_R]
            pp[par][1][_slab(r), :] = o[FFT_R:]
    _loop(first)

    def second(r):
        rows = pl.ds(pl.multiple_of(r * FFT_R, FFT_R), FFT_R)
        for par in (0, 1):
            o = jnp.dot(t2_ref[2 * r + par], _cat_bf16(pp[par][0][_across(r), :], pp[par][1][_across(r), :]),
                        preferred_element_type=F32)
            h_ref[0, 2 * par, rows, :] = o[:FFT_R].astype(h_ref.dtype)
            h_ref[0, 2 * par + 1, rows, :] = o[FFT_R:].astype(h_ref.dtype)
    _loop(second)


def _spectra(hs, hd, t1r, t2, seq):
    ncol = hs.shape[1] // LANES
    nct = ncol // HYENA_ORDER
    prow = FFT_R * FFT_PITCH
    const = lambda a: pl.BlockSpec(a.shape, lambda o, c: (0,) * a.ndim, pipeline_mode=pl.Buffered(1))
    return pl.pallas_call(
        _spectra_kernel,
        out_shape=jax.ShapeDtypeStruct((HYENA_ORDER, 4, seq, nct * LANES), BF16),
        grid=(HYENA_ORDER, nct),
        in_specs=[pl.BlockSpec((seq, LANES), lambda o, c: (0, o * nct + c)),
                  pl.BlockSpec((seq, LANES), lambda o, c: (0, o * nct + c)),
                  const(t1r), const(t2)],
        out_specs=pl.BlockSpec((1, 4, seq, LANES), lambda o, c: (o, 0, 0, c)),
        scratch_shapes=[pltpu.VMEM((prow, LANES), F32)] * 6,
        compiler_params=_cparams(("arbitrary", "arbitrary")),
        name="hyspectra",
    )(hs, hd, t1r, t2)


def _short_conv(x, w_ref, b_ref):
    rows = x.shape[0]
    sub = 8
    ri = lax.broadcasted_iota(jnp.int32, (sub, x.shape[1]), 0)
    prev = pltpu.roll(x, 1, 0)
    prev = jnp.concatenate([jnp.where(ri == 0, 0.0, prev[:sub]), prev[sub:]], axis=0)
    nxt = pltpu.roll(x, rows - 1, 0)
    nxt = jnp.concatenate([nxt[:rows - sub], jnp.where(ri == sub - 1, 0.0, nxt[rows - sub:])], axis=0)
    return prev * w_ref[0:1, :] + x * w_ref[1:2, :] + nxt * w_ref[2:3, :] + b_ref[...]


def _hyconv_kernel(z_ref, g_ref, cwz_ref, cbz_ref, cwg_ref, cbg_ref, skip_ref, h_ref,
                   t1_ref, t2_ref, t3_ref, t4_ref, o_ref,
                   xr_ref, xi_ref, per_ref, pei_ref, por_ref, poi_ref, qer_ref, qei_ref, qor_ref, qoi_ref,
                   *, conv_z):
    seq = z_ref.shape[0] // 2
    for half, dst in ((0, xr_ref), (1, xi_ref)):
        z = z_ref[half * seq:(half + 1) * seq, :].astype(F32)
        if conv_z:
            z = _short_conv(z, cwz_ref, cbz_ref)
        for j in range(FFT_R):
            dst[j * FFT_PITCH:j * FFT_PITCH + FFT_R, :] = z[j * FFT_R:(j + 1) * FFT_R]

    pp = ((per_ref, pei_ref), (por_ref, poi_ref))
    qq = ((qer_ref, qei_ref), (qor_ref, qoi_ref))
    yr_ref, yi_ref = qer_ref, qei_ref
    t1 = jnp.concatenate([t1_ref[0], t1_ref[1]], axis=0)
    t4 = jnp.concatenate([t4_ref[0], t4_ref[1]], axis=1)
    halves = (slice(0, LANES), slice(LANES, 2 * LANES))

    def first(r):
        o = jnp.dot(t1, _pair_bf16(xr_ref, xi_ref, _across(r), _across(r + 1)), preferred_element_type=F32)
        for par in (0, 1):
            for k, cols in zip((r, r + 1), halves):
                pp[par][0][_slab(k), :] = o[2 * par * FFT_R:(2 * par + 1) * FFT_R, cols]
                pp[par][1][_slab(k), :] = o[(2 * par + 1) * FFT_R:(2 * par + 2) * FFT_R, cols]
    _loop(first, step=2)

    def second(r):
        rows = pl.ds(pl.multiple_of(r * FFT_R, FFT_R), FFT_R)
        for par in (0, 1):
            o = jnp.dot(t2_ref[2 * r + par], _cat_bf16(pp[par][0][_across(r), :], pp[par][1][_across(r), :]),
                        preferred_element_type=F32)
            hr = h_ref[0, 2 * par, rows, :].astype(F32)
            hi = h_ref[0, 2 * par + 1, rows, :].astype(F32)
            ar, ai = o[:FFT_R], o[FFT_R:]
            qq[par][0][_slab(r), :] = ar * hr - ai * hi
            qq[par][1][_slab(r), :] = ar * hi + ai * hr
    _loop(second)

    def third(r):
        for par in (0, 1):
            o = jnp.dot(t3_ref[2 * r + par], _cat_bf16(qq[par][0][_slab(r), :], qq[par][1][_slab(r), :]),
                        preferred_element_type=F32)
            pp[par][0][_across(r), :] = o[:FFT_R]
            pp[par][1][_across(r), :] = o[FFT_R:]
    _loop(third)

    def fourth(r):
        quad = lambda k: jnp.concatenate([pp[0][0][_slab(k), :], pp[0][1][_slab(k), :],
                                          pp[1][0][_slab(k), :], pp[1][1][_slab(k), :]], axis=0).astype(BF16)
        o = jnp.dot(t4, jnp.concatenate([quad(r), quad(r + 1)], axis=1), preferred_element_type=F32)
        for k, cols in zip((r, r + 1), halves):
            yr_ref[_across(k), :] = o[:FFT_R, cols]
            yi_ref[_across(k), :] = o[FFT_R:, cols]
    _loop(fourth, step=2)

    skip = skip_ref[...]
    for half, (y_ref, x_ref) in enumerate(((yr_ref, xr_ref), (yi_ref, xi_ref))):
        g = _short_conv(g_ref[half * seq:(half + 1) * seq, :].astype(F32), cwg_ref, cbg_ref)
        for j in range(FFT_R):
            src = slice(j * FFT_PITCH, j * FFT_PITCH + FFT_R)
            dst = slice(half * seq + j * FFT_R, half * seq + (j + 1) * FFT_R)
            o_ref[dst, :] = (g[j * FFT_R:(j + 1) * FFT_R] * (y_ref[src, :] + skip * x_ref[src, :])).astype(o_ref.dtype)


def _hyconv(zsrc, zcol0, gcol0, proj, conv_w, conv_b, zpart, gpart, skip, spec, order, tabs, batch, seq, conv_z):
    t1, t2, t3, t4 = tabs
    nct = HYENA_WIDTH // LANES
    prow = FFT_R * FFT_PITCH
    const = lambda a: pl.BlockSpec(a.shape, lambda c, p: (0,) * a.ndim, pipeline_mode=pl.Buffered(1))
    return pl.pallas_call(
        functools.partial(_hyconv_kernel, conv_z=conv_z),
        out_shape=jax.ShapeDtypeStruct((batch * seq, HYENA_WIDTH), BF16),
        grid=(nct, batch // 2),
        in_specs=[
            pl.BlockSpec((2 * seq, LANES), lambda c, p: (p, zcol0 + c)),
            pl.BlockSpec((2 * seq, LANES), lambda c, p: (p, gcol0 + c)),
            pl.BlockSpec((3, LANES), lambda c, p: (0, zpart * nct + c)),
            pl.BlockSpec((1, LANES), lambda c, p: (0, zpart * nct + c)),
            pl.BlockSpec((3, LANES), lambda c, p: (0, gpart * nct + c)),
            pl.BlockSpec((1, LANES), lambda c, p: (0, gpart * nct + c)),
            pl.BlockSpec((1, LANES), lambda c, p: (0, c)),
            pl.BlockSpec((1, 4, seq, LANES), lambda c, p: (order, 0, 0, c), pipeline_mode=pl.Buffered(1)),
            const(t1), const(t2), const(t3), const(t4),
        ],
        out_specs=pl.BlockSpec((2 * seq, LANES), lambda c, p: (p, c)),
        scratch_shapes=[pltpu.VMEM((prow, LANES), F32)] * 10,
        compiler_params=_cparams(("arbitrary", "arbitrary")),
        name=f"hyconv{order}",
    )(zsrc, proj, conv_w, conv_b, conv_w, conv_b, skip[order:order + 1], spec, t1, t2, t3, t4)


def _hyena(proj, hycol0, conv_w, conv_b, w1, b1, w2, b2, w3, b3, w_out, freq, skip, batch, seq):
    hid = w2.shape[0]
    eye = jnp.eye(FILT_GROUPS, dtype=F32)
    bdiag = lambda a: jnp.kron(eye, a)
    tile = lambda a: jnp.tile(a[None], (1, FILT_GROUPS))
    bands = tile(jnp.linspace(1e-4, FILTER_BANDS - 1, FILTER_BANDS, dtype=F32))
    min_decay = math.log(DECAY_TARGET) / SLOW_DECAY_PCT
    max_decay = math.log(DECAY_TARGET) / FAST_DECAY_PCT
    deltas = jnp.abs(jnp.linspace(min_decay, max_decay, HYENA_WIDTH, dtype=F32))
    wo4 = w_out.reshape(hid, HYENA_ORDER, 2, HYENA_WIDTH)
    w1t = jnp.pad(w1[0:1], ((0, FILTER_BANDS - 1), (0, 0)))
    hs, hd = _filters(
        bands, bdiag(w1t), bdiag(w1[1:1 + FILTER_BANDS]), bdiag(w1[1 + FILTER_BANDS:]),
        tile(b1), bdiag(w2), tile(b2), bdiag(w3), tile(b3), tile(freq),
        wo4[:, :, 0].reshape(hid, -1), wo4[:, :, 1].reshape(hid, -1),
        jnp.tile(deltas, HYENA_ORDER)[None], seq)
    t1, t2, t3, t4, t1r = _dft_tables(seq)
    spec = _spectra(hs, hd, t1r, t2, seq)
    c0 = hycol0 // LANES
    nct = HYENA_WIDTH // LANES
    cb = conv_b[None]
    z1 = _hyconv(proj, c0, c0 + nct, proj, conv_w, cb, 0, 1, skip, spec, 0, (t1, t2, t3, t4), batch, seq, True)
    return _hyconv(z1, 0, c0 + 2 * nct, proj, conv_w, cb, 0, 2, skip, spec, 1, (t1, t2, t3, t4), batch, seq, False)


def _rope_tables(seq):
    pos = np.arange(seq, dtype=np.float32)
    inv_freq = (ROPE_THETA ** (-np.arange(0, ROPE_DIM, 2, dtype=np.float32) / ROPE_DIM)).astype(np.float32)
    ang = pos[:, None] * inv_freq[None, :]
    cos, sin = np.cos(ang).astype(np.float32), np.sin(ang).astype(np.float32)
    half = ROPE_DIM // 2
    c = np.ones((seq, QK_DIM), np.float32)
    s = np.zeros((seq, QK_DIM), np.float32)
    c[:, :half] = cos
    c[:, half:ROPE_DIM] = cos
    s[:, :half] = -sin
    s[:, half:ROPE_DIM] = sin
    perm = np.zeros((LANES, LANES), np.float32)
    for j in range(LANES):
        if j % QK_DIM < half:
            perm[j + half, j] = 1.0
        elif j % QK_DIM < ROPE_DIM:
            perm[j - half, j] = 1.0
    rep = LANES // QK_DIM
    return (jnp.asarray(np.tile(c, (1, rep))), jnp.asarray(np.tile(s, (1, rep))), jnp.asarray(perm).astype(BF16))


def kernel(x, norm1_g, w_in, short_conv_w, short_conv_b, q_norm_g, k_norm_g, lambda_q1, lambda_k1, lambda_q2, lambda_k2, subln_g, filt_w1, filt_b1, filt_w2, filt_b2, filt_w3, filt_b3, filt_w_out, filt_freq, hyena_skip, w_branch_attn, w_branch_hyena, w_out, norm2_g, w_router, w_gate, w_up, w_down):
    b, l, d = x.shape
    depth = w_in.shape[0]
    n = b * l
    cap = CAPACITY_FACTOR * l // N_EXPERTS
    q_cols = ATTN_HEADS * 2 * QK_DIM
    ctab, stab, perm = _rope_tables(l)
    xc = x.reshape(n, d)
    for li in range(depth):
        lambda_init = 0.8 - 0.6 * math.exp(-0.3 * li)
        proj = _inproj(xc, norm1_g[li][None], w_in[li].astype(BF16))
        scale = math.log2(math.e) / math.sqrt(QK_DIM)
        gqk = jnp.concatenate([jnp.tile(q_norm_g[li], q_cols // QK_DIM) * scale,
                               jnp.tile(k_norm_g[li], q_cols // QK_DIM)])[None]
        qk = _qkprep(proj, ctab, stab, perm, gqk, l)
        lamv = jnp.stack([lambda_q1[li], lambda_k1[li], lambda_q2[li], lambda_k2[li]])
        attn = _attention(qk, proj, lamv, subln_g[li][None], b, l, lambda_init)

        hyena = _hyena(proj, 3 * q_cols, short_conv_w[li], short_conv_b[li], filt_w1[li], filt_b1[li],
                       filt_w2[li], filt_b2[li], filt_w3[li], filt_b3[li], filt_w_out[li], filt_freq[li],
                       hyena_skip[li], b, l)

        wr_pad = jnp.pad(w_router[li], ((0, 0), (0, LANES - N_EXPERTS))).astype(BF16)
        x1, u2, aff = _merge(attn, hyena, proj, xc, w_branch_attn[li].astype(BF16),
                             w_branch_hyena[li].astype(BF16), w_out[li].astype(BF16),
                             norm2_g[li][None], wr_pad)
        pos, post, afft, offs = _topk(aff, b, l, cap)
        xin = _gather(offs, post, u2, b, l, cap)
        weo = _experts(offs, post, afft, xin, w_gate, w_up, w_down, li, b, l, cap)
        xc = _combine(offs, pos, x1, weo, b, l, cap)
    return xc.reshape(b, l, d)
```
